```python
import math
import jax
import jax.numpy as jnp
from jax import lax
import numpy as np

D_MODEL = 1024
BATCH = 16
SEQ = 256
DEPTH = 2
DEC_BATCH = 4
DEC_SEQ = 2048
PAST_LEN = 512

GRID_W = 64
HEAD_DIM = 64
M_HEADS = 4
M_WIDTH = M_HEADS * HEAD_DIM
G_HEADS = 4
G_WIDTH = G_HEADS * HEAD_DIM
CONV_W = 5
A_HEADS = 8
A_KV_HEADS = 2
A_WIDTH = A_HEADS * HEAD_DIM
KV_WIDTH = A_KV_HEADS * HEAD_DIM
MIX_WIDTH = M_WIDTH + G_WIDTH + A_WIDTH
N_DIR = 2
CHUNK = 64
Q_BLOCK = 128
D_FF = 2816
ROPE_BASE = 10000.0
EPS = 1e-6
N_MOD = 9

SPLIT_SIZES = (M_WIDTH, M_WIDTH, M_WIDTH, M_WIDTH, N_DIR * M_HEADS, N_DIR * M_HEADS,
               3 * G_WIDTH, G_WIDTH, N_DIR * G_HEADS, N_DIR * G_HEADS,
               A_WIDTH, KV_WIDTH, KV_WIDTH)
IN_WIDTH = sum(SPLIT_SIZES)

kernel_name = 'hybrid_mlstm_deltanet_gqa_prefix_dit'


def _rms(x, g):
    xf = x.astype(jnp.float32)
    y = xf * lax.rsqrt(jnp.mean(xf * xf, axis=-1, keepdims=True) + EPS)
    return (y * g.astype(jnp.float32)).astype(x.dtype)


def _swiglu(h, w_in, w_out):
    gate, up = jnp.split(h @ w_in, 2, axis=-1)
    return (jax.nn.silu(gate) * up) @ w_out


def _modulation(cond, ada_w, ada_b):
    m = jax.nn.silu(cond) @ ada_w + ada_b
    return m.reshape(m.shape[0], N_MOD, 1, D_MODEL)


def _axial_rope(n):
    rows = n // GRID_W
    row = jnp.repeat(jnp.arange(rows, dtype=jnp.float32), GRID_W)
    col = jnp.tile(jnp.arange(GRID_W, dtype=jnp.float32), rows)
    n_freq = HEAD_DIM // 4
    inv = ROPE_BASE ** (-jnp.arange(n_freq, dtype=jnp.float32) / n_freq)
    ang = jnp.stack([row[:, None] * inv, col[:, None] * inv], axis=1)
    return jnp.cos(ang), jnp.sin(ang)


def _apply_rope(x, cos, sin):
    B, T, H, _ = x.shape
    xf = x.astype(jnp.float32).reshape(B, T, H, 2, 2, HEAD_DIM // 4)
    x1, x2 = xf[..., 0, :], xf[..., 1, :]
    c = cos[None, :, None]
    s = sin[None, :, None]
    out = jnp.stack([x1 * c - x2 * s, x2 * c + x1 * s], axis=-2)
    return out.reshape(x.shape).astype(x.dtype)


def _short_conv(x, w):
    C = x.shape[-1]
    return lax.conv_general_dilated(
        x, w[:, None, :].astype(x.dtype), window_strides=(1,),
        padding=((CONV_W // 2, CONV_W // 2),),
        dimension_numbers=('NWC', 'WIO', 'NWC'), feature_group_count=C)


def _attention(q, k, v):
    B, Sq, H, Dh = q.shape
    kvh = k.shape[2]
    grp = H // kvh
    nb = Sq // Q_BLOCK
    qb = q.reshape(B, nb, Q_BLOCK, kvh, grp, Dh).transpose(1, 0, 2, 3, 4, 5)
    scale = 1.0 / math.sqrt(Dh)

    def block(qblk):
        s = jnp.einsum('bqkgd,bskd->bkgqs', qblk, k).astype(jnp.float32) * scale
        p = jax.nn.softmax(s, axis=-1).astype(v.dtype)
        return jnp.einsum('bkgqs,bskd->bqkgd', p, v)

    o = lax.map(block, qb)
    return o.transpose(1, 0, 2, 3, 4, 5).reshape(B, Sq, H * Dh)


def _mlstm_scan(q, k, v, ig, lf, state):
    B, T, H, Dh = q.shape
    nc = T // CHUNK
    causal = jnp.tril(jnp.ones((CHUNK, CHUNK), bool))[None, :, :, None]

    def chunks(a):
        return jnp.moveaxis(a.reshape(B, nc, CHUNK, *a.shape[2:]), 1, 0)

    def step(carry, xs):
        C, n, m = carry
        qc, kc, vc, ic, fc = xs
        b = jnp.cumsum(fc, axis=1)
        inter = b + m[:, None, :]
        dlog = jnp.where(causal, b[:, :, None, :] - b[:, None, :, :] + ic[:, None, :, :], -jnp.inf)
        mt = jnp.maximum(inter, dlog.max(axis=2))
        s = jnp.einsum('bthd,bjhd->btjh', qc, kc) * jnp.exp(dlog - mt[:, :, None, :])
        a_int = jnp.exp(inter - mt)
        num = jnp.einsum('btjh,bjhe->bthe', s, vc) + a_int[..., None] * jnp.einsum('bthd,bhde->bthe', qc, C)
        den = s.sum(axis=2) + a_int * jnp.einsum('bthd,bhd->bth', qc, n)
        h = num / jnp.maximum(jnp.abs(den), jnp.exp(-mt))[..., None]
        bl = b[:, -1]
        lw = bl[:, None, :] - b + ic
        m_new = jnp.maximum(bl + m, lw.max(axis=1))
        wj = jnp.exp(lw - m_new[:, None, :])
        dec = jnp.exp(bl + m - m_new)
        C = dec[..., None, None] * C + jnp.einsum('bjh,bjhd,bjhe->bhde', wj, kc, vc)
        n = dec[..., None] * n + jnp.einsum('bjh,bjhd->bhd', wj, kc)
        return (C, n, m_new), h

    carry, hs = lax.scan(step, state, (chunks(q), chunks(k), chunks(v), chunks(ig), chunks(lf)))
    return jnp.moveaxis(hs, 0, 1).reshape(B, T, H, Dh), carry


def _delta_scan(q, k, v, g, beta, S0):
    B, T, H, Dh = q.shape
    nc = T // CHUNK

    def chunks(a):
        return jnp.moveaxis(a.reshape(B, nc, CHUNK, H, *a.shape[3:]), 3, 2)

    qc, kc, vc, gc, bc = (chunks(a) for a in (q, k, v, g, beta))
    gcum = jnp.cumsum(gc, axis=-1)
    lower = jnp.tril(jnp.ones((CHUNK, CHUNK), bool))
    strict = jnp.tril(jnp.ones((CHUNK, CHUNK), bool), -1)
    decay = jnp.exp(jnp.where(lower, gcum[..., :, None] - gcum[..., None, :], -jnp.inf))
    kb = kc * bc[..., None]
    a_mat = jnp.eye(CHUNK, dtype=jnp.float32) + jnp.where(
        strict, jnp.einsum('bnhtd,bnhjd->bnhtj', kb, kc) * decay, 0.0)
    u = lax.linalg.triangular_solve(a_mat, vc * bc[..., None], left_side=True, lower=True, unit_diagonal=True)
    w = lax.linalg.triangular_solve(a_mat, kb * jnp.exp(gcum)[..., None], left_side=True, lower=True, unit_diagonal=True)
    qk = jnp.einsum('bnhtd,bnhjd->bnhtj', qc, kc) * decay
    q_dec = qc * jnp.exp(gcum)[..., None]
    k_dec = kc * jnp.exp(gcum[..., -1:] - gcum)[..., None]
    g_last = jnp.exp(gcum[..., -1])

    def step(S, xs):
        qd, kd, uc, wc, qkc, gl = xs
        v_new = uc - jnp.einsum('bhtd,bhde->bhte', wc, S)
        o = jnp.einsum('bhtd,bhde->bhte', qd, S) + jnp.einsum('bhtj,bhje->bhte', qkc, v_new)
        S = S * gl[..., None, None] + jnp.einsum('bhtd,bhte->bhde', kd, v_new)
        return S, o

    xs = tuple(jnp.moveaxis(a, 1, 0) for a in (q_dec, k_dec, u, w, qk, g_last))
    S, o = lax.scan(step, S0, xs)
    o = jnp.moveaxis(jnp.moveaxis(o, 0, 1), 2, 3).reshape(B, T, H, Dh)
    return o, S


def _flip(a):
    return jnp.flip(a, axis=1)


def _mlstm_bidir(q, k, v, ig, lf, C0, n0, m0):
    hf, (Cf, nf, mf) = _mlstm_scan(q, k, v, ig[:, :, 0], lf[:, :, 0], (C0[:, 0], n0[:, 0], m0[:, 0]))
    hb, (Cb, nb, mb) = _mlstm_scan(_flip(q), _flip(k), _flip(v), _flip(ig[:, :, 1]), _flip(lf[:, :, 1]),
                                   (C0[:, 1], n0[:, 1], m0[:, 1]))
    states = (jnp.stack([Cf, Cb], axis=1), jnp.stack([nf, nb], axis=1), jnp.stack([mf, mb], axis=1))
    return hf + _flip(hb), states


def _delta_bidir(q, k, v, g, beta, S0):
    of, Sf = _delta_scan(q, k, v, g[:, :, 0], beta[:, :, 0], S0[:, 0])
    ob, Sb = _delta_scan(_flip(q), _flip(k), _flip(v), _flip(g[:, :, 1]), _flip(beta[:, :, 1]), S0[:, 1])
    return of + _flip(ob), jnp.stack([Sf, Sb], axis=1)


def _mix(h, lw, ctx):
    B, T, _ = h.shape
    f32 = jnp.float32
    z = h @ lw['w_in']
    (mq, mk, mv, mo, mi, mf, gqkv, gz, ga, gb, aq, ak, av) = jnp.split(
        z, np.cumsum(SPLIT_SIZES)[:-1].tolist(), axis=-1)

    def heads(a, n):
        return a.reshape(B, T, n, HEAD_DIM)

    if ctx is None:
        C0 = jnp.zeros((B, N_DIR, M_HEADS, HEAD_DIM, HEAD_DIM), f32)
        n0 = jnp.zeros((B, N_DIR, M_HEADS, HEAD_DIM), f32)
        m0 = jnp.zeros((B, N_DIR, M_HEADS), f32)
        S0 = jnp.zeros((B, N_DIR, G_HEADS, HEAD_DIM, HEAD_DIM), f32)
    else:
        ck, cv, C0, n0, m0, S0 = ctx
        C0, n0, m0, S0 = C0.astype(f32), n0.astype(f32), m0.astype(f32), S0.astype(f32)

    m_q = heads(mq, M_HEADS).astype(f32)
    m_k = heads(mk, M_HEADS).astype(f32) / math.sqrt(HEAD_DIM)
    m_v = heads(mv, M_HEADS).astype(f32)
    ig = mi.reshape(B, T, N_DIR, M_HEADS).astype(f32)
    lf = jax.nn.log_sigmoid(mf.reshape(B, T, N_DIR, M_HEADS).astype(f32) + lw['m_fbias'].astype(f32))
    mh, (Cn, nn_, mn) = _mlstm_bidir(m_q, m_k, m_v, ig, lf, C0, n0, m0)
    m_out = (_rms(mh, lw['m_norm'].reshape(M_HEADS, HEAD_DIM)).reshape(B, T, M_WIDTH)
             * jax.nn.sigmoid(mo.astype(f32))).astype(h.dtype)

    qkv = jax.nn.silu(_short_conv(gqkv, lw['g_conv']))
    gq, gk, gv = jnp.split(qkv, 3, axis=-1)

    def l2(a):
        return a * lax.rsqrt(jnp.sum(a * a, axis=-1, keepdims=True) + EPS)

    d_q = l2(heads(gq, G_HEADS).astype(f32)) / math.sqrt(HEAD_DIM)
    d_k = l2(heads(gk, G_HEADS).astype(f32))
    d_v = heads(gv, G_HEADS).astype(f32)
    g_log = -jnp.exp(lw['g_alog'].astype(f32)) * jax.nn.softplus(
        ga.reshape(B, T, N_DIR, G_HEADS).astype(f32) + lw['g_dtb'].astype(f32))
    beta = jax.nn.sigmoid(gb.reshape(B, T, N_DIR, G_HEADS).astype(f32))
    go, Sn = _delta_bidir(d_q, d_k, d_v, g_log, beta, S0)
    g_out = (_rms(go, lw['g_norm']) * jax.nn.silu(heads(gz, G_HEADS).astype(f32))).reshape(
        B, T, G_WIDTH).astype(h.dtype)

    a_q = _rms(heads(aq, A_HEADS), lw['q_norm'])
    a_k = _rms(heads(ak, A_KV_HEADS), lw['k_norm'])
    a_v = heads(av, A_KV_HEADS)
    if ctx is None:
        keys, vals = a_k, a_v
    else:
        cos, sin = _axial_rope(T)
        a_q = _apply_rope(a_q, cos, sin)
        keys = jnp.concatenate([ck.astype(h.dtype), _apply_rope(a_k, cos, sin)], axis=1)
        vals = jnp.concatenate([cv.astype(h.dtype), a_v], axis=1)
    a_out = _attention(a_q, keys, vals)

    out = jnp.concatenate([m_out, g_out, a_out], axis=-1) @ lw['w_out']
    new_ctx = (a_k, a_v, Cn, nn_, mn, Sn) if ctx is None else None
    return out, new_ctx


def _layer(x, mod, lw, ctx):
    h = _rms(x, lw['norm'][0]) * (1 + mod[:, 0]) + mod[:, 1]
    x = x + 0.5 * mod[:, 2] * _swiglu(h, lw['ffn_in'][0], lw['ffn_out'][0])
    h = _rms(x, lw['norm'][1]) * (1 + mod[:, 3]) + mod[:, 4]
    mix, new_ctx = _mix(h, lw, ctx)
    x = x + mod[:, 5] * mix
    h = _rms(x, lw['norm'][2]) * (1 + mod[:, 6]) + mod[:, 7]
    x = x + 0.5 * mod[:, 8] * _swiglu(h, lw['ffn_in'][1], lw['ffn_out'][1])
    return x, new_ctx


def setup_inputs(seed: int = 0) -> dict:
    key = jax.random.key(seed)
    ks = jax.random.split(key, 26)
    f32 = jnp.float32

    def nrm(k, shape, s=1.0):
        return s * jax.random.normal(k, shape, f32)

    dt = jnp.exp(jax.random.uniform(ks[21], (DEPTH, N_DIR, G_HEADS), f32, math.log(1e-3), math.log(1e-1)))
    return {
        'x_prompt': nrm(ks[0], (BATCH, SEQ, D_MODEL)),
        'x_sample': nrm(ks[1], (DEC_BATCH, DEC_SEQ, D_MODEL)),
        'c': nrm(ks[2], (DEC_BATCH, D_MODEL)),
        'cache_k': nrm(ks[3], (DEC_BATCH, DEPTH, PAST_LEN, A_KV_HEADS, HEAD_DIM)),
        'cache_v': nrm(ks[4], (DEC_BATCH, DEPTH, PAST_LEN, A_KV_HEADS, HEAD_DIM)),
        'state_mlstm_C': nrm(ks[5], (DEC_BATCH, DEPTH, N_DIR, M_HEADS, HEAD_DIM, HEAD_DIM), 0.1),
        'state_mlstm_n': nrm(ks[6], (DEC_BATCH, DEPTH, N_DIR, M_HEADS, HEAD_DIM), 0.1),
        'state_mlstm_m': nrm(ks[7], (DEC_BATCH, DEPTH, N_DIR, M_HEADS)),
        'state_delta_S': nrm(ks[8], (DEC_BATCH, DEPTH, N_DIR, G_HEADS, HEAD_DIM, HEAD_DIM), 0.1),
        'c_ctx': nrm(ks[9], (D_MODEL,)),
        'ada_w': nrm(ks[10], (DEPTH, D_MODEL, N_MOD * D_MODEL), 0.5 * D_MODEL ** -0.5),
        'ada_b': nrm(ks[11], (DEPTH, N_MOD * D_MODEL), 0.01),
        'norm_g': 1.0 + nrm(ks[12], (DEPTH, 3, D_MODEL), 0.02),
        'ffn_w_in': nrm(ks[13], (DEPTH, 2, D_MODEL, 2 * D_FF), D_MODEL ** -0.5),
        'ffn_w_out': nrm(ks[14], (DEPTH, 2, D_FF, D_MODEL), D_FF ** -0.5),
        'w_in': nrm(ks[15], (DEPTH, D_MODEL, IN_WIDTH), D_MODEL ** -0.5),
        'w_out': nrm(ks[16], (DEPTH, MIX_WIDTH, D_MODEL), MIX_WIDTH ** -0.5),
        'mlstm_f_bias': 3.0 + 3.0 * jax.random.uniform(ks[17], (DEPTH, N_DIR, M_HEADS), f32),
        'mlstm_norm': 1.0 + nrm(ks[18], (DEPTH, M_WIDTH), 0.02),
        'delta_conv': nrm(ks[19], (DEPTH, CONV_W, 3 * G_WIDTH), CONV_W ** -0.5),
        'delta_a_log': jnp.log(jax.random.uniform(ks[20], (DEPTH, N_DIR, G_HEADS), f32, 1.0, 16.0)),
        'delta_dt_bias': dt + jnp.log(-jnp.expm1(-dt)),
        'delta_norm': 1.0 + nrm(ks[22], (DEPTH, HEAD_DIM), 0.02),
        'attn_q_norm': 1.0 + nrm(ks[23], (DEPTH, HEAD_DIM), 0.02),
        'attn_k_norm': 1.0 + nrm(ks[24], (DEPTH, HEAD_DIM), 0.02),
        'final_norm': 1.0 + nrm(ks[25], (D_MODEL,), 0.02),
    }


def reference(x_prompt, x_sample, c, cache_k, cache_v, state_mlstm_C, state_mlstm_n,
              state_mlstm_m, state_delta_S, c_ctx, ada_w, ada_b, norm_g, ffn_w_in,
              ffn_w_out, w_in, w_out, mlstm_f_bias, mlstm_norm, delta_conv, delta_a_log,
              delta_dt_bias, delta_norm, attn_q_norm, attn_k_norm, final_norm):
    xp, xs = x_prompt, x_sample
    ks, vs, Cs, ns, ms, Ss = [], [], [], [], [], []
    for l in range(DEPTH):
        lw = {
            'norm': norm_g[l], 'ffn_in': ffn_w_in[l], 'ffn_out': ffn_w_out[l],
            'w_in': w_in[l], 'w_out': w_out[l],
            'm_fbias': mlstm_f_bias[l], 'm_norm': mlstm_norm[l],
            'g_conv': delta_conv[l], 'g_alog': delta_a_log[l], 'g_dtb': delta_dt_bias[l],
            'g_norm': delta_norm[l], 'q_norm': attn_q_norm[l], 'k_norm': attn_k_norm[l],
        }
        xp, (k_l, v_l, C_l, n_l, m_l, S_l) = _layer(xp, _modulation(c_ctx[None, :], ada_w[l], ada_b[l]), lw, None)
        ks.append(k_l)
        vs.append(v_l)
        Cs.append(C_l)
        ns.append(n_l)
        ms.append(m_l)
        Ss.append(S_l)
        ctx = (cache_k[:, l], cache_v[:, l], state_mlstm_C[:, l], state_mlstm_n[:, l],
               state_mlstm_m[:, l], state_delta_S[:, l])
        xs, _ = _layer(xs, _modulation(c, ada_w[l], ada_b[l]), lw, ctx)
    y_prompt = _rms(xp, final_norm)
    y_sample = _rms(xs, final_norm)
    new_k = jnp.stack(ks, axis=1)
    new_v = jnp.stack(vs, axis=1)
    new_C = jnp.stack(Cs, axis=1)
    new_n = jnp.stack(ns, axis=1)
    new_m = jnp.stack(ms, axis=1)
    new_S = jnp.stack(Ss, axis=1)
    return (y_prompt, y_sample, new_k, new_v, new_C, new_n, new_m, new_S)
```

```python
import functools
import math

import jax
import jax.numpy as jnp
from jax import lax
from jax.experimental import pallas as pl
from jax.experimental.pallas import tpu as pltpu

F32 = jnp.float32
BF16 = jnp.bfloat16

D_MODEL = 1024
HEAD_DIM = 64
N_HEADS = 4
SCAN_W = N_HEADS * HEAD_DIM
A_HEADS = 8
A_KV_HEADS = 2
A_WIDTH = A_HEADS * HEAD_DIM
KV_WIDTH = A_KV_HEADS * HEAD_DIM
N_DIR = 2
CHUNK = 64
CONV_W = 5
D_FF = 2816
FF_CHUNK = 256
GRID_W = 64
ROPE_BASE = 10000.0
EPS = 1e-6
N_MOD = 9
GATE_W = 128
Z_WIDTH = 2 * D_MODEL + A_WIDTH + 2 * KV_WIDTH + GATE_W

SEG = 256
TM = 512
HALO = 8
VMEM_LIMIT = 56 * 1024 * 1024


def _bf(x):
    return x.astype(BF16)


def _dot(a, b):
    return jnp.dot(_bf(a), _bf(b), preferred_element_type=F32)


def _dot_nt(a, b):
    return lax.dot_general(_bf(a), _bf(b), (((1,), (1,)), ((), ())), preferred_element_type=F32)


def _split(a, terms):
    parts = []
    rest = a
    for _ in range(terms):
        p = _bf(rest)
        parts.append(p)
        rest = rest - p.astype(F32)
    return parts


def _dot_exact_rhs(a, b01, terms=2):
    out = None
    for p in _split(a, terms):
        d = jnp.dot(p, b01, preferred_element_type=F32)
        out = d if out is None else out + d
    return out


def _dot3(a, b):
    a_hi, a_lo = _split(a, 2)
    b_hi, b_lo = _split(b, 2)
    out = jnp.dot(a_hi, b_hi, preferred_element_type=F32)
    out += jnp.dot(a_lo, b_hi, preferred_element_type=F32)
    return out + jnp.dot(a_hi, b_lo, preferred_element_type=F32)


def _iota(shape, dim):
    return lax.broadcasted_iota(jnp.int32, shape, dim)


def _head_of(idx):
    return jnp.right_shift(idx, 6)


def _in_head(idx):
    return jnp.bitwise_and(idx, HEAD_DIM - 1)


def _block_ones(n):
    return (_head_of(_iota((n, n), 0)) == _head_of(_iota((n, n), 1)))


def _head_sum(x):
    n = x.shape[-1]
    return _dot_exact_rhs(x, _block_ones(n).astype(BF16), terms=2)


def _head_rms(x, g_row):
    ms = _head_sum(x * x) * (1.0 / HEAD_DIM)
    return x * lax.rsqrt(ms + EPS) * g_row


def _rms_mod(x, g_row, scale_row, shift_row):
    y = x * lax.rsqrt(jnp.mean(x * x, axis=-1, keepdims=True) + EPS) * g_row
    return y * (1.0 + scale_row) + shift_row


def _softplus(x):
    return jnp.maximum(x, 0.0) + jnp.log1p(jnp.exp(-jnp.abs(x)))


def _log_sigmoid(x):
    return -_softplus(-x)


def _sigmoid(x):
    return jax.nn.sigmoid(x)


def _silu(x):
    return x * jax.nn.sigmoid(x)


def _bd(x_all, mask_bd):
    rep = jnp.concatenate([x_all] * N_HEADS, axis=0)
    return jnp.where(mask_bd, rep, 0.0)


def _expand_gate(gates, col0):
    sel = (_iota((GATE_W, SCAN_W), 0) == col0 + _head_of(_iota((GATE_W, SCAN_W), 1))).astype(BF16)
    return _dot_exact_rhs(gates, sel, terms=3)


def _chunk_scan(x, op, ident, rev):
    rows = x.shape[0]
    pos = _in_head(_iota(x.shape, 0))
    s = 1
    while s < CHUNK:
        if rev:
            y = pltpu.roll(x, rows - s, 0)
            ok = pos < CHUNK - s
        else:
            y = pltpu.roll(x, s, 0)
            ok = pos >= s
        x = op(x, jnp.where(ok, y, ident))
        s *= 2
    return x


def _diag_row(x):
    eye = _iota(x.shape, 0) == _in_head(_iota(x.shape, 1))
    return jnp.sum(jnp.where(eye, x, 0.0), axis=0, keepdims=True)


def _tri(rev, strict):
    t = _iota((CHUNK, SCAN_W), 0)
    j = _in_head(_iota((CHUNK, SCAN_W), 1))
    if rev:
        return (j > t) if strict else (j >= t)
    return (j < t) if strict else (j <= t)


def _chunk_order(rev):
    n = SEG // CHUNK
    return range(n - 1, -1, -1) if rev else range(n)


def _mod_kernel(cond_ref, w_ref, b_ref, o_ref):
    a = _silu(cond_ref[...])
    o_ref[0] = _dot(a, w_ref[0]) + b_ref[0]


def _modulation(cond, ada_w, ada_b):
    depth, _, width = ada_w.shape
    tn = D_MODEL
    rows = cond.shape[0]
    return pl.pallas_call(
        _mod_kernel,
        grid=(depth, width // tn),
        in_specs=[
            pl.BlockSpec((rows, D_MODEL), lambda l, j: (0, 0)),
            pl.BlockSpec((1, D_MODEL, tn), lambda l, j: (l, 0, j)),
            pl.BlockSpec((1, 1, tn), lambda l, j: (l, 0, j)),
        ],
        out_specs=pl.BlockSpec((1, rows, tn), lambda l, j: (l, 0, j)),
        out_shape=jax.ShapeDtypeStruct((depth, rows, width), F32),
        compiler_params=pltpu.CompilerParams(
            dimension_semantics=("arbitrary", "arbitrary"), vmem_limit_bytes=VMEM_LIMIT),
        name="modulation",
    )(cond, ada_w, ada_b.reshape(depth, 1, width))


def _ffn(h, w_in_ref, w_out_ref):
    acc = None
    for c in range(D_FF // FF_CHUNK):
        lo = c * FF_CHUNK
        g = jnp.dot(h, w_in_ref[0, :, lo:lo + FF_CHUNK], preferred_element_type=F32)
        u = jnp.dot(h, w_in_ref[0, :, D_FF + lo:D_FF + lo + FF_CHUNK], preferred_element_type=F32)
        a = _bf(_silu(g) * u)
        d = jnp.dot(a, w_out_ref[0, lo:lo + FF_CHUNK, :], preferred_element_type=F32)
        acc = d if acc is None else acc + d
    return acc


def _rope(x, cos, sin_signed):
    w = x.shape[-1]
    reps = w // cos.shape[-1]
    if reps > 1:
        cos = jnp.concatenate([cos] * reps, axis=-1)
        sin_signed = jnp.concatenate([sin_signed] * reps, axis=-1)
    first = jnp.bitwise_and(_iota(x.shape, 1), 31) < 16
    partner = jnp.where(first, pltpu.roll(x, w - 16, 1), pltpu.roll(x, 16, 1))
    return x * cos + partner * sin_signed


def _dense1_kernel(x_ref, mod_ref, ng_ref, wfi_ref, wfo_ref, wz_ref, qn_ref, kn_ref, cos_ref, sin_ref,
                   x1_ref, zm_ref, zg_ref, aq_ref, ak_ref, av_ref, gate_ref):
    x = x_ref[...]
    mod = mod_ref[0, 0]
    h = _bf(_rms_mod(x, ng_ref[0, 0:1], mod[0:1], mod[1:2]))
    x1 = x + 0.5 * mod[2:3] * _ffn(h, wfi_ref, wfo_ref)
    x1_ref[...] = x1
    h2 = _bf(_rms_mod(x1, ng_ref[0, 1:2], mod[3:4], mod[4:5]))

    def proj(lo, width):
        return jnp.dot(h2, wz_ref[0, :, lo:lo + width], preferred_element_type=F32)

    zm_ref[...] = proj(0, D_MODEL)
    zg_ref[...] = proj(D_MODEL, D_MODEL)
    cos = cos_ref[...]
    sin = sin_ref[...]
    off = 2 * D_MODEL
    q = _head_rms(proj(off, A_WIDTH), qn_ref[0])
    aq_ref[...] = _rope(q, cos, sin) * (1.0 / math.sqrt(HEAD_DIM))
    k = _head_rms(proj(off + A_WIDTH, KV_WIDTH), kn_ref[0])
    ak_ref[...] = _rope(k, cos, sin)
    av_ref[...] = proj(off + A_WIDTH + KV_WIDTH, KV_WIDTH)
    gate_ref[...] = proj(off + A_WIDTH + 2 * KV_WIDTH, GATE_W)


def _resident(shape, index_map):
    return pl.BlockSpec(shape, index_map, pipeline_mode=pl.Buffered(1))


def _mod_group(i, n_ctx_tiles, tiles_per_seq):
    return jnp.where(i < n_ctx_tiles, 0, 1 + jnp.maximum(i - n_ctx_tiles, 0) // tiles_per_seq)


def _dense1(layer, x, mod, norm_g, wfi, wfo, wz, qn, kn, cos_t, sin_t, n_ctx, t_lat):
    n_tok = x.shape[0]
    grp = functools.partial(_mod_group, n_ctx_tiles=n_ctx // TM, tiles_per_seq=t_lat // TM)
    row = lambda w: pl.BlockSpec((TM, w), lambda i: (i, 0))
    out_w = (D_MODEL, D_MODEL, D_MODEL, A_WIDTH, KV_WIDTH, KV_WIDTH, GATE_W)
    return pl.pallas_call(
        _dense1_kernel,
        grid=(n_tok // TM,),
        in_specs=[
            row(D_MODEL),
            pl.BlockSpec((1, 1, N_MOD, D_MODEL), lambda i: (layer, grp(i), 0, 0)),
            pl.BlockSpec((1, 3, D_MODEL), lambda i: (layer, 0, 0)),
            _resident((1, D_MODEL, 2 * D_FF), lambda i: (2 * layer, 0, 0)),
            _resident((1, D_FF, D_MODEL), lambda i: (2 * layer, 0, 0)),
            _resident((1, D_MODEL, Z_WIDTH), lambda i: (layer, 0, 0)),
            pl.BlockSpec((1, 1, A_WIDTH), lambda i: (layer, 0, 0)),
            pl.BlockSpec((1, 1, KV_WIDTH), lambda i: (layer, 0, 0)),
            row(KV_WIDTH),
            row(KV_WIDTH),
        ],
        out_specs=[row(w) for w in out_w],
        out_shape=[jax.ShapeDtypeStruct((n_tok, w), F32) for w in out_w],
        compiler_params=pltpu.CompilerParams(
            dimension_semantics=("arbitrary",), vmem_limit_bytes=VMEM_LIMIT),
        name="dense1",
    )(x, mod, norm_g, wfi, wfo, wz, qn, kn, cos_t, sin_t)


def _dense2_kernel(x_ref, mod_ref, ng_ref, hf_ref, hb_ref, mo_ref, mn_ref, of_ref, ob_ref, gz_ref, gn_ref,
                   ao_ref, wo_ref, wfi_ref, wfo_ref, fin_ref, *out_refs, final):
    mod = mod_ref[0, 0]
    m_out = _head_rms(hf_ref[...] + hb_ref[...], mn_ref[0]) * _sigmoid(mo_ref[...])
    g_out = _head_rms(of_ref[...] + ob_ref[...], gn_ref[0]) * _silu(gz_ref[...])
    mix = jnp.dot(_bf(m_out), wo_ref[0, 0:SCAN_W, :], preferred_element_type=F32)
    mix += jnp.dot(_bf(g_out), wo_ref[0, SCAN_W:2 * SCAN_W, :], preferred_element_type=F32)
    mix += jnp.dot(_bf(ao_ref[...]), wo_ref[0, 2 * SCAN_W:, :], preferred_element_type=F32)
    x2 = x_ref[...] + mod[5:6] * mix
    h = _bf(_rms_mod(x2, ng_ref[0, 2:3], mod[6:7], mod[7:8]))
    x3 = x2 + 0.5 * mod[8:9] * _ffn(h, wfi_ref, wfo_ref)
    if final:
        out_refs[0][...] = x3 * lax.rsqrt(jnp.mean(x3 * x3, axis=-1, keepdims=True) + EPS) * fin_ref[...]
    else:
        out_refs[0][...] = x3


def _dense2(layer, x1, mod, norm_g, hf, hb, zm, m_norm, of, ob, zg, g_norm, a_out, wo, wfi, wfo, fin,
            n_ctx, t_lat, final):
    n_tok = x1.shape[0]
    grp = functools.partial(_mod_group, n_ctx_tiles=n_ctx // TM, tiles_per_seq=t_lat // TM)
    row = lambda w: pl.BlockSpec((TM, w), lambda i: (i, 0))
    last_quarter = pl.BlockSpec((TM, SCAN_W), lambda i: (i, 3))
    lane_row = lambda w: pl.BlockSpec((1, 1, w), lambda i: (layer, 0, 0))
    return pl.pallas_call(
        functools.partial(_dense2_kernel, final=final),
        grid=(n_tok // TM,),
        in_specs=[
            row(D_MODEL),
            pl.BlockSpec((1, 1, N_MOD, D_MODEL), lambda i: (layer, grp(i), 0, 0)),
            pl.BlockSpec((1, 3, D_MODEL), lambda i: (layer, 0, 0)),
            row(SCAN_W), row(SCAN_W), last_quarter, lane_row(SCAN_W),
            row(SCAN_W), row(SCAN_W), last_quarter, lane_row(SCAN_W),
            row(A_WIDTH),
            _resident((1, D_MODEL, D_MODEL), lambda i: (layer, 0, 0)),
            _resident((1, D_MODEL, 2 * D_FF), lambda i: (2 * layer + 1, 0, 0)),
            _resident((1, D_FF, D_MODEL), lambda i: (2 * layer + 1, 0, 0)),
            pl.BlockSpec((1, D_MODEL), lambda i: (0, 0)),
        ],
        out_specs=row(D_MODEL),
        out_shape=jax.ShapeDtypeStruct((n_tok, D_MODEL), F32),
        compiler_params=pltpu.CompilerParams(
            dimension_semantics=("arbitrary",), vmem_limit_bytes=VMEM_LIMIT),
        name="dense2",
    )(x1, mod, norm_g, hf, hb, zm, m_norm, of, ob, zg, g_norm, a_out, wo, wfi, wfo, fin)


class _Segs:
    def __init__(self, n_ctx, t_ctx, n_lat_seq, t_lat):
        assert t_ctx == SEG and t_lat % SEG == 0
        self.n_ctx_seg = n_ctx // SEG
        self.per_seq = t_lat // SEG
        self.n_lat_seq = n_lat_seq
        self.n_seg = self.n_ctx_seg + n_lat_seq * self.per_seq
        self.n_seq = self.n_ctx_seg + n_lat_seq

    def is_ctx(self, s):
        return s < self.n_ctx_seg

    def lat(self, s):
        r = jnp.maximum(s - self.n_ctx_seg, 0)
        return r // self.per_seq, r % self.per_seq

    def blk(self, s, rev):
        b, j = self.lat(s)
        j = self.per_seq - 1 - j if rev else j
        return jnp.where(self.is_ctx(s), s, self.n_ctx_seg + b * self.per_seq + j)

    def seq(self, s):
        return jnp.where(self.is_ctx(s), s, self.n_ctx_seg + self.lat(s)[0])

    def lat_seq(self, s):
        return self.lat(s)[0]

    def first(self, s):
        return jnp.logical_or(self.is_ctx(s), self.lat(s)[1] == 0)

    def last(self, s):
        return jnp.logical_or(self.is_ctx(s), self.lat(s)[1] == self.per_seq - 1)


def _mlstm_segment(zm_ref, gate_ref, fb_row, out_ref, c_ref, n_ref, m_ref, rev, d):
    mask_bd = _block_ones(SCAN_W)
    ones_bd = mask_bd.astype(BF16)
    gates = gate_ref[...]
    ig = _expand_gate(gates, d * N_HEADS)
    lf = _log_sigmoid(_expand_gate(gates, 8 + d * N_HEADS) + fb_row)
    b_all = _chunk_scan(lf, jnp.add, 0.0, rev)
    r_all = ig - b_all
    cm_all = _chunk_scan(r_all, jnp.maximum, -jnp.inf, rev)
    tri = _tri(rev, strict=False)
    edge = 0 if rev else CHUNK - 1
    for c in _chunk_order(rev):
        rows = slice(c * CHUNK, (c + 1) * CHUNK)
        q = zm_ref[rows, 0:SCAN_W]
        k = zm_ref[rows, SCAN_W:2 * SCAN_W] * (1.0 / math.sqrt(HEAD_DIM))
        v = zm_ref[rows, 2 * SCAN_W:3 * SCAN_W]
        b = b_all[rows]
        r = r_all[rows]
        i_g = ig[rows]
        m_row = m_ref[...]
        n_row = n_ref[...]
        c_bd = c_ref[...]
        mx = jnp.maximum(m_row, cm_all[rows])
        w = jnp.exp(jnp.where(tri, _diag_row(r) - mx, -jnp.inf))
        a_int = jnp.exp(m_row - mx)
        s = _dot_nt(q, _bd(k, mask_bd)) * w
        num = _dot(s, _bd(v, mask_bd)) + a_int * _dot(q, c_bd)
        den = _dot_exact_rhs(s, ones_bd) + a_int * _dot_exact_rhs(q * n_row, ones_bd)
        out_ref[rows, :] = num / jnp.maximum(jnp.abs(den), jnp.exp(-(b + mx)))
        bl = b[edge:edge + 1]
        lw = bl - b + i_g
        m_new = jnp.maximum(bl + m_row, jnp.max(lw, axis=0, keepdims=True))
        kw = jnp.exp(lw - m_new) * k
        dec = jnp.exp(bl + m_row - m_new)
        c_ref[...] = c_bd * dec + jnp.where(mask_bd, _dot(kw.T, v), 0.0)
        n_ref[...] = n_row * dec + jnp.sum(kw, axis=0, keepdims=True)
        m_ref[...] = m_new


def _mlstm_kernel(zmf_ref, zmb_ref, gf_ref, gb_ref, fb_ref, c0_ref, n0_ref, m0_ref,
                  hf_ref, hb_ref, cf_ref, cb_ref, nf_ref, nb_ref, mf_ref, mb_ref,
                  c_s, n_s, m_s, *, segs):
    s = pl.program_id(0)

    @pl.when(segs.first(s))
    def _():
        ctx = segs.is_ctx(s)
        for d in range(N_DIR):
            c_s[d] = jnp.where(ctx, 0.0, c0_ref[0, d])
            n_s[d] = jnp.where(ctx, 0.0, n0_ref[0, d])
            m_s[d] = jnp.where(ctx, 0.0, m0_ref[0, d])

    _mlstm_segment(zmf_ref, gf_ref, fb_ref[0, 0:1], hf_ref, c_s.at[0], n_s.at[0], m_s.at[0], False, 0)
    _mlstm_segment(zmb_ref, gb_ref, fb_ref[0, 1:2], hb_ref, c_s.at[1], n_s.at[1], m_s.at[1], True, 1)

    @pl.when(segs.last(s))
    def _():
        cf_ref[0] = c_s[0]
        cb_ref[0] = c_s[1]
        nf_ref[0] = n_s[0]
        nb_ref[0] = n_s[1]
        mf_ref[0] = m_s[0]
        mb_ref[0] = m_s[1]


def _mlstm(layer, segs, zm, gates, fb_rows, c0, n0, m0):
    n_tok = zm.shape[0]
    fwd = lambda w: pl.BlockSpec((SEG, w), lambda s: (segs.blk(s, False), 0))
    bwd = lambda w: pl.BlockSpec((SEG, w), lambda s: (segs.blk(s, True), 0))
    st_in = lambda r: pl.BlockSpec((1, N_DIR, r, SCAN_W), lambda s: (segs.lat_seq(s), 0, 0, 0))
    st_out = lambda r: pl.BlockSpec((1, r, SCAN_W), lambda s: (segs.seq(s), 0, 0))
    st_shape = lambda r: jax.ShapeDtypeStruct((segs.n_seq, r, SCAN_W), F32)
    return pl.pallas_call(
        functools.partial(_mlstm_kernel, segs=segs),
        grid=(segs.n_seg,),
        in_specs=[
            fwd(D_MODEL), bwd(D_MODEL), fwd(GATE_W), bwd(GATE_W),
            pl.BlockSpec((1, N_DIR, SCAN_W), lambda s: (layer, 0, 0)),
            st_in(SCAN_W), st_in(1), st_in(1),
        ],
        out_specs=[fwd(SCAN_W), bwd(SCAN_W),
                   st_out(SCAN_W), st_out(SCAN_W), st_out(1), st_out(1), st_out(1), st_out(1)],
        out_shape=[jax.ShapeDtypeStruct((n_tok, SCAN_W), F32)] * 2
        + [st_shape(SCAN_W)] * 2 + [st_shape(1)] * 4,
        scratch_shapes=[pltpu.VMEM((N_DIR, SCAN_W, SCAN_W), F32),
                        pltpu.VMEM((N_DIR, 1, SCAN_W), F32),
                        pltpu.VMEM((N_DIR, 1, SCAN_W), F32)],
        compiler_params=pltpu.CompilerParams(
            dimension_semantics=("arbitrary",), vmem_limit_bytes=VMEM_LIMIT),
        name="mlstm",
    )(zm, zm, gates, gates, fb_rows, c0, n0, m0)


def _short_conv(prev_ref, cur_ref, next_ref, w_ref, has_prev, has_next):
    w3 = 3 * SCAN_W
    prev = jnp.where(has_prev, prev_ref[:, 0:w3], 0.0)
    nxt = jnp.where(has_next, next_ref[:, 0:w3], 0.0)
    xp = jnp.concatenate([prev, cur_ref[:, 0:w3], nxt], axis=0)
    rows = xp.shape[0]
    acc = None
    for i in range(CONV_W):
        shift = (CONV_W // 2 - i) % rows
        y = xp if shift == 0 else pltpu.roll(xp, shift, 0)
        t = y[HALO:HALO + SEG] * w_ref[0, i:i + 1, :]
        acc = t if acc is None else acc + t
    return acc


def _neumann_inverse(n_all, mask_bd):
    eye = (_iota(n_all.shape, 0) == _in_head(_iota(n_all.shape, 1))).astype(F32)
    p = -n_all
    t = eye + p
    for _ in range(5):
        p = _dot3(p, _bd(p, mask_bd))
        t = t + _dot3(t, _bd(p, mask_bd))
    return t


def _delta_segment(prev_ref, cur_ref, next_ref, gate_ref, cw_ref, alog_row, dtb_row, has_prev, has_next,
                   out_ref, s_ref, rev, d):
    mask_bd = _block_ones(SCAN_W)
    qkv = _silu(_short_conv(prev_ref, cur_ref, next_ref, cw_ref, has_prev, has_next))

    def l2(a):
        return a * lax.rsqrt(_head_sum(a * a) + EPS)

    q_all = l2(qkv[:, 0:SCAN_W]) * (1.0 / math.sqrt(HEAD_DIM))
    k_all = l2(qkv[:, SCAN_W:2 * SCAN_W])
    v_all = qkv[:, 2 * SCAN_W:3 * SCAN_W]
    gates = gate_ref[...]
    g_all = -jnp.exp(alog_row) * _softplus(_expand_gate(gates, 16 + d * N_HEADS) + dtb_row)
    beta_all = _sigmoid(_expand_gate(gates, 24 + d * N_HEADS))
    gc_all = _chunk_scan(g_all, jnp.add, 0.0, rev)
    tri = _tri(rev, strict=False)
    strict = _tri(rev, strict=True)
    edge = 0 if rev else CHUNK - 1
    for c in _chunk_order(rev):
        rows = slice(c * CHUNK, (c + 1) * CHUNK)
        q, k, v = q_all[rows], k_all[rows], v_all[rows]
        beta, gc = beta_all[rows], gc_all[rows]
        decay = jnp.exp(jnp.where(tri, gc + _diag_row(-gc), -jnp.inf))
        eg = jnp.exp(gc)
        kb = k * beta
        k_bd = _bd(k, mask_bd)
        kk = _dot_nt(kb, k_bd)
        qk = _dot_nt(q, k_bd) * decay
        t_all = _neumann_inverse(jnp.where(strict, kk * decay, 0.0), mask_bd)
        u = _dot(t_all, _bd(v * beta, mask_bd))
        w = _dot(t_all, _bd(kb * eg, mask_bd))
        gl = gc[edge:edge + 1]
        k_dec = k * jnp.exp(gl - gc)
        s_bd = s_ref[...]
        v_new = u - _dot(w, s_bd)
        out_ref[rows, :] = _dot(q * eg, s_bd) + _dot(qk, _bd(v_new, mask_bd))
        s_ref[...] = s_bd * jnp.exp(gl) + jnp.where(mask_bd, _dot(k_dec.T, v_new), 0.0)


def _delta_kernel(pf_ref, cf_ref, nf_ref, pb_ref, cb_ref, nb_ref, gf_ref, gb_ref, cw_ref, al_ref, dt_ref,
                  s0_ref, of_ref, ob_ref, sf_ref, sb_ref, s_s, *, segs):
    s = pl.program_id(0)
    lat = jnp.logical_not(segs.is_ctx(s))
    j = segs.lat(s)[1]

    @pl.when(segs.first(s))
    def _():
        for d in range(N_DIR):
            s_s[d] = jnp.where(lat, s0_ref[0, d], 0.0)

    inner_lo = jnp.logical_and(lat, j > 0)
    inner_hi = jnp.logical_and(lat, j < segs.per_seq - 1)
    _delta_segment(pf_ref, cf_ref, nf_ref, gf_ref, cw_ref, al_ref[0, 0:1], dt_ref[0, 0:1],
                   inner_lo, inner_hi, of_ref, s_s.at[0], False, 0)
    _delta_segment(pb_ref, cb_ref, nb_ref, gb_ref, cw_ref, al_ref[0, 1:2], dt_ref[0, 1:2],
                   inner_hi, inner_lo, ob_ref, s_s.at[1], True, 1)

    @pl.when(segs.last(s))
    def _():
        sf_ref[0] = s_s[0]
        sb_ref[0] = s_s[1]


def _delta(layer, segs, zg, gates, conv_w, alog_rows, dtb_rows, s0):
    n_tok = zg.shape[0]
    per = SEG // HALO
    n_halo = n_tok // HALO

    def cur(rev, w):
        return pl.BlockSpec((SEG, w), lambda s: (segs.blk(s, rev), 0))

    def prev(rev):
        return pl.BlockSpec((HALO, D_MODEL), lambda s: (jnp.maximum(segs.blk(s, rev) * per - 1, 0), 0))

    def nxt(rev):
        return pl.BlockSpec((HALO, D_MODEL),
                            lambda s: (jnp.minimum((segs.blk(s, rev) + 1) * per, n_halo - 1), 0))

    lane_rows = pl.BlockSpec((1, N_DIR, SCAN_W), lambda s: (layer, 0, 0))
    st_out = pl.BlockSpec((1, SCAN_W, SCAN_W), lambda s: (segs.seq(s), 0, 0))
    return pl.pallas_call(
        functools.partial(_delta_kernel, segs=segs),
        grid=(segs.n_seg,),
        in_specs=[
            prev(False), cur(False, D_MODEL), nxt(False),
            prev(True), cur(True, D_MODEL), nxt(True),
            cur(False, GATE_W), cur(True, GATE_W),
            pl.BlockSpec((1, CONV_W, 3 * SCAN_W), lambda s: (layer, 0, 0)),
            lane_rows, lane_rows,
            pl.BlockSpec((1, N_DIR, SCAN_W, SCAN_W), lambda s: (segs.lat_seq(s), 0, 0, 0)),
        ],
        out_specs=[cur(False, SCAN_W), cur(True, SCAN_W), st_out, st_out],
        out_shape=[jax.ShapeDtypeStruct((n_tok, SCAN_W), F32)] * 2
        + [jax.ShapeDtypeStruct((segs.n_seq, SCAN_W, SCAN_W), F32)] * 2,
        scratch_shapes=[pltpu.VMEM((N_DIR, SCAN_W, SCAN_W), F32)],
        compiler_params=pltpu.CompilerParams(
            dimension_semantics=("arbitrary",), vmem_limit_bytes=VMEM_LIMIT),
        name="delta",
    )(zg, zg, zg, zg, zg, zg, gates, gates, conv_w, alog_rows, dtb_rows, s0)


def _attend(q_ref, kv_pairs, o_ref):
    lane = _iota((SEG, KV_WIDTH), 1)
    group = A_HEADS // A_KV_HEADS
    for col in range(A_WIDTH // KV_WIDTH):
        x = q_ref[:, col * KV_WIDTH:(col + 1) * KV_WIDTH]
        outs = []
        for e in range(2):
            kvh = (2 * col + e) // group
            xe = x if e == kvh else pltpu.roll(x, HEAD_DIM, 1)
            qh = jnp.where(_head_of(lane) == kvh, xe, 0.0)
            scores = [_dot_nt(qh, k) for k, _ in kv_pairs]
            m = functools.reduce(jnp.maximum, [jnp.max(sc, axis=-1, keepdims=True) for sc in scores])
            ps = [jnp.exp(sc - m) for sc in scores]
            den = functools.reduce(jnp.add, [jnp.sum(p, axis=-1, keepdims=True) for p in ps])
            ps = [p / den for p in ps]
            o = functools.reduce(jnp.add, [_dot(p, v) for p, (_, v) in zip(ps, kv_pairs)])
            o = jnp.where(_head_of(lane) == kvh, o, 0.0)
            outs.append(o if e == kvh else pltpu.roll(o, HEAD_DIM, 1))
        o_ref[:, col * KV_WIDTH:(col + 1) * KV_WIDTH] = outs[0] + outs[1]


def _attn_kernel(q_ref, kc_ref, vc_ref, kl_ref, vl_ref, ck_ref, cv_ref, o_ref, *, segs):
    s = pl.program_id(0)

    @pl.when(segs.is_ctx(s))
    def _():
        _attend(q_ref, [(kc_ref[...], vc_ref[...])], o_ref)

    @pl.when(jnp.logical_not(segs.is_ctx(s)))
    def _():
        _attend(q_ref, [(ck_ref[0, 0], cv_ref[0, 0]), (kl_ref[...], vl_ref[...])], o_ref)


def _attention(layer, segs, aq, ak, av, cache_k, cache_v):
    n_tok = aq.shape[0]
    t_lat = segs.per_seq * SEG
    past = cache_k.shape[2]
    lat_blocks_before = segs.n_ctx_seg * SEG // t_lat
    own = lambda w: pl.BlockSpec((SEG, w), lambda s: (s, 0))
    lat_kv = pl.BlockSpec((t_lat, KV_WIDTH), lambda s: (lat_blocks_before + segs.lat_seq(s), 0))
    cache = pl.BlockSpec((1, 1, past, KV_WIDTH), lambda s: (segs.lat_seq(s), layer, 0, 0))
    return pl.pallas_call(
        functools.partial(_attn_kernel, segs=segs),
        grid=(segs.n_seg,),
        in_specs=[own(A_WIDTH), own(KV_WIDTH), own(KV_WIDTH), lat_kv, lat_kv, cache, cache],
        out_specs=own(A_WIDTH),
        out_shape=jax.ShapeDtypeStruct((n_tok, A_WIDTH), F32),
        compiler_params=pltpu.CompilerParams(
            dimension_semantics=("arbitrary",), vmem_limit_bytes=VMEM_LIMIT),
        name="attention",
    )(aq, ak, av, ak, av, cache_k, cache_v)


def _permute_w_in(w_in):
    m, g, a, kv = SCAN_W, SCAN_W, A_WIDTH, KV_WIDTH
    sizes = (m, m, m, m, 8, 8, 3 * g, g, 8, 8, a, kv, kv)
    offs = [0]
    for sz in sizes:
        offs.append(offs[-1] + sz)
    piece = lambda i: w_in[..., offs[i]:offs[i + 1]]
    pad = jnp.zeros(w_in.shape[:-1] + (GATE_W - 32,), w_in.dtype)
    order = [0, 1, 2, 3, 6, 7, 10, 11, 12, 4, 5, 8, 9]
    return jnp.concatenate([piece(i) for i in order] + [pad], axis=-1)


def _rope_tables(n_ctx, n_lat_seq, t_lat):
    rows = t_lat // GRID_W
    row = jnp.repeat(jnp.arange(rows, dtype=F32), GRID_W)
    col = jnp.tile(jnp.arange(GRID_W, dtype=F32), rows)
    n_freq = HEAD_DIM // 4
    inv = ROPE_BASE ** (-jnp.arange(n_freq, dtype=F32) / n_freq)
    ang = jnp.stack([row[:, None] * inv, col[:, None] * inv], axis=1)
    cos, sin = jnp.cos(ang), jnp.sin(ang)
    cos_h = jnp.concatenate([cos, cos], axis=-1).reshape(t_lat, HEAD_DIM)
    sin_h = jnp.concatenate([-sin, sin], axis=-1).reshape(t_lat, HEAD_DIM)
    lat = lambda a: jnp.tile(a, (n_lat_seq, KV_WIDTH // HEAD_DIM))
    cos_t = jnp.concatenate([jnp.ones((n_ctx, KV_WIDTH), F32), lat(cos_h)], axis=0)
    sin_t = jnp.concatenate([jnp.zeros((n_ctx, KV_WIDTH), F32), lat(sin_h)], axis=0)
    return cos_t, sin_t


def _block_diag(x):
    eye = jnp.eye(N_HEADS, dtype=x.dtype)
    y = x[..., :, :, None, :] * eye[:, None, :, None]
    return y.reshape(x.shape[:-3] + (SCAN_W, SCAN_W))


def _diag_blocks(x):
    y = x.reshape(x.shape[:-2] + (N_HEADS, HEAD_DIM, N_HEADS, HEAD_DIM))
    return jnp.stack([y[..., h, :, h, :] for h in range(N_HEADS)], axis=-3)


def _head_lanes(x):
    return jnp.repeat(x, HEAD_DIM, axis=-1)


def kernel(x_prompt, x_sample, c, cache_k, cache_v, state_mlstm_C, state_mlstm_n, state_mlstm_m,
           state_delta_S, c_ctx, ada_w, ada_b, norm_g, ffn_w_in, ffn_w_out, w_in, w_out, mlstm_f_bias,
           mlstm_norm, delta_conv, delta_a_log, delta_dt_bias, delta_norm, attn_q_norm, attn_k_norm,
           final_norm):
    batch, t_ctx, d_model = x_prompt.shape
    n_lat_seq, t_lat, _ = x_sample.shape
    depth = ada_w.shape[0]
    assert d_model == D_MODEL and norm_g.shape[1] == 3 and ffn_w_in.shape[-1] == 2 * D_FF
    n_ctx = batch * t_ctx
    n_tok = n_ctx + n_lat_seq * t_lat
    assert n_ctx % TM == 0 and t_lat % TM == 0 and n_ctx % t_lat == 0
    segs = _Segs(n_ctx, t_ctx, n_lat_seq, t_lat)

    cond = jnp.concatenate([c_ctx[None, :], c, jnp.zeros((8 - 1 - n_lat_seq, D_MODEL), F32)], axis=0)
    mod = _modulation(cond, ada_w, ada_b).reshape(depth, 8, N_MOD, D_MODEL)

    wfi = _bf(ffn_w_in).reshape(depth * 2, D_MODEL, 2 * D_FF)
    wfo = _bf(ffn_w_out).reshape(depth * 2, D_FF, D_MODEL)
    wz = _bf(_permute_w_in(w_in))
    wo = _bf(w_out)
    qn = jnp.tile(attn_q_norm, (1, A_HEADS)).reshape(depth, 1, A_WIDTH)
    kn = jnp.tile(attn_k_norm, (1, A_KV_HEADS)).reshape(depth, 1, KV_WIDTH)
    gn = jnp.tile(delta_norm, (1, N_HEADS)).reshape(depth, 1, SCAN_W)
    mn = mlstm_norm.reshape(depth, 1, SCAN_W)
    fin = final_norm.reshape(1, D_MODEL)
    fb_rows = _head_lanes(mlstm_f_bias)
    alog_rows = _head_lanes(delta_a_log)
    dtb_rows = _head_lanes(delta_dt_bias)
    cos_t, sin_t = _rope_tables(n_ctx, n_lat_seq, t_lat)
    ck = cache_k.reshape(cache_k.shape[:3] + (KV_WIDTH,))
    cv = cache_v.reshape(cache_v.shape[:3] + (KV_WIDTH,))

    x = jnp.concatenate([x_prompt.reshape(n_ctx, D_MODEL), x_sample.reshape(n_lat_seq * t_lat, D_MODEL)], axis=0)
    ks, vs, cs, ns, ms, ss = [], [], [], [], [], []
    for l in range(depth):
        x1, zm, zg, aq, ak, av, gates = _dense1(l, x, mod, norm_g, wfi, wfo, wz, qn, kn, cos_t, sin_t,
                                                n_ctx, t_lat)
        c0 = _block_diag(state_mlstm_C[:, l])
        n0 = state_mlstm_n[:, l].reshape(n_lat_seq, N_DIR, 1, SCAN_W)
        m0 = _head_lanes(state_mlstm_m[:, l]).reshape(n_lat_seq, N_DIR, 1, SCAN_W)
        hf, hb, c_f, c_b, n_f, n_b, m_f, m_b = _mlstm(l, segs, zm, gates, fb_rows, c0, n0, m0)
        s0 = _block_diag(state_delta_S[:, l])
        o_f, o_b, s_f, s_b = _delta(l, segs, zg, gates, delta_conv, alog_rows, dtb_rows, s0)
        a_out = _attention(l, segs, aq, ak, av, ck, cv)
        x = _dense2(l, x1, mod, norm_g, hf, hb, zm, mn, o_f, o_b, zg, gn, a_out, wo, wfi, wfo, fin,
                    n_ctx, t_lat, final=(l == depth - 1))

        both = lambda f, b: jnp.stack([f[:batch], b[:batch]], axis=1)
        ks.append(ak[:n_ctx].reshape(batch, t_ctx, A_KV_HEADS, HEAD_DIM))
        vs.append(av[:n_ctx].reshape(batch, t_ctx, A_KV_HEADS, HEAD_DIM))
        cs.append(_diag_blocks(both(c_f, c_b)))
        ns.append(both(n_f, n_b).reshape(batch, N_DIR, N_HEADS, HEAD_DIM))
        ms.append(both(m_f, m_b).reshape(batch, N_DIR, N_HEADS, HEAD_DIM)[..., 0])
        ss.append(_diag_blocks(both(s_f, s_b)))

    y_prompt = x[:n_ctx].reshape(batch, t_ctx, D_MODEL)
    y_sample = x[n_ctx:].reshape(n_lat_seq, t_lat, D_MODEL)
    stack = lambda xs: jnp.stack(xs, axis=1)
    return (y_prompt, y_sample, stack(ks), stack(vs), stack(cs), stack(ns), stack(ms), stack(ss))
```

```python
import functools
import math

import jax
import jax.numpy as jnp
from jax import lax
from jax.experimental import pallas as pl
from jax.experimental.pallas import tpu as pltpu

F32 = jnp.float32
BF16 = jnp.bfloat16

D_MODEL = 1024
HEAD_DIM = 64
N_HEADS = 4
SCAN_W = N_HEADS * HEAD_DIM
A_HEADS = 8
A_KV_HEADS = 2
A_WIDTH = A_HEADS * HEAD_DIM
KV_WIDTH = A_KV_HEADS * HEAD_DIM
N_DIR = 2
CHUNK = 64
CONV_W = 5
D_FF = 2816
FF_CHUNK = 256
GRID_W = 64
ROPE_BASE = 10000.0
EPS = 1e-6
N_MOD = 9
GATE_W = 128
Z_WIDTH = 2 * D_MODEL + A_WIDTH + 2 * KV_WIDTH + GATE_W

SEG = 256
TM = 512
HALO = 8
VMEM_LIMIT = 56 * 1024 * 1024


def _bf(x):
    return x.astype(BF16)


def _dot(a, b):
    return jnp.dot(_bf(a), _bf(b), preferred_element_type=F32)


def _dot_nt(a, b):
    return lax.dot_general(_bf(a), _bf(b), (((1,), (1,)), ((), ())), preferred_element_type=F32)


def _split(a, terms):
    parts = []
    rest = a
    for _ in range(terms):
        p = _bf(rest)
        parts.append(p)
        rest = rest - p.astype(F32)
    return parts


def _dot_exact_rhs(a, b01, terms=2):
    out = None
    for p in _split(a, terms):
        d = jnp.dot(p, b01, preferred_element_type=F32)
        out = d if out is None else out + d
    return out


def _dot3(a, b):
    a_hi, a_lo = _split(a, 2)
    b_hi, b_lo = _split(b, 2)
    out = jnp.dot(a_hi, b_hi, preferred_element_type=F32)
    out += jnp.dot(a_lo, b_hi, preferred_element_type=F32)
    return out + jnp.dot(a_hi, b_lo, preferred_element_type=F32)


def _iota(shape, dim):
    return lax.broadcasted_iota(jnp.int32, shape, dim)


def _head_of(idx):
    return jnp.right_shift(idx, 6)


def _in_head(idx):
    return jnp.bitwise_and(idx, HEAD_DIM - 1)


def _block_ones(n):
    return (_head_of(_iota((n, n), 0)) == _head_of(_iota((n, n), 1)))


def _head_sum(x):
    n = x.shape[-1]
    return _dot_exact_rhs(x, _block_ones(n).astype(BF16), terms=2)


def _head_rms(x, g_row):
    ms = _head_sum(x * x) * (1.0 / HEAD_DIM)
    return x * lax.rsqrt(ms + EPS) * g_row


def _rms_mod(x, g_row, scale_row, shift_row):
    y = x * lax.rsqrt(jnp.mean(x * x, axis=-1, keepdims=True) + EPS) * g_row
    return y * (1.0 + scale_row) + shift_row


def _softplus(x):
    return jnp.maximum(x, 0.0) + jnp.log1p(jnp.exp(-jnp.abs(x)))


def _log_sigmoid(x):
    return -_softplus(-x)


def _sigmoid(x):
    return jax.nn.sigmoid(x)


def _silu(x):
    return x * jax.nn.sigmoid(x)


def _bd(x_all, mask_bd):
    rep = jnp.concatenate([x_all] * N_HEADS, axis=0)
    return jnp.where(mask_bd, rep, 0.0)


def _expand_gate(gates, col0):
    sel = (_iota((GATE_W, SCAN_W), 0) == col0 + _head_of(_iota((GATE_W, SCAN_W), 1))).astype(BF16)
    return _dot_exact_rhs(gates, sel, terms=3)


def _chunk_scan(x, op, ident, rev):
    rows = x.shape[0]
    pos = _in_head(_iota(x.shape, 0))
    s = 1
    while s < CHUNK:
        if rev:
            y = pltpu.roll(x, rows - s, 0)
            ok = pos < CHUNK - s
        else:
            y = pltpu.roll(x, s, 0)
            ok = pos >= s
        x = op(x, jnp.where(ok, y, ident))
        s *= 2
    return x


def _diag_row(x):
    eye = _iota(x.shape, 0) == _in_head(_iota(x.shape, 1))
    return jnp.sum(jnp.where(eye, x, 0.0), axis=0, keepdims=True)


def _tri(rev, strict):
    t = _iota((CHUNK, SCAN_W), 0)
    j = _in_head(_iota((CHUNK, SCAN_W), 1))
    if rev:
        return (j > t) if strict else (j >= t)
    return (j < t) if strict else (j <= t)


def _chunk_order(rev):
    n = SEG // CHUNK
    return range(n - 1, -1, -1) if rev else range(n)


def _mod_kernel(cond_ref, w_ref, b_ref, o_ref):
    a = _silu(cond_ref[...])
    o_ref[0] = _dot(a, w_ref[0]) + b_ref[0]


def _modulation(cond, ada_w, ada_b):
    depth, _, width = ada_w.shape
    tn = D_MODEL
    rows = cond.shape[0]
    return pl.pallas_call(
        _mod_kernel,
        grid=(depth, width // tn),
        in_specs=[
            pl.BlockSpec((rows, D_MODEL), lambda l, j: (0, 0)),
            pl.BlockSpec((1, D_MODEL, tn), lambda l, j: (l, 0, j)),
            pl.BlockSpec((1, 1, tn), lambda l, j: (l, 0, j)),
        ],
        out_specs=pl.BlockSpec((1, rows, tn), lambda l, j: (l, 0, j)),
        out_shape=jax.ShapeDtypeStruct((depth, rows, width), F32),
        compiler_params=pltpu.CompilerParams(
            dimension_semantics=("arbitrary", "arbitrary"), vmem_limit_bytes=VMEM_LIMIT),
        name="modulation",
    )(cond, ada_w, ada_b.reshape(depth, 1, width))


def _ffn(h, w_in_ref, w_out_ref):
    acc = None
    for c in range(D_FF // FF_CHUNK):
        lo = c * FF_CHUNK
        g = jnp.dot(h, w_in_ref[0, :, lo:lo + FF_CHUNK], preferred_element_type=F32)
        u = jnp.dot(h, w_in_ref[0, :, D_FF + lo:D_FF + lo + FF_CHUNK], preferred_element_type=F32)
        a = _bf(_silu(g) * u)
        d = jnp.dot(a, w_out_ref[0, lo:lo + FF_CHUNK, :], preferred_element_type=F32)
        acc = d if acc is None else acc + d
    return acc


def _rope(x, cos, sin_signed):
    w = x.shape[-1]
    reps = w // cos.shape[-1]
    if reps > 1:
        cos = jnp.concatenate([cos] * reps, axis=-1)
        sin_signed = jnp.concatenate([sin_signed] * reps, axis=-1)
    first = jnp.bitwise_and(_iota(x.shape, 1), 31) < 16
    partner = jnp.where(first, pltpu.roll(x, w - 16, 1), pltpu.roll(x, 16, 1))
    return x * cos + partner * sin_signed


def _dense1_kernel(x_ref, mod_ref, ng_ref, wfi_ref, wfo_ref, wz_ref, qn_ref, kn_ref, cos_ref, sin_ref,
                   x1_ref, zm_ref, zg_ref, aq_ref, ak_ref, av_ref, gate_ref):
    x = x_ref[...]
    mod = mod_ref[0, 0]
    h = _bf(_rms_mod(x, ng_ref[0, 0:1], mod[0:1], mod[1:2]))
    x1 = x + 0.5 * mod[2:3] * _ffn(h, wfi_ref, wfo_ref)
    x1_ref[...] = x1
    h2 = _bf(_rms_mod(x1, ng_ref[0, 1:2], mod[3:4], mod[4:5]))

    def proj(lo, width):
        return jnp.dot(h2, wz_ref[0, :, lo:lo + width], preferred_element_type=F32)

    zm_ref[...] = proj(0, D_MODEL)
    zg_ref[...] = proj(D_MODEL, D_MODEL)
    cos = cos_ref[...]
    sin = sin_ref[...]
    off = 2 * D_MODEL
    q = _head_rms(proj(off, A_WIDTH), qn_ref[0])
    aq_ref[...] = _rope(q, cos, sin) * (1.0 / math.sqrt(HEAD_DIM))
    k = _head_rms(proj(off + A_WIDTH, KV_WIDTH), kn_ref[0])
    ak_ref[...] = _rope(k, cos, sin)
    av_ref[...] = proj(off + A_WIDTH + KV_WIDTH, KV_WIDTH)
    gate_ref[...] = proj(off + A_WIDTH + 2 * KV_WIDTH, GATE_W)


def _resident(shape, index_map):
    return pl.BlockSpec(shape, index_map, pipeline_mode=pl.Buffered(1))


def _mod_group(i, n_ctx_tiles, tiles_per_seq):
    return jnp.where(i < n_ctx_tiles, 0, 1 + jnp.maximum(i - n_ctx_tiles, 0) // tiles_per_seq)


def _dense1(layer, x, mod, norm_g, wfi, wfo, wz, qn, kn, cos_t, sin_t, n_ctx, t_lat):
    n_tok = x.shape[0]
    grp = functools.partial(_mod_group, n_ctx_tiles=n_ctx // TM, tiles_per_seq=t_lat // TM)
    row = lambda w: pl.BlockSpec((TM, w), lambda i: (i, 0))
    out_w = (D_MODEL, D_MODEL, D_MODEL, A_WIDTH, KV_WIDTH, KV_WIDTH, GATE_W)
    return pl.pallas_call(
        _dense1_kernel,
        grid=(n_tok // TM,),
        in_specs=[
            row(D_MODEL),
            pl.BlockSpec((1, 1, N_MOD, D_MODEL), lambda i: (layer, grp(i), 0, 0)),
            pl.BlockSpec((1, 3, D_MODEL), lambda i: (layer, 0, 0)),
            _resident((1, D_MODEL, 2 * D_FF), lambda i: (2 * layer, 0, 0)),
            _resident((1, D_FF, D_MODEL), lambda i: (2 * layer, 0, 0)),
            _resident((1, D_MODEL, Z_WIDTH), lambda i: (layer, 0, 0)),
            pl.BlockSpec((1, 1, A_WIDTH), lambda i: (layer, 0, 0)),
            pl.BlockSpec((1, 1, KV_WIDTH), lambda i: (layer, 0, 0)),
            row(KV_WIDTH),
            row(KV_WIDTH),
        ],
        out_specs=[row(w) for w in out_w],
        out_shape=[jax.ShapeDtypeStruct((n_tok, w), F32) for w in out_w],
        compiler_params=pltpu.CompilerParams(
            dimension_semantics=("arbitrary",), vmem_limit_bytes=VMEM_LIMIT),
        name="dense1",
    )(x, mod, norm_g, wfi, wfo, wz, qn, kn, cos_t, sin_t)


def _dense2_kernel(x_ref, mod_ref, ng_ref, hf_ref, hb_ref, mo_ref, mn_ref, of_ref, ob_ref, gz_ref, gn_ref,
                   ao_ref, wo_ref, wfi_ref, wfo_ref, fin_ref, *out_refs, final):
    mod = mod_ref[0, 0]
    m_out = _head_rms(hf_ref[...] + hb_ref[...], mn_ref[0]) * _sigmoid(mo_ref[...])
    g_out = _head_rms(of_ref[...] + ob_ref[...], gn_ref[0]) * _silu(gz_ref[...])
    mix = jnp.dot(_bf(m_out), wo_ref[0, 0:SCAN_W, :], preferred_element_type=F32)
    mix += jnp.dot(_bf(g_out), wo_ref[0, SCAN_W:2 * SCAN_W, :], preferred_element_type=F32)
    mix += jnp.dot(_bf(ao_ref[...]), wo_ref[0, 2 * SCAN_W:, :], preferred_element_type=F32)
    x2 = x_ref[...] + mod[5:6] * mix
    h = _bf(_rms_mod(x2, ng_ref[0, 2:3], mod[6:7], mod[7:8]))
    x3 = x2 + 0.5 * mod[8:9] * _ffn(h, wfi_ref, wfo_ref)
    if final:
        out_refs[0][...] = x3 * lax.rsqrt(jnp.mean(x3 * x3, axis=-1, keepdims=True) + EPS) * fin_ref[...]
    else:
        out_refs[0][...] = x3


def _dense2(layer, x1, mod, norm_g, hf, hb, zm, m_norm, of, ob, zg, g_norm, a_out, wo, wfi, wfo, fin,
            n_ctx, t_lat, final):
    n_tok = x1.shape[0]
    grp = functools.partial(_mod_group, n_ctx_tiles=n_ctx // TM, tiles_per_seq=t_lat // TM)
    row = lambda w: pl.BlockSpec((TM, w), lambda i: (i, 0))
    last_quarter = pl.BlockSpec((TM, SCAN_W), lambda i: (i, 3))
    lane_row = lambda w: pl.BlockSpec((1, 1, w), lambda i: (layer, 0, 0))
    return pl.pallas_call(
        functools.partial(_dense2_kernel, final=final),
        grid=(n_tok // TM,),
        in_specs=[
            row(D_MODEL),
            pl.BlockSpec((1, 1, N_MOD, D_MODEL), lambda i: (layer, grp(i), 0, 0)),
            pl.BlockSpec((1, 3, D_MODEL), lambda i: (layer, 0, 0)),
            row(SCAN_W), row(SCAN_W), last_quarter, lane_row(SCAN_W),
            row(SCAN_W), row(SCAN_W), last_quarter, lane_row(SCAN_W),
            row(A_WIDTH),
            _resident((1, D_MODEL, D_MODEL), lambda i: (layer, 0, 0)),
            _resident((1, D_MODEL, 2 * D_FF), lambda i: (2 * layer + 1, 0, 0)),
            _resident((1, D_FF, D_MODEL), lambda i: (2 * layer + 1, 0, 0)),
            pl.BlockSpec((1, D_MODEL), lambda i: (0, 0)),
        ],
        out_specs=row(D_MODEL),
        out_shape=jax.ShapeDtypeStruct((n_tok, D_MODEL), F32),
        compiler_params=pltpu.CompilerParams(
            dimension_semantics=("arbitrary",), vmem_limit_bytes=VMEM_LIMIT),
        name="dense2",
    )(x1, mod, norm_g, hf, hb, zm, m_norm, of, ob, zg, g_norm, a_out, wo, wfi, wfo, fin)


class _Segs:
    def __init__(self, n_ctx, t_ctx, n_lat_seq, t_lat):
        assert t_ctx == SEG and t_lat % SEG == 0
        self.n_ctx_seg = n_ctx // SEG
        self.per_seq = t_lat // SEG
        self.n_lat_seq = n_lat_seq
        self.n_seg = self.n_ctx_seg + n_lat_seq * self.per_seq
        self.n_seq = self.n_ctx_seg + n_lat_seq

    def is_ctx(self, s):
        return s < self.n_ctx_seg

    def lat(self, s):
        r = jnp.maximum(s - self.n_ctx_seg, 0)
        return r // self.per_seq, r % self.per_seq

    def blk(self, s, rev):
        b, j = self.lat(s)
        j = self.per_seq - 1 - j if rev else j
        return jnp.where(self.is_ctx(s), s, self.n_ctx_seg + b * self.per_seq + j)

    def seq(self, s):
        return jnp.where(self.is_ctx(s), s, self.n_ctx_seg + self.lat(s)[0])

    def lat_seq(self, s):
        return self.lat(s)[0]

    def first(self, s):
        return jnp.logical_or(self.is_ctx(s), self.lat(s)[1] == 0)

    def last(self, s):
        return jnp.logical_or(self.is_ctx(s), self.lat(s)[1] == self.per_seq - 1)


def _mlstm_segment(zm_ref, gate_ref, fb_row, out_ref, c_ref, n_ref, m_ref, rev, d):
    mask_bd = _block_ones(SCAN_W)
    ones_bd = mask_bd.astype(BF16)
    gates = gate_ref[...]
    ig = _expand_gate(gates, d * N_HEADS)
    lf = _log_sigmoid(_expand_gate(gates, 8 + d * N_HEADS) + fb_row)
    b_all = _chunk_scan(lf, jnp.add, 0.0, rev)
    r_all = ig - b_all
    cm_all = _chunk_scan(r_all, jnp.maximum, -jnp.inf, rev)
    tri = _tri(rev, strict=False)
    edge = 0 if rev else CHUNK - 1
    for c in _chunk_order(rev):
        rows = slice(c * CHUNK, (c + 1) * CHUNK)
        q = zm_ref[rows, 0:SCAN_W]
        k = zm_ref[rows, SCAN_W:2 * SCAN_W] * (1.0 / math.sqrt(HEAD_DIM))
        v = zm_ref[rows, 2 * SCAN_W:3 * SCAN_W]
        b = b_all[rows]
        r = r_all[rows]
        i_g = ig[rows]
        m_row = m_ref[...]
        n_row = n_ref[...]
        c_bd = c_ref[...]
        mx = jnp.maximum(m_row, cm_all[rows])
        w = jnp.exp(jnp.where(tri, _diag_row(r) - mx, -jnp.inf))
        a_int = jnp.exp(m_row - mx)
        s = _dot_nt(q, _bd(k, mask_bd)) * w
        num = _dot(s, _bd(v, mask_bd)) + a_int * _dot(q, c_bd)
        den = _dot_exact_rhs(s, ones_bd) + a_int * _dot_exact_rhs(q * n_row, ones_bd)
        out_ref[rows, :] = num / jnp.maximum(jnp.abs(den), jnp.exp(-(b + mx)))
        bl = b[edge:edge + 1]
        lw = bl - b + i_g
        m_new = jnp.maximum(bl + m_row, jnp.max(lw, axis=0, keepdims=True))
        kw = jnp.exp(lw - m_new) * k
        dec = jnp.exp(bl + m_row - m_new)
        c_ref[...] = c_bd * dec + jnp.where(mask_bd, _dot(kw.T, v), 0.0)
        n_ref[...] = n_row * dec + jnp.sum(kw, axis=0, keepdims=True)
        m_ref[...] = m_new


def _mlstm_kernel(zmf_ref, zmb_ref, gf_ref, gb_ref, fb_ref, c0_ref, n0_ref, m0_ref,
                  hf_ref, hb_ref, cf_ref, cb_ref, nf_ref, nb_ref, mf_ref, mb_ref,
                  c_s, n_s, m_s, *, segs):
    s = pl.program_id(0)

    @pl.when(segs.first(s))
    def _():
        ctx = segs.is_ctx(s)
        for d in range(N_DIR):
            c_s[d] = jnp.where(ctx, 0.0, c0_ref[0, d])
            n_s[d] = jnp.where(ctx, 0.0, n0_ref[0, d])
            m_s[d] = jnp.where(ctx, 0.0, m0_ref[0, d])

    _mlstm_segment(zmf_ref, gf_ref, fb_ref[0, 0:1], hf_ref, c_s.at[0], n_s.at[0], m_s.at[0], False, 0)
    _mlstm_segment(zmb_ref, gb_ref, fb_ref[0, 1:2], hb_ref, c_s.at[1], n_s.at[1], m_s.at[1], True, 1)

    @pl.when(segs.last(s))
    def _():
        cf_ref[0] = c_s[0]
        cb_ref[0] = c_s[1]
        nf_ref[0] = n_s[0]
        nb_ref[0] = n_s[1]
        mf_ref[0] = m_s[0]
        mb_ref[0] = m_s[1]


def _mlstm(layer, segs, zm, gates, fb_rows, c0, n0, m0):
    n_tok = zm.shape[0]
    fwd = lambda w: pl.BlockSpec((SEG, w), lambda s: (segs.blk(s, False), 0))
    bwd = lambda w: pl.BlockSpec((SEG, w), lambda s: (segs.blk(s, True), 0))
    st_in = lambda r: pl.BlockSpec((1, N_DIR, r, SCAN_W), lambda s: (segs.lat_seq(s), 0, 0, 0))
    st_out = lambda r: pl.BlockSpec((1, r, SCAN_W), lambda s: (segs.seq(s), 0, 0))
    st_shape = lambda r: jax.ShapeDtypeStruct((segs.n_seq, r, SCAN_W), F32)
    return pl.pallas_call(
        functools.partial(_mlstm_kernel, segs=segs),
        grid=(segs.n_seg,),
        in_specs=[
            fwd(D_MODEL), bwd(D_MODEL), fwd(GATE_W), bwd(GATE_W),
            pl.BlockSpec((1, N_DIR, SCAN_W), lambda s: (layer, 0, 0)),
            st_in(SCAN_W), st_in(1), st_in(1),
        ],
        out_specs=[fwd(SCAN_W), bwd(SCAN_W),
                   st_out(SCAN_W), st_out(SCAN_W), st_out(1), st_out(1), st_out(1), st_out(1)],
        out_shape=[jax.ShapeDtypeStruct((n_tok, SCAN_W), F32)] * 2
        + [st_shape(SCAN_W)] * 2 + [st_shape(1)] * 4,
        scratch_shapes=[pltpu.VMEM((N_DIR, SCAN_W, SCAN_W), F32),
                        pltpu.VMEM((N_DIR, 1, SCAN_W), F32),
                        pltpu.VMEM((N_DIR, 1, SCAN_W), F32)],
        compiler_params=pltpu.CompilerParams(
            dimension_semantics=("arbitrary",), vmem_limit_bytes=VMEM_LIMIT),
        name="mlstm",
    )(zm, zm, gates, gates, fb_rows, c0, n0, m0)


def _short_conv(prev_ref, cur_ref, next_ref, w_ref, has_prev, has_next):
    w3 = 3 * SCAN_W
    prev = jnp.where(has_prev, prev_ref[:, 0:w3], 0.0)
    nxt = jnp.where(has_next, next_ref[:, 0:w3], 0.0)
    xp = jnp.concatenate([prev, cur_ref[:, 0:w3], nxt], axis=0)
    rows = xp.shape[0]
    acc = None
    for i in range(CONV_W):
        shift = (CONV_W // 2 - i) % rows
        y = xp if shift == 0 else pltpu.roll(xp, shift, 0)
        t = y[HALO:HALO + SEG] * w_ref[0, i:i + 1, :]
        acc = t if acc is None else acc + t
    return acc


def _bdot(a, b):
    return lax.dot_general(a, b, (((2,), (1,)), ((0,), (0,))), preferred_element_type=F32)


def _bdot_nt(a, b):
    return lax.dot_general(a, b, (((2,), (2,)), ((0,), (0,))), preferred_element_type=F32)


def _bd3(x, mask_bf):
    return jnp.concatenate([x] * N_HEADS, axis=1) * mask_bf


def _neumann_inverse(n_all, mask_bf):
    eye = (_iota(n_all.shape[1:], 0) == _in_head(_iota(n_all.shape[1:], 1))).astype(F32)
    p = -n_all
    t = eye + p
    levels = 6
    for lvl in range(levels):
        first, last = lvl == 0, lvl == levels - 1
        p_hi, p_lo = _split(p, 2)
        w_hi, w_lo = _bd3(p_hi, mask_bf), _bd3(p_lo, mask_bf)
        lhs_hi, lhs_lo = [], []
        if not first:
            t_hi, t_lo = _split(t, 2)
            lhs_hi += [t_hi, t_lo]
            lhs_lo += [t_hi]
        if not last:
            lhs_hi += [p_hi, p_lo]
            lhs_lo += [p_hi]
        a = _bdot(jnp.concatenate(lhs_hi, axis=1), w_hi)
        b = _bdot(jnp.concatenate(lhs_lo, axis=1) if len(lhs_lo) > 1 else lhs_lo[0], w_lo)
        ra, rb = 0, 0
        if not first:
            t = t + (a[:, 0:CHUNK] + a[:, CHUNK:2 * CHUNK] + b[:, 0:CHUNK])
            ra, rb = 2 * CHUNK, CHUNK
        if not last:
            p = a[:, ra:ra + CHUNK] + a[:, ra + CHUNK:ra + 2 * CHUNK] + b[:, rb:rb + CHUNK]
    return t


def _delta_prep(prev_ref, cur_ref, next_ref, gate_ref, cw_ref, alog_row, dtb_row, has_prev, has_next, rev, d):
    qkv = _silu(_short_conv(prev_ref, cur_ref, next_ref, cw_ref, has_prev, has_next))

    def l2(a):
        return a * lax.rsqrt(_head_sum(a * a) + EPS)

    q = l2(qkv[:, 0:SCAN_W]) * (1.0 / math.sqrt(HEAD_DIM))
    k = l2(qkv[:, SCAN_W:2 * SCAN_W])
    v = qkv[:, 2 * SCAN_W:3 * SCAN_W]
    gates = gate_ref[...]
    g = -jnp.exp(alog_row) * _softplus(_expand_gate(gates, 16 + d * N_HEADS) + dtb_row)
    beta = _sigmoid(_expand_gate(gates, 24 + d * N_HEADS))
    gc = _chunk_scan(g, jnp.add, 0.0, rev)
    stack = lambda a: a.reshape(SEG // CHUNK, CHUNK, SCAN_W)
    edge = 0 if rev else CHUNK - 1
    gc = stack(gc)
    gl = gc[:, edge:edge + 1, :]
    eg = jnp.exp(gc)
    kb = stack(k * beta)
    return dict(q=stack(q), k=stack(k), kb=kb, vb=stack(v * beta), kbe=kb * eg, qd=stack(q) * eg,
                kd=stack(k) * jnp.exp(gl - gc), gc=gc, gl=jnp.exp(gl))


def _delta_kernel(pf_ref, cf_ref, nf_ref, pb_ref, cb_ref, nb_ref, gf_ref, gb_ref, cw_ref, al_ref, dt_ref,
                  s0_ref, of_ref, ob_ref, sf_ref, sb_ref, s_s, *, segs):
    s = pl.program_id(0)
    lat = jnp.logical_not(segs.is_ctx(s))
    j = segs.lat(s)[1]

    @pl.when(segs.first(s))
    def _():
        for d in range(N_DIR):
            s_s[d] = jnp.where(lat, s0_ref[0, d], 0.0)

    inner_lo = jnp.logical_and(lat, j > 0)
    inner_hi = jnp.logical_and(lat, j < segs.per_seq - 1)
    fwd = _delta_prep(pf_ref, cf_ref, nf_ref, gf_ref, cw_ref, al_ref[0, 0:1], dt_ref[0, 0:1],
                      inner_lo, inner_hi, False, 0)
    bwd = _delta_prep(pb_ref, cb_ref, nb_ref, gb_ref, cw_ref, al_ref[0, 1:2], dt_ref[0, 1:2],
                      inner_hi, inner_lo, True, 1)
    z = {name: jnp.concatenate([fwd[name], bwd[name]], axis=0) for name in fwd}

    n_chunk = SEG // CHUNK
    mask_bd = _block_ones(SCAN_W)
    mask_bf = mask_bd.astype(BF16)
    shape = (N_DIR * n_chunk, CHUNK, SCAN_W)
    is_rev = _iota(shape, 0) >= n_chunk
    t_idx, j_idx = _iota(shape, 1), _in_head(_iota(shape, 2))
    ahead = jnp.where(is_rev, j_idx - t_idx, t_idx - j_idx)
    tri = ahead >= 0
    strict = ahead > 0
    neg_diag = jnp.sum(jnp.where(t_idx == j_idx, -z["gc"], 0.0), axis=1, keepdims=True)
    decay = jnp.exp(jnp.where(tri, z["gc"] + neg_diag, -jnp.inf))
    k_bd = _bd3(_bf(z["k"]), mask_bf)
    kq = _bdot_nt(_bf(jnp.concatenate([z["kb"], z["q"]], axis=1)), k_bd)
    qk = _bf(kq[:, CHUNK:] * decay)
    t_all = _bf(_neumann_inverse(jnp.where(strict, kq[:, 0:CHUNK] * decay, 0.0), mask_bf))
    u = _bdot(t_all, _bd3(_bf(z["vb"]), mask_bf))
    w = _bdot(t_all, _bd3(_bf(z["kbe"]), mask_bf))
    wq = _bf(jnp.concatenate([w, z["qd"]], axis=1))

    outs = (of_ref, ob_ref)
    state = [s_s[0], s_s[1]]
    for i in range(n_chunk):
        for d in range(N_DIR):
            c = n_chunk - 1 - i if d else i
            p = d * n_chunk + c
            s_bd = state[d]
            ws = jnp.dot(wq[p], _bf(s_bd), preferred_element_type=F32)
            v_new = u[p] - ws[0:CHUNK]
            v_bd = jnp.concatenate([_bf(v_new)] * N_HEADS, axis=0) * mask_bf
            outs[d][c * CHUNK:(c + 1) * CHUNK, :] = ws[CHUNK:] + jnp.dot(qk[p], v_bd,
                                                                        preferred_element_type=F32)
            state[d] = s_bd * z["gl"][p] + jnp.where(mask_bd, _dot(z["kd"][p].T, v_new), 0.0)
    s_s[0] = state[0]
    s_s[1] = state[1]

    @pl.when(segs.last(s))
    def _():
        sf_ref[0] = state[0]
        sb_ref[0] = state[1]


def _delta(layer, segs, zg, gates, conv_w, alog_rows, dtb_rows, s0):
    n_tok = zg.shape[0]
    per = SEG // HALO
    n_halo = n_tok // HALO

    def cur(rev, w):
        return pl.BlockSpec((SEG, w), lambda s: (segs.blk(s, rev), 0))

    def prev(rev):
        return pl.BlockSpec((HALO, D_MODEL), lambda s: (jnp.maximum(segs.blk(s, rev) * per - 1, 0), 0))

    def nxt(rev):
        return pl.BlockSpec((HALO, D_MODEL),
                            lambda s: (jnp.minimum((segs.blk(s, rev) + 1) * per, n_halo - 1), 0))

    lane_rows = pl.BlockSpec((1, N_DIR, SCAN_W), lambda s: (layer, 0, 0))
    st_out = pl.BlockSpec((1, SCAN_W, SCAN_W), lambda s: (segs.seq(s), 0, 0))
    return pl.pallas_call(
        functools.partial(_delta_kernel, segs=segs),
        grid=(segs.n_seg,),
        in_specs=[
            prev(False), cur(False, D_MODEL), nxt(False),
            prev(True), cur(True, D_MODEL), nxt(True),
            cur(False, GATE_W), cur(True, GATE_W),
            pl.BlockSpec((1, CONV_W, 3 * SCAN_W), lambda s: (layer, 0, 0)),
            lane_rows, lane_rows,
            pl.BlockSpec((1, N_DIR, SCAN_W, SCAN_W), lambda s: (segs.lat_seq(s), 0, 0, 0)),
        ],
        out_specs=[cur(False, SCAN_W), cur(True, SCAN_W), st_out, st_out],
        out_shape=[jax.ShapeDtypeStruct((n_tok, SCAN_W), F32)] * 2
        + [jax.ShapeDtypeStruct((segs.n_seq, SCAN_W, SCAN_W), F32)] * 2,
        scratch_shapes=[pltpu.VMEM((N_DIR, SCAN_W, SCAN_W), F32)],
        compiler_params=pltpu.CompilerParams(
            dimension_semantics=("arbitrary",), vmem_limit_bytes=VMEM_LIMIT),
        name="delta",
    )(zg, zg, zg, zg, zg, zg, gates, gates, conv_w, alog_rows, dtb_rows, s0)


def _attend(q_ref, kv_pairs, o_ref):
    lane = _iota((SEG, KV_WIDTH), 1)
    group = A_HEADS // A_KV_HEADS
    for col in range(A_WIDTH // KV_WIDTH):
        x = q_ref[:, col * KV_WIDTH:(col + 1) * KV_WIDTH]
        outs = []
        for e in range(2):
            kvh = (2 * col + e) // group
            xe = x if e == kvh else pltpu.roll(x, HEAD_DIM, 1)
            qh = jnp.where(_head_of(lane) == kvh, xe, 0.0)
            scores = [_dot_nt(qh, k) for k, _ in kv_pairs]
            m = functools.reduce(jnp.maximum, [jnp.max(sc, axis=-1, keepdims=True) for sc in scores])
            ps = [jnp.exp(sc - m) for sc in scores]
            den = functools.reduce(jnp.add, [jnp.sum(p, axis=-1, keepdims=True) for p in ps])
            ps = [p / den for p in ps]
            o = functools.reduce(jnp.add, [_dot(p, v) for p, (_, v) in zip(ps, kv_pairs)])
            o = jnp.where(_head_of(lane) == kvh, o, 0.0)
            outs.append(o if e == kvh else pltpu.roll(o, HEAD_DIM, 1))
        o_ref[:, col * KV_WIDTH:(col + 1) * KV_WIDTH] = outs[0] + outs[1]


def _attn_kernel(q_ref, kc_ref, vc_ref, kl_ref, vl_ref, ck_ref, cv_ref, o_ref, *, segs):
    s = pl.program_id(0)

    @pl.when(segs.is_ctx(s))
    def _():
        _attend(q_ref, [(kc_ref[...], vc_ref[...])], o_ref)

    @pl.when(jnp.logical_not(segs.is_ctx(s)))
    def _():
        _attend(q_ref, [(ck_ref[0, 0], cv_ref[0, 0]), (kl_ref[...], vl_ref[...])], o_ref)


def _attention(layer, segs, aq, ak, av, cache_k, cache_v):
    n_tok = aq.shape[0]
    t_lat = segs.per_seq * SEG
    past = cache_k.shape[2]
    lat_blocks_before = segs.n_ctx_seg * SEG // t_lat
    own = lambda w: pl.BlockSpec((SEG, w), lambda s: (s, 0))
    lat_kv = pl.BlockSpec((t_lat, KV_WIDTH), lambda s: (lat_blocks_before + segs.lat_seq(s), 0))
    cache = pl.BlockSpec((1, 1, past, KV_WIDTH), lambda s: (segs.lat_seq(s), layer, 0, 0))
    return pl.pallas_call(
        functools.partial(_attn_kernel, segs=segs),
        grid=(segs.n_seg,),
        in_specs=[own(A_WIDTH), own(KV_WIDTH), own(KV_WIDTH), lat_kv, lat_kv, cache, cache],
        out_specs=own(A_WIDTH),
        out_shape=jax.ShapeDtypeStruct((n_tok, A_WIDTH), F32),
        compiler_params=pltpu.CompilerParams(
            dimension_semantics=("arbitrary",), vmem_limit_bytes=VMEM_LIMIT),
        name="attention",
    )(aq, ak, av, ak, av, cache_k, cache_v)


def _permute_w_in(w_in):
    m, g, a, kv = SCAN_W, SCAN_W, A_WIDTH, KV_WIDTH
    sizes = (m, m, m, m, 8, 8, 3 * g, g, 8, 8, a, kv, kv)
    offs = [0]
    for sz in sizes:
        offs.append(offs[-1] + sz)
    piece = lambda i: w_in[..., offs[i]:offs[i + 1]]
    pad = jnp.zeros(w_in.shape[:-1] + (GATE_W - 32,), w_in.dtype)
    order = [0, 1, 2, 3, 6, 7, 10, 11, 12, 4, 5, 8, 9]
    return jnp.concatenate([piece(i) for i in order] + [pad], axis=-1)


def _rope_tables(n_ctx, n_lat_seq, t_lat):
    rows = t_lat // GRID_W
    row = jnp.repeat(jnp.arange(rows, dtype=F32), GRID_W)
    col = jnp.tile(jnp.arange(GRID_W, dtype=F32), rows)
    n_freq = HEAD_DIM // 4
    inv = ROPE_BASE ** (-jnp.arange(n_freq, dtype=F32) / n_freq)
    ang = jnp.stack([row[:, None] * inv, col[:, None] * inv], axis=1)
    cos, sin = jnp.cos(ang), jnp.sin(ang)
    cos_h = jnp.concatenate([cos, cos], axis=-1).reshape(t_lat, HEAD_DIM)
    sin_h = jnp.concatenate([-sin, sin], axis=-1).reshape(t_lat, HEAD_DIM)
    lat = lambda a: jnp.tile(a, (n_lat_seq, KV_WIDTH // HEAD_DIM))
    cos_t = jnp.concatenate([jnp.ones((n_ctx, KV_WIDTH), F32), lat(cos_h)], axis=0)
    sin_t = jnp.concatenate([jnp.zeros((n_ctx, KV_WIDTH), F32), lat(sin_h)], axis=0)
    return cos_t, sin_t


def _block_diag(x):
    eye = jnp.eye(N_HEADS, dtype=x.dtype)
    y = x[..., :, :, None, :] * eye[:, None, :, None]
    return y.reshape(x.shape[:-3] + (SCAN_W, SCAN_W))


def _diag_blocks(x):
    y = x.reshape(x.shape[:-2] + (N_HEADS, HEAD_DIM, N_HEADS, HEAD_DIM))
    return jnp.stack([y[..., h, :, h, :] for h in range(N_HEADS)], axis=-3)


def _head_lanes(x):
    return jnp.repeat(x, HEAD_DIM, axis=-1)


def kernel(x_prompt, x_sample, c, cache_k, cache_v, state_mlstm_C, state_mlstm_n, state_mlstm_m,
           state_delta_S, c_ctx, ada_w, ada_b, norm_g, ffn_w_in, ffn_w_out, w_in, w_out, mlstm_f_bias,
           mlstm_norm, delta_conv, delta_a_log, delta_dt_bias, delta_norm, attn_q_norm, attn_k_norm,
           final_norm):
    batch, t_ctx, d_model = x_prompt.shape
    n_lat_seq, t_lat, _ = x_sample.shape
    depth = ada_w.shape[0]
    assert d_model == D_MODEL and norm_g.shape[1] == 3 and ffn_w_in.shape[-1] == 2 * D_FF
    n_ctx = batch * t_ctx
    n_tok = n_ctx + n_lat_seq * t_lat
    assert n_ctx % TM == 0 and t_lat % TM == 0 and n_ctx % t_lat == 0
    segs = _Segs(n_ctx, t_ctx, n_lat_seq, t_lat)

    cond = jnp.concatenate([c_ctx[None, :], c, jnp.zeros((8 - 1 - n_lat_seq, D_MODEL), F32)], axis=0)
    mod = _modulation(cond, ada_w, ada_b).reshape(depth, 8, N_MOD, D_MODEL)

    wfi = _bf(ffn_w_in).reshape(depth * 2, D_MODEL, 2 * D_FF)
    wfo = _bf(ffn_w_out).reshape(depth * 2, D_FF, D_MODEL)
    wz = _bf(_permute_w_in(w_in))
    wo = _bf(w_out)
    qn = jnp.tile(attn_q_norm, (1, A_HEADS)).reshape(depth, 1, A_WIDTH)
    kn = jnp.tile(attn_k_norm, (1, A_KV_HEADS)).reshape(depth, 1, KV_WIDTH)
    gn = jnp.tile(delta_norm, (1, N_HEADS)).reshape(depth, 1, SCAN_W)
    mn = mlstm_norm.reshape(depth, 1, SCAN_W)
    fin = final_norm.reshape(1, D_MODEL)
    fb_rows = _head_lanes(mlstm_f_bias)
    alog_rows = _head_lanes(delta_a_log)
    dtb_rows = _head_lanes(delta_dt_bias)
    cos_t, sin_t = _rope_tables(n_ctx, n_lat_seq, t_lat)
    ck = cache_k.reshape(cache_k.shape[:3] + (KV_WIDTH,))
    cv = cache_v.reshape(cache_v.shape[:3] + (KV_WIDTH,))

    x = jnp.concatenate([x_prompt.reshape(n_ctx, D_MODEL), x_sample.reshape(n_lat_seq * t_lat, D_MODEL)], axis=0)
    ks, vs, cs, ns, ms, ss = [], [], [], [], [], []
    for l in range(depth):
        x1, zm, zg, aq, ak, av, gates = _dense1(l, x, mod, norm_g, wfi, wfo, wz, qn, kn, cos_t, sin_t,
                                                n_ctx, t_lat)
        c0 = _block_diag(state_mlstm_C[:, l])
        n0 = state_mlstm_n[:, l].reshape(n_lat_seq, N_DIR, 1, SCAN_W)
        m0 = _head_lanes(state_mlstm_m[:, l]).reshape(n_lat_seq, N_DIR, 1, SCAN_W)
        hf, hb, c_f, c_b, n_f, n_b, m_f, m_b = _mlstm(l, segs, zm, gates, fb_rows, c0, n0, m0)
        s0 = _block_diag(state_delta_S[:, l])
        o_f, o_b, s_f, s_b = _delta(l, segs, zg, gates, delta_conv, alog_rows, dtb_rows, s0)
        a_out = _attention(l, segs, aq, ak, av, ck, cv)
        x = _dense2(l, x1, mod, norm_g, hf, hb, zm, mn, o_f, o_b, zg, gn, a_out, wo, wfi, wfo, fin,
                    n_ctx, t_lat, final=(l == depth - 1))

        both = lambda f, b: jnp.stack([f[:batch], b[:batch]], axis=1)
        ks.append(ak[:n_ctx].reshape(batch, t_ctx, A_KV_HEADS, HEAD_DIM))
        vs.append(av[:n_ctx].reshape(batch, t_ctx, A_KV_HEADS, HEAD_DIM))
        cs.append(_diag_blocks(both(c_f, c_b)))
        ns.append(both(n_f, n_b).reshape(batch, N_DIR, N_HEADS, HEAD_DIM))
        ms.append(both(m_f, m_b).reshape(batch, N_DIR, N_HEADS, HEAD_DIM)[..., 0])
        ss.append(_diag_blocks(both(s_f, s_b)))

    y_prompt = x[:n_ctx].reshape(batch, t_ctx, D_MODEL)
    y_sample = x[n_ctx:].reshape(n_lat_seq, t_lat, D_MODEL)
    stack = lambda xs: jnp.stack(xs, axis=1)
    return (y_prompt, y_sample, stack(ks), stack(vs), stack(cs), stack(ns), stack(ms), stack(ss))
```

```python
import functools
import math

import numpy as np

import jax
import jax.numpy as jnp
from jax import lax
from jax.experimental import pallas as pl
from jax.experimental.pallas import tpu as pltpu

F32 = jnp.float32
BF16 = jnp.bfloat16

D_MODEL = 1024
HEAD_DIM = 64
N_HEADS = 4
SCAN_W = N_HEADS * HEAD_DIM
A_HEADS = 8
A_KV_HEADS = 2
A_GROUP = A_HEADS // A_KV_HEADS
A_WIDTH = A_HEADS * HEAD_DIM
KV_WIDTH = A_KV_HEADS * HEAD_DIM
N_DIR = 2
CHUNK = 64
CONV_W = 5
D_FF = 2816
FF_CHUNK = 256
GRID_W = 64
ROPE_BASE = 10000.0
EPS = 1e-6
N_MOD = 9
MOD_ROWS = 8
GATE_W = 128
Z_WIDTH = 2 * D_MODEL + A_WIDTH + 2 * KV_WIDTH + GATE_W

SEG = 256
N_CHUNK = SEG // CHUNK
TM = 512
HALO = 8
KV_BLOCK = 512
VMEM_LIMIT = 56 * 1024 * 1024


def _bf(x):
    return x.astype(BF16)


def _dot(a, b):
    return jnp.dot(_bf(a), _bf(b), preferred_element_type=F32)


def _bdot(a, b):
    return lax.dot_general(a, b, (((2,), (1,)), ((0,), (0,))), preferred_element_type=F32)


def _bdot_nt(a, b):
    return lax.dot_general(a, b, (((2,), (2,)), ((0,), (0,))), preferred_element_type=F32)


def _split(a, terms):
    parts = []
    rest = a
    for _ in range(terms):
        p = _bf(rest)
        parts.append(p)
        rest = rest - p.astype(F32)
    return parts


def _dot_exact_rhs(a, b01, terms=2):
    parts = _split(a, terms)
    if a.shape[-1] % 128 == 0:
        return jnp.dot(jnp.concatenate(parts, axis=-1), jnp.concatenate([b01] * terms, axis=0),
                       preferred_element_type=F32)
    out = None
    for p in parts:
        d = jnp.dot(p, b01, preferred_element_type=F32)
        out = d if out is None else out + d
    return out


def _iota(shape, dim):
    return lax.broadcasted_iota(jnp.int32, shape, dim)


def _head_of(idx):
    return jnp.right_shift(idx, 6)


def _in_head(idx):
    return jnp.bitwise_and(idx, HEAD_DIM - 1)


def _block_ones(n):
    return (_head_of(_iota((n, n), 0)) == _head_of(_iota((n, n), 1)))


def _head_sum(x):
    n = x.shape[-1]
    return _dot_exact_rhs(x, _block_ones(n).astype(BF16), terms=2)


def _head_rms(x, g_row):
    ms = _head_sum(x * x) * (1.0 / HEAD_DIM)
    return x * lax.rsqrt(ms + EPS) * g_row


def _rms_mod(x, g_row, scale_row, shift_row):
    y = x * lax.rsqrt(jnp.mean(x * x, axis=-1, keepdims=True) + EPS) * g_row
    return y * (1.0 + scale_row) + shift_row


def _softplus(x):
    return jnp.maximum(x, 0.0) + jnp.log1p(jnp.exp(-jnp.abs(x)))


def _log_sigmoid(x):
    return -_softplus(-x)


def _sigmoid(x):
    return jax.nn.sigmoid(x)


def _silu(x):
    return x * jax.nn.sigmoid(x)


def _bd3(x, mask_bf):
    return jnp.concatenate([x] * N_HEADS, axis=1) * mask_bf


def _to_block_diag(x4):
    spread = (_iota((HEAD_DIM, SCAN_W), 0) == _in_head(_iota((HEAD_DIM, SCAN_W), 1))).astype(BF16)
    rows = x4.reshape(SCAN_W, HEAD_DIM)
    return jnp.where(_block_ones(SCAN_W), _dot_exact_rhs(rows, spread, terms=3), 0.0)


def _from_block_diag(x):
    fold = (_in_head(_iota((SCAN_W, HEAD_DIM), 0)) == _iota((SCAN_W, HEAD_DIM), 1)).astype(BF16)
    return _dot_exact_rhs(x, fold, terms=3).reshape(N_HEADS, HEAD_DIM, HEAD_DIM)


def _expand_gate(gates, col0):
    sel = (_iota((GATE_W, SCAN_W), 0) == col0 + _head_of(_iota((GATE_W, SCAN_W), 1))).astype(BF16)
    return _dot_exact_rhs(gates, sel, terms=3)


def _chunk_scan(x, op, ident, rev):
    rows = x.shape[0]
    pos = _in_head(_iota(x.shape, 0))
    s = 1
    while s < CHUNK:
        if rev:
            y = pltpu.roll(x, rows - s, 0)
            ok = pos < CHUNK - s
        else:
            y = pltpu.roll(x, s, 0)
            ok = pos >= s
        x = op(x, jnp.where(ok, y, ident))
        s *= 2
    return x


def _chunks(a):
    return a.reshape(N_CHUNK, CHUNK, SCAN_W)


def _problem_masks():
    shape = (N_DIR * N_CHUNK, CHUNK, SCAN_W)
    t_idx, j_idx = _iota(shape, 1), _in_head(_iota(shape, 2))
    ahead = jnp.where(_iota(shape, 0) >= N_CHUNK, j_idx - t_idx, t_idx - j_idx)
    return t_idx == j_idx, ahead >= 0, ahead > 0


def _scan_order(d):
    return range(N_CHUNK - 1, -1, -1) if d else range(N_CHUNK)


def _mod_kernel(cond_ref, w_ref, b_ref, o_ref):
    a = _silu(cond_ref[...])
    o_ref[0, 0] = _dot(a, w_ref[0]) + b_ref[0]


def _modulation(cond, ada_w, ada_b):
    depth = ada_w.shape[0]
    return pl.pallas_call(
        _mod_kernel,
        grid=(depth, N_MOD),
        in_specs=[
            pl.BlockSpec((MOD_ROWS, D_MODEL), lambda l, j: (0, 0)),
            pl.BlockSpec((1, D_MODEL, D_MODEL), lambda l, j: (l, 0, j)),
            pl.BlockSpec((1, 1, D_MODEL), lambda l, j: (l, 0, j)),
        ],
        out_specs=pl.BlockSpec((1, 1, MOD_ROWS, D_MODEL), lambda l, j: (l, j, 0, 0)),
        out_shape=jax.ShapeDtypeStruct((depth, N_MOD, MOD_ROWS, D_MODEL), F32),
        compiler_params=pltpu.CompilerParams(
            dimension_semantics=("arbitrary", "arbitrary"), vmem_limit_bytes=VMEM_LIMIT),
        name="modulation",
    )(cond, ada_w, ada_b.reshape(depth, 1, N_MOD * D_MODEL))


def _ffn(h, w_in_ref, w_out_ref):
    acc = None
    for c in range(D_FF // FF_CHUNK):
        lo = c * FF_CHUNK
        g = jnp.dot(h, w_in_ref[0, :, lo:lo + FF_CHUNK], preferred_element_type=F32)
        u = jnp.dot(h, w_in_ref[0, :, D_FF + lo:D_FF + lo + FF_CHUNK], preferred_element_type=F32)
        a = _bf(_silu(g) * u)
        d = jnp.dot(a, w_out_ref[0, lo:lo + FF_CHUNK, :], preferred_element_type=F32)
        acc = d if acc is None else acc + d
    return acc


def _rope(x, cos, sin_signed):
    w = x.shape[-1]
    reps = w // cos.shape[-1]
    if reps > 1:
        cos = jnp.concatenate([cos] * reps, axis=-1)
        sin_signed = jnp.concatenate([sin_signed] * reps, axis=-1)
    first = jnp.bitwise_and(_iota(x.shape, 1), 31) < 16
    partner = jnp.where(first, pltpu.roll(x, w - 16, 1), pltpu.roll(x, 16, 1))
    return x * cos + partner * sin_signed


class _Tiles:
    def __init__(self, n_ctx, t_lat):
        assert n_ctx % TM == 0 and t_lat % TM == 0
        self.n_ctx = n_ctx // TM
        self.per_seq = t_lat // TM

    def is_ctx(self, i):
        return i < self.n_ctx

    def lat(self, i):
        return jnp.maximum(i - self.n_ctx, 0)

    def mod_row(self, i):
        return jnp.where(self.is_ctx(i), 0, 1 + self.lat(i) // self.per_seq)


def _mod_rows(mod_ref, row):
    return lambda m: mod_ref[0, m, pl.ds(row, 1), :]


def _dense1_kernel(xa_ref, xb_ref, mod_ref, ng_ref, wfi_ref, wfo_ref, wz_ref, qn_ref, kn_ref, cos_ref, sin_ref,
                   x1_ref, zm_ref, zg_ref, aq_ref, ak_ref, av_ref, gate_ref, *, tiles):
    i = pl.program_id(0)
    mod = _mod_rows(mod_ref, tiles.mod_row(i))
    x = jnp.where(tiles.is_ctx(i), xa_ref[...], xb_ref[...])
    h = _bf(_rms_mod(x, ng_ref[0, 0:1], mod(0), mod(1)))
    x1 = x + 0.5 * mod(2) * _ffn(h, wfi_ref, wfo_ref)
    x1_ref[...] = x1
    h2 = _bf(_rms_mod(x1, ng_ref[0, 1:2], mod(3), mod(4)))

    def proj(lo, width):
        return jnp.dot(h2, wz_ref[0, :, lo:lo + width], preferred_element_type=F32)

    zm_ref[...] = proj(0, D_MODEL)
    zg_ref[...] = proj(D_MODEL, D_MODEL)
    cos = cos_ref[...]
    sin = sin_ref[...]
    off = 2 * D_MODEL
    q = _head_rms(proj(off, A_WIDTH), qn_ref[0])
    aq_ref[...] = _rope(q, cos, sin) * (1.0 / math.sqrt(HEAD_DIM))
    k = _head_rms(proj(off + A_WIDTH, KV_WIDTH), kn_ref[0])
    ak_ref[...] = _rope(k, cos, sin)
    av_ref[...] = proj(off + A_WIDTH + KV_WIDTH, KV_WIDTH)
    gate_ref[...] = proj(off + A_WIDTH + 2 * KV_WIDTH, GATE_W)


def _resident(shape, index_map):
    return pl.BlockSpec(shape, index_map, pipeline_mode=pl.Buffered(1))


def _dense1(layer, tiles, n_tok, x_ctx, x_lat, lat_tile0, mod, norm_g, wfi, wfo, wz, qn, kn, cos_t, sin_t):
    row = lambda w: pl.BlockSpec((TM, w), lambda i: (i, 0))
    rope_tile = lambda i: jnp.where(tiles.is_ctx(i), tiles.per_seq, tiles.lat(i) % tiles.per_seq)
    rope = pl.BlockSpec((TM, KV_WIDTH), lambda i: (rope_tile(i), 0))
    out_w = (D_MODEL, D_MODEL, D_MODEL, A_WIDTH, KV_WIDTH, KV_WIDTH, GATE_W)
    return pl.pallas_call(
        functools.partial(_dense1_kernel, tiles=tiles),
        grid=(n_tok // TM,),
        in_specs=[
            pl.BlockSpec((TM, D_MODEL), lambda i: (jnp.minimum(i, tiles.n_ctx - 1), 0)),
            pl.BlockSpec((TM, D_MODEL), lambda i: (tiles.lat(i) + lat_tile0, 0)),
            pl.BlockSpec((1, N_MOD, MOD_ROWS, D_MODEL), lambda i: (layer, 0, 0, 0)),
            pl.BlockSpec((1, 3, D_MODEL), lambda i: (layer, 0, 0)),
            _resident((1, D_MODEL, 2 * D_FF), lambda i: (2 * layer, 0, 0)),
            _resident((1, D_FF, D_MODEL), lambda i: (2 * layer, 0, 0)),
            _resident((1, D_MODEL, Z_WIDTH), lambda i: (layer, 0, 0)),
            pl.BlockSpec((1, 1, A_WIDTH), lambda i: (layer, 0, 0)),
            pl.BlockSpec((1, 1, KV_WIDTH), lambda i: (layer, 0, 0)),
            rope, rope,
        ],
        out_specs=[row(w) for w in out_w],
        out_shape=[jax.ShapeDtypeStruct((n_tok, w), F32) for w in out_w],
        compiler_params=pltpu.CompilerParams(
            dimension_semantics=("arbitrary",), vmem_limit_bytes=VMEM_LIMIT),
        name="dense1",
    )(x_ctx, x_lat, mod, norm_g, wfi, wfo, wz, qn, kn, cos_t, sin_t)


def _dense2_kernel(x_ref, mod_ref, ng_ref, hf_ref, hb_ref, mo_ref, mn_ref, of_ref, ob_ref, gz_ref, gn_ref,
                   ao_ref, wo_ref, wfi_ref, wfo_ref, fin_ref, out_ref, *, tiles, final):
    mod = _mod_rows(mod_ref, tiles.mod_row(pl.program_id(0)))
    m_out = _head_rms(hf_ref[...] + hb_ref[...], mn_ref[0]) * _sigmoid(mo_ref[...])
    g_out = _head_rms(of_ref[...] + ob_ref[...], gn_ref[0]) * _silu(gz_ref[...])
    mix = jnp.dot(_bf(m_out), wo_ref[0, 0:SCAN_W, :], preferred_element_type=F32)
    mix += jnp.dot(_bf(g_out), wo_ref[0, SCAN_W:2 * SCAN_W, :], preferred_element_type=F32)
    mix += jnp.dot(_bf(ao_ref[...]), wo_ref[0, 2 * SCAN_W:, :], preferred_element_type=F32)
    x2 = x_ref[...] + mod(5) * mix
    h = _bf(_rms_mod(x2, ng_ref[0, 2:3], mod(6), mod(7)))
    x3 = x2 + 0.5 * mod(8) * _ffn(h, wfi_ref, wfo_ref)
    if final:
        out_ref[...] = x3 * lax.rsqrt(jnp.mean(x3 * x3, axis=-1, keepdims=True) + EPS) * fin_ref[...]
    else:
        out_ref[...] = x3


def _dense2(layer, tiles, x1, mod, norm_g, hf, hb, zm, m_norm, of, ob, zg, g_norm, a_out, wo, wfi, wfo, fin,
            final):
    n_tok = x1.shape[0]
    row = lambda w: pl.BlockSpec((TM, w), lambda i: (i, 0))
    last_quarter = pl.BlockSpec((TM, SCAN_W), lambda i: (i, 3))
    lane_row = lambda w: pl.BlockSpec((1, 1, w), lambda i: (layer, 0, 0))
    return pl.pallas_call(
        functools.partial(_dense2_kernel, tiles=tiles, final=final),
        grid=(n_tok // TM,),
        in_specs=[
            row(D_MODEL),
            pl.BlockSpec((1, N_MOD, MOD_ROWS, D_MODEL), lambda i: (layer, 0, 0, 0)),
            pl.BlockSpec((1, 3, D_MODEL), lambda i: (layer, 0, 0)),
            row(SCAN_W), row(SCAN_W), last_quarter, lane_row(SCAN_W),
            row(SCAN_W), row(SCAN_W), last_quarter, lane_row(SCAN_W),
            row(A_WIDTH),
            _resident((1, D_MODEL, D_MODEL), lambda i: (layer, 0, 0)),
            _resident((1, D_MODEL, 2 * D_FF), lambda i: (2 * layer + 1, 0, 0)),
            _resident((1, D_FF, D_MODEL), lambda i: (2 * layer + 1, 0, 0)),
            pl.BlockSpec((1, D_MODEL), lambda i: (0, 0)),
        ],
        out_specs=row(D_MODEL),
        out_shape=jax.ShapeDtypeStruct((n_tok, D_MODEL), F32),
        compiler_params=pltpu.CompilerParams(
            dimension_semantics=("arbitrary",), vmem_limit_bytes=VMEM_LIMIT),
        name="dense2",
    )(x1, mod, norm_g, hf, hb, zm, m_norm, of, ob, zg, g_norm, a_out, wo, wfi, wfo, fin)


class _Segs:
    def __init__(self, n_ctx, t_ctx, n_lat_seq, t_lat):
        assert t_ctx == SEG and t_lat % SEG == 0
        self.n_ctx_seg = n_ctx // SEG
        self.per_seq = t_lat // SEG
        self.n_lat_seq = n_lat_seq
        self.n_seg = self.n_ctx_seg + n_lat_seq * self.per_seq
        self.n_seq = self.n_ctx_seg + n_lat_seq

    def is_ctx(self, s):
        return s < self.n_ctx_seg

    def lat(self, s):
        r = jnp.maximum(s - self.n_ctx_seg, 0)
        return r // self.per_seq, r % self.per_seq

    def blk(self, s, rev):
        b, j = self.lat(s)
        j = self.per_seq - 1 - j if rev else j
        return jnp.where(self.is_ctx(s), s, self.n_ctx_seg + b * self.per_seq + j)

    def seq(self, s):
        return jnp.where(self.is_ctx(s), s, self.n_ctx_seg + self.lat(s)[0])

    def lat_seq(self, s):
        return self.lat(s)[0]

    def first(self, s):
        return jnp.logical_or(self.is_ctx(s), self.lat(s)[1] == 0)

    def last(self, s):
        return jnp.logical_or(self.is_ctx(s), self.lat(s)[1] == self.per_seq - 1)


_STATE_MAT = (N_DIR, N_HEADS, HEAD_DIM, HEAD_DIM)


def _state_specs(layer, segs):
    s_in = pl.BlockSpec((1, 1) + _STATE_MAT, lambda s: (segs.lat_seq(s), layer, 0, 0, 0, 0))
    s_out = pl.BlockSpec((1,) + _STATE_MAT, lambda s: (segs.seq(s), 0, 0, 0, 0))
    return s_in, s_out, jax.ShapeDtypeStruct((segs.n_seq,) + _STATE_MAT, F32)


def _mlstm_gates(gate_ref, fb_row, rev, d):
    gates = gate_ref[...]
    ig = _expand_gate(gates, d * N_HEADS)
    lf = _log_sigmoid(_expand_gate(gates, 8 + d * N_HEADS) + fb_row)
    b = _chunk_scan(lf, jnp.add, 0.0, rev)
    r = ig - b
    cm = _chunk_scan(r, jnp.maximum, -jnp.inf, rev)
    edge = 0 if rev else CHUNK - 1
    b = _chunks(b)
    bl = b[:, edge:edge + 1, :]
    lw = bl - b + _chunks(ig)
    return dict(b=b, r=_chunks(r), cm=_chunks(cm), bl=bl, lw=lw, lw_max=jnp.max(lw, axis=1, keepdims=True))


def _mlstm_kernel(zmf_ref, zmb_ref, gf_ref, gb_ref, fb_ref, c0_ref, n0_ref, m0_ref,
                  hf_ref, hb_ref, c_out, n_out, m_out, c_s, n_s, m_s, *, segs):
    s = pl.program_id(0)

    @pl.when(segs.first(s))
    def _():
        ctx = segs.is_ctx(s)
        for d in range(N_DIR):
            c_s[d] = jnp.where(ctx, 0.0, _to_block_diag(c0_ref[0, 0, d]))
            n_s[d] = jnp.where(ctx, 0.0, n0_ref[0, 0, d])
            m_s[d] = jnp.where(ctx, 0.0, m0_ref[0, 0, d])

    g = [_mlstm_gates(gf_ref, fb_ref[0, 0:1], False, 0), _mlstm_gates(gb_ref, fb_ref[0, 1:2], True, 1)]
    zm = (zmf_ref, zmb_ref)

    m_at = [[None] * N_CHUNK for _ in range(N_DIR)]
    m_next = [[None] * N_CHUNK for _ in range(N_DIR)]
    m_fin = []
    for d in range(N_DIR):
        m_row = m_s[d]
        for c in _scan_order(d):
            m_at[d][c] = m_row
            m_row = jnp.maximum(g[d]["bl"][c] + m_row, g[d]["lw_max"][c])
            m_next[d][c] = m_row
        m_fin.append(m_row)

    cat = lambda name: jnp.concatenate([g[0][name], g[1][name]], axis=0)
    rows_of = lambda xs: jnp.concatenate([x[None] for d in range(N_DIR) for x in xs[d]], axis=0)
    b, r, cm, bl, lw = cat("b"), cat("r"), cat("cm"), cat("bl"), cat("lw")
    m_c, m_n = rows_of(m_at), rows_of(m_next)
    col = lambda j: jnp.concatenate([_chunks(ref[:, j * SCAN_W:(j + 1) * SCAN_W]) for ref in zm], axis=0)
    q, k, v = col(0), col(1) * (1.0 / math.sqrt(HEAD_DIM)), col(2)

    diag, tri, _ = _problem_masks()
    mask_bd = _block_ones(SCAN_W)
    mask_bf = mask_bd.astype(BF16)

    mx = jnp.maximum(m_c, cm)
    r_row = jnp.sum(jnp.where(diag, r, 0.0), axis=1, keepdims=True)
    w = jnp.exp(jnp.where(tri, r_row - mx, -jnp.inf))
    a_int = jnp.exp(m_c - mx)
    sc = _bdot_nt(_bf(q), _bd3(_bf(k), mask_bf)) * w
    kw = jnp.exp(lw - m_n) * k
    dec = jnp.exp(bl + m_c - m_n)
    kw_sum = jnp.sum(kw, axis=1, keepdims=True)

    c_at = [None] * (N_DIR * N_CHUNK)
    n_at = [None] * (N_DIR * N_CHUNK)
    c_fin, n_fin = [], []
    for d in range(N_DIR):
        c_bd, n_row = c_s[d], n_s[d]
        for c in _scan_order(d):
            p = d * N_CHUNK + c
            c_at[p], n_at[p] = c_bd, n_row
            c_bd = c_bd * dec[p] + jnp.where(mask_bd, _dot(kw[p].T, v[p]), 0.0)
            n_row = n_row * dec[p] + kw_sum[p]
        c_fin.append(c_bd)
        n_fin.append(n_row)
    c_all = jnp.concatenate([_bf(x)[None] for x in c_at], axis=0)
    n_all = jnp.concatenate([x[None] for x in n_at], axis=0)

    num = _bdot(_bf(sc), _bd3(_bf(v), mask_bf)) + a_int * _bdot(_bf(q), c_all)
    flat = lambda a: a.reshape(N_DIR * SEG, SCAN_W)
    sums = _dot_exact_rhs(jnp.concatenate([flat(sc), flat(q * n_all)], axis=0), mask_bf)
    den = sums[0:N_DIR * SEG].reshape(sc.shape) + a_int * sums[N_DIR * SEG:].reshape(sc.shape)
    h = num / jnp.maximum(jnp.abs(den), jnp.exp(-(b + mx)))
    hf_ref[...] = h[0:N_CHUNK].reshape(SEG, SCAN_W)
    hb_ref[...] = h[N_CHUNK:].reshape(SEG, SCAN_W)

    for d in range(N_DIR):
        c_s[d] = c_fin[d]
        n_s[d] = n_fin[d]
        m_s[d] = m_fin[d]

    @pl.when(segs.last(s))
    def _():
        for d in range(N_DIR):
            c_out[0, d] = _from_block_diag(c_fin[d])
            n_out[0, d] = n_fin[d]
            m_out[0, d] = m_fin[d]


def _mlstm(layer, segs, zm, gates, fb_rows, state_c, n0, m0):
    n_tok = zm.shape[0]
    fwd = lambda w: pl.BlockSpec((SEG, w), lambda s: (segs.blk(s, False), 0))
    bwd = lambda w: pl.BlockSpec((SEG, w), lambda s: (segs.blk(s, True), 0))
    row_in = pl.BlockSpec((1, 1, N_DIR, 1, SCAN_W), lambda s: (segs.lat_seq(s), layer, 0, 0, 0))
    row_out = pl.BlockSpec((1, N_DIR, 1, SCAN_W), lambda s: (segs.seq(s), 0, 0, 0))
    row_shape = jax.ShapeDtypeStruct((segs.n_seq, N_DIR, 1, SCAN_W), F32)
    mat_in, mat_out, mat_shape = _state_specs(layer, segs)
    return pl.pallas_call(
        functools.partial(_mlstm_kernel, segs=segs),
        grid=(segs.n_seg,),
        in_specs=[
            fwd(D_MODEL), bwd(D_MODEL), fwd(GATE_W), bwd(GATE_W),
            pl.BlockSpec((1, N_DIR, SCAN_W), lambda s: (layer, 0, 0)),
            mat_in, row_in, row_in,
        ],
        out_specs=[fwd(SCAN_W), bwd(SCAN_W), mat_out, row_out, row_out],
        out_shape=[jax.ShapeDtypeStruct((n_tok, SCAN_W), F32)] * 2 + [mat_shape, row_shape, row_shape],
        scratch_shapes=[pltpu.VMEM((N_DIR, SCAN_W, SCAN_W), F32),
                        pltpu.VMEM((N_DIR, 1, SCAN_W), F32),
                        pltpu.VMEM((N_DIR, 1, SCAN_W), F32)],
        compiler_params=pltpu.CompilerParams(
            dimension_semantics=("arbitrary",), vmem_limit_bytes=VMEM_LIMIT),
        name="mlstm",
    )(zm, zm, gates, gates, fb_rows, state_c, n0, m0)


def _short_conv(prev_ref, cur_ref, next_ref, w_ref, has_prev, has_next):
    w3 = 3 * SCAN_W
    prev = jnp.where(has_prev, prev_ref[:, 0:w3], 0.0)
    nxt = jnp.where(has_next, next_ref[:, 0:w3], 0.0)
    xp = jnp.concatenate([prev, cur_ref[:, 0:w3], nxt], axis=0)
    rows = xp.shape[0]
    acc = None
    for i in range(CONV_W):
        shift = (CONV_W // 2 - i) % rows
        y = xp if shift == 0 else pltpu.roll(xp, shift, 0)
        t = y[HALO:HALO + SEG] * w_ref[0, i:i + 1, :]
        acc = t if acc is None else acc + t
    return acc


def _neumann_inverse(n_all, mask_bf):
    eye = (_iota(n_all.shape[1:], 0) == _in_head(_iota(n_all.shape[1:], 1))).astype(F32)
    p = -n_all
    t = eye + p
    levels = 6
    for lvl in range(levels):
        first, last = lvl == 0, lvl == levels - 1
        p_hi, p_lo = _split(p, 2)
        w_hi, w_lo = _bd3(p_hi, mask_bf), _bd3(p_lo, mask_bf)
        lhs_hi, lhs_lo = [], []
        if not first:
            t_hi, t_lo = _split(t, 2)
            lhs_hi += [t_hi, t_lo]
            lhs_lo += [t_hi]
        if not last:
            lhs_hi += [p_hi, p_lo]
            lhs_lo += [p_hi]
        a = _bdot(jnp.concatenate(lhs_hi, axis=1), w_hi)
        b = _bdot(jnp.concatenate(lhs_lo, axis=1) if len(lhs_lo) > 1 else lhs_lo[0], w_lo)
        ra, rb = 0, 0
        if not first:
            t = t + (a[:, 0:CHUNK] + a[:, CHUNK:2 * CHUNK] + b[:, 0:CHUNK])
            ra, rb = 2 * CHUNK, CHUNK
        if not last:
            p = a[:, ra:ra + CHUNK] + a[:, ra + CHUNK:ra + 2 * CHUNK] + b[:, rb:rb + CHUNK]
    return t


def _delta_prep(prev_ref, cur_ref, next_ref, gate_ref, cw_ref, alog_row, dtb_row, has_prev, has_next, rev, d):
    qkv = _silu(_short_conv(prev_ref, cur_ref, next_ref, cw_ref, has_prev, has_next))

    def l2(a):
        return a * lax.rsqrt(_head_sum(a * a) + EPS)

    q = l2(qkv[:, 0:SCAN_W]) * (1.0 / math.sqrt(HEAD_DIM))
    k = l2(qkv[:, SCAN_W:2 * SCAN_W])
    v = qkv[:, 2 * SCAN_W:3 * SCAN_W]
    gates = gate_ref[...]
    g = -jnp.exp(alog_row) * _softplus(_expand_gate(gates, 16 + d * N_HEADS) + dtb_row)
    beta = _sigmoid(_expand_gate(gates, 24 + d * N_HEADS))
    gc = _chunks(_chunk_scan(g, jnp.add, 0.0, rev))
    edge = 0 if rev else CHUNK - 1
    gl = gc[:, edge:edge + 1, :]
    eg = jnp.exp(gc)
    kb = _chunks(k * beta)
    return dict(q=_chunks(q), k=_chunks(k), kb=kb, vb=_chunks(v * beta), kbe=kb * eg, qd=_chunks(q) * eg,
                kd=_chunks(k) * jnp.exp(gl - gc), gc=gc, gl=jnp.exp(gl))


def _delta_kernel(pf_ref, cf_ref, nf_ref, pb_ref, cb_ref, nb_ref, gf_ref, gb_ref, cw_ref, al_ref, dt_ref,
                  s0_ref, of_ref, ob_ref, s_out, s_s, *, segs):
    s = pl.program_id(0)
    lat = jnp.logical_not(segs.is_ctx(s))
    j = segs.lat(s)[1]

    @pl.when(segs.first(s))
    def _():
        for d in range(N_DIR):
            s_s[d] = jnp.where(lat, _to_block_diag(s0_ref[0, 0, d]), 0.0)

    inner_lo = jnp.logical_and(lat, j > 0)
    inner_hi = jnp.logical_and(lat, j < segs.per_seq - 1)
    fwd = _delta_prep(pf_ref, cf_ref, nf_ref, gf_ref, cw_ref, al_ref[0, 0:1], dt_ref[0, 0:1],
                      inner_lo, inner_hi, False, 0)
    bwd = _delta_prep(pb_ref, cb_ref, nb_ref, gb_ref, cw_ref, al_ref[0, 1:2], dt_ref[0, 1:2],
                      inner_hi, inner_lo, True, 1)
    z = {name: jnp.concatenate([fwd[name], bwd[name]], axis=0) for name in fwd}

    mask_bd = _block_ones(SCAN_W)
    mask_bf = mask_bd.astype(BF16)
    diag, tri, strict = _problem_masks()
    neg_diag = jnp.sum(jnp.where(diag, -z["gc"], 0.0), axis=1, keepdims=True)
    decay = jnp.exp(jnp.where(tri, z["gc"] + neg_diag, -jnp.inf))
    k_bd = _bd3(_bf(z["k"]), mask_bf)
    kq = _bdot_nt(_bf(jnp.concatenate([z["kb"], z["q"]], axis=1)), k_bd)
    qk = _bf(kq[:, CHUNK:] * decay)
    t_all = _bf(_neumann_inverse(jnp.where(strict, kq[:, 0:CHUNK] * decay, 0.0), mask_bf))
    u = _bdot(t_all, _bd3(_bf(z["vb"]), mask_bf))
    w = _bdot(t_all, _bd3(_bf(z["kbe"]), mask_bf))
    wq = _bf(jnp.concatenate([w, z["qd"]], axis=1))

    outs = (of_ref, ob_ref)
    state = [s_s[0], s_s[1]]
    for i in range(N_CHUNK):
        for d in range(N_DIR):
            c = N_CHUNK - 1 - i if d else i
            p = d * N_CHUNK + c
            s_bd = state[d]
            ws = jnp.dot(wq[p], _bf(s_bd), preferred_element_type=F32)
            v_new = u[p] - ws[0:CHUNK]
            v_bd = jnp.concatenate([_bf(v_new)] * N_HEADS, axis=0) * mask_bf
            outs[d][c * CHUNK:(c + 1) * CHUNK, :] = ws[CHUNK:] + jnp.dot(qk[p], v_bd,
                                                                        preferred_element_type=F32)
            state[d] = s_bd * z["gl"][p] + jnp.where(mask_bd, _dot(z["kd"][p].T, v_new), 0.0)
    s_s[0] = state[0]
    s_s[1] = state[1]

    @pl.when(segs.last(s))
    def _():
        for d in range(N_DIR):
            s_out[0, d] = _from_block_diag(state[d])


def _delta(layer, segs, zg, gates, conv_w, alog_rows, dtb_rows, state_s):
    n_tok = zg.shape[0]
    per = SEG // HALO
    n_halo = n_tok // HALO

    def cur(rev, w):
        return pl.BlockSpec((SEG, w), lambda s: (segs.blk(s, rev), 0))

    def prev(rev):
        return pl.BlockSpec((HALO, D_MODEL), lambda s: (jnp.maximum(segs.blk(s, rev) * per - 1, 0), 0))

    def nxt(rev):
        return pl.BlockSpec((HALO, D_MODEL),
                            lambda s: (jnp.minimum((segs.blk(s, rev) + 1) * per, n_halo - 1), 0))

    lane_rows = pl.BlockSpec((1, N_DIR, SCAN_W), lambda s: (layer, 0, 0))
    mat_in, mat_out, mat_shape = _state_specs(layer, segs)
    return pl.pallas_call(
        functools.partial(_delta_kernel, segs=segs),
        grid=(segs.n_seg,),
        in_specs=[
            prev(False), cur(False, D_MODEL), nxt(False),
            prev(True), cur(True, D_MODEL), nxt(True),
            cur(False, GATE_W), cur(True, GATE_W),
            pl.BlockSpec((1, CONV_W, 3 * SCAN_W), lambda s: (layer, 0, 0)),
            lane_rows, lane_rows, mat_in,
        ],
        out_specs=[cur(False, SCAN_W), cur(True, SCAN_W), mat_out],
        out_shape=[jax.ShapeDtypeStruct((n_tok, SCAN_W), F32)] * 2 + [mat_shape],
        scratch_shapes=[pltpu.VMEM((N_DIR, SCAN_W, SCAN_W), F32)],
        compiler_params=pltpu.CompilerParams(
            dimension_semantics=("arbitrary",), vmem_limit_bytes=VMEM_LIMIT),
        name="delta",
    )(zg, zg, zg, zg, zg, zg, gates, gates, conv_w, alog_rows, dtb_rows, state_s)


def _attend(q_ref, kv_blocks, o_ref):
    qt = q_ref[...].T
    zeros = jnp.zeros((HEAD_DIM, A_GROUP * SEG), BF16)
    kv = [(_bf(k), _bf(v.T)) for k, v in kv_blocks]
    group_out = []
    for g in range(A_KV_HEADS):
        heads = range(g * A_GROUP, (g + 1) * A_GROUP)
        slab = _bf(jnp.concatenate([qt[h * HEAD_DIM:(h + 1) * HEAD_DIM, :] for h in heads], axis=1))
        w_g = jnp.concatenate([slab if i == g else zeros for i in range(A_KV_HEADS)], axis=0)
        m = l = acc = None
        nxt = jnp.dot(kv[0][0], w_g, preferred_element_type=F32)
        for i, (_, v_t) in enumerate(kv):
            sc = nxt
            if i + 1 < len(kv):
                nxt = jnp.dot(kv[i + 1][0], w_g, preferred_element_type=F32)
            m_blk = jnp.max(sc, axis=0, keepdims=True)
            m_new = m_blk if m is None else jnp.maximum(m, m_blk)
            p = jnp.exp(sc - m_new)
            pv = jnp.dot(v_t, _bf(p), preferred_element_type=F32)
            p_sum = jnp.sum(p, axis=0, keepdims=True)
            if m is None:
                l, acc = p_sum, pv
            else:
                alpha = jnp.exp(m - m_new)
                l, acc = l * alpha + p_sum, acc * alpha + pv
            m = m_new
        group_out.append((acc / l)[g * HEAD_DIM:(g + 1) * HEAD_DIM])
    for col in range(A_WIDTH // KV_WIDTH):
        g, h0 = (2 * col) // A_GROUP, (2 * col) % A_GROUP
        pair = jnp.concatenate([group_out[g][:, h0 * SEG:(h0 + 1) * SEG],
                                group_out[g][:, (h0 + 1) * SEG:(h0 + 2) * SEG]], axis=0)
        o_ref[:, col * KV_WIDTH:(col + 1) * KV_WIDTH] = pair.T


def _attn_kernel(q_ref, kc_ref, vc_ref, kl_ref, vl_ref, ck_ref, cv_ref, o_ref, *, segs):
    s = pl.program_id(0)

    @pl.when(segs.is_ctx(s))
    def _():
        _attend(q_ref, [(kc_ref[...], vc_ref[...])], o_ref)

    @pl.when(jnp.logical_not(segs.is_ctx(s)))
    def _():
        blocks = [(ck_ref[0, 0], cv_ref[0, 0])]
        for lo in range(0, kl_ref.shape[0], KV_BLOCK):
            blocks.append((kl_ref[lo:lo + KV_BLOCK, :], vl_ref[lo:lo + KV_BLOCK, :]))
        _attend(q_ref, blocks, o_ref)


def _attention(layer, segs, aq, ak, av, cache_k, cache_v):
    n_tok = aq.shape[0]
    t_lat = segs.per_seq * SEG
    past = cache_k.shape[2]
    assert t_lat % KV_BLOCK == 0
    lat_blocks_before = segs.n_ctx_seg * SEG // t_lat
    own = lambda w: pl.BlockSpec((SEG, w), lambda s: (s, 0))
    lat_kv = pl.BlockSpec((t_lat, KV_WIDTH), lambda s: (lat_blocks_before + segs.lat_seq(s), 0))
    cache = pl.BlockSpec((1, 1, past, KV_WIDTH), lambda s: (segs.lat_seq(s), layer, 0, 0))
    return pl.pallas_call(
        functools.partial(_attn_kernel, segs=segs),
        grid=(segs.n_seg,),
        in_specs=[own(A_WIDTH), own(KV_WIDTH), own(KV_WIDTH), lat_kv, lat_kv, cache, cache],
        out_specs=own(A_WIDTH),
        out_shape=jax.ShapeDtypeStruct((n_tok, A_WIDTH), F32),
        compiler_params=pltpu.CompilerParams(
            dimension_semantics=("arbitrary",), vmem_limit_bytes=VMEM_LIMIT),
        name="attention",
    )(aq, ak, av, ak, av, cache_k, cache_v)


def _permute_w_in(w_in):
    m, g, a, kv = SCAN_W, SCAN_W, A_WIDTH, KV_WIDTH
    sizes = (m, m, m, m, 8, 8, 3 * g, g, 8, 8, a, kv, kv)
    offs = [0]
    for sz in sizes:
        offs.append(offs[-1] + sz)
    piece = lambda i: w_in[..., offs[i]:offs[i + 1]]
    pad = jnp.zeros(w_in.shape[:-1] + (GATE_W - 32,), w_in.dtype)
    order = [0, 1, 2, 3, 6, 7, 10, 11, 12, 4, 5, 8, 9]
    return jnp.concatenate([piece(i) for i in order] + [pad], axis=-1)


def _rope_tables(t_lat):
    rows = t_lat // GRID_W
    row = np.repeat(np.arange(rows, dtype=np.float64), GRID_W)
    col = np.tile(np.arange(GRID_W, dtype=np.float64), rows)
    n_freq = HEAD_DIM // 4
    inv = ROPE_BASE ** (-np.arange(n_freq, dtype=np.float64) / n_freq)
    ang = np.stack([row[:, None] * inv, col[:, None] * inv], axis=1)
    cos, sin = np.cos(ang), np.sin(ang)
    cos_h = np.concatenate([cos, cos], axis=-1).reshape(t_lat, HEAD_DIM)
    sin_h = np.concatenate([-sin, sin], axis=-1).reshape(t_lat, HEAD_DIM)
    wide = lambda a: np.tile(a, (1, KV_WIDTH // HEAD_DIM))
    cos_t = np.concatenate([wide(cos_h), np.ones((TM, KV_WIDTH))], axis=0)
    sin_t = np.concatenate([wide(sin_h), np.zeros((TM, KV_WIDTH))], axis=0)
    return jnp.asarray(cos_t, F32), jnp.asarray(sin_t, F32)


def _head_lanes(x):
    return jnp.repeat(x, HEAD_DIM, axis=-1)


def kernel(x_prompt, x_sample, c, cache_k, cache_v, state_mlstm_C, state_mlstm_n, state_mlstm_m,
           state_delta_S, c_ctx, ada_w, ada_b, norm_g, ffn_w_in, ffn_w_out, w_in, w_out, mlstm_f_bias,
           mlstm_norm, delta_conv, delta_a_log, delta_dt_bias, delta_norm, attn_q_norm, attn_k_norm,
           final_norm):
    batch, t_ctx, d_model = x_prompt.shape
    n_lat_seq, t_lat, _ = x_sample.shape
    depth = ada_w.shape[0]
    assert d_model == D_MODEL and norm_g.shape[1] == 3 and ffn_w_in.shape[-1] == 2 * D_FF
    assert 1 + n_lat_seq <= MOD_ROWS
    n_ctx = batch * t_ctx
    n_tok = n_ctx + n_lat_seq * t_lat
    assert n_ctx % t_lat == 0
    segs = _Segs(n_ctx, t_ctx, n_lat_seq, t_lat)
    tiles = _Tiles(n_ctx, t_lat)

    cond = jnp.concatenate([c_ctx[None, :], c, jnp.zeros((MOD_ROWS - 1 - n_lat_seq, D_MODEL), F32)], axis=0)
    mod = _modulation(cond, ada_w, ada_b)

    wfi = _bf(ffn_w_in).reshape(depth * 2, D_MODEL, 2 * D_FF)
    wfo = _bf(ffn_w_out).reshape(depth * 2, D_FF, D_MODEL)
    wz = _bf(_permute_w_in(w_in))
    wo = _bf(w_out)
    qn = jnp.tile(attn_q_norm, (1, A_HEADS)).reshape(depth, 1, A_WIDTH)
    kn = jnp.tile(attn_k_norm, (1, A_KV_HEADS)).reshape(depth, 1, KV_WIDTH)
    gn = jnp.tile(delta_norm, (1, N_HEADS)).reshape(depth, 1, SCAN_W)
    mn = mlstm_norm.reshape(depth, 1, SCAN_W)
    fin = final_norm.reshape(1, D_MODEL)
    fb_rows = _head_lanes(mlstm_f_bias)
    alog_rows = _head_lanes(delta_a_log)
    dtb_rows = _head_lanes(delta_dt_bias)
    cos_t, sin_t = _rope_tables(t_lat)
    ck = cache_k.reshape(cache_k.shape[:3] + (KV_WIDTH,))
    cv = cache_v.reshape(cache_v.shape[:3] + (KV_WIDTH,))
    n0 = state_mlstm_n.reshape(n_lat_seq, depth, N_DIR, 1, SCAN_W)
    m0 = _head_lanes(state_mlstm_m).reshape(n_lat_seq, depth, N_DIR, 1, SCAN_W)

    x_ctx, x_lat, lat_tile0 = x_prompt.reshape(n_ctx, D_MODEL), x_sample.reshape(n_tok - n_ctx, D_MODEL), 0
    ks, vs, cs, ns, ms, ss = [], [], [], [], [], []
    for l in range(depth):
        x1, zm, zg, aq, ak, av, gates = _dense1(l, tiles, n_tok, x_ctx, x_lat, lat_tile0, mod, norm_g,
                                                wfi, wfo, wz, qn, kn, cos_t, sin_t)
        hf, hb, c_new, n_new, m_new = _mlstm(l, segs, zm, gates, fb_rows, state_mlstm_C, n0, m0)
        o_f, o_b, s_new = _delta(l, segs, zg, gates, delta_conv, alog_rows, dtb_rows, state_delta_S)
        a_out = _attention(l, segs, aq, ak, av, ck, cv)
        x = _dense2(l, tiles, x1, mod, norm_g, hf, hb, zm, mn, o_f, o_b, zg, gn, a_out, wo, wfi, wfo, fin,
                    final=(l == depth - 1))
        x_ctx, x_lat, lat_tile0 = x, x, tiles.n_ctx

        ks.append(ak[:n_ctx].reshape(batch, t_ctx, A_KV_HEADS, HEAD_DIM))
        vs.append(av[:n_ctx].reshape(batch, t_ctx, A_KV_HEADS, HEAD_DIM))
        cs.append(c_new[:batch])
        ns.append(n_new[:batch].reshape(batch, N_DIR, N_HEADS, HEAD_DIM))
        ms.append(m_new[:batch].reshape(batch, N_DIR, N_HEADS, HEAD_DIM)[..., 0])
        ss.append(s_new[:batch])

    y_prompt = x[:n_ctx].reshape(batch, t_ctx, D_MODEL)
    y_sample = x[n_ctx:].reshape(n_lat_seq, t_lat, D_MODEL)
    stack = lambda xs: jnp.stack(xs, axis=1)
    return (y_prompt, y_sample, stack(ks), stack(vs), stack(cs), stack(ns), stack(ms), stack(ss))
```

```python
import functools
import math

import numpy as np

import jax
import jax.numpy as jnp
from jax import lax
from jax.experimental import pallas as pl
from jax.experimental.pallas import tpu as pltpu

F32 = jnp.float32
BF16 = jnp.bfloat16

D_MODEL = 1024
HEAD_DIM = 64
N_HEADS = 4
SCAN_W = N_HEADS * HEAD_DIM
A_HEADS = 8
A_KV_HEADS = 2
A_GROUP = A_HEADS // A_KV_HEADS
A_WIDTH = A_HEADS * HEAD_DIM
KV_WIDTH = A_KV_HEADS * HEAD_DIM
N_DIR = 2
CHUNK = 64
CONV_W = 5
D_FF = 2816
FF_CHUNK = 256
GRID_W = 64
ROPE_BASE = 10000.0
EPS = 1e-6
N_MOD = 9
MOD_ROWS = 8
GATE_W = 128
Z_WIDTH = 2 * D_MODEL + A_WIDTH + 2 * KV_WIDTH + GATE_W

SEG = 256
N_CHUNK = SEG // CHUNK
TM = 512
ROW_GROUP = 512
HALO = 8
KV_BLOCK = 512
VMEM_LIMIT = 56 * 1024 * 1024


def _bf(x):
    return x.astype(BF16)


def _dot(a, b):
    return jnp.dot(_bf(a), _bf(b), preferred_element_type=F32)


def _bdot(a, b):
    return lax.dot_general(a, b, (((2,), (1,)), ((0,), (0,))), preferred_element_type=F32)


def _bdot_nt(a, b):
    return lax.dot_general(a, b, (((2,), (2,)), ((0,), (0,))), preferred_element_type=F32)


def _split(a, terms):
    parts = []
    rest = a
    for _ in range(terms):
        p = _bf(rest)
        parts.append(p)
        rest = rest - p.astype(F32)
    return parts


def _dot_exact_rhs(a, b01, terms=2):
    parts = _split(a, terms)
    if a.shape[-1] % 128 == 0:
        return jnp.dot(jnp.concatenate(parts, axis=-1), jnp.concatenate([b01] * terms, axis=0),
                       preferred_element_type=F32)
    out = None
    for p in parts:
        d = jnp.dot(p, b01, preferred_element_type=F32)
        out = d if out is None else out + d
    return out


def _iota(shape, dim):
    return lax.broadcasted_iota(jnp.int32, shape, dim)


def _head_of(idx):
    return jnp.right_shift(idx, 6)


def _in_head(idx):
    return jnp.bitwise_and(idx, HEAD_DIM - 1)


def _block_ones(n):
    return (_head_of(_iota((n, n), 0)) == _head_of(_iota((n, n), 1)))


def _head_sum(x):
    n = x.shape[-1]
    return _dot_exact_rhs(x, _block_ones(n).astype(BF16), terms=1)


def _head_rms(x, g_row):
    ms = _head_sum(x * x) * (1.0 / HEAD_DIM)
    return x * lax.rsqrt(ms + EPS) * g_row


def _rms_mod(x, g_row, scale_row, shift_row):
    y = x * lax.rsqrt(jnp.mean(x * x, axis=-1, keepdims=True) + EPS) * g_row
    return y * (1.0 + scale_row) + shift_row


def _softplus(x):
    return jnp.maximum(x, 0.0) + jnp.log1p(jnp.exp(-jnp.abs(x)))


def _log_sigmoid(x):
    return -_softplus(-x)


def _sigmoid(x):
    return jax.nn.sigmoid(x)


def _silu(x):
    return x * jax.nn.sigmoid(x)


def _bd3(x, mask_bf):
    return jnp.concatenate([x] * N_HEADS, axis=1) * mask_bf


def _to_block_diag(x4):
    spread = (_iota((HEAD_DIM, SCAN_W), 0) == _in_head(_iota((HEAD_DIM, SCAN_W), 1))).astype(BF16)
    rows = x4.reshape(SCAN_W, HEAD_DIM)
    return jnp.where(_block_ones(SCAN_W), _dot_exact_rhs(rows, spread, terms=3), 0.0)


def _from_block_diag(x):
    fold = (_in_head(_iota((SCAN_W, HEAD_DIM), 0)) == _iota((SCAN_W, HEAD_DIM), 1)).astype(BF16)
    return _dot_exact_rhs(x, fold, terms=3).reshape(N_HEADS, HEAD_DIM, HEAD_DIM)


def _expand_gate(gates, col0):
    sel = (_iota((GATE_W, SCAN_W), 0) == col0 + _head_of(_iota((GATE_W, SCAN_W), 1))).astype(BF16)
    return _dot_exact_rhs(gates, sel, terms=3)


def _chunk_scan(x, op, ident, rev):
    rows = x.shape[0]
    pos = _in_head(_iota(x.shape, 0))
    s = 1
    while s < CHUNK:
        if rev:
            y = pltpu.roll(x, rows - s, 0)
            ok = pos < CHUNK - s
        else:
            y = pltpu.roll(x, s, 0)
            ok = pos >= s
        x = op(x, jnp.where(ok, y, ident))
        s *= 2
    return x


def _chunks(a):
    return a.reshape(N_CHUNK, CHUNK, SCAN_W)


def _problem_masks():
    shape = (N_DIR * N_CHUNK, CHUNK, SCAN_W)
    t_idx, j_idx = _iota(shape, 1), _in_head(_iota(shape, 2))
    ahead = jnp.where(_iota(shape, 0) >= N_CHUNK, j_idx - t_idx, t_idx - j_idx)
    return t_idx == j_idx, ahead >= 0, ahead > 0


def _scan_order(d):
    return range(N_CHUNK - 1, -1, -1) if d else range(N_CHUNK)


def _mod_kernel(cond_ref, w_ref, b_ref, o_ref):
    a = _silu(cond_ref[...])
    o_ref[0, 0] = _dot(a, w_ref[0]) + b_ref[0]


def _modulation(cond, ada_w, ada_b):
    depth = ada_w.shape[0]
    return pl.pallas_call(
        _mod_kernel,
        grid=(depth, N_MOD),
        in_specs=[
            pl.BlockSpec((MOD_ROWS, D_MODEL), lambda l, j: (0, 0)),
            pl.BlockSpec((1, D_MODEL, D_MODEL), lambda l, j: (l, 0, j)),
            pl.BlockSpec((1, 1, D_MODEL), lambda l, j: (l, 0, j)),
        ],
        out_specs=pl.BlockSpec((1, 1, MOD_ROWS, D_MODEL), lambda l, j: (l, j, 0, 0)),
        out_shape=jax.ShapeDtypeStruct((depth, N_MOD, MOD_ROWS, D_MODEL), F32),
        compiler_params=pltpu.CompilerParams(
            dimension_semantics=("arbitrary", "arbitrary"), vmem_limit_bytes=VMEM_LIMIT),
        name="modulation",
    )(cond, ada_w, ada_b.reshape(depth, 1, N_MOD * D_MODEL))


def _ffn(h, w_in_ref, w_out_ref):
    n_chunks = D_FF // FF_CHUNK

    def gate_up(c):
        lo = c * FF_CHUNK
        g = jnp.dot(h, w_in_ref[0, :, lo:lo + FF_CHUNK], preferred_element_type=F32)
        u = jnp.dot(h, w_in_ref[0, :, D_FF + lo:D_FF + lo + FF_CHUNK], preferred_element_type=F32)
        return g, u

    acc = None
    nxt = gate_up(0)
    for c in range(n_chunks):
        g, u = nxt
        if c + 1 < n_chunks:
            nxt = gate_up(c + 1)
        a = _bf(_silu(g) * u)
        d = jnp.dot(a, w_out_ref[0, c * FF_CHUNK:(c + 1) * FF_CHUNK, :], preferred_element_type=F32)
        acc = d if acc is None else acc + d
    return acc


def _rope(x, cos, sin_signed):
    w = x.shape[-1]
    reps = w // cos.shape[-1]
    if reps > 1:
        cos = jnp.concatenate([cos] * reps, axis=-1)
        sin_signed = jnp.concatenate([sin_signed] * reps, axis=-1)
    first = jnp.bitwise_and(_iota(x.shape, 1), 31) < 16
    partner = jnp.where(first, pltpu.roll(x, w - 16, 1), pltpu.roll(x, 16, 1))
    return x * cos + partner * sin_signed


class _Tiles:
    def __init__(self, n_ctx, t_lat):
        assert n_ctx % TM == 0 and t_lat % TM == 0
        self.n_ctx = n_ctx // TM
        self.per_seq = t_lat // TM

    def is_ctx(self, i):
        return i < self.n_ctx

    def lat(self, i):
        return jnp.maximum(i - self.n_ctx, 0)

    def mod_row(self, i):
        return jnp.where(self.is_ctx(i), 0, 1 + self.lat(i) // self.per_seq)


def _mod_rows(mod_ref, row):
    return lambda m: mod_ref[0, m, pl.ds(row, 1), :]


def _dense1_kernel(xa_ref, xb_ref, mod_ref, ng_ref, wfi_ref, wfo_ref, wz_ref, qn_ref, kn_ref, cos_ref, sin_ref,
                   x1_ref, zm_ref, zg_ref, aq_ref, ak_ref, av_ref, gate_ref, *, tiles):
    i = pl.program_id(0)
    mod = _mod_rows(mod_ref, tiles.mod_row(i))
    is_ctx = tiles.is_ctx(i)
    for r in range(TM // ROW_GROUP):
        rows = slice(r * ROW_GROUP, (r + 1) * ROW_GROUP)
        x = jnp.where(is_ctx, xa_ref[rows, :], xb_ref[rows, :])
        h = _bf(_rms_mod(x, ng_ref[0, 0:1], mod(0), mod(1)))
        x1 = x + 0.5 * mod(2) * _ffn(h, wfi_ref, wfo_ref)
        x1_ref[rows, :] = x1
        h2 = _bf(_rms_mod(x1, ng_ref[0, 1:2], mod(3), mod(4)))

        def proj(lo, width):
            return jnp.dot(h2, wz_ref[0, :, lo:lo + width], preferred_element_type=F32)

        zm_ref[rows, :] = proj(0, D_MODEL)
        zg_ref[rows, :] = proj(D_MODEL, D_MODEL)
        cos = cos_ref[rows, :]
        sin = sin_ref[rows, :]
        off = 2 * D_MODEL
        q = _head_rms(proj(off, A_WIDTH), qn_ref[0])
        aq_ref[rows, :] = _rope(q, cos, sin) * (1.0 / math.sqrt(HEAD_DIM))
        k = _head_rms(proj(off + A_WIDTH, KV_WIDTH), kn_ref[0])
        ak_ref[rows, :] = _rope(k, cos, sin)
        av_ref[rows, :] = proj(off + A_WIDTH + KV_WIDTH, KV_WIDTH)
        gate_ref[rows, :] = proj(off + A_WIDTH + 2 * KV_WIDTH, GATE_W)


def _resident(shape, index_map):
    return pl.BlockSpec(shape, index_map, pipeline_mode=pl.Buffered(1))


def _dense1(layer, tiles, n_tok, x_ctx, x_lat, lat_tile0, mod, norm_g, wfi, wfo, wz, qn, kn, cos_t, sin_t):
    row = lambda w: pl.BlockSpec((TM, w), lambda i: (i, 0))
    rope_tile = lambda i: jnp.where(tiles.is_ctx(i), tiles.per_seq, tiles.lat(i) % tiles.per_seq)
    rope = pl.BlockSpec((TM, KV_WIDTH), lambda i: (rope_tile(i), 0))
    out_w = (D_MODEL, D_MODEL, D_MODEL, A_WIDTH, KV_WIDTH, KV_WIDTH, GATE_W)
    return pl.pallas_call(
        functools.partial(_dense1_kernel, tiles=tiles),
        grid=(n_tok // TM,),
        in_specs=[
            pl.BlockSpec((TM, D_MODEL), lambda i: (jnp.minimum(i, tiles.n_ctx - 1), 0)),
            pl.BlockSpec((TM, D_MODEL), lambda i: (tiles.lat(i) + lat_tile0, 0)),
            pl.BlockSpec((1, N_MOD, MOD_ROWS, D_MODEL), lambda i: (layer, 0, 0, 0)),
            pl.BlockSpec((1, 3, D_MODEL), lambda i: (layer, 0, 0)),
            _resident((1, D_MODEL, 2 * D_FF), lambda i: (2 * layer, 0, 0)),
            _resident((1, D_FF, D_MODEL), lambda i: (2 * layer, 0, 0)),
            _resident((1, D_MODEL, Z_WIDTH), lambda i: (layer, 0, 0)),
            pl.BlockSpec((1, 1, A_WIDTH), lambda i: (layer, 0, 0)),
            pl.BlockSpec((1, 1, KV_WIDTH), lambda i: (layer, 0, 0)),
            rope, rope,
        ],
        out_specs=[row(w) for w in out_w],
        out_shape=[jax.ShapeDtypeStruct((n_tok, w), F32) for w in out_w],
        compiler_params=pltpu.CompilerParams(
            dimension_semantics=("arbitrary",), vmem_limit_bytes=VMEM_LIMIT),
        name="dense1",
    )(x_ctx, x_lat, mod, norm_g, wfi, wfo, wz, qn, kn, cos_t, sin_t)


def _dense2_kernel(x_ref, mod_ref, ng_ref, hf_ref, hb_ref, mo_ref, mn_ref, of_ref, ob_ref, gz_ref, gn_ref,
                   ao_ref, wo_ref, wfi_ref, wfo_ref, fin_ref, *out_refs, tiles, final):
    i = pl.program_id(0)
    mod = _mod_rows(mod_ref, tiles.mod_row(i))
    m_out = _head_rms(hf_ref[...] + hb_ref[...], mn_ref[0]) * _sigmoid(mo_ref[...])
    g_out = _head_rms(of_ref[...] + ob_ref[...], gn_ref[0]) * _silu(gz_ref[...])
    mix = jnp.dot(_bf(m_out), wo_ref[0, 0:SCAN_W, :], preferred_element_type=F32)
    mix += jnp.dot(_bf(g_out), wo_ref[0, SCAN_W:2 * SCAN_W, :], preferred_element_type=F32)
    mix += jnp.dot(_bf(ao_ref[...]), wo_ref[0, 2 * SCAN_W:, :], preferred_element_type=F32)
    x2 = x_ref[...] + mod(5) * mix
    h = _bf(_rms_mod(x2, ng_ref[0, 2:3], mod(6), mod(7)))
    x3 = x2 + 0.5 * mod(8) * _ffn(h, wfi_ref, wfo_ref)
    if not final:
        out_refs[0][...] = x3
        return
    y = x3 * lax.rsqrt(jnp.mean(x3 * x3, axis=-1, keepdims=True) + EPS) * fin_ref[...]
    y_ctx_ref, y_lat_ref = out_refs

    @pl.when(tiles.is_ctx(i))
    def _():
        y_ctx_ref[...] = y

    @pl.when(jnp.logical_not(tiles.is_ctx(i)))
    def _():
        y_lat_ref[...] = y


def _dense2(layer, tiles, x1, mod, norm_g, hf, hb, zm, m_norm, of, ob, zg, g_norm, a_out, wo, wfi, wfo, fin,
            final):
    n_tok = x1.shape[0]
    row = lambda w: pl.BlockSpec((TM, w), lambda i: (i, 0))
    last_quarter = pl.BlockSpec((TM, SCAN_W), lambda i: (i, 3))
    lane_row = lambda w: pl.BlockSpec((1, 1, w), lambda i: (layer, 0, 0))
    if final:
        n_ctx = tiles.n_ctx * TM
        out_specs = [pl.BlockSpec((TM, D_MODEL), lambda i: (jnp.minimum(i, tiles.n_ctx - 1), 0)),
                     pl.BlockSpec((TM, D_MODEL), lambda i: (tiles.lat(i), 0))]
        out_shape = [jax.ShapeDtypeStruct((n_ctx, D_MODEL), F32),
                     jax.ShapeDtypeStruct((n_tok - n_ctx, D_MODEL), F32)]
    else:
        out_specs, out_shape = row(D_MODEL), jax.ShapeDtypeStruct((n_tok, D_MODEL), F32)
    return pl.pallas_call(
        functools.partial(_dense2_kernel, tiles=tiles, final=final),
        grid=(n_tok // TM,),
        in_specs=[
            row(D_MODEL),
            pl.BlockSpec((1, N_MOD, MOD_ROWS, D_MODEL), lambda i: (layer, 0, 0, 0)),
            pl.BlockSpec((1, 3, D_MODEL), lambda i: (layer, 0, 0)),
            row(SCAN_W), row(SCAN_W), last_quarter, lane_row(SCAN_W),
            row(SCAN_W), row(SCAN_W), last_quarter, lane_row(SCAN_W),
            row(A_WIDTH),
            _resident((1, D_MODEL, D_MODEL), lambda i: (layer, 0, 0)),
            _resident((1, D_MODEL, 2 * D_FF), lambda i: (2 * layer + 1, 0, 0)),
            _resident((1, D_FF, D_MODEL), lambda i: (2 * layer + 1, 0, 0)),
            pl.BlockSpec((1, D_MODEL), lambda i: (0, 0)),
        ],
        out_specs=out_specs,
        out_shape=out_shape,
        compiler_params=pltpu.CompilerParams(
            dimension_semantics=("arbitrary",), vmem_limit_bytes=VMEM_LIMIT),
        name="dense2",
    )(x1, mod, norm_g, hf, hb, zm, m_norm, of, ob, zg, g_norm, a_out, wo, wfi, wfo, fin)


class _Segs:
    def __init__(self, n_ctx, t_ctx, n_lat_seq, t_lat):
        assert t_ctx == SEG and t_lat % SEG == 0
        self.n_ctx_seg = n_ctx // SEG
        self.per_seq = t_lat // SEG
        self.n_lat_seq = n_lat_seq
        self.n_seg = self.n_ctx_seg + n_lat_seq * self.per_seq
        self.n_seq = self.n_ctx_seg + n_lat_seq

    def is_ctx(self, s):
        return s < self.n_ctx_seg

    def lat(self, s):
        r = jnp.maximum(s - self.n_ctx_seg, 0)
        return r // self.per_seq, r % self.per_seq

    def blk(self, s, rev):
        b, j = self.lat(s)
        j = self.per_seq - 1 - j if rev else j
        return jnp.where(self.is_ctx(s), s, self.n_ctx_seg + b * self.per_seq + j)

    def seq(self, s):
        return jnp.where(self.is_ctx(s), s, self.n_ctx_seg + self.lat(s)[0])

    def lat_seq(self, s):
        return self.lat(s)[0]

    def first(self, s):
        return jnp.logical_or(self.is_ctx(s), self.lat(s)[1] == 0)

    def last(self, s):
        return jnp.logical_or(self.is_ctx(s), self.lat(s)[1] == self.per_seq - 1)


_STATE_MAT = (N_DIR, N_HEADS, HEAD_DIM, HEAD_DIM)


def _state_specs(layer, segs):
    s_in = pl.BlockSpec((1, 1) + _STATE_MAT, lambda s: (segs.lat_seq(s), layer, 0, 0, 0, 0))
    s_out = pl.BlockSpec((1,) + _STATE_MAT, lambda s: (segs.seq(s), 0, 0, 0, 0))
    return s_in, s_out, jax.ShapeDtypeStruct((segs.n_seq,) + _STATE_MAT, F32)


def _mlstm_gates(gate_ref, fb_row, rev, d):
    gates = gate_ref[...]
    ig = _expand_gate(gates, d * N_HEADS)
    lf = _log_sigmoid(_expand_gate(gates, 8 + d * N_HEADS) + fb_row)
    b = _chunk_scan(lf, jnp.add, 0.0, rev)
    r = ig - b
    cm = _chunk_scan(r, jnp.maximum, -jnp.inf, rev)
    edge = 0 if rev else CHUNK - 1
    b = _chunks(b)
    bl = b[:, edge:edge + 1, :]
    lw = bl - b + _chunks(ig)
    return dict(b=b, r=_chunks(r), cm=_chunks(cm), bl=bl, lw=lw, lw_max=jnp.max(lw, axis=1, keepdims=True))


def _mlstm_kernel(zmf_ref, zmb_ref, gf_ref, gb_ref, fb_ref, c0_ref, n0_ref, m0_ref,
                  hf_ref, hb_ref, c_out, n_out, m_out, c_s, n_s, m_s, *, segs):
    s = pl.program_id(0)

    @pl.when(segs.first(s))
    def _():
        ctx = segs.is_ctx(s)
        for d in range(N_DIR):
            c_s[d] = jnp.where(ctx, 0.0, _to_block_diag(c0_ref[0, 0, d]))
            n_s[d] = jnp.where(ctx, 0.0, n0_ref[0, 0, d])
            m_s[d] = jnp.where(ctx, 0.0, m0_ref[0, 0, d])

    g = [_mlstm_gates(gf_ref, fb_ref[0, 0:1], False, 0), _mlstm_gates(gb_ref, fb_ref[0, 1:2], True, 1)]
    zm = (zmf_ref, zmb_ref)

    m_at = [[None] * N_CHUNK for _ in range(N_DIR)]
    m_next = [[None] * N_CHUNK for _ in range(N_DIR)]
    m_fin = []
    for d in range(N_DIR):
        m_row = m_s[d]
        for c in _scan_order(d):
            m_at[d][c] = m_row
            m_row = jnp.maximum(g[d]["bl"][c] + m_row, g[d]["lw_max"][c])
            m_next[d][c] = m_row
        m_fin.append(m_row)

    cat = lambda name: jnp.concatenate([g[0][name], g[1][name]], axis=0)
    rows_of = lambda xs: jnp.concatenate([x[None] for d in range(N_DIR) for x in xs[d]], axis=0)
    b, r, cm, bl, lw = cat("b"), cat("r"), cat("cm"), cat("bl"), cat("lw")
    m_c, m_n = rows_of(m_at), rows_of(m_next)
    col = lambda j: jnp.concatenate([_chunks(ref[:, j * SCAN_W:(j + 1) * SCAN_W]) for ref in zm], axis=0)
    q, k, v = col(0), col(1) * (1.0 / math.sqrt(HEAD_DIM)), col(2)

    diag, tri, _ = _problem_masks()
    mask_bd = _block_ones(SCAN_W)
    mask_bf = mask_bd.astype(BF16)

    mx = jnp.maximum(m_c, cm)
    r_row = jnp.sum(jnp.where(diag, r, 0.0), axis=1, keepdims=True)
    w = jnp.exp(jnp.where(tri, r_row - mx, -jnp.inf))
    a_int = jnp.exp(m_c - mx)
    sc = _bdot_nt(_bf(q), _bd3(_bf(k), mask_bf)) * w
    kw = jnp.exp(lw - m_n) * k
    dec = jnp.exp(bl + m_c - m_n)
    kw_sum = jnp.sum(kw, axis=1, keepdims=True)

    c_at = [None] * (N_DIR * N_CHUNK)
    n_at = [None] * (N_DIR * N_CHUNK)
    c_fin, n_fin = [], []
    for d in range(N_DIR):
        c_bd, n_row = c_s[d], n_s[d]
        for c in _scan_order(d):
            p = d * N_CHUNK + c
            c_at[p], n_at[p] = c_bd, n_row
            c_bd = c_bd * dec[p] + jnp.where(mask_bd, _dot(kw[p].T, v[p]), 0.0)
            n_row = n_row * dec[p] + kw_sum[p]
        c_fin.append(c_bd)
        n_fin.append(n_row)
    c_all = jnp.concatenate([_bf(x)[None] for x in c_at], axis=0)
    n_all = jnp.concatenate([x[None] for x in n_at], axis=0)

    num = _bdot(_bf(sc), _bd3(_bf(v), mask_bf)) + a_int * _bdot(_bf(q), c_all)
    flat = lambda a: a.reshape(N_DIR * SEG, SCAN_W)
    sums = _dot_exact_rhs(jnp.concatenate([flat(sc), flat(q * n_all)], axis=0), mask_bf)
    den = sums[0:N_DIR * SEG].reshape(sc.shape) + a_int * sums[N_DIR * SEG:].reshape(sc.shape)
    h = num / jnp.maximum(jnp.abs(den), jnp.exp(-(b + mx)))
    hf_ref[...] = h[0:N_CHUNK].reshape(SEG, SCAN_W)
    hb_ref[...] = h[N_CHUNK:].reshape(SEG, SCAN_W)

    for d in range(N_DIR):
        c_s[d] = c_fin[d]
        n_s[d] = n_fin[d]
        m_s[d] = m_fin[d]

    @pl.when(segs.last(s))
    def _():
        for d in range(N_DIR):
            c_out[0, d] = _from_block_diag(c_fin[d])
            n_out[0, d] = n_fin[d]
            m_out[0, d] = m_fin[d]


def _mlstm(layer, segs, zm, gates, fb_rows, state_c, n0, m0):
    n_tok = zm.shape[0]
    fwd = lambda w: pl.BlockSpec((SEG, w), lambda s: (segs.blk(s, False), 0))
    bwd = lambda w: pl.BlockSpec((SEG, w), lambda s: (segs.blk(s, True), 0))
    row_in = pl.BlockSpec((1, 1, N_DIR, 1, SCAN_W), lambda s: (segs.lat_seq(s), layer, 0, 0, 0))
    row_out = pl.BlockSpec((1, N_DIR, 1, SCAN_W), lambda s: (segs.seq(s), 0, 0, 0))
    row_shape = jax.ShapeDtypeStruct((segs.n_seq, N_DIR, 1, SCAN_W), F32)
    mat_in, mat_out, mat_shape = _state_specs(layer, segs)
    return pl.pallas_call(
        functools.partial(_mlstm_kernel, segs=segs),
        grid=(segs.n_seg,),
        in_specs=[
            fwd(D_MODEL), bwd(D_MODEL), fwd(GATE_W), bwd(GATE_W),
            pl.BlockSpec((1, N_DIR, SCAN_W), lambda s: (layer, 0, 0)),
            mat_in, row_in, row_in,
        ],
        out_specs=[fwd(SCAN_W), bwd(SCAN_W), mat_out, row_out, row_out],
        out_shape=[jax.ShapeDtypeStruct((n_tok, SCAN_W), F32)] * 2 + [mat_shape, row_shape, row_shape],
        scratch_shapes=[pltpu.VMEM((N_DIR, SCAN_W, SCAN_W), F32),
                        pltpu.VMEM((N_DIR, 1, SCAN_W), F32),
                        pltpu.VMEM((N_DIR, 1, SCAN_W), F32)],
        compiler_params=pltpu.CompilerParams(
            dimension_semantics=("arbitrary",), vmem_limit_bytes=VMEM_LIMIT),
        name="mlstm",
    )(zm, zm, gates, gates, fb_rows, state_c, n0, m0)


def _short_conv(prev_ref, cur_ref, next_ref, w_ref, has_prev, has_next):
    w3 = 3 * SCAN_W
    prev = jnp.where(has_prev, prev_ref[:, 0:w3], 0.0)
    nxt = jnp.where(has_next, next_ref[:, 0:w3], 0.0)
    xp = jnp.concatenate([prev, cur_ref[:, 0:w3], nxt], axis=0)
    rows = xp.shape[0]
    acc = None
    for i in range(CONV_W):
        shift = (CONV_W // 2 - i) % rows
        y = xp if shift == 0 else pltpu.roll(xp, shift, 0)
        t = y[HALO:HALO + SEG] * w_ref[0, i:i + 1, :]
        acc = t if acc is None else acc + t
    return acc


def _neumann_inverse(n_all, mask_bf):
    eye = (_iota(n_all.shape[1:], 0) == _in_head(_iota(n_all.shape[1:], 1))).astype(F32)
    p = -n_all
    t = eye + p
    levels = 6
    for lvl in range(levels):
        first, last = lvl == 0, lvl == levels - 1
        p_hi, p_lo = _split(p, 2)
        w_hi, w_lo = _bd3(p_hi, mask_bf), _bd3(p_lo, mask_bf)
        lhs_hi, lhs_lo = [], []
        if not first:
            t_hi, t_lo = _split(t, 2)
            lhs_hi += [t_hi, t_lo]
            lhs_lo += [t_hi]
        if not last:
            lhs_hi += [p_hi, p_lo]
            lhs_lo += [p_hi]
        a = _bdot(jnp.concatenate(lhs_hi, axis=1), w_hi)
        b = _bdot(jnp.concatenate(lhs_lo, axis=1) if len(lhs_lo) > 1 else lhs_lo[0], w_lo)
        ra, rb = 0, 0
        if not first:
            t = t + (a[:, 0:CHUNK] + a[:, CHUNK:2 * CHUNK] + b[:, 0:CHUNK])
            ra, rb = 2 * CHUNK, CHUNK
        if not last:
            p = a[:, ra:ra + CHUNK] + a[:, ra + CHUNK:ra + 2 * CHUNK] + b[:, rb:rb + CHUNK]
    return t


def _delta_prep(prev_ref, cur_ref, next_ref, gate_ref, cw_ref, alog_row, dtb_row, has_prev, has_next, rev, d):
    qkv = _silu(_short_conv(prev_ref, cur_ref, next_ref, cw_ref, has_prev, has_next))

    def l2(a):
        return a * lax.rsqrt(_head_sum(a * a) + EPS)

    q = l2(qkv[:, 0:SCAN_W]) * (1.0 / math.sqrt(HEAD_DIM))
    k = l2(qkv[:, SCAN_W:2 * SCAN_W])
    v = qkv[:, 2 * SCAN_W:3 * SCAN_W]
    gates = gate_ref[...]
    g = -jnp.exp(alog_row) * _softplus(_expand_gate(gates, 16 + d * N_HEADS) + dtb_row)
    beta = _sigmoid(_expand_gate(gates, 24 + d * N_HEADS))
    gc = _chunks(_chunk_scan(g, jnp.add, 0.0, rev))
    edge = 0 if rev else CHUNK - 1
    gl = gc[:, edge:edge + 1, :]
    eg = jnp.exp(gc)
    kb = _chunks(k * beta)
    return dict(q=_chunks(q), k=_chunks(k), kb=kb, vb=_chunks(v * beta), kbe=kb * eg, qd=_chunks(q) * eg,
                kd=_chunks(k) * jnp.exp(gl - gc), gc=gc, gl=jnp.exp(gl))


def _delta_kernel(pf_ref, cf_ref, nf_ref, pb_ref, cb_ref, nb_ref, gf_ref, gb_ref, cw_ref, al_ref, dt_ref,
                  s0_ref, of_ref, ob_ref, s_out, s_s, *, segs):
    s = pl.program_id(0)
    lat = jnp.logical_not(segs.is_ctx(s))
    j = segs.lat(s)[1]

    @pl.when(segs.first(s))
    def _():
        for d in range(N_DIR):
            s_s[d] = jnp.where(lat, _to_block_diag(s0_ref[0, 0, d]), 0.0)

    inner_lo = jnp.logical_and(lat, j > 0)
    inner_hi = jnp.logical_and(lat, j < segs.per_seq - 1)
    fwd = _delta_prep(pf_ref, cf_ref, nf_ref, gf_ref, cw_ref, al_ref[0, 0:1], dt_ref[0, 0:1],
                      inner_lo, inner_hi, False, 0)
    bwd = _delta_prep(pb_ref, cb_ref, nb_ref, gb_ref, cw_ref, al_ref[0, 1:2], dt_ref[0, 1:2],
                      inner_hi, inner_lo, True, 1)
    z = {name: jnp.concatenate([fwd[name], bwd[name]], axis=0) for name in fwd}

    mask_bd = _block_ones(SCAN_W)
    mask_bf = mask_bd.astype(BF16)
    diag, tri, strict = _problem_masks()
    neg_diag = jnp.sum(jnp.where(diag, -z["gc"], 0.0), axis=1, keepdims=True)
    decay = jnp.exp(jnp.where(tri, z["gc"] + neg_diag, -jnp.inf))
    k_bd = _bd3(_bf(z["k"]), mask_bf)
    kq = _bdot_nt(_bf(jnp.concatenate([z["kb"], z["q"]], axis=1)), k_bd)
    qk = _bf(kq[:, CHUNK:] * decay)
    t_all = _bf(_neumann_inverse(jnp.where(strict, kq[:, 0:CHUNK] * decay, 0.0), mask_bf))
    u = _bdot(t_all, _bd3(_bf(z["vb"]), mask_bf))
    w = _bdot(t_all, _bd3(_bf(z["kbe"]), mask_bf))
    wq = _bf(jnp.concatenate([w, z["qd"]], axis=1))

    outs = (of_ref, ob_ref)
    state = [s_s[0], s_s[1]]
    for i in range(N_CHUNK):
        for d in range(N_DIR):
            c = N_CHUNK - 1 - i if d else i
            p = d * N_CHUNK + c
            s_bd = state[d]
            ws = jnp.dot(wq[p], _bf(s_bd), preferred_element_type=F32)
            v_new = u[p] - ws[0:CHUNK]
            v_bd = jnp.concatenate([_bf(v_new)] * N_HEADS, axis=0) * mask_bf
            outs[d][c * CHUNK:(c + 1) * CHUNK, :] = ws[CHUNK:] + jnp.dot(qk[p], v_bd,
                                                                        preferred_element_type=F32)
            state[d] = s_bd * z["gl"][p] + jnp.where(mask_bd, _dot(z["kd"][p].T, v_new), 0.0)
    s_s[0] = state[0]
    s_s[1] = state[1]

    @pl.when(segs.last(s))
    def _():
        for d in range(N_DIR):
            s_out[0, d] = _from_block_diag(state[d])


def _delta(layer, segs, zg, gates, conv_w, alog_rows, dtb_rows, state_s):
    n_tok = zg.shape[0]
    per = SEG // HALO
    n_halo = n_tok // HALO

    def cur(rev, w):
        return pl.BlockSpec((SEG, w), lambda s: (segs.blk(s, rev), 0))

    def prev(rev):
        return pl.BlockSpec((HALO, D_MODEL), lambda s: (jnp.maximum(segs.blk(s, rev) * per - 1, 0), 0))

    def nxt(rev):
        return pl.BlockSpec((HALO, D_MODEL),
                            lambda s: (jnp.minimum((segs.blk(s, rev) + 1) * per, n_halo - 1), 0))

    lane_rows = pl.BlockSpec((1, N_DIR, SCAN_W), lambda s: (layer, 0, 0))
    mat_in, mat_out, mat_shape = _state_specs(layer, segs)
    return pl.pallas_call(
        functools.partial(_delta_kernel, segs=segs),
        grid=(segs.n_seg,),
        in_specs=[
            prev(False), cur(False, D_MODEL), nxt(False),
            prev(True), cur(True, D_MODEL), nxt(True),
            cur(False, GATE_W), cur(True, GATE_W),
            pl.BlockSpec((1, CONV_W, 3 * SCAN_W), lambda s: (layer, 0, 0)),
            lane_rows, lane_rows, mat_in,
        ],
        out_specs=[cur(False, SCAN_W), cur(True, SCAN_W), mat_out],
        out_shape=[jax.ShapeDtypeStruct((n_tok, SCAN_W), F32)] * 2 + [mat_shape],
        scratch_shapes=[pltpu.VMEM((N_DIR, SCAN_W, SCAN_W), F32)],
        compiler_params=pltpu.CompilerParams(
            dimension_semantics=("arbitrary",), vmem_limit_bytes=VMEM_LIMIT),
        name="delta",
    )(zg, zg, zg, zg, zg, zg, gates, gates, conv_w, alog_rows, dtb_rows, state_s)


def _attend(q_ref, kv_blocks, o_ref):
    qt = q_ref[...].T
    zeros = jnp.zeros((HEAD_DIM, A_GROUP * SEG), BF16)
    ones = lambda n: jnp.ones((16, n), F32)
    kv = [(_bf(k), _bf(jnp.concatenate([v.T, ones(v.shape[0])], axis=0))) for k, v in kv_blocks]
    group_out = []
    for g in range(A_KV_HEADS):
        heads = range(g * A_GROUP, (g + 1) * A_GROUP)
        slab = _bf(jnp.concatenate([qt[h * HEAD_DIM:(h + 1) * HEAD_DIM, :] for h in heads], axis=1))
        w_g = jnp.concatenate([slab if i == g else zeros for i in range(A_KV_HEADS)], axis=0)
        m = acc = None
        nxt = jnp.dot(kv[0][0], w_g, preferred_element_type=F32)
        for i, (_, v_t) in enumerate(kv):
            sc = nxt
            if i + 1 < len(kv):
                nxt = jnp.dot(kv[i + 1][0], w_g, preferred_element_type=F32)
            m_blk = jnp.max(sc, axis=0, keepdims=True)
            m_new = m_blk if m is None else jnp.maximum(m, m_blk)
            pv = jnp.dot(v_t, _bf(jnp.exp(sc - m_new)), preferred_element_type=F32)
            acc = pv if m is None else acc * jnp.exp(m - m_new) + pv
            m = m_new
        lo = g * HEAD_DIM
        group_out.append(acc[lo:lo + HEAD_DIM] / acc[KV_WIDTH:KV_WIDTH + 1])
    for col in range(A_WIDTH // KV_WIDTH):
        g, h0 = (2 * col) // A_GROUP, (2 * col) % A_GROUP
        pair = jnp.concatenate([group_out[g][:, h0 * SEG:(h0 + 1) * SEG],
                                group_out[g][:, (h0 + 1) * SEG:(h0 + 2) * SEG]], axis=0)
        o_ref[:, col * KV_WIDTH:(col + 1) * KV_WIDTH] = pair.T


def _attn_kernel(q_ref, kc_ref, vc_ref, kl_ref, vl_ref, ck_ref, cv_ref, o_ref, *, segs):
    s = pl.program_id(0)

    @pl.when(segs.is_ctx(s))
    def _():
        _attend(q_ref, [(kc_ref[...], vc_ref[...])], o_ref)

    @pl.when(jnp.logical_not(segs.is_ctx(s)))
    def _():
        blocks = []
        for lo in range(0, ck_ref.shape[2], KV_BLOCK):
            blocks.append((ck_ref[0, 0, lo:lo + KV_BLOCK, :], cv_ref[0, 0, lo:lo + KV_BLOCK, :]))
        for lo in range(0, kl_ref.shape[0], KV_BLOCK):
            blocks.append((kl_ref[lo:lo + KV_BLOCK, :], vl_ref[lo:lo + KV_BLOCK, :]))
        _attend(q_ref, blocks, o_ref)


def _attention(layer, segs, aq, ak, av, cache_k, cache_v):
    n_tok = aq.shape[0]
    t_lat = segs.per_seq * SEG
    past = cache_k.shape[2]
    assert t_lat % KV_BLOCK == 0
    lat_blocks_before = segs.n_ctx_seg * SEG // t_lat
    own = lambda w: pl.BlockSpec((SEG, w), lambda s: (s, 0))
    lat_kv = pl.BlockSpec((t_lat, KV_WIDTH), lambda s: (lat_blocks_before + segs.lat_seq(s), 0))
    cache = pl.BlockSpec((1, 1, past, KV_WIDTH), lambda s: (segs.lat_seq(s), layer, 0, 0))
    return pl.pallas_call(
        functools.partial(_attn_kernel, segs=segs),
        grid=(segs.n_seg,),
        in_specs=[own(A_WIDTH), own(KV_WIDTH), own(KV_WIDTH), lat_kv, lat_kv, cache, cache],
        out_specs=own(A_WIDTH),
        out_shape=jax.ShapeDtypeStruct((n_tok, A_WIDTH), F32),
        compiler_params=pltpu.CompilerParams(
            dimension_semantics=("arbitrary",), vmem_limit_bytes=VMEM_LIMIT),
        name="attention",
    )(aq, ak, av, ak, av, cache_k, cache_v)


def _permute_w_in(w_in):
    m, g, a, kv = SCAN_W, SCAN_W, A_WIDTH, KV_WIDTH
    sizes = (m, m, m, m, 8, 8, 3 * g, g, 8, 8, a, kv, kv)
    offs = [0]
    for sz in sizes:
        offs.append(offs[-1] + sz)
    piece = lambda i: w_in[..., offs[i]:offs[i + 1]]
    pad = jnp.zeros(w_in.shape[:-1] + (GATE_W - 32,), w_in.dtype)
    order = [0, 1, 2, 3, 6, 7, 10, 11, 12, 4, 5, 8, 9]
    return jnp.concatenate([piece(i) for i in order] + [pad], axis=-1)


def _rope_tables(t_lat):
    rows = t_lat // GRID_W
    row = np.repeat(np.arange(rows, dtype=np.float64), GRID_W)
    col = np.tile(np.arange(GRID_W, dtype=np.float64), rows)
    n_freq = HEAD_DIM // 4
    inv = ROPE_BASE ** (-np.arange(n_freq, dtype=np.float64) / n_freq)
    ang = np.stack([row[:, None] * inv, col[:, None] * inv], axis=1)
    cos, sin = np.cos(ang), np.sin(ang)
    cos_h = np.concatenate([cos, cos], axis=-1).reshape(t_lat, HEAD_DIM)
    sin_h = np.concatenate([-sin, sin], axis=-1).reshape(t_lat, HEAD_DIM)
    wide = lambda a: np.tile(a, (1, KV_WIDTH // HEAD_DIM))
    cos_t = np.concatenate([wide(cos_h), np.ones((TM, KV_WIDTH))], axis=0)
    sin_t = np.concatenate([wide(sin_h), np.zeros((TM, KV_WIDTH))], axis=0)
    return jnp.asarray(cos_t, F32), jnp.asarray(sin_t, F32)


def _head_lanes(x):
    return jnp.repeat(x, HEAD_DIM, axis=-1)


def kernel(x_prompt, x_sample, c, cache_k, cache_v, state_mlstm_C, state_mlstm_n, state_mlstm_m,
           state_delta_S, c_ctx, ada_w, ada_b, norm_g, ffn_w_in, ffn_w_out, w_in, w_out, mlstm_f_bias,
           mlstm_norm, delta_conv, delta_a_log, delta_dt_bias, delta_norm, attn_q_norm, attn_k_norm,
           final_norm):
    batch, t_ctx, d_model = x_prompt.shape
    n_lat_seq, t_lat, _ = x_sample.shape
    depth = ada_w.shape[0]
    assert d_model == D_MODEL and norm_g.shape[1] == 3 and ffn_w_in.shape[-1] == 2 * D_FF
    assert 1 + n_lat_seq <= MOD_ROWS
    n_ctx = batch * t_ctx
    n_tok = n_ctx + n_lat_seq * t_lat
    assert n_ctx % t_lat == 0
    segs = _Segs(n_ctx, t_ctx, n_lat_seq, t_lat)
    tiles = _Tiles(n_ctx, t_lat)

    cond = jnp.concatenate([c_ctx[None, :], c, jnp.zeros((MOD_ROWS - 1 - n_lat_seq, D_MODEL), F32)], axis=0)
    mod = _modulation(cond, ada_w, ada_b)

    wfi = _bf(ffn_w_in).reshape(depth * 2, D_MODEL, 2 * D_FF)
    wfo = _bf(ffn_w_out).reshape(depth * 2, D_FF, D_MODEL)
    wz = _bf(_permute_w_in(w_in))
    wo = _bf(w_out)
    qn = jnp.tile(attn_q_norm, (1, A_HEADS)).reshape(depth, 1, A_WIDTH)
    kn = jnp.tile(attn_k_norm, (1, A_KV_HEADS)).reshape(depth, 1, KV_WIDTH)
    gn = jnp.tile(delta_norm, (1, N_HEADS)).reshape(depth, 1, SCAN_W)
    mn = mlstm_norm.reshape(depth, 1, SCAN_W)
    fin = final_norm.reshape(1, D_MODEL)
    fb_rows = _head_lanes(mlstm_f_bias)
    alog_rows = _head_lanes(delta_a_log)
    dtb_rows = _head_lanes(delta_dt_bias)
    cos_t, sin_t = _rope_tables(t_lat)
    ck = cache_k.reshape(cache_k.shape[:3] + (KV_WIDTH,))
    cv = cache_v.reshape(cache_v.shape[:3] + (KV_WIDTH,))
    n0 = state_mlstm_n.reshape(n_lat_seq, depth, N_DIR, 1, SCAN_W)
    m0 = _head_lanes(state_mlstm_m).reshape(n_lat_seq, depth, N_DIR, 1, SCAN_W)

    x_ctx, x_lat, lat_tile0 = x_prompt.reshape(n_ctx, D_MODEL), x_sample.reshape(n_tok - n_ctx, D_MODEL), 0
    ks, vs, cs, ns, ms, ss = [], [], [], [], [], []
    for l in range(depth):
        x1, zm, zg, aq, ak, av, gates = _dense1(l, tiles, n_tok, x_ctx, x_lat, lat_tile0, mod, norm_g,
                                                wfi, wfo, wz, qn, kn, cos_t, sin_t)
        hf, hb, c_new, n_new, m_new = _mlstm(l, segs, zm, gates, fb_rows, state_mlstm_C, n0, m0)
        o_f, o_b, s_new = _delta(l, segs, zg, gates, delta_conv, alog_rows, dtb_rows, state_delta_S)
        a_out = _attention(l, segs, aq, ak, av, ck, cv)
        x = _dense2(l, tiles, x1, mod, norm_g, hf, hb, zm, mn, o_f, o_b, zg, gn, a_out, wo, wfi, wfo, fin,
                    final=(l == depth - 1))
        if l < depth - 1:
            x_ctx, x_lat, lat_tile0 = x, x, tiles.n_ctx

        ks.append(ak[:n_ctx].reshape(batch, t_ctx, A_KV_HEADS, HEAD_DIM))
        vs.append(av[:n_ctx].reshape(batch, t_ctx, A_KV_HEADS, HEAD_DIM))
        cs.append(c_new[:batch])
        ns.append(n_new[:batch].reshape(batch, N_DIR, N_HEADS, HEAD_DIM))
        ms.append(m_new[:batch].reshape(batch, N_DIR, N_HEADS, HEAD_DIM)[..., 0])
        ss.append(s_new[:batch])

    y_prompt = x[0].reshape(batch, t_ctx, D_MODEL)
    y_sample = x[1].reshape(n_lat_seq, t_lat, D_MODEL)
    stack = lambda xs: jnp.stack(xs, axis=1)
    return (y_prompt, y_sample, stack(ks), stack(vs), stack(cs), stack(ns), stack(ms), stack(ss))
```

```python
import functools
import math

import numpy as np

import jax
import jax.numpy as jnp
from jax import lax
from jax.experimental import pallas as pl
from jax.experimental.pallas import tpu as pltpu

F32 = jnp.float32
BF16 = jnp.bfloat16

D_MODEL = 1024
HEAD_DIM = 64
N_HEADS = 4
SCAN_W = N_HEADS * HEAD_DIM
A_HEADS = 8
A_KV_HEADS = 2
A_GROUP = A_HEADS // A_KV_HEADS
A_WIDTH = A_HEADS * HEAD_DIM
KV_WIDTH = A_KV_HEADS * HEAD_DIM
N_DIR = 2
CHUNK = 64
CONV_W = 5
D_FF = 2816
FF_CHUNK = 256
GRID_W = 64
ROPE_BASE = 10000.0
EPS = 1e-6
N_MOD = 9
MOD_ROWS = 8
GATE_W = 128
Z_WIDTH = 2 * D_MODEL + A_WIDTH + 2 * KV_WIDTH + GATE_W

SEG = 256
N_CHUNK = SEG // CHUNK
TM = 512
ROW_GROUP = 512
HALO = 8
KV_BLOCK = 512
ATT_LANES = 2048
VMEM_LIMIT = 56 * 1024 * 1024


def _bf(x):
    return x.astype(BF16)


def _dot(a, b):
    return jnp.dot(_bf(a), _bf(b), preferred_element_type=F32)


def _bdot(a, b):
    return lax.dot_general(a, b, (((2,), (1,)), ((0,), (0,))), preferred_element_type=F32)


def _bdot_nt(a, b):
    return lax.dot_general(a, b, (((2,), (2,)), ((0,), (0,))), preferred_element_type=F32)


def _split(a, terms):
    parts = []
    rest = a
    for _ in range(terms):
        p = _bf(rest)
        parts.append(p)
        rest = rest - p.astype(F32)
    return parts


def _dot_exact_rhs(a, b01, terms=2):
    parts = _split(a, terms)
    if a.shape[-1] % 128 == 0:
        return jnp.dot(jnp.concatenate(parts, axis=-1), jnp.concatenate([b01] * terms, axis=0),
                       preferred_element_type=F32)
    out = None
    for p in parts:
        d = jnp.dot(p, b01, preferred_element_type=F32)
        out = d if out is None else out + d
    return out


def _iota(shape, dim):
    return lax.broadcasted_iota(jnp.int32, shape, dim)


def _head_of(idx):
    return jnp.right_shift(idx, 6)


def _in_head(idx):
    return jnp.bitwise_and(idx, HEAD_DIM - 1)


def _block_ones(n):
    return (_head_of(_iota((n, n), 0)) == _head_of(_iota((n, n), 1)))


def _head_sum(x):
    n = x.shape[-1]
    return _dot_exact_rhs(x, _block_ones(n).astype(BF16), terms=1)


def _head_rms(x, g_row):
    ms = _head_sum(x * x) * (1.0 / HEAD_DIM)
    return x * lax.rsqrt(ms + EPS) * g_row


def _rms_mod(x, g_row, scale_row, shift_row):
    y = x * lax.rsqrt(jnp.mean(x * x, axis=-1, keepdims=True) + EPS) * g_row
    return y * (1.0 + scale_row) + shift_row


def _softplus(x):
    return jnp.maximum(x, 0.0) + jnp.log1p(jnp.exp(-jnp.abs(x)))


def _log_sigmoid(x):
    return -_softplus(-x)


def _sigmoid(x):
    return jax.nn.sigmoid(x)


def _silu(x):
    return x * jax.nn.sigmoid(x)


def _bd3(x, mask_bf):
    return jnp.concatenate([x] * N_HEADS, axis=1) * mask_bf


def _to_block_diag(x4):
    spread = (_iota((HEAD_DIM, SCAN_W), 0) == _in_head(_iota((HEAD_DIM, SCAN_W), 1))).astype(BF16)
    rows = x4.reshape(SCAN_W, HEAD_DIM)
    return jnp.where(_block_ones(SCAN_W), _dot_exact_rhs(rows, spread, terms=3), 0.0)


def _from_block_diag(x):
    fold = (_in_head(_iota((SCAN_W, HEAD_DIM), 0)) == _iota((SCAN_W, HEAD_DIM), 1)).astype(BF16)
    return _dot_exact_rhs(x, fold, terms=3).reshape(N_HEADS, HEAD_DIM, HEAD_DIM)


def _expand_gate(gates, col0):
    sel = (_iota((GATE_W, SCAN_W), 0) == col0 + _head_of(_iota((GATE_W, SCAN_W), 1))).astype(BF16)
    return _dot_exact_rhs(gates, sel, terms=3)


def _chunk_scan(x, op, ident, rev):
    rows = x.shape[0]
    pos = _in_head(_iota(x.shape, 0))
    s = 1
    while s < CHUNK:
        if rev:
            y = pltpu.roll(x, rows - s, 0)
            ok = pos < CHUNK - s
        else:
            y = pltpu.roll(x, s, 0)
            ok = pos >= s
        x = op(x, jnp.where(ok, y, ident))
        s *= 2
    return x


def _chunks(a):
    return a.reshape(N_CHUNK, CHUNK, SCAN_W)


def _problem_masks():
    shape = (N_DIR * N_CHUNK, CHUNK, SCAN_W)
    t_idx, j_idx = _iota(shape, 1), _in_head(_iota(shape, 2))
    ahead = jnp.where(_iota(shape, 0) >= N_CHUNK, j_idx - t_idx, t_idx - j_idx)
    return t_idx == j_idx, ahead >= 0, ahead > 0


def _scan_order(d):
    return range(N_CHUNK - 1, -1, -1) if d else range(N_CHUNK)


def _mod_kernel(cond_ref, w_ref, b_ref, o_ref):
    a = _silu(cond_ref[...])
    o_ref[0, 0] = _dot(a, w_ref[0]) + b_ref[0]


def _modulation(cond, ada_w, ada_b):
    depth = ada_w.shape[0]
    return pl.pallas_call(
        _mod_kernel,
        grid=(depth, N_MOD),
        in_specs=[
            pl.BlockSpec((MOD_ROWS, D_MODEL), lambda l, j: (0, 0)),
            pl.BlockSpec((1, D_MODEL, D_MODEL), lambda l, j: (l, 0, j)),
            pl.BlockSpec((1, 1, D_MODEL), lambda l, j: (l, 0, j)),
        ],
        out_specs=pl.BlockSpec((1, 1, MOD_ROWS, D_MODEL), lambda l, j: (l, j, 0, 0)),
        out_shape=jax.ShapeDtypeStruct((depth, N_MOD, MOD_ROWS, D_MODEL), F32),
        compiler_params=pltpu.CompilerParams(
            dimension_semantics=("arbitrary", "arbitrary"), vmem_limit_bytes=VMEM_LIMIT),
        name="modulation",
    )(cond, ada_w, ada_b.reshape(depth, 1, N_MOD * D_MODEL))


def _ffn(h, w_in_ref, w_out_ref):
    n_chunks = D_FF // FF_CHUNK

    def gate_up(c):
        lo = c * FF_CHUNK
        g = jnp.dot(h, w_in_ref[0, :, lo:lo + FF_CHUNK], preferred_element_type=F32)
        u = jnp.dot(h, w_in_ref[0, :, D_FF + lo:D_FF + lo + FF_CHUNK], preferred_element_type=F32)
        return g, u

    acc = None
    nxt = gate_up(0)
    for c in range(n_chunks):
        g, u = nxt
        if c + 1 < n_chunks:
            nxt = gate_up(c + 1)
        a = _bf(_silu(g) * u)
        d = jnp.dot(a, w_out_ref[0, c * FF_CHUNK:(c + 1) * FF_CHUNK, :], preferred_element_type=F32)
        acc = d if acc is None else acc + d
    return acc


def _rope(x, cos, sin_signed):
    w = x.shape[-1]
    reps = w // cos.shape[-1]
    if reps > 1:
        cos = jnp.concatenate([cos] * reps, axis=-1)
        sin_signed = jnp.concatenate([sin_signed] * reps, axis=-1)
    first = jnp.bitwise_and(_iota(x.shape, 1), 31) < 16
    partner = jnp.where(first, pltpu.roll(x, w - 16, 1), pltpu.roll(x, 16, 1))
    return x * cos + partner * sin_signed


class _Tiles:
    def __init__(self, n_ctx, t_lat):
        assert n_ctx % TM == 0 and t_lat % TM == 0
        self.n_ctx = n_ctx // TM
        self.per_seq = t_lat // TM

    def is_ctx(self, i):
        return i < self.n_ctx

    def lat(self, i):
        return jnp.maximum(i - self.n_ctx, 0)

    def mod_row(self, i):
        return jnp.where(self.is_ctx(i), 0, 1 + self.lat(i) // self.per_seq)


def _mod_rows(mod_ref, row):
    return lambda m: mod_ref[0, m, pl.ds(row, 1), :]


def _dense1_kernel(xa_ref, xb_ref, mod_ref, ng_ref, wfi_ref, wfo_ref, wz_ref, qn_ref, kn_ref, cos_ref, sin_ref,
                   x1_ref, zm_ref, zg_ref, aq_ref, ak_ref, av_ref, gate_ref, *, tiles):
    i = pl.program_id(0)
    mod = _mod_rows(mod_ref, tiles.mod_row(i))
    is_ctx = tiles.is_ctx(i)
    for r in range(TM // ROW_GROUP):
        rows = slice(r * ROW_GROUP, (r + 1) * ROW_GROUP)
        x = jnp.where(is_ctx, xa_ref[rows, :], xb_ref[rows, :])
        h = _bf(_rms_mod(x, ng_ref[0, 0:1], mod(0), mod(1)))
        x1 = x + 0.5 * mod(2) * _ffn(h, wfi_ref, wfo_ref)
        x1_ref[rows, :] = x1
        h2 = _bf(_rms_mod(x1, ng_ref[0, 1:2], mod(3), mod(4)))

        z = jnp.dot(h2, wz_ref[0], preferred_element_type=F32)

        def proj(lo, width):
            return z[:, lo:lo + width]

        zm_ref[rows, :] = proj(0, D_MODEL)
        zg_ref[rows, :] = proj(D_MODEL, D_MODEL)
        cos = cos_ref[rows, :]
        sin = sin_ref[rows, :]
        off = 2 * D_MODEL
        q = _head_rms(proj(off, A_WIDTH), qn_ref[0])
        aq_ref[rows, :] = _rope(q, cos, sin) * (1.0 / math.sqrt(HEAD_DIM))
        k = _head_rms(proj(off + A_WIDTH, KV_WIDTH), kn_ref[0])
        ak_ref[rows, :] = _rope(k, cos, sin)
        av_ref[rows, :] = proj(off + A_WIDTH + KV_WIDTH, KV_WIDTH)
        gate_ref[rows, :] = proj(off + A_WIDTH + 2 * KV_WIDTH, GATE_W)


def _resident(shape, index_map):
    return pl.BlockSpec(shape, index_map, pipeline_mode=pl.Buffered(1))


def _dense1(layer, tiles, n_tok, x_ctx, x_lat, lat_tile0, mod, norm_g, wfi, wfo, wz, qn, kn, cos_t, sin_t):
    row = lambda w: pl.BlockSpec((TM, w), lambda i: (i, 0))
    rope_tile = lambda i: jnp.where(tiles.is_ctx(i), tiles.per_seq, tiles.lat(i) % tiles.per_seq)
    rope = pl.BlockSpec((TM, KV_WIDTH), lambda i: (rope_tile(i), 0))
    out_w = (D_MODEL, D_MODEL, D_MODEL, A_WIDTH, KV_WIDTH, KV_WIDTH, GATE_W)
    return pl.pallas_call(
        functools.partial(_dense1_kernel, tiles=tiles),
        grid=(n_tok // TM,),
        in_specs=[
            pl.BlockSpec((TM, D_MODEL), lambda i: (jnp.minimum(i, tiles.n_ctx - 1), 0)),
            pl.BlockSpec((TM, D_MODEL), lambda i: (tiles.lat(i) + lat_tile0, 0)),
            pl.BlockSpec((1, N_MOD, MOD_ROWS, D_MODEL), lambda i: (layer, 0, 0, 0)),
            pl.BlockSpec((1, 3, D_MODEL), lambda i: (layer, 0, 0)),
            _resident((1, D_MODEL, 2 * D_FF), lambda i: (2 * layer, 0, 0)),
            _resident((1, D_FF, D_MODEL), lambda i: (2 * layer, 0, 0)),
            _resident((1, D_MODEL, Z_WIDTH), lambda i: (layer, 0, 0)),
            pl.BlockSpec((1, 1, A_WIDTH), lambda i: (layer, 0, 0)),
            pl.BlockSpec((1, 1, KV_WIDTH), lambda i: (layer, 0, 0)),
            rope, rope,
        ],
        out_specs=[row(w) for w in out_w],
        out_shape=[jax.ShapeDtypeStruct((n_tok, w), F32) for w in out_w],
        compiler_params=pltpu.CompilerParams(
            dimension_semantics=("arbitrary",), vmem_limit_bytes=VMEM_LIMIT),
        name="dense1",
    )(x_ctx, x_lat, mod, norm_g, wfi, wfo, wz, qn, kn, cos_t, sin_t)


def _dense2_kernel(x_ref, mod_ref, ng_ref, hf_ref, hb_ref, mo_ref, mn_ref, of_ref, ob_ref, gz_ref, gn_ref,
                   ao_ref, wo_ref, wfi_ref, wfo_ref, fin_ref, *out_refs, tiles, final):
    i = pl.program_id(0)
    mod = _mod_rows(mod_ref, tiles.mod_row(i))
    m_out = _head_rms(hf_ref[...] + hb_ref[...], mn_ref[0]) * _sigmoid(mo_ref[...])
    g_out = _head_rms(of_ref[...] + ob_ref[...], gn_ref[0]) * _silu(gz_ref[...])
    mix = jnp.dot(_bf(m_out), wo_ref[0, 0:SCAN_W, :], preferred_element_type=F32)
    mix += jnp.dot(_bf(g_out), wo_ref[0, SCAN_W:2 * SCAN_W, :], preferred_element_type=F32)
    mix += jnp.dot(_bf(ao_ref[...]), wo_ref[0, 2 * SCAN_W:, :], preferred_element_type=F32)
    x2 = x_ref[...] + mod(5) * mix
    h = _bf(_rms_mod(x2, ng_ref[0, 2:3], mod(6), mod(7)))
    x3 = x2 + 0.5 * mod(8) * _ffn(h, wfi_ref, wfo_ref)
    if not final:
        out_refs[0][...] = x3
        return
    y = x3 * lax.rsqrt(jnp.mean(x3 * x3, axis=-1, keepdims=True) + EPS) * fin_ref[...]
    y_ctx_ref, y_lat_ref = out_refs

    @pl.when(tiles.is_ctx(i))
    def _():
        y_ctx_ref[...] = y

    @pl.when(jnp.logical_not(tiles.is_ctx(i)))
    def _():
        y_lat_ref[...] = y


def _dense2(layer, tiles, x1, mod, norm_g, hf, hb, zm, m_norm, of, ob, zg, g_norm, a_out, wo, wfi, wfo, fin,
            final):
    n_tok = x1.shape[0]
    row = lambda w: pl.BlockSpec((TM, w), lambda i: (i, 0))
    last_quarter = pl.BlockSpec((TM, SCAN_W), lambda i: (i, 3))
    lane_row = lambda w: pl.BlockSpec((1, 1, w), lambda i: (layer, 0, 0))
    if final:
        n_ctx = tiles.n_ctx * TM
        out_specs = [pl.BlockSpec((TM, D_MODEL), lambda i: (jnp.minimum(i, tiles.n_ctx - 1), 0)),
                     pl.BlockSpec((TM, D_MODEL), lambda i: (tiles.lat(i), 0))]
        out_shape = [jax.ShapeDtypeStruct((n_ctx, D_MODEL), F32),
                     jax.ShapeDtypeStruct((n_tok - n_ctx, D_MODEL), F32)]
    else:
        out_specs, out_shape = row(D_MODEL), jax.ShapeDtypeStruct((n_tok, D_MODEL), F32)
    return pl.pallas_call(
        functools.partial(_dense2_kernel, tiles=tiles, final=final),
        grid=(n_tok // TM,),
        in_specs=[
            row(D_MODEL),
            pl.BlockSpec((1, N_MOD, MOD_ROWS, D_MODEL), lambda i: (layer, 0, 0, 0)),
            pl.BlockSpec((1, 3, D_MODEL), lambda i: (layer, 0, 0)),
            row(SCAN_W), row(SCAN_W), last_quarter, lane_row(SCAN_W),
            row(SCAN_W), row(SCAN_W), last_quarter, lane_row(SCAN_W),
            row(A_WIDTH),
            _resident((1, D_MODEL, D_MODEL), lambda i: (layer, 0, 0)),
            _resident((1, D_MODEL, 2 * D_FF), lambda i: (2 * layer + 1, 0, 0)),
            _resident((1, D_FF, D_MODEL), lambda i: (2 * layer + 1, 0, 0)),
            pl.BlockSpec((1, D_MODEL), lambda i: (0, 0)),
        ],
        out_specs=out_specs,
        out_shape=out_shape,
        compiler_params=pltpu.CompilerParams(
            dimension_semantics=("arbitrary",), vmem_limit_bytes=VMEM_LIMIT),
        name="dense2",
    )(x1, mod, norm_g, hf, hb, zm, m_norm, of, ob, zg, g_norm, a_out, wo, wfi, wfo, fin)


class _Segs:
    def __init__(self, n_ctx, t_ctx, n_lat_seq, t_lat):
        assert t_ctx == SEG and t_lat % SEG == 0
        self.n_ctx_seg = n_ctx // SEG
        self.per_seq = t_lat // SEG
        self.n_lat_seq = n_lat_seq
        self.n_seg = self.n_ctx_seg + n_lat_seq * self.per_seq
        self.n_seq = self.n_ctx_seg + n_lat_seq

    def is_ctx(self, s):
        return s < self.n_ctx_seg

    def lat(self, s):
        r = jnp.maximum(s - self.n_ctx_seg, 0)
        return r // self.per_seq, r % self.per_seq

    def blk(self, s, rev):
        b, j = self.lat(s)
        j = self.per_seq - 1 - j if rev else j
        return jnp.where(self.is_ctx(s), s, self.n_ctx_seg + b * self.per_seq + j)

    def seq(self, s):
        return jnp.where(self.is_ctx(s), s, self.n_ctx_seg + self.lat(s)[0])

    def lat_seq(self, s):
        return self.lat(s)[0]

    def first(self, s):
        return jnp.logical_or(self.is_ctx(s), self.lat(s)[1] == 0)

    def last(self, s):
        return jnp.logical_or(self.is_ctx(s), self.lat(s)[1] == self.per_seq - 1)


_STATE_MAT = (N_DIR, N_HEADS, HEAD_DIM, HEAD_DIM)


def _state_specs(layer, segs):
    s_in = pl.BlockSpec((1, 1) + _STATE_MAT, lambda s: (segs.lat_seq(s), layer, 0, 0, 0, 0))
    s_out = pl.BlockSpec((1,) + _STATE_MAT, lambda s: (segs.seq(s), 0, 0, 0, 0))
    return s_in, s_out, jax.ShapeDtypeStruct((segs.n_seq,) + _STATE_MAT, F32)


N_GATE_ROWS = N_DIR * N_HEADS


def _lane_scan(x, op, ident, rev):
    n = x.shape[-1]
    pos = _in_head(_iota(x.shape, x.ndim - 1))
    s = 1
    while s < CHUNK:
        if rev:
            y, ok = pltpu.roll(x, n - s, x.ndim - 1), pos < CHUNK - s
        else:
            y, ok = pltpu.roll(x, s, x.ndim - 1), pos >= s
        x = op(x, jnp.where(ok, y, ident))
        s *= 2
    return x


def _pieces(x, terms):
    out = []
    for _ in range(terms):
        p = _bf(x).astype(F32)
        out.append(p)
        x = x - p
    return jnp.concatenate(out, axis=0)


def _expander(n_qty):
    k = np.arange(SCAN_W)[:, None]
    n = np.arange(N_DIR * n_qty * SCAN_W)[None, :]
    rows_per_piece = n_qty * N_GATE_ROWS
    assert 3 * rows_per_piece <= SCAN_W
    qty, row = (k % rows_per_piece) // N_GATE_ROWS, k % N_GATE_ROWS
    d, qty_n, head = n // (n_qty * SCAN_W), (n // SCAN_W) % n_qty, (n % SCAN_W) // HEAD_DIM
    used = k < 3 * rows_per_piece
    return jnp.asarray(used & (qty == qty_n) & (row == d * N_HEADS + head), BF16)


def _expand(quantities, expand_ref):
    n_qty = len(quantities)
    packed = jnp.concatenate(quantities, axis=0)
    pad = jnp.zeros((SCAN_W - 3 * packed.shape[0], SEG), F32)
    lhs = _bf(jnp.concatenate([_pieces(packed, 3), pad], axis=0).T)
    wide = jnp.dot(lhs, expand_ref[...], preferred_element_type=F32)

    def stack(j):
        cols = [wide[:, (d * n_qty + j) * SCAN_W:(d * n_qty + j + 1) * SCAN_W] for d in range(N_DIR)]
        return jnp.concatenate([_chunks(x) for x in cols], axis=0)

    return [stack(j) for j in range(n_qty)]


def _mlstm_compact(gf_ref, gb_ref, fb_col, m_col, expand2_ref, expand4_ref):
    rows = _iota((N_GATE_ROWS, SEG), 0)
    is_b = rows >= N_HEADS
    gt_f, gt_b = gf_ref[...].T, gb_ref[...].T
    pick = lambda lo: jnp.where(is_b, gt_b[lo:lo + N_GATE_ROWS], gt_f[lo:lo + N_GATE_ROWS])
    ig = pick(0)
    lf = _log_sigmoid(pick(N_GATE_ROWS) + fb_col)
    s_idx, t_idx = _iota((SEG, 2 * SEG), 0), _iota((SEG, 2 * SEG), 1)
    t_loc = jnp.bitwise_and(t_idx, SEG - 1)
    tri = jnp.logical_and(_head_of(s_idx) == _head_of(t_loc),
                          jnp.where(t_idx >= SEG, s_idx - t_loc, t_loc - s_idx) >= 0).astype(BF16)
    sums = jnp.dot(_bf(_pieces(lf, 3)), tri, preferred_element_type=F32)
    sums = sums[0:N_GATE_ROWS] + sums[N_GATE_ROWS:2 * N_GATE_ROWS] + sums[2 * N_GATE_ROWS:]
    b = jnp.where(is_b, sums[:, SEG:], sums[:, 0:SEG])
    r = ig - b
    cm = jnp.where(is_b, _lane_scan(r, jnp.maximum, -jnp.inf, True), _lane_scan(r, jnp.maximum, -jnp.inf, False))
    chunk_of = _head_of(_iota((N_GATE_ROWS, SEG), 1))

    def on_lanes(cols, mirrored):
        out = jnp.zeros((N_GATE_ROWS, SEG), F32)
        for c in range(N_CHUNK):
            col = jnp.where(is_b, cols[N_CHUNK - 1 - c], cols[c]) if mirrored else cols[c]
            out = jnp.where(chunk_of == c, col, out)
        return out

    in_chunk = lambda x, c: x[:, c * CHUNK:(c + 1) * CHUNK]
    bl_cols = [jnp.sum(in_chunk(lf, c), axis=1, keepdims=True) for c in range(N_CHUNK)]
    bl = on_lanes(bl_cols, False)
    lw = bl - b + ig
    lw_cols = [jnp.max(in_chunk(lw, c), axis=1, keepdims=True) for c in range(N_CHUNK)]

    is_b_col = is_b[:, 0:1]
    at_step = lambda cols, i: jnp.where(is_b_col, cols[N_CHUNK - 1 - i], cols[i])
    m_in, m_out = [], []
    for i in range(N_CHUNK):
        m_in.append(m_col)
        m_col = jnp.maximum(at_step(bl_cols, i) + m_col, at_step(lw_cols, i))
        m_out.append(m_col)
    m_c, m_n = on_lanes(m_in, True), on_lanes(m_out, True)
    kw_fac, dec = _expand([jnp.exp(lw - m_n), jnp.exp(bl + m_c - m_n)], expand2_ref)
    mx = jnp.maximum(m_c, cm)
    r, mx, a_int, neg_mt = _expand([r, mx, jnp.exp(m_c - mx), -(b + mx)], expand4_ref)
    return (r, mx, a_int, neg_mt, kw_fac, dec), m_col


def _mlstm_kernel(zmf_ref, zmb_ref, gf_ref, gb_ref, fb_ref, expand2_ref, expand4_ref, c0_ref, n0_ref, m0_ref,
                  hf_ref, hb_ref, c_out, n_out, m_out, c_s, n_s, m_s, *, segs):
    s = pl.program_id(0)

    @pl.when(segs.first(s))
    def _():
        ctx = segs.is_ctx(s)
        m_s[...] = jnp.where(ctx, 0.0, m0_ref[0, 0])
        for d in range(N_DIR):
            c_s[d] = jnp.where(ctx, 0.0, _to_block_diag(c0_ref[0, 0, d]))
            n_s[d] = jnp.where(ctx, 0.0, n0_ref[0, 0, d])

    (r, mx, a_int, neg_mt, kw_fac, dec), m_fin = _mlstm_compact(gf_ref, gb_ref, fb_ref[0], m_s[...],
                                                                expand2_ref, expand4_ref)
    zm = (zmf_ref, zmb_ref)
    col = lambda j: jnp.concatenate([_chunks(ref[:, j * SCAN_W:(j + 1) * SCAN_W]) for ref in zm], axis=0)
    q, k, v = col(0), col(1) * (1.0 / math.sqrt(HEAD_DIM)), col(2)

    diag, tri, _ = _problem_masks()
    mask_bd = _block_ones(SCAN_W)
    mask_bf = mask_bd.astype(BF16)

    r_row = jnp.sum(jnp.where(diag, r, 0.0), axis=1, keepdims=True)
    w = jnp.exp(jnp.where(tri, r_row - mx, -jnp.inf))
    sc = _bdot_nt(_bf(q), _bd3(_bf(k), mask_bf)) * w
    kw = kw_fac * k
    kw_sum = jnp.sum(kw, axis=1, keepdims=True)

    c_at = [None] * (N_DIR * N_CHUNK)
    n_at = [None] * (N_DIR * N_CHUNK)
    c_fin, n_fin = [], []
    for d in range(N_DIR):
        c_bd, n_row = c_s[d], n_s[d]
        for c in _scan_order(d):
            p = d * N_CHUNK + c
            c_at[p], n_at[p] = c_bd, n_row
            dec_row = dec[p, 0:1, :]
            c_bd = c_bd * dec_row + jnp.where(mask_bd, _dot(kw[p].T, v[p]), 0.0)
            n_row = n_row * dec_row + kw_sum[p]
        c_fin.append(c_bd)
        n_fin.append(n_row)
    c_all = jnp.concatenate([_bf(x)[None] for x in c_at], axis=0)
    n_all = jnp.concatenate([x[None] for x in n_at], axis=0)

    num = _bdot(_bf(sc), _bd3(_bf(v), mask_bf)) + a_int * _bdot(_bf(q), c_all)
    flat = lambda a: a.reshape(N_DIR * SEG, SCAN_W)
    sums = _dot_exact_rhs(jnp.concatenate([flat(sc), flat(q * n_all)], axis=0), mask_bf)
    den = sums[0:N_DIR * SEG].reshape(sc.shape) + a_int * sums[N_DIR * SEG:].reshape(sc.shape)
    h = num / jnp.maximum(jnp.abs(den), jnp.exp(neg_mt))
    hf_ref[...] = h[0:N_CHUNK].reshape(SEG, SCAN_W)
    hb_ref[...] = h[N_CHUNK:].reshape(SEG, SCAN_W)

    m_s[...] = m_fin
    for d in range(N_DIR):
        c_s[d] = c_fin[d]
        n_s[d] = n_fin[d]

    @pl.when(segs.last(s))
    def _():
        m_out[0] = m_fin
        for d in range(N_DIR):
            c_out[0, d] = _from_block_diag(c_fin[d])
            n_out[0, d] = n_fin[d]


def _mlstm(layer, segs, zm, gates, fb_cols, state_c, n0, m0):
    n_tok = zm.shape[0]
    fwd = lambda w: pl.BlockSpec((SEG, w), lambda s: (segs.blk(s, False), 0))
    bwd = lambda w: pl.BlockSpec((SEG, w), lambda s: (segs.blk(s, True), 0))
    row_in = pl.BlockSpec((1, 1, N_DIR, 1, SCAN_W), lambda s: (segs.lat_seq(s), layer, 0, 0, 0))
    row_out = pl.BlockSpec((1, N_DIR, 1, SCAN_W), lambda s: (segs.seq(s), 0, 0, 0))
    row_shape = jax.ShapeDtypeStruct((segs.n_seq, N_DIR, 1, SCAN_W), F32)
    mat_in, mat_out, mat_shape = _state_specs(layer, segs)
    expand2, expand4 = _expander(2), _expander(4)
    return pl.pallas_call(
        functools.partial(_mlstm_kernel, segs=segs),
        grid=(segs.n_seg,),
        in_specs=[
            fwd(D_MODEL), bwd(D_MODEL), fwd(GATE_W), bwd(GATE_W),
            pl.BlockSpec((1, N_GATE_ROWS, 1), lambda s: (layer, 0, 0)),
            pl.BlockSpec(expand2.shape, lambda s: (0, 0)),
            pl.BlockSpec(expand4.shape, lambda s: (0, 0)),
            mat_in, row_in,
            pl.BlockSpec((1, 1, N_GATE_ROWS, 1), lambda s: (segs.lat_seq(s), layer, 0, 0)),
        ],
        out_specs=[fwd(SCAN_W), bwd(SCAN_W), mat_out, row_out,
                   pl.BlockSpec((1, N_GATE_ROWS, 1), lambda s: (segs.seq(s), 0, 0))],
        out_shape=[jax.ShapeDtypeStruct((n_tok, SCAN_W), F32)] * 2
        + [mat_shape, row_shape, jax.ShapeDtypeStruct((segs.n_seq, N_GATE_ROWS, 1), F32)],
        scratch_shapes=[pltpu.VMEM((N_DIR, SCAN_W, SCAN_W), F32),
                        pltpu.VMEM((N_DIR, 1, SCAN_W), F32),
                        pltpu.VMEM((N_GATE_ROWS, 1), F32)],
        compiler_params=pltpu.CompilerParams(
            dimension_semantics=("arbitrary",), vmem_limit_bytes=VMEM_LIMIT),
        name="mlstm",
    )(zm, zm, gates, gates, fb_cols, expand2, expand4, state_c, n0, m0)


def _short_conv(prev_ref, cur_ref, next_ref, w_ref, has_prev, has_next):
    w3 = 3 * SCAN_W
    prev = jnp.where(has_prev, prev_ref[:, 0:w3], 0.0)
    nxt = jnp.where(has_next, next_ref[:, 0:w3], 0.0)
    xp = jnp.concatenate([prev, cur_ref[:, 0:w3], nxt], axis=0)
    rows = xp.shape[0]
    acc = None
    for i in range(CONV_W):
        shift = (CONV_W // 2 - i) % rows
        y = xp if shift == 0 else pltpu.roll(xp, shift, 0)
        t = y[HALO:HALO + SEG] * w_ref[0, i:i + 1, :]
        acc = t if acc is None else acc + t
    return acc


def _neumann_inverse(n_all, mask_bf):
    eye = (_iota(n_all.shape[1:], 0) == _in_head(_iota(n_all.shape[1:], 1))).astype(F32)
    p = -n_all
    t = eye + p
    levels = 6
    for lvl in range(levels):
        first, last = lvl == 0, lvl == levels - 1
        p_hi, p_lo = _split(p, 2)
        w_hi, w_lo = _bd3(p_hi, mask_bf), _bd3(p_lo, mask_bf)
        lhs_hi, lhs_lo = [], []
        if not first:
            t_hi, t_lo = _split(t, 2)
            lhs_hi += [t_hi, t_lo]
            lhs_lo += [t_hi]
        if not last:
            lhs_hi += [p_hi, p_lo]
            lhs_lo += [p_hi]
        a = _bdot(jnp.concatenate(lhs_hi, axis=1), w_hi)
        b = _bdot(jnp.concatenate(lhs_lo, axis=1) if len(lhs_lo) > 1 else lhs_lo[0], w_lo)
        ra, rb = 0, 0
        if not first:
            t = t + (a[:, 0:CHUNK] + a[:, CHUNK:2 * CHUNK] + b[:, 0:CHUNK])
            ra, rb = 2 * CHUNK, CHUNK
        if not last:
            p = a[:, ra:ra + CHUNK] + a[:, ra + CHUNK:ra + 2 * CHUNK] + b[:, rb:rb + CHUNK]
    return t


def _delta_prep(prev_ref, cur_ref, next_ref, gate_ref, cw_ref, alog_row, dtb_row, has_prev, has_next, rev, d):
    qkv = _silu(_short_conv(prev_ref, cur_ref, next_ref, cw_ref, has_prev, has_next))

    def l2(a):
        return a * lax.rsqrt(_head_sum(a * a) + EPS)

    q = l2(qkv[:, 0:SCAN_W]) * (1.0 / math.sqrt(HEAD_DIM))
    k = l2(qkv[:, SCAN_W:2 * SCAN_W])
    v = qkv[:, 2 * SCAN_W:3 * SCAN_W]
    gates = gate_ref[...]
    g = -jnp.exp(alog_row) * _softplus(_expand_gate(gates, 16 + d * N_HEADS) + dtb_row)
    beta = _sigmoid(_expand_gate(gates, 24 + d * N_HEADS))
    gc = _chunks(_chunk_scan(g, jnp.add, 0.0, rev))
    edge = 0 if rev else CHUNK - 1
    gl = gc[:, edge:edge + 1, :]
    eg = jnp.exp(gc)
    kb = _chunks(k * beta)
    return dict(q=_chunks(q), k=_chunks(k), kb=kb, vb=_chunks(v * beta), kbe=kb * eg, qd=_chunks(q) * eg,
                kd=_chunks(k) * jnp.exp(gl - gc), gc=gc, gl=jnp.exp(gl))


def _delta_kernel(pf_ref, cf_ref, nf_ref, pb_ref, cb_ref, nb_ref, gf_ref, gb_ref, cw_ref, al_ref, dt_ref,
                  s0_ref, of_ref, ob_ref, s_out, s_s, *, segs):
    s = pl.program_id(0)
    lat = jnp.logical_not(segs.is_ctx(s))
    j = segs.lat(s)[1]

    @pl.when(segs.first(s))
    def _():
        for d in range(N_DIR):
            s_s[d] = jnp.where(lat, _to_block_diag(s0_ref[0, 0, d]), 0.0)

    inner_lo = jnp.logical_and(lat, j > 0)
    inner_hi = jnp.logical_and(lat, j < segs.per_seq - 1)
    fwd = _delta_prep(pf_ref, cf_ref, nf_ref, gf_ref, cw_ref, al_ref[0, 0:1], dt_ref[0, 0:1],
                      inner_lo, inner_hi, False, 0)
    bwd = _delta_prep(pb_ref, cb_ref, nb_ref, gb_ref, cw_ref, al_ref[0, 1:2], dt_ref[0, 1:2],
                      inner_hi, inner_lo, True, 1)
    z = {name: jnp.concatenate([fwd[name], bwd[name]], axis=0) for name in fwd}

    mask_bd = _block_ones(SCAN_W)
    mask_bf = mask_bd.astype(BF16)
    diag, tri, strict = _problem_masks()
    neg_diag = jnp.sum(jnp.where(diag, -z["gc"], 0.0), axis=1, keepdims=True)
    decay = jnp.exp(jnp.where(tri, z["gc"] + neg_diag, -jnp.inf))
    k_bd = _bd3(_bf(z["k"]), mask_bf)
    kq = _bdot_nt(_bf(jnp.concatenate([z["kb"], z["q"]], axis=1)), k_bd)
    qk = _bf(kq[:, CHUNK:] * decay)
    t_all = _bf(_neumann_inverse(jnp.where(strict, kq[:, 0:CHUNK] * decay, 0.0), mask_bf))
    u = _bdot(t_all, _bd3(_bf(z["vb"]), mask_bf))
    w = _bdot(t_all, _bd3(_bf(z["kbe"]), mask_bf))
    wq = _bf(jnp.concatenate([w, z["qd"]], axis=1))

    outs = (of_ref, ob_ref)
    state = [s_s[0], s_s[1]]
    for i in range(N_CHUNK):
        for d in range(N_DIR):
            c = N_CHUNK - 1 - i if d else i
            p = d * N_CHUNK + c
            s_bd = state[d]
            ws = jnp.dot(wq[p], _bf(s_bd), preferred_element_type=F32)
            v_new = u[p] - ws[0:CHUNK]
            v_bd = jnp.concatenate([_bf(v_new)] * N_HEADS, axis=0) * mask_bf
            outs[d][c * CHUNK:(c + 1) * CHUNK, :] = ws[CHUNK:] + jnp.dot(qk[p], v_bd,
                                                                        preferred_element_type=F32)
            state[d] = s_bd * z["gl"][p] + jnp.where(mask_bd, _dot(z["kd"][p].T, v_new), 0.0)
    s_s[0] = state[0]
    s_s[1] = state[1]

    @pl.when(segs.last(s))
    def _():
        for d in range(N_DIR):
            s_out[0, d] = _from_block_diag(state[d])


def _delta(layer, segs, zg, gates, conv_w, alog_rows, dtb_rows, state_s):
    n_tok = zg.shape[0]
    per = SEG // HALO
    n_halo = n_tok // HALO

    def cur(rev, w):
        return pl.BlockSpec((SEG, w), lambda s: (segs.blk(s, rev), 0))

    def prev(rev):
        return pl.BlockSpec((HALO, D_MODEL), lambda s: (jnp.maximum(segs.blk(s, rev) * per - 1, 0), 0))

    def nxt(rev):
        return pl.BlockSpec((HALO, D_MODEL),
                            lambda s: (jnp.minimum((segs.blk(s, rev) + 1) * per, n_halo - 1), 0))

    lane_rows = pl.BlockSpec((1, N_DIR, SCAN_W), lambda s: (layer, 0, 0))
    mat_in, mat_out, mat_shape = _state_specs(layer, segs)
    return pl.pallas_call(
        functools.partial(_delta_kernel, segs=segs),
        grid=(segs.n_seg,),
        in_specs=[
            prev(False), cur(False, D_MODEL), nxt(False),
            prev(True), cur(True, D_MODEL), nxt(True),
            cur(False, GATE_W), cur(True, GATE_W),
            pl.BlockSpec((1, CONV_W, 3 * SCAN_W), lambda s: (layer, 0, 0)),
            lane_rows, lane_rows, mat_in,
        ],
        out_specs=[cur(False, SCAN_W), cur(True, SCAN_W), mat_out],
        out_shape=[jax.ShapeDtypeStruct((n_tok, SCAN_W), F32)] * 2 + [mat_shape],
        scratch_shapes=[pltpu.VMEM((N_DIR, SCAN_W, SCAN_W), F32)],
        compiler_params=pltpu.CompilerParams(
            dimension_semantics=("arbitrary",), vmem_limit_bytes=VMEM_LIMIT),
        name="delta",
    )(zg, zg, zg, zg, zg, zg, gates, gates, conv_w, alog_rows, dtb_rows, state_s)


def _attend(q_ref, kv_blocks, o_ref):
    qt = q_ref[...].T
    zeros = jnp.zeros((HEAD_DIM, A_GROUP * SEG), BF16)
    ones = lambda n: jnp.ones((16, n), F32)
    kv = [(_bf(k), _bf(jnp.concatenate([v.T, ones(v.shape[0])], axis=0))) for k, v in kv_blocks]
    w_groups = []
    for g in range(A_KV_HEADS):
        heads = range(g * A_GROUP, (g + 1) * A_GROUP)
        slab = _bf(jnp.concatenate([qt[h * HEAD_DIM:(h + 1) * HEAD_DIM, :] for h in heads], axis=1))
        w_groups.append(jnp.concatenate([slab if i == g else zeros for i in range(A_KV_HEADS)], axis=0))
    w_all = jnp.concatenate(w_groups, axis=1)
    group_out = [None] * A_KV_HEADS
    for q0 in range(0, A_HEADS * SEG, ATT_LANES):
        w_c = w_all[:, q0:q0 + ATT_LANES]
        m = acc = None
        nxt = jnp.dot(kv[0][0], w_c, preferred_element_type=F32)
        for i, (_, v_t) in enumerate(kv):
            sc = nxt
            if i + 1 < len(kv):
                nxt = jnp.dot(kv[i + 1][0], w_c, preferred_element_type=F32)
            m_blk = jnp.max(sc, axis=0, keepdims=True)
            m_new = m_blk if m is None else jnp.maximum(m, m_blk)
            pv = jnp.dot(v_t, _bf(jnp.exp(sc - m_new)), preferred_element_type=F32)
            acc = pv if m is None else acc * jnp.exp(m - m_new) + pv
            m = m_new
        out = acc[0:KV_WIDTH] / acc[KV_WIDTH:KV_WIDTH + 1]
        for g in range(A_KV_HEADS):
            g0, g1 = g * A_GROUP * SEG, (g + 1) * A_GROUP * SEG
            lo, hi = max(g0, q0), min(g1, q0 + ATT_LANES)
            if lo < hi:
                piece = out[g * HEAD_DIM:(g + 1) * HEAD_DIM, lo - q0:hi - q0]
                group_out[g] = piece if group_out[g] is None else jnp.concatenate([group_out[g], piece], axis=1)
    for col in range(A_WIDTH // KV_WIDTH):
        g, h0 = (2 * col) // A_GROUP, (2 * col) % A_GROUP
        pair = jnp.concatenate([group_out[g][:, h0 * SEG:(h0 + 1) * SEG],
                                group_out[g][:, (h0 + 1) * SEG:(h0 + 2) * SEG]], axis=0)
        o_ref[:, col * KV_WIDTH:(col + 1) * KV_WIDTH] = pair.T


def _attn_kernel(q_ref, kc_ref, vc_ref, kl_ref, vl_ref, ck_ref, cv_ref, o_ref, *, segs):
    s = pl.program_id(0)

    @pl.when(segs.is_ctx(s))
    def _():
        _attend(q_ref, [(kc_ref[...], vc_ref[...])], o_ref)

    @pl.when(jnp.logical_not(segs.is_ctx(s)))
    def _():
        blocks = []
        past, t_lat = ck_ref.shape[2], kl_ref.shape[0]
        for lo in range(0, past, KV_BLOCK):
            hi = min(lo + KV_BLOCK, past)
            blocks.append((ck_ref[0, 0, lo:hi, :], cv_ref[0, 0, lo:hi, :]))
        for lo in range(0, t_lat, KV_BLOCK):
            hi = min(lo + KV_BLOCK, t_lat)
            blocks.append((kl_ref[lo:hi, :], vl_ref[lo:hi, :]))
        _attend(q_ref, blocks, o_ref)


def _attention(layer, segs, aq, ak, av, cache_k, cache_v):
    n_tok = aq.shape[0]
    t_lat = segs.per_seq * SEG
    past = cache_k.shape[2]
    assert t_lat % KV_BLOCK == 0
    lat_blocks_before = segs.n_ctx_seg * SEG // t_lat
    own = lambda w: pl.BlockSpec((SEG, w), lambda s: (s, 0))
    lat_kv = pl.BlockSpec((t_lat, KV_WIDTH), lambda s: (lat_blocks_before + segs.lat_seq(s), 0))
    cache = pl.BlockSpec((1, 1, past, KV_WIDTH), lambda s: (segs.lat_seq(s), layer, 0, 0))
    return pl.pallas_call(
        functools.partial(_attn_kernel, segs=segs),
        grid=(segs.n_seg,),
        in_specs=[own(A_WIDTH), own(KV_WIDTH), own(KV_WIDTH), lat_kv, lat_kv, cache, cache],
        out_specs=own(A_WIDTH),
        out_shape=jax.ShapeDtypeStruct((n_tok, A_WIDTH), F32),
        compiler_params=pltpu.CompilerParams(
            dimension_semantics=("arbitrary",), vmem_limit_bytes=VMEM_LIMIT),
        name="attention",
    )(aq, ak, av, ak, av, cache_k, cache_v)


def _permute_w_in(w_in):
    m, g, a, kv = SCAN_W, SCAN_W, A_WIDTH, KV_WIDTH
    sizes = (m, m, m, m, 8, 8, 3 * g, g, 8, 8, a, kv, kv)
    offs = [0]
    for sz in sizes:
        offs.append(offs[-1] + sz)
    piece = lambda i: w_in[..., offs[i]:offs[i + 1]]
    pad = jnp.zeros(w_in.shape[:-1] + (GATE_W - 32,), w_in.dtype)
    order = [0, 1, 2, 3, 6, 7, 10, 11, 12, 4, 5, 8, 9]
    return jnp.concatenate([piece(i) for i in order] + [pad], axis=-1)


def _rope_tables(t_lat):
    rows = t_lat // GRID_W
    row = np.repeat(np.arange(rows, dtype=np.float64), GRID_W)
    col = np.tile(np.arange(GRID_W, dtype=np.float64), rows)
    n_freq = HEAD_DIM // 4
    inv = ROPE_BASE ** (-np.arange(n_freq, dtype=np.float64) / n_freq)
    ang = np.stack([row[:, None] * inv, col[:, None] * inv], axis=1)
    cos, sin = np.cos(ang), np.sin(ang)
    cos_h = np.concatenate([cos, cos], axis=-1).reshape(t_lat, HEAD_DIM)
    sin_h = np.concatenate([-sin, sin], axis=-1).reshape(t_lat, HEAD_DIM)
    wide = lambda a: np.tile(a, (1, KV_WIDTH // HEAD_DIM))
    cos_t = np.concatenate([wide(cos_h), np.ones((TM, KV_WIDTH))], axis=0)
    sin_t = np.concatenate([wide(sin_h), np.zeros((TM, KV_WIDTH))], axis=0)
    return jnp.asarray(cos_t, F32), jnp.asarray(sin_t, F32)


def _head_lanes(x):
    return jnp.repeat(x, HEAD_DIM, axis=-1)


def kernel(x_prompt, x_sample, c, cache_k, cache_v, state_mlstm_C, state_mlstm_n, state_mlstm_m,
           state_delta_S, c_ctx, ada_w, ada_b, norm_g, ffn_w_in, ffn_w_out, w_in, w_out, mlstm_f_bias,
           mlstm_norm, delta_conv, delta_a_log, delta_dt_bias, delta_norm, attn_q_norm, attn_k_norm,
           final_norm):
    batch, t_ctx, d_model = x_prompt.shape
    n_lat_seq, t_lat, _ = x_sample.shape
    depth = ada_w.shape[0]
    assert d_model == D_MODEL and norm_g.shape[1] == 3 and ffn_w_in.shape[-1] == 2 * D_FF
    assert 1 + n_lat_seq <= MOD_ROWS
    n_ctx = batch * t_ctx
    n_tok = n_ctx + n_lat_seq * t_lat
    assert n_ctx % t_lat == 0
    segs = _Segs(n_ctx, t_ctx, n_lat_seq, t_lat)
    tiles = _Tiles(n_ctx, t_lat)

    cond = jnp.concatenate([c_ctx[None, :], c, jnp.zeros((MOD_ROWS - 1 - n_lat_seq, D_MODEL), F32)], axis=0)
    mod = _modulation(cond, ada_w, ada_b)

    wfi = _bf(ffn_w_in).reshape(depth * 2, D_MODEL, 2 * D_FF)
    wfo = _bf(ffn_w_out).reshape(depth * 2, D_FF, D_MODEL)
    wz = _bf(_permute_w_in(w_in))
    wo = _bf(w_out)
    qn = jnp.tile(attn_q_norm, (1, A_HEADS)).reshape(depth, 1, A_WIDTH)
    kn = jnp.tile(attn_k_norm, (1, A_KV_HEADS)).reshape(depth, 1, KV_WIDTH)
    gn = jnp.tile(delta_norm, (1, N_HEADS)).reshape(depth, 1, SCAN_W)
    mn = mlstm_norm.reshape(depth, 1, SCAN_W)
    fin = final_norm.reshape(1, D_MODEL)
    fb_cols = mlstm_f_bias.reshape(depth, N_GATE_ROWS, 1)
    alog_rows = _head_lanes(delta_a_log)
    dtb_rows = _head_lanes(delta_dt_bias)
    cos_t, sin_t = _rope_tables(t_lat)
    ck = cache_k.reshape(cache_k.shape[:3] + (KV_WIDTH,))
    cv = cache_v.reshape(cache_v.shape[:3] + (KV_WIDTH,))
    n0 = state_mlstm_n.reshape(n_lat_seq, depth, N_DIR, 1, SCAN_W)
    m0 = state_mlstm_m.reshape(n_lat_seq, depth, N_GATE_ROWS, 1)

    x_ctx, x_lat, lat_tile0 = x_prompt.reshape(n_ctx, D_MODEL), x_sample.reshape(n_tok - n_ctx, D_MODEL), 0
    ks, vs, cs, ns, ms, ss = [], [], [], [], [], []
    for l in range(depth):
        x1, zm, zg, aq, ak, av, gates = _dense1(l, tiles, n_tok, x_ctx, x_lat, lat_tile0, mod, norm_g,
                                                wfi, wfo, wz, qn, kn, cos_t, sin_t)
        hf, hb, c_new, n_new, m_new = _mlstm(l, segs, zm, gates, fb_cols, state_mlstm_C, n0, m0)
        o_f, o_b, s_new = _delta(l, segs, zg, gates, delta_conv, alog_rows, dtb_rows, state_delta_S)
        a_out = _attention(l, segs, aq, ak, av, ck, cv)
        x = _dense2(l, tiles, x1, mod, norm_g, hf, hb, zm, mn, o_f, o_b, zg, gn, a_out, wo, wfi, wfo, fin,
                    final=(l == depth - 1))
        if l < depth - 1:
            x_ctx, x_lat, lat_tile0 = x, x, tiles.n_ctx

        ks.append(ak[:n_ctx].reshape(batch, t_ctx, A_KV_HEADS, HEAD_DIM))
        vs.append(av[:n_ctx].reshape(batch, t_ctx, A_KV_HEADS, HEAD_DIM))
        cs.append(c_new[:batch])
        ns.append(n_new[:batch].reshape(batch, N_DIR, N_HEADS, HEAD_DIM))
        ms.append(m_new[:batch].reshape(batch, N_DIR, N_HEADS))
        ss.append(s_new[:batch])

    y_prompt = x[0].reshape(batch, t_ctx, D_MODEL)
    y_sample = x[1].reshape(n_lat_seq, t_lat, D_MODEL)
    stack = lambda xs: jnp.stack(xs, axis=1)
    return (y_prompt, y_sample, stack(ks), stack(vs), stack(cs), stack(ns), stack(ms), stack(ss))
```

```python
import functools
import math

import numpy as np

import jax
import jax.numpy as jnp
from jax import lax
from jax.experimental import pallas as pl
from jax.experimental.pallas import tpu as pltpu

F32 = jnp.float32
BF16 = jnp.bfloat16

D_MODEL = 1024
HEAD_DIM = 64
N_HEADS = 4
SCAN_W = N_HEADS * HEAD_DIM
A_HEADS = 8
A_KV_HEADS = 2
A_GROUP = A_HEADS // A_KV_HEADS
A_WIDTH = A_HEADS * HEAD_DIM
KV_WIDTH = A_KV_HEADS * HEAD_DIM
N_DIR = 2
CHUNK = 64
CONV_W = 5
D_FF = 2816
FF_CHUNK = 256
GRID_W = 64
ROPE_BASE = 10000.0
EPS = 1e-6
N_MOD = 9
MOD_ROWS = 8
GATE_W = 128
Z_WIDTH = 2 * D_MODEL + A_WIDTH + 2 * KV_WIDTH + GATE_W

SEG = 256
N_CHUNK = SEG // CHUNK
TM = 512
ROW_GROUP = 512
N_PAIR = 2
HALO = 8
KV_BLOCK = 512
ATT_LANES = 2048
VMEM_LIMIT = 56 * 1024 * 1024


def _bf(x):
    return x.astype(BF16)


def _dot(a, b):
    return jnp.dot(_bf(a), _bf(b), preferred_element_type=F32)


def _bdot(a, b):
    return lax.dot_general(a, b, (((2,), (1,)), ((0,), (0,))), preferred_element_type=F32)


def _bdot_nt(a, b):
    return lax.dot_general(a, b, (((2,), (2,)), ((0,), (0,))), preferred_element_type=F32)


def _split(a, terms):
    parts = []
    rest = a
    for _ in range(terms):
        p = _bf(rest)
        parts.append(p)
        rest = rest - p.astype(F32)
    return parts


def _dot_exact_rhs(a, b01, terms=2):
    parts = _split(a, terms)
    if a.shape[-1] % 128 == 0:
        return jnp.dot(jnp.concatenate(parts, axis=-1), jnp.concatenate([b01] * terms, axis=0),
                       preferred_element_type=F32)
    out = None
    for p in parts:
        d = jnp.dot(p, b01, preferred_element_type=F32)
        out = d if out is None else out + d
    return out


def _iota(shape, dim):
    return lax.broadcasted_iota(jnp.int32, shape, dim)


def _head_of(idx):
    return jnp.right_shift(idx, 6)


def _in_head(idx):
    return jnp.bitwise_and(idx, HEAD_DIM - 1)


def _block_ones(n):
    return (_head_of(_iota((n, n), 0)) == _head_of(_iota((n, n), 1)))


def _head_sum(x):
    n = x.shape[-1]
    return _dot_exact_rhs(x, _block_ones(n).astype(BF16), terms=1)


def _head_rms(x, g_row):
    ms = _head_sum(x * x) * (1.0 / HEAD_DIM)
    return x * lax.rsqrt(ms + EPS) * g_row


def _rms_mod(x, g_row, scale_row, shift_row):
    y = x * lax.rsqrt(jnp.mean(x * x, axis=-1, keepdims=True) + EPS) * g_row
    return y * (1.0 + scale_row) + shift_row


def _softplus(x):
    return jnp.maximum(x, 0.0) + jnp.log1p(jnp.exp(-jnp.abs(x)))


def _log_sigmoid(x):
    return -_softplus(-x)


def _sigmoid(x):
    return jax.nn.sigmoid(x)


def _silu(x):
    return x * jax.nn.sigmoid(x)


def _bd3(x, mask_bf):
    return jnp.concatenate([x] * N_HEADS, axis=1) * mask_bf


def _to_block_diag(x4):
    spread = (_iota((HEAD_DIM, SCAN_W), 0) == _in_head(_iota((HEAD_DIM, SCAN_W), 1))).astype(BF16)
    rows = x4.reshape(SCAN_W, HEAD_DIM)
    return jnp.where(_block_ones(SCAN_W), _dot_exact_rhs(rows, spread, terms=3), 0.0)


def _from_block_diag(x):
    fold = (_in_head(_iota((SCAN_W, HEAD_DIM), 0)) == _iota((SCAN_W, HEAD_DIM), 1)).astype(BF16)
    return _dot_exact_rhs(x, fold, terms=3).reshape(N_HEADS, HEAD_DIM, HEAD_DIM)


def _chunks(a):
    return a.reshape(N_CHUNK, CHUNK, SCAN_W)


def _problem_masks():
    shape = (N_PAIR * N_DIR * N_CHUNK, CHUNK, SCAN_W)
    t_idx, j_idx = _iota(shape, 1), _in_head(_iota(shape, 2))
    is_rev = jnp.bitwise_and(_iota(shape, 0), N_DIR * N_CHUNK - 1) >= N_CHUNK
    ahead = jnp.where(is_rev, j_idx - t_idx, t_idx - j_idx)
    return t_idx == j_idx, ahead >= 0, ahead > 0


def _scan_order(d):
    return range(N_CHUNK - 1, -1, -1) if d else range(N_CHUNK)


def _mod_kernel(cond_ref, w_ref, b_ref, o_ref):
    a = _silu(cond_ref[...])
    o_ref[0, 0] = _dot(a, w_ref[0]) + b_ref[0]


def _modulation(cond, ada_w, ada_b):
    depth = ada_w.shape[0]
    return pl.pallas_call(
        _mod_kernel,
        grid=(depth, N_MOD),
        in_specs=[
            pl.BlockSpec((MOD_ROWS, D_MODEL), lambda l, j: (0, 0)),
            pl.BlockSpec((1, D_MODEL, D_MODEL), lambda l, j: (l, 0, j)),
            pl.BlockSpec((1, 1, D_MODEL), lambda l, j: (l, 0, j)),
        ],
        out_specs=pl.BlockSpec((1, 1, MOD_ROWS, D_MODEL), lambda l, j: (l, j, 0, 0)),
        out_shape=jax.ShapeDtypeStruct((depth, N_MOD, MOD_ROWS, D_MODEL), F32),
        compiler_params=pltpu.CompilerParams(
            dimension_semantics=("arbitrary", "arbitrary"), vmem_limit_bytes=VMEM_LIMIT),
        name="modulation",
    )(cond, ada_w, ada_b.reshape(depth, 1, N_MOD * D_MODEL))


def _ffn(h, w_in_ref, w_out_ref):
    n_chunks = D_FF // FF_CHUNK

    def gate_up(c):
        lo = c * FF_CHUNK
        g = jnp.dot(h, w_in_ref[0, :, lo:lo + FF_CHUNK], preferred_element_type=F32)
        u = jnp.dot(h, w_in_ref[0, :, D_FF + lo:D_FF + lo + FF_CHUNK], preferred_element_type=F32)
        return g, u

    acc = None
    nxt = gate_up(0)
    for c in range(n_chunks):
        g, u = nxt
        if c + 1 < n_chunks:
            nxt = gate_up(c + 1)
        a = _bf(_silu(g) * u)
        d = jnp.dot(a, w_out_ref[0, c * FF_CHUNK:(c + 1) * FF_CHUNK, :], preferred_element_type=F32)
        acc = d if acc is None else acc + d
    return acc


def _rope(x, cos, sin_signed):
    w = x.shape[-1]
    reps = w // cos.shape[-1]
    if reps > 1:
        cos = jnp.concatenate([cos] * reps, axis=-1)
        sin_signed = jnp.concatenate([sin_signed] * reps, axis=-1)
    first = jnp.bitwise_and(_iota(x.shape, 1), 31) < 16
    partner = jnp.where(first, pltpu.roll(x, w - 16, 1), pltpu.roll(x, 16, 1))
    return x * cos + partner * sin_signed


class _Tiles:
    def __init__(self, n_ctx, t_lat):
        assert n_ctx % TM == 0 and t_lat % TM == 0
        self.n_ctx = n_ctx // TM
        self.per_seq = t_lat // TM

    def is_ctx(self, i):
        return i < self.n_ctx

    def lat(self, i):
        return jnp.maximum(i - self.n_ctx, 0)

    def mod_row(self, i):
        return jnp.where(self.is_ctx(i), 0, 1 + self.lat(i) // self.per_seq)


def _mod_rows(mod_ref, row):
    return lambda m: mod_ref[0, m, pl.ds(row, 1), :]


def _dense1_kernel(xa_ref, xb_ref, mod_ref, ng_ref, wfi_ref, wfo_ref, wz_ref, qn_ref, kn_ref, cos_ref, sin_ref,
                   x1_ref, zm_ref, zg_ref, aq_ref, ak_ref, av_ref, gate_ref, *, tiles):
    i = pl.program_id(0)
    mod = _mod_rows(mod_ref, tiles.mod_row(i))
    is_ctx = tiles.is_ctx(i)
    for r in range(TM // ROW_GROUP):
        rows = slice(r * ROW_GROUP, (r + 1) * ROW_GROUP)
        x = jnp.where(is_ctx, xa_ref[rows, :], xb_ref[rows, :])
        h = _bf(_rms_mod(x, ng_ref[0, 0:1], mod(0), mod(1)))
        x1 = x + 0.5 * mod(2) * _ffn(h, wfi_ref, wfo_ref)
        x1_ref[rows, :] = x1
        h2 = _bf(_rms_mod(x1, ng_ref[0, 1:2], mod(3), mod(4)))

        z = jnp.dot(h2, wz_ref[0], preferred_element_type=F32)

        def proj(lo, width):
            return z[:, lo:lo + width]

        zm_ref[rows, :] = proj(0, D_MODEL)
        zg_ref[rows, :] = proj(D_MODEL, D_MODEL)
        cos = cos_ref[rows, :]
        sin = sin_ref[rows, :]
        off = 2 * D_MODEL
        q = _head_rms(proj(off, A_WIDTH), qn_ref[0])
        aq_ref[rows, :] = _rope(q, cos, sin) * (1.0 / math.sqrt(HEAD_DIM))
        k = _head_rms(proj(off + A_WIDTH, KV_WIDTH), kn_ref[0])
        ak_ref[rows, :] = _rope(k, cos, sin)
        av_ref[rows, :] = proj(off + A_WIDTH + KV_WIDTH, KV_WIDTH)
        gate_ref[rows, :] = proj(off + A_WIDTH + 2 * KV_WIDTH, GATE_W)


def _resident(shape, index_map):
    return pl.BlockSpec(shape, index_map, pipeline_mode=pl.Buffered(1))


def _dense1(layer, tiles, n_tok, x_ctx, x_lat, lat_tile0, mod, norm_g, wfi, wfo, wz, qn, kn, cos_t, sin_t):
    row = lambda w: pl.BlockSpec((TM, w), lambda i: (i, 0))
    rope_tile = lambda i: jnp.where(tiles.is_ctx(i), tiles.per_seq, tiles.lat(i) % tiles.per_seq)
    rope = pl.BlockSpec((TM, KV_WIDTH), lambda i: (rope_tile(i), 0))
    out_w = (D_MODEL, D_MODEL, D_MODEL, A_WIDTH, KV_WIDTH, KV_WIDTH, GATE_W)
    return pl.pallas_call(
        functools.partial(_dense1_kernel, tiles=tiles),
        grid=(n_tok // TM,),
        in_specs=[
            pl.BlockSpec((TM, D_MODEL), lambda i: (jnp.minimum(i, tiles.n_ctx - 1), 0)),
            pl.BlockSpec((TM, D_MODEL), lambda i: (tiles.lat(i) + lat_tile0, 0)),
            pl.BlockSpec((1, N_MOD, MOD_ROWS, D_MODEL), lambda i: (layer, 0, 0, 0)),
            pl.BlockSpec((1, 3, D_MODEL), lambda i: (layer, 0, 0)),
            _resident((1, D_MODEL, 2 * D_FF), lambda i: (2 * layer, 0, 0)),
            _resident((1, D_FF, D_MODEL), lambda i: (2 * layer, 0, 0)),
            _resident((1, D_MODEL, Z_WIDTH), lambda i: (layer, 0, 0)),
            pl.BlockSpec((1, 1, A_WIDTH), lambda i: (layer, 0, 0)),
            pl.BlockSpec((1, 1, KV_WIDTH), lambda i: (layer, 0, 0)),
            rope, rope,
        ],
        out_specs=[row(w) for w in out_w],
        out_shape=[jax.ShapeDtypeStruct((n_tok, w), F32) for w in out_w],
        compiler_params=pltpu.CompilerParams(
            dimension_semantics=("arbitrary",), vmem_limit_bytes=VMEM_LIMIT),
        name="dense1",
    )(x_ctx, x_lat, mod, norm_g, wfi, wfo, wz, qn, kn, cos_t, sin_t)


def _dense2_kernel(x_ref, mod_ref, ng_ref, hf_ref, hb_ref, mo_ref, mn_ref, of_ref, ob_ref, gz_ref, gn_ref,
                   ao_ref, wo_ref, wfi_ref, wfo_ref, fin_ref, *out_refs, tiles, final):
    i = pl.program_id(0)
    mod = _mod_rows(mod_ref, tiles.mod_row(i))
    m_out = _head_rms(hf_ref[...] + hb_ref[...], mn_ref[0]) * _sigmoid(mo_ref[...])
    g_out = _head_rms(of_ref[...] + ob_ref[...], gn_ref[0]) * _silu(gz_ref[...])
    mix = jnp.dot(_bf(m_out), wo_ref[0, 0:SCAN_W, :], preferred_element_type=F32)
    mix += jnp.dot(_bf(g_out), wo_ref[0, SCAN_W:2 * SCAN_W, :], preferred_element_type=F32)
    mix += jnp.dot(_bf(ao_ref[...]), wo_ref[0, 2 * SCAN_W:, :], preferred_element_type=F32)
    x2 = x_ref[...] + mod(5) * mix
    h = _bf(_rms_mod(x2, ng_ref[0, 2:3], mod(6), mod(7)))
    x3 = x2 + 0.5 * mod(8) * _ffn(h, wfi_ref, wfo_ref)
    if not final:
        out_refs[0][...] = x3
        return
    y = x3 * lax.rsqrt(jnp.mean(x3 * x3, axis=-1, keepdims=True) + EPS) * fin_ref[...]
    y_ctx_ref, y_lat_ref = out_refs

    @pl.when(tiles.is_ctx(i))
    def _():
        y_ctx_ref[...] = y

    @pl.when(jnp.logical_not(tiles.is_ctx(i)))
    def _():
        y_lat_ref[...] = y


def _dense2(layer, tiles, x1, mod, norm_g, hf, hb, zm, m_norm, of, ob, zg, g_norm, a_out, wo, wfi, wfo, fin,
            final):
    n_tok = x1.shape[0]
    row = lambda w: pl.BlockSpec((TM, w), lambda i: (i, 0))
    last_quarter = pl.BlockSpec((TM, SCAN_W), lambda i: (i, 3))
    lane_row = lambda w: pl.BlockSpec((1, 1, w), lambda i: (layer, 0, 0))
    if final:
        n_ctx = tiles.n_ctx * TM
        out_specs = [pl.BlockSpec((TM, D_MODEL), lambda i: (jnp.minimum(i, tiles.n_ctx - 1), 0)),
                     pl.BlockSpec((TM, D_MODEL), lambda i: (tiles.lat(i), 0))]
        out_shape = [jax.ShapeDtypeStruct((n_ctx, D_MODEL), F32),
                     jax.ShapeDtypeStruct((n_tok - n_ctx, D_MODEL), F32)]
    else:
        out_specs, out_shape = row(D_MODEL), jax.ShapeDtypeStruct((n_tok, D_MODEL), F32)
    return pl.pallas_call(
        functools.partial(_dense2_kernel, tiles=tiles, final=final),
        grid=(n_tok // TM,),
        in_specs=[
            row(D_MODEL),
            pl.BlockSpec((1, N_MOD, MOD_ROWS, D_MODEL), lambda i: (layer, 0, 0, 0)),
            pl.BlockSpec((1, 3, D_MODEL), lambda i: (layer, 0, 0)),
            row(SCAN_W), row(SCAN_W), last_quarter, lane_row(SCAN_W),
            row(SCAN_W), row(SCAN_W), last_quarter, lane_row(SCAN_W),
            row(A_WIDTH),
            _resident((1, D_MODEL, D_MODEL), lambda i: (layer, 0, 0)),
            _resident((1, D_MODEL, 2 * D_FF), lambda i: (2 * layer + 1, 0, 0)),
            _resident((1, D_FF, D_MODEL), lambda i: (2 * layer + 1, 0, 0)),
            pl.BlockSpec((1, D_MODEL), lambda i: (0, 0)),
        ],
        out_specs=out_specs,
        out_shape=out_shape,
        compiler_params=pltpu.CompilerParams(
            dimension_semantics=("arbitrary",), vmem_limit_bytes=VMEM_LIMIT),
        name="dense2",
    )(x1, mod, norm_g, hf, hb, zm, m_norm, of, ob, zg, g_norm, a_out, wo, wfi, wfo, fin)


class _Segs:
    def __init__(self, n_ctx, t_ctx, n_lat_seq, t_lat):
        assert t_ctx == SEG and t_lat % SEG == 0
        self.n_ctx_seg = n_ctx // SEG
        self.per_seq = t_lat // SEG
        self.n_lat_seq = n_lat_seq
        self.n_seg = self.n_ctx_seg + n_lat_seq * self.per_seq
        self.n_seq = self.n_ctx_seg + n_lat_seq

    def is_ctx(self, s):
        return s < self.n_ctx_seg

    def lat(self, s):
        r = jnp.maximum(s - self.n_ctx_seg, 0)
        return r // self.per_seq, r % self.per_seq

    def blk(self, s, rev):
        b, j = self.lat(s)
        j = self.per_seq - 1 - j if rev else j
        return jnp.where(self.is_ctx(s), s, self.n_ctx_seg + b * self.per_seq + j)

    def seq(self, s):
        return jnp.where(self.is_ctx(s), s, self.n_ctx_seg + self.lat(s)[0])

    def lat_seq(self, s):
        return self.lat(s)[0]

    def first(self, s):
        return jnp.logical_or(self.is_ctx(s), self.lat(s)[1] == 0)

    def last(self, s):
        return jnp.logical_or(self.is_ctx(s), self.lat(s)[1] == self.per_seq - 1)


_STATE_MAT = (N_DIR, N_HEADS, HEAD_DIM, HEAD_DIM)


class _Pairs:
    def __init__(self, n_ctx, t_ctx, n_lat_seq, t_lat):
        assert t_ctx == SEG and t_lat % SEG == 0 and n_ctx == N_PAIR * t_lat and n_lat_seq % N_PAIR == 0
        self.rows = t_lat
        self.per_seq = t_lat // SEG
        self.n_ctx_step = t_lat // t_ctx
        self.n_step = self.n_ctx_step + (n_lat_seq // N_PAIR) * self.per_seq
        self.n_slot = self.n_ctx_step + n_lat_seq // N_PAIR

    def view(self, x):
        return x.reshape(x.shape[0] // self.rows, self.rows, x.shape[1])

    def is_ctx(self, s):
        return s < self.n_ctx_step

    def lat(self, s):
        r = jnp.maximum(s - self.n_ctx_step, 0)
        return r // self.per_seq, r % self.per_seq

    def macro(self, s):
        return jnp.where(self.is_ctx(s), 0, 1 + self.lat(s)[0])

    def seg(self, s, rev):
        j = self.lat(s)[1]
        return jnp.where(self.is_ctx(s), s, self.per_seq - 1 - j if rev else j)

    def slot(self, s):
        return jnp.where(self.is_ctx(s), s, self.n_ctx_step + self.lat(s)[0])

    def first(self, s):
        return jnp.logical_or(self.is_ctx(s), self.lat(s)[1] == 0)

    def last(self, s):
        return jnp.logical_or(self.is_ctx(s), self.lat(s)[1] == self.per_seq - 1)

    def block(self, rev, w):
        return pl.BlockSpec((N_PAIR, SEG, w), lambda s: (self.macro(s), self.seg(s, rev), 0))

    def state_in(self, layer, tail):
        zeros = (0,) * len(tail)
        return pl.BlockSpec((N_PAIR, 1) + tail, lambda s: (self.lat(s)[0], layer) + zeros)

    def state_out(self, tail):
        zeros = (0,) * len(tail)
        spec = pl.BlockSpec((1, N_PAIR) + tail, lambda s: (self.slot(s), 0) + zeros)
        return spec, jax.ShapeDtypeStruct((self.n_slot, N_PAIR) + tail, F32)

    def ctx_states(self, x):
        ctx = jnp.swapaxes(x[:self.n_ctx_step], 0, 1)
        return ctx.reshape((N_PAIR * self.n_ctx_step,) + x.shape[2:])


N_GATE_ROWS = N_DIR * N_HEADS


def _lane_scan(x, op, ident, rev):
    n = x.shape[-1]
    pos = _in_head(_iota(x.shape, x.ndim - 1))
    s = 1
    while s < CHUNK:
        if rev:
            y, ok = pltpu.roll(x, n - s, x.ndim - 1), pos < CHUNK - s
        else:
            y, ok = pltpu.roll(x, s, x.ndim - 1), pos >= s
        x = op(x, jnp.where(ok, y, ident))
        s *= 2
    return x


def _pieces(x, terms):
    out = []
    for _ in range(terms):
        p = _bf(x).astype(F32)
        out.append(p)
        x = x - p
    return jnp.concatenate(out, axis=0)


def _expander(n_qty):
    k = np.arange(SCAN_W)[:, None]
    n = np.arange(N_DIR * n_qty * SCAN_W)[None, :]
    rows_per_piece = n_qty * N_GATE_ROWS
    assert 3 * rows_per_piece <= SCAN_W
    qty, row = (k % rows_per_piece) // N_GATE_ROWS, k % N_GATE_ROWS
    d, qty_n, head = n // (n_qty * SCAN_W), (n // SCAN_W) % n_qty, (n % SCAN_W) // HEAD_DIM
    used = k < 3 * rows_per_piece
    return jnp.asarray(used & (qty == qty_n) & (row == d * N_HEADS + head), BF16)


def _expand(quantities, expand_ref):
    n_qty = len(quantities)
    packed = jnp.concatenate(quantities, axis=0)
    pad = jnp.zeros((SCAN_W - 3 * packed.shape[0], SEG), F32)
    lhs = _bf(jnp.concatenate([_pieces(packed, 3), pad], axis=0).T)
    wide = jnp.dot(lhs, expand_ref[...], preferred_element_type=F32)

    def stack(j):
        cols = [wide[:, (d * n_qty + j) * SCAN_W:(d * n_qty + j + 1) * SCAN_W] for d in range(N_DIR)]
        return jnp.concatenate([_chunks(x) for x in cols], axis=0)

    return [stack(j) for j in range(n_qty)]


def _gate_rows(gf_ref, gb_ref):
    is_b = _iota((N_GATE_ROWS, SEG), 0) >= N_HEADS
    gt_f, gt_b = gf_ref[...].T, gb_ref[...].T
    return is_b, lambda lo: jnp.where(is_b, gt_b[lo:lo + N_GATE_ROWS], gt_f[lo:lo + N_GATE_ROWS])


def _chunk_sums(x, is_b):
    s_idx, t_idx = _iota((SEG, 2 * SEG), 0), _iota((SEG, 2 * SEG), 1)
    t_loc = jnp.bitwise_and(t_idx, SEG - 1)
    tri = jnp.logical_and(_head_of(s_idx) == _head_of(t_loc),
                          jnp.where(t_idx >= SEG, s_idx - t_loc, t_loc - s_idx) >= 0).astype(BF16)
    sums = jnp.dot(_bf(_pieces(x, 3)), tri, preferred_element_type=F32)
    sums = sums[0:N_GATE_ROWS] + sums[N_GATE_ROWS:2 * N_GATE_ROWS] + sums[2 * N_GATE_ROWS:]
    return jnp.where(is_b, sums[:, SEG:], sums[:, 0:SEG])


def _on_lanes(cols, mirrored, is_b):
    chunk_of = _head_of(_iota((N_GATE_ROWS, SEG), 1))
    out = jnp.zeros((N_GATE_ROWS, SEG), F32)
    for c in range(N_CHUNK):
        col = jnp.where(is_b, cols[N_CHUNK - 1 - c], cols[c]) if mirrored else cols[c]
        out = jnp.where(chunk_of == c, col, out)
    return out


def _mlstm_compact(gf_ref, gb_ref, fb_col, m_col, expand2_ref, expand4_ref):
    is_b, pick = _gate_rows(gf_ref, gb_ref)
    ig = pick(0)
    lf = _log_sigmoid(pick(N_GATE_ROWS) + fb_col)
    b = _chunk_sums(lf, is_b)
    r = ig - b
    cm = jnp.where(is_b, _lane_scan(r, jnp.maximum, -jnp.inf, True), _lane_scan(r, jnp.maximum, -jnp.inf, False))
    on_lanes = functools.partial(_on_lanes, is_b=is_b)
    in_chunk = lambda x, c: x[:, c * CHUNK:(c + 1) * CHUNK]
    bl_cols = [jnp.sum(in_chunk(lf, c), axis=1, keepdims=True) for c in range(N_CHUNK)]
    bl = on_lanes(bl_cols, False)
    lw = bl - b + ig
    lw_cols = [jnp.max(in_chunk(lw, c), axis=1, keepdims=True) for c in range(N_CHUNK)]

    is_b_col = is_b[:, 0:1]
    at_step = lambda cols, i: jnp.where(is_b_col, cols[N_CHUNK - 1 - i], cols[i])
    m_in, m_out = [], []
    for i in range(N_CHUNK):
        m_in.append(m_col)
        m_col = jnp.maximum(at_step(bl_cols, i) + m_col, at_step(lw_cols, i))
        m_out.append(m_col)
    m_c, m_n = on_lanes(m_in, True), on_lanes(m_out, True)
    kw_fac, dec = _expand([jnp.exp(lw - m_n), jnp.exp(bl + m_c - m_n)], expand2_ref)
    mx = jnp.maximum(m_c, cm)
    r, mx, a_int, neg_mt = _expand([r, mx, jnp.exp(m_c - mx), -(b + mx)], expand4_ref)
    return (r, mx, a_int, neg_mt, kw_fac, dec), m_col


def _mlstm_kernel(zmf_ref, zmb_ref, gf_ref, gb_ref, fb_ref, expand2_ref, expand4_ref, c0_ref, n0_ref, m0_ref,
                  hf_ref, hb_ref, c_out, n_out, m_out, c_s, n_s, m_s, *, pairs):
    s = pl.program_id(0)
    members = range(N_PAIR)

    @pl.when(pairs.first(s))
    def _():
        ctx = pairs.is_ctx(s)
        for m in members:
            m_s[m] = jnp.where(ctx, 0.0, m0_ref[m, 0])
            for d in range(N_DIR):
                c_s[m, d] = jnp.where(ctx, 0.0, _to_block_diag(c0_ref[m, 0, d]))
                n_s[m, d] = jnp.where(ctx, 0.0, n0_ref[m, 0, d])

    gate_stacks, m_fin = [], []
    for m in members:
        stacks, m_col = _mlstm_compact(gf_ref.at[m], gb_ref.at[m], fb_ref[0], m_s[m], expand2_ref, expand4_ref)
        gate_stacks.append(stacks)
        m_fin.append(m_col)
    r, mx, a_int, neg_mt, kw_fac, dec = (jnp.concatenate([g[j] for g in gate_stacks], axis=0) for j in range(6))
    col = lambda j: jnp.concatenate([_chunks(ref[m, :, j * SCAN_W:(j + 1) * SCAN_W])
                                     for m in members for ref in (zmf_ref, zmb_ref)], axis=0)
    q, k, v = col(0), col(1) * (1.0 / math.sqrt(HEAD_DIM)), col(2)

    diag, tri, _ = _problem_masks()
    mask_bd = _block_ones(SCAN_W)
    mask_bf = mask_bd.astype(BF16)

    r_row = jnp.sum(jnp.where(diag, r, 0.0), axis=1, keepdims=True)
    w = jnp.exp(jnp.where(tri, r_row - mx, -jnp.inf))
    sc = _bdot_nt(_bf(q), _bd3(_bf(k), mask_bf)) * w
    kw = kw_fac * k
    kw_sum = jnp.sum(kw, axis=1, keepdims=True)

    n_prob = N_PAIR * N_DIR * N_CHUNK
    c_at, n_at = [None] * n_prob, [None] * n_prob
    c_fin, n_fin = {}, {}
    for m in members:
        for d in range(N_DIR):
            c_bd, n_row = c_s[m, d], n_s[m, d]
            for c in _scan_order(d):
                p = (m * N_DIR + d) * N_CHUNK + c
                c_at[p], n_at[p] = c_bd, n_row
                dec_row = dec[p, 0:1, :]
                c_bd = c_bd * dec_row + jnp.where(mask_bd, _dot(kw[p].T, v[p]), 0.0)
                n_row = n_row * dec_row + kw_sum[p]
            c_fin[m, d], n_fin[m, d] = c_bd, n_row
    c_all = jnp.concatenate([_bf(x)[None] for x in c_at], axis=0)
    n_all = jnp.concatenate([x[None] for x in n_at], axis=0)

    num = _bdot(_bf(sc), _bd3(_bf(v), mask_bf)) + a_int * _bdot(_bf(q), c_all)
    n_rows = n_prob * CHUNK
    flat = lambda a: a.reshape(n_rows, SCAN_W)
    sums = _dot_exact_rhs(jnp.concatenate([flat(sc), flat(q * n_all)], axis=0), mask_bf)
    den = sums[0:n_rows].reshape(sc.shape) + a_int * sums[n_rows:].reshape(sc.shape)
    h = num / jnp.maximum(jnp.abs(den), jnp.exp(neg_mt))
    for m in members:
        for d, out_ref in enumerate((hf_ref, hb_ref)):
            p0 = (m * N_DIR + d) * N_CHUNK
            out_ref[m] = h[p0:p0 + N_CHUNK].reshape(SEG, SCAN_W)

    for m in members:
        m_s[m] = m_fin[m]
        for d in range(N_DIR):
            c_s[m, d] = c_fin[m, d]
            n_s[m, d] = n_fin[m, d]

    @pl.when(pairs.last(s))
    def _():
        for m in members:
            m_out[0, m] = m_fin[m]
            for d in range(N_DIR):
                c_out[0, m, d] = _from_block_diag(c_fin[m, d])
                n_out[0, m, d] = n_fin[m, d]


def _mlstm(layer, pairs, zm, gates, fb_cols, state_c, n0, m0):
    row_tail, col_tail = (N_DIR, 1, SCAN_W), (N_GATE_ROWS, 1)
    c_spec, c_shape = pairs.state_out(_STATE_MAT)
    n_spec, n_shape = pairs.state_out(row_tail)
    m_spec, m_shape = pairs.state_out(col_tail)
    expand2, expand4 = _expander(2), _expander(4)
    h_shape = jax.ShapeDtypeStruct(zm.shape[:2] + (SCAN_W,), F32)
    return pl.pallas_call(
        functools.partial(_mlstm_kernel, pairs=pairs),
        grid=(pairs.n_step,),
        in_specs=[
            pairs.block(False, D_MODEL), pairs.block(True, D_MODEL),
            pairs.block(False, GATE_W), pairs.block(True, GATE_W),
            pl.BlockSpec((1, N_GATE_ROWS, 1), lambda s: (layer, 0, 0)),
            pl.BlockSpec(expand2.shape, lambda s: (0, 0)),
            pl.BlockSpec(expand4.shape, lambda s: (0, 0)),
            pairs.state_in(layer, _STATE_MAT), pairs.state_in(layer, row_tail), pairs.state_in(layer, col_tail),
        ],
        out_specs=[pairs.block(False, SCAN_W), pairs.block(True, SCAN_W), c_spec, n_spec, m_spec],
        out_shape=[h_shape, h_shape, c_shape, n_shape, m_shape],
        scratch_shapes=[pltpu.VMEM((N_PAIR, N_DIR, SCAN_W, SCAN_W), F32),
                        pltpu.VMEM((N_PAIR, N_DIR, 1, SCAN_W), F32),
                        pltpu.VMEM((N_PAIR, N_GATE_ROWS, 1), F32)],
        compiler_params=pltpu.CompilerParams(
            dimension_semantics=("arbitrary",), vmem_limit_bytes=VMEM_LIMIT),
        name="mlstm",
    )(zm, zm, gates, gates, fb_cols, expand2, expand4, state_c, n0, m0)


def _short_conv(prev_ref, cur_ref, next_ref, w_ref, has_prev, has_next):
    w3 = 3 * SCAN_W
    prev = jnp.where(has_prev, prev_ref[:, 0:w3], 0.0)
    nxt = jnp.where(has_next, next_ref[:, 0:w3], 0.0)
    xp = jnp.concatenate([prev, cur_ref[:, 0:w3], nxt], axis=0)
    rows = xp.shape[0]
    acc = None
    for i in range(CONV_W):
        shift = (CONV_W // 2 - i) % rows
        y = xp if shift == 0 else pltpu.roll(xp, shift, 0)
        t = y[HALO:HALO + SEG] * w_ref[0, i:i + 1, :]
        acc = t if acc is None else acc + t
    return acc


def _neumann_inverse(n_all, mask_bf):
    eye = (_iota(n_all.shape[1:], 0) == _in_head(_iota(n_all.shape[1:], 1))).astype(F32)
    p = -n_all
    t = eye + p
    levels = 6
    for lvl in range(levels):
        first, last = lvl == 0, lvl == levels - 1
        p_hi, p_lo = _split(p, 2)
        w_hi, w_lo = _bd3(p_hi, mask_bf), _bd3(p_lo, mask_bf)
        lhs_hi, lhs_lo = [], []
        if not first:
            t_hi, t_lo = _split(t, 2)
            lhs_hi += [t_hi, t_lo]
            lhs_lo += [t_hi]
        if not last:
            lhs_hi += [p_hi, p_lo]
            lhs_lo += [p_hi]
        a = _bdot(jnp.concatenate(lhs_hi, axis=1), w_hi)
        b = _bdot(jnp.concatenate(lhs_lo, axis=1) if len(lhs_lo) > 1 else lhs_lo[0], w_lo)
        ra, rb = 0, 0
        if not first:
            t = t + (a[:, 0:CHUNK] + a[:, CHUNK:2 * CHUNK] + b[:, 0:CHUNK])
            ra, rb = 2 * CHUNK, CHUNK
        if not last:
            p = a[:, ra:ra + CHUNK] + a[:, ra + CHUNK:ra + 2 * CHUNK] + b[:, rb:rb + CHUNK]
    return t


def _delta_qkv(prev_ref, cur_ref, next_ref, cw_ref, has_prev, has_next):
    qkv = _silu(_short_conv(prev_ref, cur_ref, next_ref, cw_ref, has_prev, has_next))

    def l2(a):
        return a * lax.rsqrt(_head_sum(a * a) + EPS)

    q = l2(qkv[:, 0:SCAN_W]) * (1.0 / math.sqrt(HEAD_DIM))
    k = l2(qkv[:, SCAN_W:2 * SCAN_W])
    return _chunks(q), _chunks(k), _chunks(qkv[:, 2 * SCAN_W:3 * SCAN_W])


def _delta_gates(gf_ref, gb_ref, alog_col, dtb_col, expand_ref):
    is_b, pick = _gate_rows(gf_ref, gb_ref)
    g = -jnp.exp(alog_col) * _softplus(pick(2 * N_GATE_ROWS) + dtb_col)
    beta = _sigmoid(pick(3 * N_GATE_ROWS))
    gc = _chunk_sums(g, is_b)
    totals = [jnp.sum(g[:, c * CHUNK:(c + 1) * CHUNK], axis=1, keepdims=True) for c in range(N_CHUNK)]
    gl = _on_lanes(totals, False, is_b)
    return _expand([gc, jnp.exp(gc), jnp.exp(gl - gc), jnp.exp(gl), beta], expand_ref)


def _delta_kernel(pf_ref, cf_ref, nf_ref, pb_ref, cb_ref, nb_ref, gf_ref, gb_ref, cw_ref, al_ref, dt_ref,
                  expand_ref, s0_ref, of_ref, ob_ref, s_out, s_s, *, pairs):
    s = pl.program_id(0)
    members = range(N_PAIR)
    lat = jnp.logical_not(pairs.is_ctx(s))
    j = pairs.lat(s)[1]

    @pl.when(pairs.first(s))
    def _():
        for m in members:
            for d in range(N_DIR):
                s_s[m, d] = jnp.where(lat, _to_block_diag(s0_ref[m, 0, d]), 0.0)

    inner_lo = jnp.logical_and(lat, j > 0)
    inner_hi = jnp.logical_and(lat, j < pairs.per_seq - 1)
    gate_stacks, qkv_stacks = [], []
    for m in members:
        gate_stacks.append(_delta_gates(gf_ref.at[m], gb_ref.at[m], al_ref[0], dt_ref[0], expand_ref))
        qkv_stacks.append(_delta_qkv(pf_ref.at[m], cf_ref.at[m], nf_ref.at[m], cw_ref, inner_lo, inner_hi))
        qkv_stacks.append(_delta_qkv(pb_ref.at[m], cb_ref.at[m], nb_ref.at[m], cw_ref, inner_hi, inner_lo))
    gc, eg, k_fac, g_last, beta = (jnp.concatenate([g[i] for g in gate_stacks], axis=0) for i in range(5))
    q, k, v = (jnp.concatenate([x[i] for x in qkv_stacks], axis=0) for i in range(3))
    kb = k * beta
    k_dec = k * k_fac

    mask_bd = _block_ones(SCAN_W)
    mask_bf = mask_bd.astype(BF16)
    diag, tri, strict = _problem_masks()
    neg_diag = jnp.sum(jnp.where(diag, -gc, 0.0), axis=1, keepdims=True)
    decay = jnp.exp(jnp.where(tri, gc + neg_diag, -jnp.inf))
    k_bd = _bd3(_bf(k), mask_bf)
    kq = _bdot_nt(_bf(jnp.concatenate([kb, q], axis=1)), k_bd)
    qk = _bf(kq[:, CHUNK:] * decay)
    t_all = _bf(_neumann_inverse(jnp.where(strict, kq[:, 0:CHUNK] * decay, 0.0), mask_bf))
    u = _bdot(t_all, _bd3(_bf(v * beta), mask_bf))
    w = _bdot(t_all, _bd3(_bf(kb * eg), mask_bf))
    wq = _bf(jnp.concatenate([w, q * eg], axis=1))

    outs = (of_ref, ob_ref)
    state = {(m, d): s_s[m, d] for m in members for d in range(N_DIR)}
    for i in range(N_CHUNK):
        for m in members:
            for d in range(N_DIR):
                c = N_CHUNK - 1 - i if d else i
                p = (m * N_DIR + d) * N_CHUNK + c
                s_bd = state[m, d]
                ws = jnp.dot(wq[p], _bf(s_bd), preferred_element_type=F32)
                v_new = u[p] - ws[0:CHUNK]
                v_bd = jnp.concatenate([_bf(v_new)] * N_HEADS, axis=0) * mask_bf
                outs[d][m, c * CHUNK:(c + 1) * CHUNK, :] = ws[CHUNK:] + jnp.dot(qk[p], v_bd,
                                                                               preferred_element_type=F32)
                state[m, d] = s_bd * g_last[p, 0:1, :] + jnp.where(mask_bd, _dot(k_dec[p].T, v_new), 0.0)
    for m in members:
        for d in range(N_DIR):
            s_s[m, d] = state[m, d]

    @pl.when(pairs.last(s))
    def _():
        for m in members:
            for d in range(N_DIR):
                s_out[0, m, d] = _from_block_diag(state[m, d])


def _delta(layer, pairs, zg, gates, conv_w, alog_cols, dtb_cols, state_s):
    per = SEG // HALO
    last_halo = pairs.rows // HALO - 1

    def prev(rev):
        return pl.BlockSpec((N_PAIR, HALO, D_MODEL),
                            lambda s: (pairs.macro(s), jnp.maximum(pairs.seg(s, rev) * per - 1, 0), 0))

    def nxt(rev):
        return pl.BlockSpec((N_PAIR, HALO, D_MODEL),
                            lambda s: (pairs.macro(s), jnp.minimum((pairs.seg(s, rev) + 1) * per, last_halo), 0))

    gate_cols = pl.BlockSpec((1, N_GATE_ROWS, 1), lambda s: (layer, 0, 0))
    s_spec, s_shape = pairs.state_out(_STATE_MAT)
    expand5 = _expander(5)
    o_shape = jax.ShapeDtypeStruct(zg.shape[:2] + (SCAN_W,), F32)
    return pl.pallas_call(
        functools.partial(_delta_kernel, pairs=pairs),
        grid=(pairs.n_step,),
        in_specs=[
            prev(False), pairs.block(False, D_MODEL), nxt(False),
            prev(True), pairs.block(True, D_MODEL), nxt(True),
            pairs.block(False, GATE_W), pairs.block(True, GATE_W),
            pl.BlockSpec((1, CONV_W, 3 * SCAN_W), lambda s: (layer, 0, 0)),
            gate_cols, gate_cols, pl.BlockSpec(expand5.shape, lambda s: (0, 0)),
            pairs.state_in(layer, _STATE_MAT),
        ],
        out_specs=[pairs.block(False, SCAN_W), pairs.block(True, SCAN_W), s_spec],
        out_shape=[o_shape, o_shape, s_shape],
        scratch_shapes=[pltpu.VMEM((N_PAIR, N_DIR, SCAN_W, SCAN_W), F32)],
        compiler_params=pltpu.CompilerParams(
            dimension_semantics=("arbitrary",), vmem_limit_bytes=VMEM_LIMIT),
        name="delta",
    )(zg, zg, zg, zg, zg, zg, gates, gates, conv_w, alog_cols, dtb_cols, expand5, state_s)


def _attend(q_ref, kv_blocks, o_ref):
    qt = q_ref[...].T
    zeros = jnp.zeros((HEAD_DIM, A_GROUP * SEG), BF16)
    ones = lambda n: jnp.ones((16, n), F32)
    kv = [(_bf(k), _bf(jnp.concatenate([v.T, ones(v.shape[0])], axis=0))) for k, v in kv_blocks]
    w_groups = []
    for g in range(A_KV_HEADS):
        heads = range(g * A_GROUP, (g + 1) * A_GROUP)
        slab = _bf(jnp.concatenate([qt[h * HEAD_DIM:(h + 1) * HEAD_DIM, :] for h in heads], axis=1))
        w_groups.append(jnp.concatenate([slab if i == g else zeros for i in range(A_KV_HEADS)], axis=0))
    w_all = jnp.concatenate(w_groups, axis=1)
    group_out = [None] * A_KV_HEADS
    for q0 in range(0, A_HEADS * SEG, ATT_LANES):
        w_c = w_all[:, q0:q0 + ATT_LANES]
        m = acc = None
        nxt = jnp.dot(kv[0][0], w_c, preferred_element_type=F32)
        for i, (_, v_t) in enumerate(kv):
            sc = nxt
            if i + 1 < len(kv):
                nxt = jnp.dot(kv[i + 1][0], w_c, preferred_element_type=F32)
            m_blk = jnp.max(sc, axis=0, keepdims=True)
            m_new = m_blk if m is None else jnp.maximum(m, m_blk)
            pv = jnp.dot(v_t, _bf(jnp.exp(sc - m_new)), preferred_element_type=F32)
            acc = pv if m is None else acc * jnp.exp(m - m_new) + pv
            m = m_new
        out = acc[0:KV_WIDTH] / acc[KV_WIDTH:KV_WIDTH + 1]
        for g in range(A_KV_HEADS):
            g0, g1 = g * A_GROUP * SEG, (g + 1) * A_GROUP * SEG
            lo, hi = max(g0, q0), min(g1, q0 + ATT_LANES)
            if lo < hi:
                piece = out[g * HEAD_DIM:(g + 1) * HEAD_DIM, lo - q0:hi - q0]
                group_out[g] = piece if group_out[g] is None else jnp.concatenate([group_out[g], piece], axis=1)
    for col in range(A_WIDTH // KV_WIDTH):
        g, h0 = (2 * col) // A_GROUP, (2 * col) % A_GROUP
        pair = jnp.concatenate([group_out[g][:, h0 * SEG:(h0 + 1) * SEG],
                                group_out[g][:, (h0 + 1) * SEG:(h0 + 2) * SEG]], axis=0)
        o_ref[:, col * KV_WIDTH:(col + 1) * KV_WIDTH] = pair.T


def _attn_kernel(q_ref, kc_ref, vc_ref, kl_ref, vl_ref, ck_ref, cv_ref, o_ref, *, segs):
    s = pl.program_id(0)

    @pl.when(segs.is_ctx(s))
    def _():
        _attend(q_ref, [(kc_ref[...], vc_ref[...])], o_ref)

    @pl.when(jnp.logical_not(segs.is_ctx(s)))
    def _():
        blocks = []
        past, t_lat = ck_ref.shape[2], kl_ref.shape[0]
        for lo in range(0, past, KV_BLOCK):
            hi = min(lo + KV_BLOCK, past)
            blocks.append((ck_ref[0, 0, lo:hi, :], cv_ref[0, 0, lo:hi, :]))
        for lo in range(0, t_lat, KV_BLOCK):
            hi = min(lo + KV_BLOCK, t_lat)
            blocks.append((kl_ref[lo:hi, :], vl_ref[lo:hi, :]))
        _attend(q_ref, blocks, o_ref)


def _attention(layer, segs, aq, ak, av, cache_k, cache_v):
    n_tok = aq.shape[0]
    t_lat = segs.per_seq * SEG
    past = cache_k.shape[2]
    assert t_lat % KV_BLOCK == 0
    lat_blocks_before = segs.n_ctx_seg * SEG // t_lat
    own = lambda w: pl.BlockSpec((SEG, w), lambda s: (s, 0))
    lat_kv = pl.BlockSpec((t_lat, KV_WIDTH), lambda s: (lat_blocks_before + segs.lat_seq(s), 0))
    cache = pl.BlockSpec((1, 1, past, KV_WIDTH), lambda s: (segs.lat_seq(s), layer, 0, 0))
    return pl.pallas_call(
        functools.partial(_attn_kernel, segs=segs),
        grid=(segs.n_seg,),
        in_specs=[own(A_WIDTH), own(KV_WIDTH), own(KV_WIDTH), lat_kv, lat_kv, cache, cache],
        out_specs=own(A_WIDTH),
        out_shape=jax.ShapeDtypeStruct((n_tok, A_WIDTH), F32),
        compiler_params=pltpu.CompilerParams(
            dimension_semantics=("arbitrary",), vmem_limit_bytes=VMEM_LIMIT),
        name="attention",
    )(aq, ak, av, ak, av, cache_k, cache_v)


def _permute_w_in(w_in):
    m, g, a, kv = SCAN_W, SCAN_W, A_WIDTH, KV_WIDTH
    sizes = (m, m, m, m, 8, 8, 3 * g, g, 8, 8, a, kv, kv)
    offs = [0]
    for sz in sizes:
        offs.append(offs[-1] + sz)
    piece = lambda i: w_in[..., offs[i]:offs[i + 1]]
    pad = jnp.zeros(w_in.shape[:-1] + (GATE_W - 32,), w_in.dtype)
    order = [0, 1, 2, 3, 6, 7, 10, 11, 12, 4, 5, 8, 9]
    return jnp.concatenate([piece(i) for i in order] + [pad], axis=-1)


def _rope_tables(t_lat):
    rows = t_lat // GRID_W
    row = np.repeat(np.arange(rows, dtype=np.float64), GRID_W)
    col = np.tile(np.arange(GRID_W, dtype=np.float64), rows)
    n_freq = HEAD_DIM // 4
    inv = ROPE_BASE ** (-np.arange(n_freq, dtype=np.float64) / n_freq)
    ang = np.stack([row[:, None] * inv, col[:, None] * inv], axis=1)
    cos, sin = np.cos(ang), np.sin(ang)
    cos_h = np.concatenate([cos, cos], axis=-1).reshape(t_lat, HEAD_DIM)
    sin_h = np.concatenate([-sin, sin], axis=-1).reshape(t_lat, HEAD_DIM)
    wide = lambda a: np.tile(a, (1, KV_WIDTH // HEAD_DIM))
    cos_t = np.concatenate([wide(cos_h), np.ones((TM, KV_WIDTH))], axis=0)
    sin_t = np.concatenate([wide(sin_h), np.zeros((TM, KV_WIDTH))], axis=0)
    return jnp.asarray(cos_t, F32), jnp.asarray(sin_t, F32)


def kernel(x_prompt, x_sample, c, cache_k, cache_v, state_mlstm_C, state_mlstm_n, state_mlstm_m,
           state_delta_S, c_ctx, ada_w, ada_b, norm_g, ffn_w_in, ffn_w_out, w_in, w_out, mlstm_f_bias,
           mlstm_norm, delta_conv, delta_a_log, delta_dt_bias, delta_norm, attn_q_norm, attn_k_norm,
           final_norm):
    batch, t_ctx, d_model = x_prompt.shape
    n_lat_seq, t_lat, _ = x_sample.shape
    depth = ada_w.shape[0]
    assert d_model == D_MODEL and norm_g.shape[1] == 3 and ffn_w_in.shape[-1] == 2 * D_FF
    assert 1 + n_lat_seq <= MOD_ROWS
    n_ctx = batch * t_ctx
    n_tok = n_ctx + n_lat_seq * t_lat
    assert n_ctx % t_lat == 0
    segs = _Segs(n_ctx, t_ctx, n_lat_seq, t_lat)
    pairs = _Pairs(n_ctx, t_ctx, n_lat_seq, t_lat)
    tiles = _Tiles(n_ctx, t_lat)

    cond = jnp.concatenate([c_ctx[None, :], c, jnp.zeros((MOD_ROWS - 1 - n_lat_seq, D_MODEL), F32)], axis=0)
    mod = _modulation(cond, ada_w, ada_b)

    wfi = _bf(ffn_w_in).reshape(depth * 2, D_MODEL, 2 * D_FF)
    wfo = _bf(ffn_w_out).reshape(depth * 2, D_FF, D_MODEL)
    wz = _bf(_permute_w_in(w_in))
    wo = _bf(w_out)
    qn = jnp.tile(attn_q_norm, (1, A_HEADS)).reshape(depth, 1, A_WIDTH)
    kn = jnp.tile(attn_k_norm, (1, A_KV_HEADS)).reshape(depth, 1, KV_WIDTH)
    gn = jnp.tile(delta_norm, (1, N_HEADS)).reshape(depth, 1, SCAN_W)
    mn = mlstm_norm.reshape(depth, 1, SCAN_W)
    fin = final_norm.reshape(1, D_MODEL)
    fb_cols = mlstm_f_bias.reshape(depth, N_GATE_ROWS, 1)
    alog_cols = delta_a_log.reshape(depth, N_GATE_ROWS, 1)
    dtb_cols = delta_dt_bias.reshape(depth, N_GATE_ROWS, 1)
    cos_t, sin_t = _rope_tables(t_lat)
    ck = cache_k.reshape(cache_k.shape[:3] + (KV_WIDTH,))
    cv = cache_v.reshape(cache_v.shape[:3] + (KV_WIDTH,))
    n0 = state_mlstm_n.reshape(n_lat_seq, depth, N_DIR, 1, SCAN_W)
    m0 = state_mlstm_m.reshape(n_lat_seq, depth, N_GATE_ROWS, 1)

    x_ctx, x_lat, lat_tile0 = x_prompt.reshape(n_ctx, D_MODEL), x_sample.reshape(n_tok - n_ctx, D_MODEL), 0
    ks, vs, cs, ns, ms, ss = [], [], [], [], [], []
    for l in range(depth):
        x1, zm, zg, aq, ak, av, gates = _dense1(l, tiles, n_tok, x_ctx, x_lat, lat_tile0, mod, norm_g,
                                                wfi, wfo, wz, qn, kn, cos_t, sin_t)
        zm_v, zg_v, gates_v = pairs.view(zm), pairs.view(zg), pairs.view(gates)
        hf, hb, c_new, n_new, m_new = _mlstm(l, pairs, zm_v, gates_v, fb_cols, state_mlstm_C, n0, m0)
        o_f, o_b, s_new = _delta(l, pairs, zg_v, gates_v, delta_conv, alog_cols, dtb_cols, state_delta_S)
        hf, hb, o_f, o_b = (a.reshape(n_tok, SCAN_W) for a in (hf, hb, o_f, o_b))
        a_out = _attention(l, segs, aq, ak, av, ck, cv)
        x = _dense2(l, tiles, x1, mod, norm_g, hf, hb, zm, mn, o_f, o_b, zg, gn, a_out, wo, wfi, wfo, fin,
                    final=(l == depth - 1))
        if l < depth - 1:
            x_ctx, x_lat, lat_tile0 = x, x, tiles.n_ctx

        ks.append(ak[:n_ctx].reshape(batch, t_ctx, A_KV_HEADS, HEAD_DIM))
        vs.append(av[:n_ctx].reshape(batch, t_ctx, A_KV_HEADS, HEAD_DIM))
        cs.append(pairs.ctx_states(c_new))
        ns.append(pairs.ctx_states(n_new).reshape(batch, N_DIR, N_HEADS, HEAD_DIM))
        ms.append(pairs.ctx_states(m_new).reshape(batch, N_DIR, N_HEADS))
        ss.append(pairs.ctx_states(s_new))

    y_prompt = x[0].reshape(batch, t_ctx, D_MODEL)
    y_sample = x[1].reshape(n_lat_seq, t_lat, D_MODEL)
    stack = lambda xs: jnp.stack(xs, axis=1)
    return (y_prompt, y_sample, stack(ks), stack(vs), stack(cs), stack(ns), stack(ms), stack(ss))
```

```python
import functools
import math

import numpy as np

import jax
import jax.numpy as jnp
from jax import lax
from jax.experimental import pallas as pl
from jax.experimental.pallas import tpu as pltpu

F32 = jnp.float32
BF16 = jnp.bfloat16

D_MODEL = 1024
HEAD_DIM = 64
N_HEADS = 4
SCAN_W = N_HEADS * HEAD_DIM
A_HEADS = 8
A_KV_HEADS = 2
A_GROUP = A_HEADS // A_KV_HEADS
A_WIDTH = A_HEADS * HEAD_DIM
KV_WIDTH = A_KV_HEADS * HEAD_DIM
N_DIR = 2
CHUNK = 64
CONV_W = 5
D_FF = 2816
FF_CHUNK = 256
GRID_W = 64
ROPE_BASE = 10000.0
EPS = 1e-6
N_MOD = 9
MOD_ROWS = 8
GATE_W = 128
Z_WIDTH = 2 * D_MODEL + A_WIDTH + 2 * KV_WIDTH + GATE_W

SEG = 256
N_CHUNK = SEG // CHUNK
TM = 512
ROW_GROUP = 512
N_PAIR = 2
HALO = 8
KV_BLOCK = 512
ATT_LANES = 2048
VMEM_LIMIT = 56 * 1024 * 1024


def _bf(x):
    return x.astype(BF16)


def _dot(a, b):
    return jnp.dot(_bf(a), _bf(b), preferred_element_type=F32)


def _bdot(a, b):
    return lax.dot_general(a, b, (((2,), (1,)), ((0,), (0,))), preferred_element_type=F32)


def _bdot_nt(a, b):
    return lax.dot_general(a, b, (((2,), (2,)), ((0,), (0,))), preferred_element_type=F32)


def _split(a, terms):
    parts = []
    rest = a
    for _ in range(terms):
        p = _bf(rest)
        parts.append(p)
        rest = rest - p.astype(F32)
    return parts


def _dot_exact_rhs(a, b01, terms=2):
    parts = _split(a, terms)
    if a.shape[-1] % 128 == 0:
        return jnp.dot(jnp.concatenate(parts, axis=-1), jnp.concatenate([b01] * terms, axis=0),
                       preferred_element_type=F32)
    out = None
    for p in parts:
        d = jnp.dot(p, b01, preferred_element_type=F32)
        out = d if out is None else out + d
    return out


def _iota(shape, dim):
    return lax.broadcasted_iota(jnp.int32, shape, dim)


def _head_of(idx):
    return jnp.right_shift(idx, 6)


def _in_head(idx):
    return jnp.bitwise_and(idx, HEAD_DIM - 1)


def _block_ones(n):
    return (_head_of(_iota((n, n), 0)) == _head_of(_iota((n, n), 1)))


def _head_sum(x):
    n = x.shape[-1]
    return _dot_exact_rhs(x, _block_ones(n).astype(BF16), terms=1)


def _head_rms(x, g_row):
    ms = _head_sum(x * x) * (1.0 / HEAD_DIM)
    return x * lax.rsqrt(ms + EPS) * g_row


def _rms_mod(x, g_row, scale_row, shift_row):
    y = x * lax.rsqrt(jnp.mean(x * x, axis=-1, keepdims=True) + EPS) * g_row
    return y * (1.0 + scale_row) + shift_row


def _softplus(x):
    return jnp.maximum(x, 0.0) + jnp.log1p(jnp.exp(-jnp.abs(x)))


def _log_sigmoid(x):
    return -_softplus(-x)


def _sigmoid(x):
    return jax.nn.sigmoid(x)


def _silu(x):
    return x * jax.nn.sigmoid(x)


def _bd3(x, mask_bf):
    return jnp.concatenate([x] * N_HEADS, axis=1) * mask_bf


def _to_block_diag(x4):
    spread = (_iota((HEAD_DIM, SCAN_W), 0) == _in_head(_iota((HEAD_DIM, SCAN_W), 1))).astype(BF16)
    rows = x4.reshape(SCAN_W, HEAD_DIM)
    return jnp.where(_block_ones(SCAN_W), _dot_exact_rhs(rows, spread, terms=3), 0.0)


def _from_block_diag(x):
    fold = (_in_head(_iota((SCAN_W, HEAD_DIM), 0)) == _iota((SCAN_W, HEAD_DIM), 1)).astype(BF16)
    return _dot_exact_rhs(x, fold, terms=3).reshape(N_HEADS, HEAD_DIM, HEAD_DIM)


def _chunks(a):
    return a.reshape(N_CHUNK, CHUNK, SCAN_W)


def _problem_masks():
    shape = (N_PAIR * N_DIR * N_CHUNK, CHUNK, SCAN_W)
    t_idx, j_idx = _iota(shape, 1), _in_head(_iota(shape, 2))
    is_rev = jnp.bitwise_and(_iota(shape, 0), N_DIR * N_CHUNK - 1) >= N_CHUNK
    ahead = jnp.where(is_rev, j_idx - t_idx, t_idx - j_idx)
    return t_idx == j_idx, ahead >= 0, ahead > 0


def _scan_order(d):
    return range(N_CHUNK - 1, -1, -1) if d else range(N_CHUNK)


def _mod_kernel(cond_ref, w_ref, b_ref, o_ref):
    a = _silu(cond_ref[...])
    o_ref[0, 0] = _dot(a, w_ref[0]) + b_ref[0]


def _modulation(cond, ada_w, ada_b):
    depth = ada_w.shape[0]
    return pl.pallas_call(
        _mod_kernel,
        grid=(depth, N_MOD),
        in_specs=[
            pl.BlockSpec((MOD_ROWS, D_MODEL), lambda l, j: (0, 0)),
            pl.BlockSpec((1, D_MODEL, D_MODEL), lambda l, j: (l, 0, j)),
            pl.BlockSpec((1, 1, D_MODEL), lambda l, j: (l, 0, j)),
        ],
        out_specs=pl.BlockSpec((1, 1, MOD_ROWS, D_MODEL), lambda l, j: (l, j, 0, 0)),
        out_shape=jax.ShapeDtypeStruct((depth, N_MOD, MOD_ROWS, D_MODEL), F32),
        compiler_params=pltpu.CompilerParams(
            dimension_semantics=("arbitrary", "arbitrary"), vmem_limit_bytes=VMEM_LIMIT),
        name="modulation",
    )(cond, ada_w, ada_b.reshape(depth, 1, N_MOD * D_MODEL))


def _ffn(h, w_in_ref, w_out_ref):
    n_chunks = D_FF // FF_CHUNK

    def gate_up(c):
        lo = c * FF_CHUNK
        g = jnp.dot(h, w_in_ref[0, :, lo:lo + FF_CHUNK], preferred_element_type=F32)
        u = jnp.dot(h, w_in_ref[0, :, D_FF + lo:D_FF + lo + FF_CHUNK], preferred_element_type=F32)
        return g, u

    acc = None
    nxt = gate_up(0)
    for c in range(n_chunks):
        g, u = nxt
        if c + 1 < n_chunks:
            nxt = gate_up(c + 1)
        a = _bf(_silu(g) * u)
        d = jnp.dot(a, w_out_ref[0, c * FF_CHUNK:(c + 1) * FF_CHUNK, :], preferred_element_type=F32)
        acc = d if acc is None else acc + d
    return acc


def _rope(x, cos, sin_signed):
    w = x.shape[-1]
    reps = w // cos.shape[-1]
    if reps > 1:
        cos = jnp.concatenate([cos] * reps, axis=-1)
        sin_signed = jnp.concatenate([sin_signed] * reps, axis=-1)
    first = jnp.bitwise_and(_iota(x.shape, 1), 31) < 16
    partner = jnp.where(first, pltpu.roll(x, w - 16, 1), pltpu.roll(x, 16, 1))
    return x * cos + partner * sin_signed


class _Tiles:
    def __init__(self, n_ctx, t_lat):
        assert n_ctx % TM == 0 and t_lat % TM == 0
        self.n_ctx = n_ctx // TM
        self.per_seq = t_lat // TM

    def is_ctx(self, i):
        return i < self.n_ctx

    def lat(self, i):
        return jnp.maximum(i - self.n_ctx, 0)

    def mod_row(self, i):
        return jnp.where(self.is_ctx(i), 0, 1 + self.lat(i) // self.per_seq)


def _mod_rows(mod_ref, row):
    return lambda m: mod_ref[0, m, pl.ds(row, 1), :]


def _dense1_kernel(xa_ref, xb_ref, mod_ref, ng_ref, wfi_ref, wfo_ref, wz_ref, qn_ref, kn_ref, cos_ref, sin_ref,
                   x1_ref, zm_ref, zg_ref, aq_ref, ak_ref, av_ref, gate_ref, *, tiles):
    i = pl.program_id(0)
    mod = _mod_rows(mod_ref, tiles.mod_row(i))
    is_ctx = tiles.is_ctx(i)
    for r in range(TM // ROW_GROUP):
        rows = slice(r * ROW_GROUP, (r + 1) * ROW_GROUP)
        x = jnp.where(is_ctx, xa_ref[rows, :], xb_ref[rows, :])
        h = _bf(_rms_mod(x, ng_ref[0, 0:1], mod(0), mod(1)))
        x1 = x + 0.5 * mod(2) * _ffn(h, wfi_ref, wfo_ref)
        x1_ref[rows, :] = x1
        h2 = _bf(_rms_mod(x1, ng_ref[0, 1:2], mod(3), mod(4)))

        z = jnp.dot(h2, wz_ref[0], preferred_element_type=F32)

        def proj(lo, width):
            return z[:, lo:lo + width]

        zm_ref[rows, :] = proj(0, D_MODEL)
        zg_ref[rows, :] = proj(D_MODEL, D_MODEL)
        cos = cos_ref[rows, :]
        sin = sin_ref[rows, :]
        off = 2 * D_MODEL
        q = _head_rms(proj(off, A_WIDTH), qn_ref[0])
        aq_ref[rows, :] = _rope(q, cos, sin) * (1.0 / math.sqrt(HEAD_DIM))
        k = _head_rms(proj(off + A_WIDTH, KV_WIDTH), kn_ref[0])
        ak_ref[rows, :] = _rope(k, cos, sin)
        av_ref[rows, :] = proj(off + A_WIDTH + KV_WIDTH, KV_WIDTH)
        gate_ref[rows, :] = proj(off + A_WIDTH + 2 * KV_WIDTH, GATE_W)


def _resident(shape, index_map):
    return pl.BlockSpec(shape, index_map, pipeline_mode=pl.Buffered(1))


def _dense1(layer, tiles, n_tok, x_ctx, x_lat, lat_tile0, mod, norm_g, wfi, wfo, wz, qn, kn, cos_t, sin_t):
    row = lambda w: pl.BlockSpec((TM, w), lambda i: (i, 0))
    rope_tile = lambda i: jnp.where(tiles.is_ctx(i), tiles.per_seq, tiles.lat(i) % tiles.per_seq)
    rope = pl.BlockSpec((TM, KV_WIDTH), lambda i: (rope_tile(i), 0))
    out_w = (D_MODEL, D_MODEL, D_MODEL, A_WIDTH, KV_WIDTH, KV_WIDTH, GATE_W)
    return pl.pallas_call(
        functools.partial(_dense1_kernel, tiles=tiles),
        grid=(n_tok // TM,),
        in_specs=[
            pl.BlockSpec((TM, D_MODEL), lambda i: (jnp.minimum(i, tiles.n_ctx - 1), 0)),
            pl.BlockSpec((TM, D_MODEL), lambda i: (tiles.lat(i) + lat_tile0, 0)),
            pl.BlockSpec((1, N_MOD, MOD_ROWS, D_MODEL), lambda i: (layer, 0, 0, 0)),
            pl.BlockSpec((1, 3, D_MODEL), lambda i: (layer, 0, 0)),
            _resident((1, D_MODEL, 2 * D_FF), lambda i: (2 * layer, 0, 0)),
            _resident((1, D_FF, D_MODEL), lambda i: (2 * layer, 0, 0)),
            _resident((1, D_MODEL, Z_WIDTH), lambda i: (layer, 0, 0)),
            pl.BlockSpec((1, 1, A_WIDTH), lambda i: (layer, 0, 0)),
            pl.BlockSpec((1, 1, KV_WIDTH), lambda i: (layer, 0, 0)),
            rope, rope,
        ],
        out_specs=[row(w) for w in out_w],
        out_shape=[jax.ShapeDtypeStruct((n_tok, w), F32) for w in out_w],
        compiler_params=pltpu.CompilerParams(
            dimension_semantics=("arbitrary",), vmem_limit_bytes=VMEM_LIMIT),
        name="dense1",
    )(x_ctx, x_lat, mod, norm_g, wfi, wfo, wz, qn, kn, cos_t, sin_t)


def _dense2_kernel(x_ref, mod_ref, ng_ref, hf_ref, hb_ref, mo_ref, mn_ref, of_ref, ob_ref, gz_ref, gn_ref,
                   ao_ref, wo_ref, wfi_ref, wfo_ref, fin_ref, *out_refs, tiles, final):
    i = pl.program_id(0)
    mod = _mod_rows(mod_ref, tiles.mod_row(i))
    m_out = _head_rms(hf_ref[...] + hb_ref[...], mn_ref[0]) * _sigmoid(mo_ref[...])
    g_out = _head_rms(of_ref[...] + ob_ref[...], gn_ref[0]) * _silu(gz_ref[...])
    mix = jnp.dot(_bf(m_out), wo_ref[0, 0:SCAN_W, :], preferred_element_type=F32)
    mix += jnp.dot(_bf(g_out), wo_ref[0, SCAN_W:2 * SCAN_W, :], preferred_element_type=F32)
    mix += jnp.dot(_bf(ao_ref[...]), wo_ref[0, 2 * SCAN_W:, :], preferred_element_type=F32)
    x2 = x_ref[...] + mod(5) * mix
    h = _bf(_rms_mod(x2, ng_ref[0, 2:3], mod(6), mod(7)))
    x3 = x2 + 0.5 * mod(8) * _ffn(h, wfi_ref, wfo_ref)
    if not final:
        out_refs[0][...] = x3
        return
    y = x3 * lax.rsqrt(jnp.mean(x3 * x3, axis=-1, keepdims=True) + EPS) * fin_ref[...]
    y_ctx_ref, y_lat_ref = out_refs

    @pl.when(tiles.is_ctx(i))
    def _():
        y_ctx_ref[...] = y

    @pl.when(jnp.logical_not(tiles.is_ctx(i)))
    def _():
        y_lat_ref[...] = y


def _dense2(layer, tiles, x1, mod, norm_g, hf, hb, zm, m_norm, of, ob, zg, g_norm, a_out, wo, wfi, wfo, fin,
            final):
    n_tok = x1.shape[0]
    row = lambda w: pl.BlockSpec((TM, w), lambda i: (i, 0))
    last_quarter = pl.BlockSpec((TM, SCAN_W), lambda i: (i, 3))
    lane_row = lambda w: pl.BlockSpec((1, 1, w), lambda i: (layer, 0, 0))
    if final:
        n_ctx = tiles.n_ctx * TM
        out_specs = [pl.BlockSpec((TM, D_MODEL), lambda i: (jnp.minimum(i, tiles.n_ctx - 1), 0)),
                     pl.BlockSpec((TM, D_MODEL), lambda i: (tiles.lat(i), 0))]
        out_shape = [jax.ShapeDtypeStruct((n_ctx, D_MODEL), F32),
                     jax.ShapeDtypeStruct((n_tok - n_ctx, D_MODEL), F32)]
    else:
        out_specs, out_shape = row(D_MODEL), jax.ShapeDtypeStruct((n_tok, D_MODEL), F32)
    return pl.pallas_call(
        functools.partial(_dense2_kernel, tiles=tiles, final=final),
        grid=(n_tok // TM,),
        in_specs=[
            row(D_MODEL),
            pl.BlockSpec((1, N_MOD, MOD_ROWS, D_MODEL), lambda i: (layer, 0, 0, 0)),
            pl.BlockSpec((1, 3, D_MODEL), lambda i: (layer, 0, 0)),
            row(SCAN_W), row(SCAN_W), last_quarter, lane_row(SCAN_W),
            row(SCAN_W), row(SCAN_W), last_quarter, lane_row(SCAN_W),
            row(A_WIDTH),
            _resident((1, D_MODEL, D_MODEL), lambda i: (layer, 0, 0)),
            _resident((1, D_MODEL, 2 * D_FF), lambda i: (2 * layer + 1, 0, 0)),
            _resident((1, D_FF, D_MODEL), lambda i: (2 * layer + 1, 0, 0)),
            pl.BlockSpec((1, D_MODEL), lambda i: (0, 0)),
        ],
        out_specs=out_specs,
        out_shape=out_shape,
        compiler_params=pltpu.CompilerParams(
            dimension_semantics=("arbitrary",), vmem_limit_bytes=VMEM_LIMIT),
        name="dense2",
    )(x1, mod, norm_g, hf, hb, zm, m_norm, of, ob, zg, g_norm, a_out, wo, wfi, wfo, fin)


class _Segs:
    def __init__(self, n_ctx, t_ctx, n_lat_seq, t_lat):
        assert t_ctx == SEG and t_lat % SEG == 0
        self.n_ctx_seg = n_ctx // SEG
        self.per_seq = t_lat // SEG
        self.n_lat_seq = n_lat_seq
        self.n_seg = self.n_ctx_seg + n_lat_seq * self.per_seq
        self.n_seq = self.n_ctx_seg + n_lat_seq

    def is_ctx(self, s):
        return s < self.n_ctx_seg

    def lat(self, s):
        r = jnp.maximum(s - self.n_ctx_seg, 0)
        return r // self.per_seq, r % self.per_seq

    def blk(self, s, rev):
        b, j = self.lat(s)
        j = self.per_seq - 1 - j if rev else j
        return jnp.where(self.is_ctx(s), s, self.n_ctx_seg + b * self.per_seq + j)

    def seq(self, s):
        return jnp.where(self.is_ctx(s), s, self.n_ctx_seg + self.lat(s)[0])

    def lat_seq(self, s):
        return self.lat(s)[0]

    def first(self, s):
        return jnp.logical_or(self.is_ctx(s), self.lat(s)[1] == 0)

    def last(self, s):
        return jnp.logical_or(self.is_ctx(s), self.lat(s)[1] == self.per_seq - 1)


_STATE_MAT = (N_DIR, N_HEADS, HEAD_DIM, HEAD_DIM)


class _Pairs:
    def __init__(self, n_ctx, t_ctx, n_lat_seq, t_lat):
        assert t_ctx == SEG and t_lat % SEG == 0 and n_ctx == N_PAIR * t_lat and n_lat_seq % N_PAIR == 0
        self.rows = t_lat
        self.per_seq = t_lat // SEG
        self.n_ctx_step = t_lat // t_ctx
        self.n_step = self.n_ctx_step + (n_lat_seq // N_PAIR) * self.per_seq
        self.n_slot = self.n_ctx_step + n_lat_seq // N_PAIR

    def view(self, x):
        return x.reshape(x.shape[0] // self.rows, self.rows, x.shape[1])

    def is_ctx(self, s):
        return s < self.n_ctx_step

    def lat(self, s):
        r = jnp.maximum(s - self.n_ctx_step, 0)
        return r // self.per_seq, r % self.per_seq

    def macro(self, s):
        return jnp.where(self.is_ctx(s), 0, 1 + self.lat(s)[0])

    def seg(self, s, rev):
        j = self.lat(s)[1]
        return jnp.where(self.is_ctx(s), s, self.per_seq - 1 - j if rev else j)

    def slot(self, s):
        return jnp.where(self.is_ctx(s), s, self.n_ctx_step + self.lat(s)[0])

    def first(self, s):
        return jnp.logical_or(self.is_ctx(s), self.lat(s)[1] == 0)

    def last(self, s):
        return jnp.logical_or(self.is_ctx(s), self.lat(s)[1] == self.per_seq - 1)

    def block(self, rev, w):
        return pl.BlockSpec((N_PAIR, SEG, w), lambda s: (self.macro(s), self.seg(s, rev), 0))

    def state_in(self, layer, tail):
        zeros = (0,) * len(tail)
        return pl.BlockSpec((N_PAIR, 1) + tail, lambda s: (self.lat(s)[0], layer) + zeros)

    def state_out(self, tail):
        zeros = (0,) * len(tail)
        spec = pl.BlockSpec((1, N_PAIR) + tail, lambda s: (self.slot(s), 0) + zeros)
        return spec, jax.ShapeDtypeStruct((self.n_slot, N_PAIR) + tail, F32)

    def ctx_states(self, x):
        ctx = jnp.swapaxes(x[:self.n_ctx_step], 0, 1)
        return ctx.reshape((N_PAIR * self.n_ctx_step,) + x.shape[2:])


N_GATE_ROWS = N_DIR * N_HEADS


def _lane_scan(x, op, ident, rev):
    n = x.shape[-1]
    pos = _in_head(_iota(x.shape, x.ndim - 1))
    s = 1
    while s < CHUNK:
        if rev:
            y, ok = pltpu.roll(x, n - s, x.ndim - 1), pos < CHUNK - s
        else:
            y, ok = pltpu.roll(x, s, x.ndim - 1), pos >= s
        x = op(x, jnp.where(ok, y, ident))
        s *= 2
    return x


def _pieces(x, terms):
    out = []
    for _ in range(terms):
        p = _bf(x).astype(F32)
        out.append(p)
        x = x - p
    return jnp.concatenate(out, axis=0)


def _expander(n_qty):
    k = np.arange(SCAN_W)[:, None]
    n = np.arange(N_DIR * n_qty * SCAN_W)[None, :]
    rows_per_piece = n_qty * N_GATE_ROWS
    assert 3 * rows_per_piece <= SCAN_W
    qty, row = (k % rows_per_piece) // N_GATE_ROWS, k % N_GATE_ROWS
    d, qty_n, head = n // (n_qty * SCAN_W), (n // SCAN_W) % n_qty, (n % SCAN_W) // HEAD_DIM
    used = k < 3 * rows_per_piece
    return jnp.asarray(used & (qty == qty_n) & (row == d * N_HEADS + head), BF16)


def _expand(quantities, expand_ref):
    n_qty = len(quantities)
    packed = jnp.concatenate(quantities, axis=0)
    pad = jnp.zeros((SCAN_W - 3 * packed.shape[0], SEG), F32)
    lhs = _bf(jnp.concatenate([_pieces(packed, 3), pad], axis=0).T)
    wide = jnp.dot(lhs, expand_ref[...], preferred_element_type=F32)

    def stack(j):
        cols = [wide[:, (d * n_qty + j) * SCAN_W:(d * n_qty + j + 1) * SCAN_W] for d in range(N_DIR)]
        return jnp.concatenate([_chunks(x) for x in cols], axis=0)

    return [stack(j) for j in range(n_qty)]


def _gate_rows(gf_ref, gb_ref):
    is_b = _iota((N_GATE_ROWS, SEG), 0) >= N_HEADS
    gt_f, gt_b = gf_ref[...].T, gb_ref[...].T
    return is_b, lambda lo: jnp.where(is_b, gt_b[lo:lo + N_GATE_ROWS], gt_f[lo:lo + N_GATE_ROWS])


def _chunk_sums(x, is_b):
    s_idx, t_idx = _iota((SEG, 2 * SEG), 0), _iota((SEG, 2 * SEG), 1)
    t_loc = jnp.bitwise_and(t_idx, SEG - 1)
    tri = jnp.logical_and(_head_of(s_idx) == _head_of(t_loc),
                          jnp.where(t_idx >= SEG, s_idx - t_loc, t_loc - s_idx) >= 0).astype(BF16)
    sums = jnp.dot(_bf(_pieces(x, 3)), tri, preferred_element_type=F32)
    sums = sums[0:N_GATE_ROWS] + sums[N_GATE_ROWS:2 * N_GATE_ROWS] + sums[2 * N_GATE_ROWS:]
    return jnp.where(is_b, sums[:, SEG:], sums[:, 0:SEG])


def _on_lanes(cols, mirrored, is_b):
    chunk_of = _head_of(_iota((N_GATE_ROWS, SEG), 1))
    out = jnp.zeros((N_GATE_ROWS, SEG), F32)
    for c in range(N_CHUNK):
        col = jnp.where(is_b, cols[N_CHUNK - 1 - c], cols[c]) if mirrored else cols[c]
        out = jnp.where(chunk_of == c, col, out)
    return out


def _mlstm_compact(gf_ref, gb_ref, fb_col, m_col, expand2_ref, expand4_ref):
    is_b, pick = _gate_rows(gf_ref, gb_ref)
    ig = pick(0)
    lf = _log_sigmoid(pick(N_GATE_ROWS) + fb_col)
    b = _chunk_sums(lf, is_b)
    r = ig - b
    cm = jnp.where(is_b, _lane_scan(r, jnp.maximum, -jnp.inf, True), _lane_scan(r, jnp.maximum, -jnp.inf, False))
    on_lanes = functools.partial(_on_lanes, is_b=is_b)
    in_chunk = lambda x, c: x[:, c * CHUNK:(c + 1) * CHUNK]
    bl_cols = [jnp.sum(in_chunk(lf, c), axis=1, keepdims=True) for c in range(N_CHUNK)]
    bl = on_lanes(bl_cols, False)
    lw = bl - b + ig
    lw_cols = [jnp.max(in_chunk(lw, c), axis=1, keepdims=True) for c in range(N_CHUNK)]

    is_b_col = is_b[:, 0:1]
    at_step = lambda cols, i: jnp.where(is_b_col, cols[N_CHUNK - 1 - i], cols[i])
    m_in, m_out = [], []
    for i in range(N_CHUNK):
        m_in.append(m_col)
        m_col = jnp.maximum(at_step(bl_cols, i) + m_col, at_step(lw_cols, i))
        m_out.append(m_col)
    m_c, m_n = on_lanes(m_in, True), on_lanes(m_out, True)
    kw_fac, dec = _expand([jnp.exp(lw - m_n), jnp.exp(bl + m_c - m_n)], expand2_ref)
    mx = jnp.maximum(m_c, cm)
    r, mx, a_int, neg_mt = _expand([r, mx, jnp.exp(m_c - mx), -(b + mx)], expand4_ref)
    return (r, mx, a_int, neg_mt, kw_fac, dec), m_col


def _mlstm_kernel(zmf_ref, zmb_ref, gf_ref, gb_ref, fb_ref, expand2_ref, expand4_ref, c0_ref, n0_ref, m0_ref,
                  hf_ref, hb_ref, c_out, n_out, m_out, c_s, n_s, m_s, *, pairs):
    s = pl.program_id(0)
    members = range(N_PAIR)

    @pl.when(pairs.first(s))
    def _():
        ctx = pairs.is_ctx(s)
        for m in members:
            m_s[m] = jnp.where(ctx, 0.0, m0_ref[m, 0])
            for d in range(N_DIR):
                c_s[m, d] = jnp.where(ctx, 0.0, _to_block_diag(c0_ref[m, 0, d]))
                n_s[m, d] = jnp.where(ctx, 0.0, n0_ref[m, 0, d])

    gate_stacks, m_fin = [], []
    for m in members:
        stacks, m_col = _mlstm_compact(gf_ref.at[m], gb_ref.at[m], fb_ref[0], m_s[m], expand2_ref, expand4_ref)
        gate_stacks.append(stacks)
        m_fin.append(m_col)
    r, mx, a_int, neg_mt, kw_fac, dec = (jnp.concatenate([g[j] for g in gate_stacks], axis=0) for j in range(6))
    col = lambda j: jnp.concatenate([_chunks(ref[m, :, j * SCAN_W:(j + 1) * SCAN_W])
                                     for m in members for ref in (zmf_ref, zmb_ref)], axis=0)
    q, k, v = col(0), col(1) * (1.0 / math.sqrt(HEAD_DIM)), col(2)

    diag, tri, _ = _problem_masks()
    mask_bd = _block_ones(SCAN_W)
    mask_bf = mask_bd.astype(BF16)

    r_row = jnp.sum(jnp.where(diag, r, 0.0), axis=1, keepdims=True)
    w = jnp.exp(jnp.where(tri, r_row - mx, -jnp.inf))
    sc = _bdot_nt(_bf(q), _bd3(_bf(k), mask_bf)) * w
    kw = kw_fac * k
    kw_sum = jnp.sum(kw, axis=1, keepdims=True)

    n_prob = N_PAIR * N_DIR * N_CHUNK
    c_at, n_at = [None] * n_prob, [None] * n_prob
    c_fin, n_fin = {}, {}
    for m in members:
        for d in range(N_DIR):
            c_bd, n_row = c_s[m, d], n_s[m, d]
            for c in _scan_order(d):
                p = (m * N_DIR + d) * N_CHUNK + c
                c_at[p], n_at[p] = c_bd, n_row
                dec_row = dec[p, 0:1, :]
                c_bd = c_bd * dec_row + jnp.where(mask_bd, _dot(kw[p].T, v[p]), 0.0)
                n_row = n_row * dec_row + kw_sum[p]
            c_fin[m, d], n_fin[m, d] = c_bd, n_row
    c_all = jnp.concatenate([_bf(x)[None] for x in c_at], axis=0)
    n_all = jnp.concatenate([x[None] for x in n_at], axis=0)

    num = _bdot(_bf(sc), _bd3(_bf(v), mask_bf)) + a_int * _bdot(_bf(q), c_all)
    n_rows = n_prob * CHUNK
    flat = lambda a: a.reshape(n_rows, SCAN_W)
    sums = _dot_exact_rhs(jnp.concatenate([flat(sc), flat(q * n_all)], axis=0), mask_bf)
    den = sums[0:n_rows].reshape(sc.shape) + a_int * sums[n_rows:].reshape(sc.shape)
    h = num / jnp.maximum(jnp.abs(den), jnp.exp(neg_mt))
    for m in members:
        for d, out_ref in enumerate((hf_ref, hb_ref)):
            p0 = (m * N_DIR + d) * N_CHUNK
            out_ref[m] = h[p0:p0 + N_CHUNK].reshape(SEG, SCAN_W)

    for m in members:
        m_s[m] = m_fin[m]
        for d in range(N_DIR):
            c_s[m, d] = c_fin[m, d]
            n_s[m, d] = n_fin[m, d]

    @pl.when(pairs.last(s))
    def _():
        for m in members:
            m_out[0, m] = m_fin[m]
            for d in range(N_DIR):
                c_out[0, m, d] = _from_block_diag(c_fin[m, d])
                n_out[0, m, d] = n_fin[m, d]


def _mlstm(layer, pairs, zm, gates, fb_cols, state_c, n0, m0):
    row_tail, col_tail = (N_DIR, 1, SCAN_W), (N_GATE_ROWS, 1)
    c_spec, c_shape = pairs.state_out(_STATE_MAT)
    n_spec, n_shape = pairs.state_out(row_tail)
    m_spec, m_shape = pairs.state_out(col_tail)
    expand2, expand4 = _expander(2), _expander(4)
    h_shape = jax.ShapeDtypeStruct(zm.shape[:2] + (SCAN_W,), F32)
    return pl.pallas_call(
        functools.partial(_mlstm_kernel, pairs=pairs),
        grid=(pairs.n_step,),
        in_specs=[
            pairs.block(False, D_MODEL), pairs.block(True, D_MODEL),
            pairs.block(False, GATE_W), pairs.block(True, GATE_W),
            pl.BlockSpec((1, N_GATE_ROWS, 1), lambda s: (layer, 0, 0)),
            pl.BlockSpec(expand2.shape, lambda s: (0, 0)),
            pl.BlockSpec(expand4.shape, lambda s: (0, 0)),
            pairs.state_in(layer, _STATE_MAT), pairs.state_in(layer, row_tail), pairs.state_in(layer, col_tail),
        ],
        out_specs=[pairs.block(False, SCAN_W), pairs.block(True, SCAN_W), c_spec, n_spec, m_spec],
        out_shape=[h_shape, h_shape, c_shape, n_shape, m_shape],
        scratch_shapes=[pltpu.VMEM((N_PAIR, N_DIR, SCAN_W, SCAN_W), F32),
                        pltpu.VMEM((N_PAIR, N_DIR, 1, SCAN_W), F32),
                        pltpu.VMEM((N_PAIR, N_GATE_ROWS, 1), F32)],
        compiler_params=pltpu.CompilerParams(
            dimension_semantics=("arbitrary",), vmem_limit_bytes=VMEM_LIMIT),
        name="mlstm",
    )(zm, zm, gates, gates, fb_cols, expand2, expand4, state_c, n0, m0)


def _short_conv(prev_ref, cur_ref, next_ref, w_ref, has_prev, has_next):
    w3 = 3 * SCAN_W
    prev = jnp.where(has_prev, prev_ref[:, 0:w3], 0.0)
    nxt = jnp.where(has_next, next_ref[:, 0:w3], 0.0)
    xp = jnp.concatenate([prev, cur_ref[:, 0:w3], nxt], axis=0)
    rows = xp.shape[0]
    acc = None
    for i in range(CONV_W):
        shift = (CONV_W // 2 - i) % rows
        y = xp if shift == 0 else pltpu.roll(xp, shift, 0)
        t = y[HALO:HALO + SEG] * w_ref[0, i:i + 1, :]
        acc = t if acc is None else acc + t
    return acc


def _neumann_inverse(n_all, mask_bf):
    eye = (_iota(n_all.shape[1:], 0) == _in_head(_iota(n_all.shape[1:], 1))).astype(F32)
    p = -n_all
    t = eye + p
    levels = 6
    for lvl in range(levels):
        first, last = lvl == 0, lvl == levels - 1
        p_hi, p_lo = _split(p, 2)
        w_hi, w_lo = _bd3(p_hi, mask_bf), _bd3(p_lo, mask_bf)
        lhs_hi, lhs_lo = [], []
        if not first:
            t_hi, t_lo = _split(t, 2)
            lhs_hi += [t_hi, t_lo]
            lhs_lo += [t_hi]
        if not last:
            lhs_hi += [p_hi, p_lo]
            lhs_lo += [p_hi]
        a = _bdot(jnp.concatenate(lhs_hi, axis=1), w_hi)
        b = _bdot(jnp.concatenate(lhs_lo, axis=1) if len(lhs_lo) > 1 else lhs_lo[0], w_lo)
        ra, rb = 0, 0
        if not first:
            t = t + (a[:, 0:CHUNK] + a[:, CHUNK:2 * CHUNK] + b[:, 0:CHUNK])
            ra, rb = 2 * CHUNK, CHUNK
        if not last:
            p = a[:, ra:ra + CHUNK] + a[:, ra + CHUNK:ra + 2 * CHUNK] + b[:, rb:rb + CHUNK]
    return t


def _delta_qkv(prev_ref, cur_ref, next_ref, cw_ref, has_prev, has_next):
    qkv = _silu(_short_conv(prev_ref, cur_ref, next_ref, cw_ref, has_prev, has_next))

    def l2(a):
        return a * lax.rsqrt(_head_sum(a * a) + EPS)

    q = l2(qkv[:, 0:SCAN_W]) * (1.0 / math.sqrt(HEAD_DIM))
    k = l2(qkv[:, SCAN_W:2 * SCAN_W])
    return jnp.concatenate([q, k, qkv[:, 2 * SCAN_W:3 * SCAN_W]], axis=1)


def _delta_stage_qkv(s, pairs, blocks, cw_ref, qkv_s, seen_s):
    members = range(N_PAIR)
    lat = jnp.logical_not(pairs.is_ctx(s))
    j = pairs.lat(s)[1]
    mirror = pairs.per_seq - 1 - j
    half = pairs.per_seq // 2

    @pl.when(pairs.is_ctx(s))
    def _():
        for m in members:
            pf, cf, nf = (r.at[m] for r in blocks[0])
            x = _delta_qkv(pf, cf, nf, cw_ref, False, False)
            qkv_s[m, 0] = x
            qkv_s[m, 1] = x

    @pl.when(jnp.logical_and(lat, j < half))
    def _():
        for m in members:
            (pf, cf, nf), (pb, cb, nb) = ((r.at[m] for r in refs) for refs in blocks)
            xf = _delta_qkv(pf, cf, nf, cw_ref, j > 0, True)
            xb = _delta_qkv(pb, cb, nb, cw_ref, True, j > 0)
            qkv_s[m, 0] = xf
            qkv_s[m, 1] = xb
            seen_s[m, pl.ds(j, 1)] = xf[None]
            seen_s[m, pl.ds(mirror, 1)] = xb[None]

    @pl.when(jnp.logical_and(lat, j >= half))
    def _():
        for m in members:
            qkv_s[m, 0] = seen_s[m, pl.ds(j, 1)][0]
            qkv_s[m, 1] = seen_s[m, pl.ds(mirror, 1)][0]


def _delta_gates(gf_ref, gb_ref, alog_col, dtb_col, expand_ref):
    is_b, pick = _gate_rows(gf_ref, gb_ref)
    g = -jnp.exp(alog_col) * _softplus(pick(2 * N_GATE_ROWS) + dtb_col)
    beta = _sigmoid(pick(3 * N_GATE_ROWS))
    gc = _chunk_sums(g, is_b)
    totals = [jnp.sum(g[:, c * CHUNK:(c + 1) * CHUNK], axis=1, keepdims=True) for c in range(N_CHUNK)]
    gl = _on_lanes(totals, False, is_b)
    return _expand([gc, jnp.exp(gc), jnp.exp(gl - gc), jnp.exp(gl), beta], expand_ref)


def _delta_kernel(pf_ref, cf_ref, nf_ref, pb_ref, cb_ref, nb_ref, gf_ref, gb_ref, cw_ref, al_ref, dt_ref,
                  expand_ref, s0_ref, of_ref, ob_ref, s_out, s_s, qkv_s, seen_s, *, pairs):
    s = pl.program_id(0)
    members = range(N_PAIR)
    lat = jnp.logical_not(pairs.is_ctx(s))

    @pl.when(pairs.first(s))
    def _():
        for m in members:
            for d in range(N_DIR):
                s_s[m, d] = jnp.where(lat, _to_block_diag(s0_ref[m, 0, d]), 0.0)

    _delta_stage_qkv(s, pairs, ((pf_ref, cf_ref, nf_ref), (pb_ref, cb_ref, nb_ref)), cw_ref, qkv_s, seen_s)
    gate_stacks = [_delta_gates(gf_ref.at[m], gb_ref.at[m], al_ref[0], dt_ref[0], expand_ref) for m in members]
    gc, eg, k_fac, g_last, beta = (jnp.concatenate([g[i] for g in gate_stacks], axis=0) for i in range(5))
    q, k, v = (jnp.concatenate([_chunks(qkv_s[m, d, :, i * SCAN_W:(i + 1) * SCAN_W])
                                for m in members for d in range(N_DIR)], axis=0) for i in range(3))
    kb = k * beta
    k_dec = k * k_fac

    mask_bd = _block_ones(SCAN_W)
    mask_bf = mask_bd.astype(BF16)
    diag, tri, strict = _problem_masks()
    neg_diag = jnp.sum(jnp.where(diag, -gc, 0.0), axis=1, keepdims=True)
    decay = jnp.exp(jnp.where(tri, gc + neg_diag, -jnp.inf))
    k_bd = _bd3(_bf(k), mask_bf)
    kq = _bdot_nt(_bf(jnp.concatenate([kb, q], axis=1)), k_bd)
    qk = _bf(kq[:, CHUNK:] * decay)
    t_all = _bf(_neumann_inverse(jnp.where(strict, kq[:, 0:CHUNK] * decay, 0.0), mask_bf))
    u = _bdot(t_all, _bd3(_bf(v * beta), mask_bf))
    w = _bdot(t_all, _bd3(_bf(kb * eg), mask_bf))
    wq = _bf(jnp.concatenate([w, q * eg], axis=1))

    outs = (of_ref, ob_ref)
    chains = [(m, d) for m in members for d in range(N_DIR)]
    k_dec_t = [k_dec[p].T for p in range(k_dec.shape[0])]
    state = jnp.concatenate([s_s[m, d][None] for m, d in chains], axis=0)
    for i in range(N_CHUNK):
        probs = [(m * N_DIR + d) * N_CHUNK + (N_CHUNK - 1 - i if d else i) for m, d in chains]
        pick = lambda x: jnp.concatenate([x[p][None] for p in probs], axis=0)
        ws = _bdot(pick(wq), _bf(state))
        v_new = pick(u) - ws[:, 0:CHUNK]
        out = ws[:, CHUNK:] + _bdot(pick(qk), _bd3(_bf(v_new), mask_bf))
        for n, (m, d) in enumerate(chains):
            c = N_CHUNK - 1 - i if d else i
            outs[d][m, c * CHUNK:(c + 1) * CHUNK, :] = out[n]
        kd_t = _bf(jnp.concatenate([k_dec_t[p][None] for p in probs], axis=0))
        state = state * pick(g_last)[:, 0:1, :] + jnp.where(mask_bd, _bdot(kd_t, _bf(v_new)), 0.0)
    for n, (m, d) in enumerate(chains):
        s_s[m, d] = state[n]

    @pl.when(pairs.last(s))
    def _():
        for n, (m, d) in enumerate(chains):
            s_out[0, m, d] = _from_block_diag(state[n])


def _delta(layer, pairs, zg, gates, conv_w, alog_cols, dtb_cols, state_s):
    per = SEG // HALO
    last_halo = pairs.rows // HALO - 1

    def prev(rev):
        return pl.BlockSpec((N_PAIR, HALO, D_MODEL),
                            lambda s: (pairs.macro(s), jnp.maximum(pairs.seg(s, rev) * per - 1, 0), 0))

    def nxt(rev):
        return pl.BlockSpec((N_PAIR, HALO, D_MODEL),
                            lambda s: (pairs.macro(s), jnp.minimum((pairs.seg(s, rev) + 1) * per, last_halo), 0))

    gate_cols = pl.BlockSpec((1, N_GATE_ROWS, 1), lambda s: (layer, 0, 0))
    s_spec, s_shape = pairs.state_out(_STATE_MAT)
    expand5 = _expander(5)
    o_shape = jax.ShapeDtypeStruct(zg.shape[:2] + (SCAN_W,), F32)
    return pl.pallas_call(
        functools.partial(_delta_kernel, pairs=pairs),
        grid=(pairs.n_step,),
        in_specs=[
            prev(False), pairs.block(False, D_MODEL), nxt(False),
            prev(True), pairs.block(True, D_MODEL), nxt(True),
            pairs.block(False, GATE_W), pairs.block(True, GATE_W),
            pl.BlockSpec((1, CONV_W, 3 * SCAN_W), lambda s: (layer, 0, 0)),
            gate_cols, gate_cols, pl.BlockSpec(expand5.shape, lambda s: (0, 0)),
            pairs.state_in(layer, _STATE_MAT),
        ],
        out_specs=[pairs.block(False, SCAN_W), pairs.block(True, SCAN_W), s_spec],
        out_shape=[o_shape, o_shape, s_shape],
        scratch_shapes=[pltpu.VMEM((N_PAIR, N_DIR, SCAN_W, SCAN_W), F32),
                        pltpu.VMEM((N_PAIR, N_DIR, SEG, 3 * SCAN_W), F32),
                        pltpu.VMEM((N_PAIR, pairs.per_seq, SEG, 3 * SCAN_W), F32)],
        compiler_params=pltpu.CompilerParams(
            dimension_semantics=("arbitrary",), vmem_limit_bytes=VMEM_LIMIT),
        name="delta",
    )(zg, zg, zg, zg, zg, zg, gates, gates, conv_w, alog_cols, dtb_cols, expand5, state_s)


def _attend(q_ref, kv_blocks, o_ref):
    qt = q_ref[...].T
    zeros = jnp.zeros((HEAD_DIM, A_GROUP * SEG), BF16)
    ones = lambda n: jnp.ones((16, n), F32)
    kv = [(_bf(k), _bf(jnp.concatenate([v.T, ones(v.shape[0])], axis=0))) for k, v in kv_blocks]
    w_groups = []
    for g in range(A_KV_HEADS):
        heads = range(g * A_GROUP, (g + 1) * A_GROUP)
        slab = _bf(jnp.concatenate([qt[h * HEAD_DIM:(h + 1) * HEAD_DIM, :] for h in heads], axis=1))
        w_groups.append(jnp.concatenate([slab if i == g else zeros for i in range(A_KV_HEADS)], axis=0))
    w_all = jnp.concatenate(w_groups, axis=1)
    group_out = [None] * A_KV_HEADS
    for q0 in range(0, A_HEADS * SEG, ATT_LANES):
        w_c = w_all[:, q0:q0 + ATT_LANES]
        m = acc = None
        nxt = jnp.dot(kv[0][0], w_c, preferred_element_type=F32)
        for i, (_, v_t) in enumerate(kv):
            sc = nxt
            if i + 1 < len(kv):
                nxt = jnp.dot(kv[i + 1][0], w_c, preferred_element_type=F32)
            m_blk = jnp.max(sc, axis=0, keepdims=True)
            m_new = m_blk if m is None else jnp.maximum(m, m_blk)
            pv = jnp.dot(v_t, _bf(jnp.exp(sc - m_new)), preferred_element_type=F32)
            acc = pv if m is None else acc * jnp.exp(m - m_new) + pv
            m = m_new
        out = acc[0:KV_WIDTH] / acc[KV_WIDTH:KV_WIDTH + 1]
        for g in range(A_KV_HEADS):
            g0, g1 = g * A_GROUP * SEG, (g + 1) * A_GROUP * SEG
            lo, hi = max(g0, q0), min(g1, q0 + ATT_LANES)
            if lo < hi:
                piece = out[g * HEAD_DIM:(g + 1) * HEAD_DIM, lo - q0:hi - q0]
                group_out[g] = piece if group_out[g] is None else jnp.concatenate([group_out[g], piece], axis=1)
    for col in range(A_WIDTH // KV_WIDTH):
        g, h0 = (2 * col) // A_GROUP, (2 * col) % A_GROUP
        pair = jnp.concatenate([group_out[g][:, h0 * SEG:(h0 + 1) * SEG],
                                group_out[g][:, (h0 + 1) * SEG:(h0 + 2) * SEG]], axis=0)
        o_ref[:, col * KV_WIDTH:(col + 1) * KV_WIDTH] = pair.T


def _attn_kernel(q_ref, kc_ref, vc_ref, kl_ref, vl_ref, ck_ref, cv_ref, o_ref, *, segs):
    s = pl.program_id(0)

    @pl.when(segs.is_ctx(s))
    def _():
        _attend(q_ref, [(kc_ref[...], vc_ref[...])], o_ref)

    @pl.when(jnp.logical_not(segs.is_ctx(s)))
    def _():
        blocks = []
        past, t_lat = ck_ref.shape[2], kl_ref.shape[0]
        for lo in range(0, past, KV_BLOCK):
            hi = min(lo + KV_BLOCK, past)
            blocks.append((ck_ref[0, 0, lo:hi, :], cv_ref[0, 0, lo:hi, :]))
        for lo in range(0, t_lat, KV_BLOCK):
            hi = min(lo + KV_BLOCK, t_lat)
            blocks.append((kl_ref[lo:hi, :], vl_ref[lo:hi, :]))
        _attend(q_ref, blocks, o_ref)


def _attention(layer, segs, aq, ak, av, cache_k, cache_v):
    n_tok = aq.shape[0]
    t_lat = segs.per_seq * SEG
    past = cache_k.shape[2]
    assert t_lat % KV_BLOCK == 0
    lat_blocks_before = segs.n_ctx_seg * SEG // t_lat
    own = lambda w: pl.BlockSpec((SEG, w), lambda s: (s, 0))
    lat_kv = pl.BlockSpec((t_lat, KV_WIDTH), lambda s: (lat_blocks_before + segs.lat_seq(s), 0))
    cache = pl.BlockSpec((1, 1, past, KV_WIDTH), lambda s: (segs.lat_seq(s), layer, 0, 0))
    return pl.pallas_call(
        functools.partial(_attn_kernel, segs=segs),
        grid=(segs.n_seg,),
        in_specs=[own(A_WIDTH), own(KV_WIDTH), own(KV_WIDTH), lat_kv, lat_kv, cache, cache],
        out_specs=own(A_WIDTH),
        out_shape=jax.ShapeDtypeStruct((n_tok, A_WIDTH), F32),
        compiler_params=pltpu.CompilerParams(
            dimension_semantics=("arbitrary",), vmem_limit_bytes=VMEM_LIMIT),
        name="attention",
    )(aq, ak, av, ak, av, cache_k, cache_v)


def _permute_w_in(w_in):
    m, g, a, kv = SCAN_W, SCAN_W, A_WIDTH, KV_WIDTH
    sizes = (m, m, m, m, 8, 8, 3 * g, g, 8, 8, a, kv, kv)
    offs = [0]
    for sz in sizes:
        offs.append(offs[-1] + sz)
    piece = lambda i: w_in[..., offs[i]:offs[i + 1]]
    pad = jnp.zeros(w_in.shape[:-1] + (GATE_W - 32,), w_in.dtype)
    order = [0, 1, 2, 3, 6, 7, 10, 11, 12, 4, 5, 8, 9]
    return jnp.concatenate([piece(i) for i in order] + [pad], axis=-1)


def _rope_tables(t_lat):
    rows = t_lat // GRID_W
    row = np.repeat(np.arange(rows, dtype=np.float64), GRID_W)
    col = np.tile(np.arange(GRID_W, dtype=np.float64), rows)
    n_freq = HEAD_DIM // 4
    inv = ROPE_BASE ** (-np.arange(n_freq, dtype=np.float64) / n_freq)
    ang = np.stack([row[:, None] * inv, col[:, None] * inv], axis=1)
    cos, sin = np.cos(ang), np.sin(ang)
    cos_h = np.concatenate([cos, cos], axis=-1).reshape(t_lat, HEAD_DIM)
    sin_h = np.concatenate([-sin, sin], axis=-1).reshape(t_lat, HEAD_DIM)
    wide = lambda a: np.tile(a, (1, KV_WIDTH // HEAD_DIM))
    cos_t = np.concatenate([wide(cos_h), np.ones((TM, KV_WIDTH))], axis=0)
    sin_t = np.concatenate([wide(sin_h), np.zeros((TM, KV_WIDTH))], axis=0)
    return jnp.asarray(cos_t, F32), jnp.asarray(sin_t, F32)


def kernel(x_prompt, x_sample, c, cache_k, cache_v, state_mlstm_C, state_mlstm_n, state_mlstm_m,
           state_delta_S, c_ctx, ada_w, ada_b, norm_g, ffn_w_in, ffn_w_out, w_in, w_out, mlstm_f_bias,
           mlstm_norm, delta_conv, delta_a_log, delta_dt_bias, delta_norm, attn_q_norm, attn_k_norm,
           final_norm):
    batch, t_ctx, d_model = x_prompt.shape
    n_lat_seq, t_lat, _ = x_sample.shape
    depth = ada_w.shape[0]
    assert d_model == D_MODEL and norm_g.shape[1] == 3 and ffn_w_in.shape[-1] == 2 * D_FF
    assert 1 + n_lat_seq <= MOD_ROWS
    n_ctx = batch * t_ctx
    n_tok = n_ctx + n_lat_seq * t_lat
    assert n_ctx % t_lat == 0
    segs = _Segs(n_ctx, t_ctx, n_lat_seq, t_lat)
    pairs = _Pairs(n_ctx, t_ctx, n_lat_seq, t_lat)
    tiles = _Tiles(n_ctx, t_lat)

    cond = jnp.concatenate([c_ctx[None, :], c, jnp.zeros((MOD_ROWS - 1 - n_lat_seq, D_MODEL), F32)], axis=0)
    mod = _modulation(cond, ada_w, ada_b)

    wfi = _bf(ffn_w_in).reshape(depth * 2, D_MODEL, 2 * D_FF)
    wfo = _bf(ffn_w_out).reshape(depth * 2, D_FF, D_MODEL)
    wz = _bf(_permute_w_in(w_in))
    wo = _bf(w_out)
    qn = jnp.tile(attn_q_norm, (1, A_HEADS)).reshape(depth, 1, A_WIDTH)
    kn = jnp.tile(attn_k_norm, (1, A_KV_HEADS)).reshape(depth, 1, KV_WIDTH)
    gn = jnp.tile(delta_norm, (1, N_HEADS)).reshape(depth, 1, SCAN_W)
    mn = mlstm_norm.reshape(depth, 1, SCAN_W)
    fin = final_norm.reshape(1, D_MODEL)
    fb_cols = mlstm_f_bias.reshape(depth, N_GATE_ROWS, 1)
    alog_cols = delta_a_log.reshape(depth, N_GATE_ROWS, 1)
    dtb_cols = delta_dt_bias.reshape(depth, N_GATE_ROWS, 1)
    cos_t, sin_t = _rope_tables(t_lat)
    ck = cache_k.reshape(cache_k.shape[:3] + (KV_WIDTH,))
    cv = cache_v.reshape(cache_v.shape[:3] + (KV_WIDTH,))
    n0 = state_mlstm_n.reshape(n_lat_seq, depth, N_DIR, 1, SCAN_W)
    m0 = state_mlstm_m.reshape(n_lat_seq, depth, N_GATE_ROWS, 1)

    x_ctx, x_lat, lat_tile0 = x_prompt.reshape(n_ctx, D_MODEL), x_sample.reshape(n_tok - n_ctx, D_MODEL), 0
    ks, vs, cs, ns, ms, ss = [], [], [], [], [], []
    for l in range(depth):
        x1, zm, zg, aq, ak, av, gates = _dense1(l, tiles, n_tok, x_ctx, x_lat, lat_tile0, mod, norm_g,
                                                wfi, wfo, wz, qn, kn, cos_t, sin_t)
        zm_v, zg_v, gates_v = pairs.view(zm), pairs.view(zg), pairs.view(gates)
        hf, hb, c_new, n_new, m_new = _mlstm(l, pairs, zm_v, gates_v, fb_cols, state_mlstm_C, n0, m0)
        o_f, o_b, s_new = _delta(l, pairs, zg_v, gates_v, delta_conv, alog_cols, dtb_cols, state_delta_S)
        hf, hb, o_f, o_b = (a.reshape(n_tok, SCAN_W) for a in (hf, hb, o_f, o_b))
        a_out = _attention(l, segs, aq, ak, av, ck, cv)
        x = _dense2(l, tiles, x1, mod, norm_g, hf, hb, zm, mn, o_f, o_b, zg, gn, a_out, wo, wfi, wfo, fin,
                    final=(l == depth - 1))
        if l < depth - 1:
            x_ctx, x_lat, lat_tile0 = x, x, tiles.n_ctx

        ks.append(ak[:n_ctx].reshape(batch, t_ctx, A_KV_HEADS, HEAD_DIM))
        vs.append(av[:n_ctx].reshape(batch, t_ctx, A_KV_HEADS, HEAD_DIM))
        cs.append(pairs.ctx_states(c_new))
        ns.append(pairs.ctx_states(n_new).reshape(batch, N_DIR, N_HEADS, HEAD_DIM))
        ms.append(pairs.ctx_states(m_new).reshape(batch, N_DIR, N_HEADS))
        ss.append(pairs.ctx_states(s_new))

    y_prompt = x[0].reshape(batch, t_ctx, D_MODEL)
    y_sample = x[1].reshape(n_lat_seq, t_lat, D_MODEL)
    stack = lambda xs: jnp.stack(xs, axis=1)
    return (y_prompt, y_sample, stack(ks), stack(vs), stack(cs), stack(ns), stack(ms), stack(ss))
```

```python
import functools
import math

import numpy as np

import jax
import jax.numpy as jnp
from jax import lax
from jax.experimental import pallas as pl
from jax.experimental.pallas import tpu as pltpu

F32 = jnp.float32
BF16 = jnp.bfloat16

D_MODEL = 1024
HEAD_DIM = 64
N_HEADS = 4
SCAN_W = N_HEADS * HEAD_DIM
A_HEADS = 8
A_KV_HEADS = 2
A_GROUP = A_HEADS // A_KV_HEADS
A_WIDTH = A_HEADS * HEAD_DIM
KV_WIDTH = A_KV_HEADS * HEAD_DIM
N_DIR = 2
CHUNK = 64
CONV_W = 5
D_FF = 2816
FF_CHUNK = 256
GRID_W = 64
ROPE_BASE = 10000.0
EPS = 1e-6
N_MOD = 9
MOD_ROWS = 8
GATE_W = 128
Z_WIDTH = 2 * D_MODEL + A_WIDTH + 2 * KV_WIDTH + GATE_W

SEG = 256
N_CHUNK = SEG // CHUNK
TM = 512
ROW_GROUP = 512
N_PAIR = 2
HALO = 8
KV_BLOCK = 512
ATT_LANES = 2048
VMEM_LIMIT = 56 * 1024 * 1024


def _bf(x):
    return x.astype(BF16)


def _dot(a, b):
    return jnp.dot(_bf(a), _bf(b), preferred_element_type=F32)


def _bdot(a, b):
    return lax.dot_general(a, b, (((2,), (1,)), ((0,), (0,))), preferred_element_type=F32)


def _bdot_nt(a, b):
    return lax.dot_general(a, b, (((2,), (2,)), ((0,), (0,))), preferred_element_type=F32)


def _split(a, terms):
    parts = []
    rest = a
    for _ in range(terms):
        p = _bf(rest)
        parts.append(p)
        rest = rest - p.astype(F32)
    return parts


def _dot_exact_rhs(a, b01, terms=2):
    parts = _split(a, terms)
    if a.shape[-1] % 128 == 0:
        return jnp.dot(jnp.concatenate(parts, axis=-1), jnp.concatenate([b01] * terms, axis=0),
                       preferred_element_type=F32)
    out = None
    for p in parts:
        d = jnp.dot(p, b01, preferred_element_type=F32)
        out = d if out is None else out + d
    return out


def _iota(shape, dim):
    return lax.broadcasted_iota(jnp.int32, shape, dim)


def _head_of(idx):
    return jnp.right_shift(idx, 6)


def _in_head(idx):
    return jnp.bitwise_and(idx, HEAD_DIM - 1)


def _block_ones(n):
    return (_head_of(_iota((n, n), 0)) == _head_of(_iota((n, n), 1)))


def _head_sum(x):
    n = x.shape[-1]
    return _dot_exact_rhs(x, _block_ones(n).astype(BF16), terms=1)


def _head_rms(x, g_row):
    ms = _head_sum(x * x) * (1.0 / HEAD_DIM)
    return x * lax.rsqrt(ms + EPS) * g_row


def _rms_mod(x, g_row, scale_row, shift_row):
    y = x * lax.rsqrt(jnp.mean(x * x, axis=-1, keepdims=True) + EPS) * g_row
    return y * (1.0 + scale_row) + shift_row


def _softplus(x):
    return jnp.maximum(x, 0.0) + jnp.log1p(jnp.exp(-jnp.abs(x)))


def _log_sigmoid(x):
    return -_softplus(-x)


def _sigmoid(x):
    return jax.nn.sigmoid(x)


def _silu(x):
    return x * jax.nn.sigmoid(x)


def _bd3(x, mask_bf):
    return jnp.concatenate([x] * N_HEADS, axis=1) * mask_bf


def _to_block_diag(x4):
    spread = (_iota((HEAD_DIM, SCAN_W), 0) == _in_head(_iota((HEAD_DIM, SCAN_W), 1))).astype(BF16)
    rows = x4.reshape(SCAN_W, HEAD_DIM)
    return jnp.where(_block_ones(SCAN_W), _dot_exact_rhs(rows, spread, terms=3), 0.0)


def _from_block_diag(x):
    fold = (_in_head(_iota((SCAN_W, HEAD_DIM), 0)) == _iota((SCAN_W, HEAD_DIM), 1)).astype(BF16)
    return _dot_exact_rhs(x, fold, terms=3).reshape(N_HEADS, HEAD_DIM, HEAD_DIM)


def _chunks(a):
    return a.reshape(N_CHUNK, CHUNK, SCAN_W)


def _problem_masks():
    shape = (N_PAIR * N_DIR * N_CHUNK, CHUNK, SCAN_W)
    t_idx, j_idx = _iota(shape, 1), _in_head(_iota(shape, 2))
    is_rev = jnp.bitwise_and(_iota(shape, 0), N_DIR * N_CHUNK - 1) >= N_CHUNK
    ahead = jnp.where(is_rev, j_idx - t_idx, t_idx - j_idx)
    return t_idx == j_idx, ahead >= 0, ahead > 0


def _scan_order(d):
    return range(N_CHUNK - 1, -1, -1) if d else range(N_CHUNK)


def _mod_kernel(cond_ref, w_ref, b_ref, o_ref):
    a = _silu(cond_ref[...])
    o_ref[0, 0] = _dot(a, w_ref[0]) + b_ref[0]


def _modulation(cond, ada_w, ada_b):
    depth = ada_w.shape[0]
    return pl.pallas_call(
        _mod_kernel,
        grid=(depth, N_MOD),
        in_specs=[
            pl.BlockSpec((MOD_ROWS, D_MODEL), lambda l, j: (0, 0)),
            pl.BlockSpec((1, D_MODEL, D_MODEL), lambda l, j: (l, 0, j)),
            pl.BlockSpec((1, 1, D_MODEL), lambda l, j: (l, 0, j)),
        ],
        out_specs=pl.BlockSpec((1, 1, MOD_ROWS, D_MODEL), lambda l, j: (l, j, 0, 0)),
        out_shape=jax.ShapeDtypeStruct((depth, N_MOD, MOD_ROWS, D_MODEL), F32),
        compiler_params=pltpu.CompilerParams(
            dimension_semantics=("arbitrary", "arbitrary"), vmem_limit_bytes=VMEM_LIMIT),
        name="modulation",
    )(cond, ada_w, ada_b.reshape(depth, 1, N_MOD * D_MODEL))


def _ffn(h, w_in_ref, w_out_ref):
    n_chunks = D_FF // FF_CHUNK

    def gate_up(c):
        lo = c * FF_CHUNK
        g = jnp.dot(h, w_in_ref[0, :, lo:lo + FF_CHUNK], preferred_element_type=F32)
        u = jnp.dot(h, w_in_ref[0, :, D_FF + lo:D_FF + lo + FF_CHUNK], preferred_element_type=F32)
        return g, u

    acc = None
    nxt = gate_up(0)
    for c in range(n_chunks):
        g, u = nxt
        if c + 1 < n_chunks:
            nxt = gate_up(c + 1)
        a = _bf(_silu(g) * u)
        d = jnp.dot(a, w_out_ref[0, c * FF_CHUNK:(c + 1) * FF_CHUNK, :], preferred_element_type=F32)
        acc = d if acc is None else acc + d
    return acc


def _rope(x, cos, sin_signed):
    w = x.shape[-1]
    reps = w // cos.shape[-1]
    if reps > 1:
        cos = jnp.concatenate([cos] * reps, axis=-1)
        sin_signed = jnp.concatenate([sin_signed] * reps, axis=-1)
    first = jnp.bitwise_and(_iota(x.shape, 1), 31) < 16
    partner = jnp.where(first, pltpu.roll(x, w - 16, 1), pltpu.roll(x, 16, 1))
    return x * cos + partner * sin_signed


class _Tiles:
    def __init__(self, n_ctx, t_lat):
        assert n_ctx % TM == 0 and t_lat % TM == 0
        self.n_ctx = n_ctx // TM
        self.per_seq = t_lat // TM

    def is_ctx(self, i):
        return i < self.n_ctx

    def lat(self, i):
        return jnp.maximum(i - self.n_ctx, 0)

    def mod_row(self, i):
        return jnp.where(self.is_ctx(i), 0, 1 + self.lat(i) // self.per_seq)


def _mod_rows(mod_ref, row):
    return lambda m: mod_ref[0, m, pl.ds(row, 1), :]


def _dense1_kernel(xa_ref, xb_ref, mod_ref, ng_ref, wfi_ref, wfo_ref, wz_ref, qn_ref, kn_ref, cos_ref, sin_ref,
                   x1_ref, zm_ref, zg_ref, aq_ref, ak_ref, av_ref, gate_ref, *, tiles):
    i = pl.program_id(0)
    mod = _mod_rows(mod_ref, tiles.mod_row(i))
    is_ctx = tiles.is_ctx(i)
    for r in range(TM // ROW_GROUP):
        rows = slice(r * ROW_GROUP, (r + 1) * ROW_GROUP)
        x = jnp.where(is_ctx, xa_ref[rows, :], xb_ref[rows, :])
        h = _bf(_rms_mod(x, ng_ref[0, 0:1], mod(0), mod(1)))
        x1 = x + 0.5 * mod(2) * _ffn(h, wfi_ref, wfo_ref)
        x1_ref[rows, :] = x1
        h2 = _bf(_rms_mod(x1, ng_ref[0, 1:2], mod(3), mod(4)))

        z = jnp.dot(h2, wz_ref[0], preferred_element_type=F32)

        def proj(lo, width):
            return z[:, lo:lo + width]

        zm_ref[rows, :] = proj(0, D_MODEL)
        zg_ref[rows, :] = proj(D_MODEL, D_MODEL)
        cos = cos_ref[rows, :]
        sin = sin_ref[rows, :]
        off = 2 * D_MODEL
        q = _head_rms(proj(off, A_WIDTH), qn_ref[0])
        aq_ref[rows, :] = _rope(q, cos, sin) * (1.0 / math.sqrt(HEAD_DIM))
        k = _head_rms(proj(off + A_WIDTH, KV_WIDTH), kn_ref[0])
        ak_ref[rows, :] = _rope(k, cos, sin)
        av_ref[rows, :] = proj(off + A_WIDTH + KV_WIDTH, KV_WIDTH)
        gate_ref[rows, :] = proj(off + A_WIDTH + 2 * KV_WIDTH, GATE_W)


def _resident(shape, index_map):
    return pl.BlockSpec(shape, index_map, pipeline_mode=pl.Buffered(1))


def _dense1(layer, tiles, n_tok, x_ctx, x_lat, lat_tile0, mod, norm_g, wfi, wfo, wz, qn, kn, cos_t, sin_t):
    row = lambda w: pl.BlockSpec((TM, w), lambda i: (i, 0))
    rope_tile = lambda i: jnp.where(tiles.is_ctx(i), tiles.per_seq, tiles.lat(i) % tiles.per_seq)
    rope = pl.BlockSpec((TM, KV_WIDTH), lambda i: (rope_tile(i), 0))
    out_w = (D_MODEL, D_MODEL, D_MODEL, A_WIDTH, KV_WIDTH, KV_WIDTH, GATE_W)
    return pl.pallas_call(
        functools.partial(_dense1_kernel, tiles=tiles),
        grid=(n_tok // TM,),
        in_specs=[
            pl.BlockSpec((TM, D_MODEL), lambda i: (jnp.minimum(i, tiles.n_ctx - 1), 0)),
            pl.BlockSpec((TM, D_MODEL), lambda i: (tiles.lat(i) + lat_tile0, 0)),
            pl.BlockSpec((1, N_MOD, MOD_ROWS, D_MODEL), lambda i: (layer, 0, 0, 0)),
            pl.BlockSpec((1, 3, D_MODEL), lambda i: (layer, 0, 0)),
            _resident((1, D_MODEL, 2 * D_FF), lambda i: (2 * layer, 0, 0)),
            _resident((1, D_FF, D_MODEL), lambda i: (2 * layer, 0, 0)),
            _resident((1, D_MODEL, Z_WIDTH), lambda i: (layer, 0, 0)),
            pl.BlockSpec((1, 1, A_WIDTH), lambda i: (layer, 0, 0)),
            pl.BlockSpec((1, 1, KV_WIDTH), lambda i: (layer, 0, 0)),
            rope, rope,
        ],
        out_specs=[row(w) for w in out_w],
        out_shape=[jax.ShapeDtypeStruct((n_tok, w), F32) for w in out_w],
        compiler_params=pltpu.CompilerParams(
            dimension_semantics=("arbitrary",), vmem_limit_bytes=VMEM_LIMIT),
        name="dense1",
    )(x_ctx, x_lat, mod, norm_g, wfi, wfo, wz, qn, kn, cos_t, sin_t)


def _dense2_kernel(x_ref, mod_ref, ng_ref, hf_ref, hb_ref, mo_ref, mn_ref, of_ref, ob_ref, gz_ref, gn_ref,
                   ao_ref, wo_ref, wfi_ref, wfo_ref, fin_ref, *out_refs, tiles, final):
    i = pl.program_id(0)
    mod = _mod_rows(mod_ref, tiles.mod_row(i))
    m_out = _head_rms(hf_ref[...] + hb_ref[...], mn_ref[0]) * _sigmoid(mo_ref[...])
    g_out = _head_rms(of_ref[...] + ob_ref[...], gn_ref[0]) * _silu(gz_ref[...])
    mix = jnp.dot(_bf(m_out), wo_ref[0, 0:SCAN_W, :], preferred_element_type=F32)
    mix += jnp.dot(_bf(g_out), wo_ref[0, SCAN_W:2 * SCAN_W, :], preferred_element_type=F32)
    mix += jnp.dot(_bf(ao_ref[...]), wo_ref[0, 2 * SCAN_W:, :], preferred_element_type=F32)
    x2 = x_ref[...] + mod(5) * mix
    h = _bf(_rms_mod(x2, ng_ref[0, 2:3], mod(6), mod(7)))
    x3 = x2 + 0.5 * mod(8) * _ffn(h, wfi_ref, wfo_ref)
    if not final:
        out_refs[0][...] = x3
        return
    y = x3 * lax.rsqrt(jnp.mean(x3 * x3, axis=-1, keepdims=True) + EPS) * fin_ref[...]
    y_ctx_ref, y_lat_ref = out_refs

    @pl.when(tiles.is_ctx(i))
    def _():
        y_ctx_ref[...] = y

    @pl.when(jnp.logical_not(tiles.is_ctx(i)))
    def _():
        y_lat_ref[...] = y


def _dense2(layer, tiles, x1, mod, norm_g, hf, hb, zm, m_norm, of, ob, zg, g_norm, a_out, wo, wfi, wfo, fin,
            final):
    n_tok = x1.shape[0]
    row = lambda w: pl.BlockSpec((TM, w), lambda i: (i, 0))
    last_quarter = pl.BlockSpec((TM, SCAN_W), lambda i: (i, 3))
    lane_row = lambda w: pl.BlockSpec((1, 1, w), lambda i: (layer, 0, 0))
    if final:
        n_ctx = tiles.n_ctx * TM
        out_specs = [pl.BlockSpec((TM, D_MODEL), lambda i: (jnp.minimum(i, tiles.n_ctx - 1), 0)),
                     pl.BlockSpec((TM, D_MODEL), lambda i: (tiles.lat(i), 0))]
        out_shape = [jax.ShapeDtypeStruct((n_ctx, D_MODEL), F32),
                     jax.ShapeDtypeStruct((n_tok - n_ctx, D_MODEL), F32)]
    else:
        out_specs, out_shape = row(D_MODEL), jax.ShapeDtypeStruct((n_tok, D_MODEL), F32)
    return pl.pallas_call(
        functools.partial(_dense2_kernel, tiles=tiles, final=final),
        grid=(n_tok // TM,),
        in_specs=[
            row(D_MODEL),
            pl.BlockSpec((1, N_MOD, MOD_ROWS, D_MODEL), lambda i: (layer, 0, 0, 0)),
            pl.BlockSpec((1, 3, D_MODEL), lambda i: (layer, 0, 0)),
            row(SCAN_W), row(SCAN_W), last_quarter, lane_row(SCAN_W),
            row(SCAN_W), row(SCAN_W), last_quarter, lane_row(SCAN_W),
            row(A_WIDTH),
            _resident((1, D_MODEL, D_MODEL), lambda i: (layer, 0, 0)),
            _resident((1, D_MODEL, 2 * D_FF), lambda i: (2 * layer + 1, 0, 0)),
            _resident((1, D_FF, D_MODEL), lambda i: (2 * layer + 1, 0, 0)),
            pl.BlockSpec((1, D_MODEL), lambda i: (0, 0)),
        ],
        out_specs=out_specs,
        out_shape=out_shape,
        compiler_params=pltpu.CompilerParams(
            dimension_semantics=("arbitrary",), vmem_limit_bytes=VMEM_LIMIT),
        name="dense2",
    )(x1, mod, norm_g, hf, hb, zm, m_norm, of, ob, zg, g_norm, a_out, wo, wfi, wfo, fin)


class _Segs:
    def __init__(self, n_ctx, t_ctx, n_lat_seq, t_lat):
        assert t_ctx == SEG and t_lat % SEG == 0
        self.n_ctx_seg = n_ctx // SEG
        self.per_seq = t_lat // SEG
        self.n_lat_seq = n_lat_seq
        self.n_seg = self.n_ctx_seg + n_lat_seq * self.per_seq
        self.n_seq = self.n_ctx_seg + n_lat_seq

    def is_ctx(self, s):
        return s < self.n_ctx_seg

    def lat(self, s):
        r = jnp.maximum(s - self.n_ctx_seg, 0)
        return r // self.per_seq, r % self.per_seq

    def blk(self, s, rev):
        b, j = self.lat(s)
        j = self.per_seq - 1 - j if rev else j
        return jnp.where(self.is_ctx(s), s, self.n_ctx_seg + b * self.per_seq + j)

    def seq(self, s):
        return jnp.where(self.is_ctx(s), s, self.n_ctx_seg + self.lat(s)[0])

    def lat_seq(self, s):
        return self.lat(s)[0]

    def first(self, s):
        return jnp.logical_or(self.is_ctx(s), self.lat(s)[1] == 0)

    def last(self, s):
        return jnp.logical_or(self.is_ctx(s), self.lat(s)[1] == self.per_seq - 1)


_STATE_MAT = (N_DIR, N_HEADS, HEAD_DIM, HEAD_DIM)


class _Pairs:
    def __init__(self, n_ctx, t_ctx, n_lat_seq, t_lat):
        assert t_ctx == SEG and t_lat % SEG == 0 and n_ctx == N_PAIR * t_lat and n_lat_seq % N_PAIR == 0
        self.rows = t_lat
        self.per_seq = t_lat // SEG
        self.n_ctx_step = t_lat // t_ctx
        self.n_step = self.n_ctx_step + (n_lat_seq // N_PAIR) * self.per_seq
        self.n_slot = self.n_ctx_step + n_lat_seq // N_PAIR

    def view(self, x):
        return x.reshape(x.shape[0] // self.rows, self.rows, x.shape[1])

    def is_ctx(self, s):
        return s < self.n_ctx_step

    def lat(self, s):
        r = jnp.maximum(s - self.n_ctx_step, 0)
        return r // self.per_seq, r % self.per_seq

    def macro(self, s):
        return jnp.where(self.is_ctx(s), 0, 1 + self.lat(s)[0])

    def seg(self, s, rev):
        j = self.lat(s)[1]
        return jnp.where(self.is_ctx(s), s, self.per_seq - 1 - j if rev else j)

    def slot(self, s):
        return jnp.where(self.is_ctx(s), s, self.n_ctx_step + self.lat(s)[0])

    def first(self, s):
        return jnp.logical_or(self.is_ctx(s), self.lat(s)[1] == 0)

    def last(self, s):
        return jnp.logical_or(self.is_ctx(s), self.lat(s)[1] == self.per_seq - 1)

    def block(self, rev, w):
        return pl.BlockSpec((N_PAIR, SEG, w), lambda s: (self.macro(s), self.seg(s, rev), 0))

    def state_in(self, layer, tail):
        zeros = (0,) * len(tail)
        return pl.BlockSpec((N_PAIR, 1) + tail, lambda s: (self.lat(s)[0], layer) + zeros)

    def state_out(self, tail):
        zeros = (0,) * len(tail)
        spec = pl.BlockSpec((1, N_PAIR) + tail, lambda s: (self.slot(s), 0) + zeros)
        return spec, jax.ShapeDtypeStruct((self.n_slot, N_PAIR) + tail, F32)

    def ctx_states(self, x):
        ctx = jnp.swapaxes(x[:self.n_ctx_step], 0, 1)
        return ctx.reshape((N_PAIR * self.n_ctx_step,) + x.shape[2:])


N_GATE_ROWS = N_DIR * N_HEADS
ALL_ROWS = N_PAIR * N_GATE_ROWS


def _lane_scan(x, op, ident, rev):
    n = x.shape[-1]
    pos = _in_head(_iota(x.shape, x.ndim - 1))
    s = 1
    while s < CHUNK:
        if rev:
            y, ok = pltpu.roll(x, n - s, x.ndim - 1), pos < CHUNK - s
        else:
            y, ok = pltpu.roll(x, s, x.ndim - 1), pos >= s
        x = op(x, jnp.where(ok, y, ident))
        s *= 2
    return x


def _pieces(x, terms):
    out = []
    for _ in range(terms):
        p = _bf(x).astype(F32)
        out.append(p)
        x = x - p
    return jnp.concatenate(out, axis=0)


def _expander(n_qty):
    k = np.arange(SCAN_W)[:, None]
    n = np.arange(N_PAIR * N_DIR * n_qty * SCAN_W)[None, :]
    rows_per_piece = n_qty * ALL_ROWS
    assert 3 * rows_per_piece <= SCAN_W
    qty, row = (k % rows_per_piece) // ALL_ROWS, k % ALL_ROWS
    grp, qty_n, head = n // (n_qty * SCAN_W), (n // SCAN_W) % n_qty, (n % SCAN_W) // HEAD_DIM
    used = k < 3 * rows_per_piece
    return jnp.asarray(used & (qty == qty_n) & (row == grp * N_HEADS + head), BF16)


def _expand(quantities, expand_ref):
    n_qty = len(quantities)
    packed = jnp.concatenate(quantities, axis=0)
    pad = jnp.zeros((SCAN_W - 3 * packed.shape[0], SEG), F32)
    lhs = _bf(jnp.concatenate([_pieces(packed, 3), pad], axis=0).T)
    wide = jnp.dot(lhs, expand_ref[...], preferred_element_type=F32)

    def stack(j):
        cols = [wide[:, (g * n_qty + j) * SCAN_W:(g * n_qty + j + 1) * SCAN_W] for g in range(N_PAIR * N_DIR)]
        return jnp.concatenate([_chunks(x) for x in cols], axis=0)

    return [stack(j) for j in range(n_qty)]


def _gate_rows(gf_ref, gb_ref):
    is_b = jnp.bitwise_and(_iota((ALL_ROWS, SEG), 0), N_GATE_ROWS - 1) >= N_HEADS
    is_b8 = _iota((N_GATE_ROWS, SEG), 0) >= N_HEADS
    gt = [(gf_ref[m].T, gb_ref[m].T) for m in range(N_PAIR)]

    def pick(lo):
        return jnp.concatenate([jnp.where(is_b8, b[lo:lo + N_GATE_ROWS], f[lo:lo + N_GATE_ROWS])
                                for f, b in gt], axis=0)

    return is_b, pick


def _chunk_sums(x, is_b):
    s_idx, t_idx = _iota((SEG, 2 * SEG), 0), _iota((SEG, 2 * SEG), 1)
    t_loc = jnp.bitwise_and(t_idx, SEG - 1)
    tri = jnp.logical_and(_head_of(s_idx) == _head_of(t_loc),
                          jnp.where(t_idx >= SEG, s_idx - t_loc, t_loc - s_idx) >= 0).astype(BF16)
    sums = jnp.dot(_bf(_pieces(x, 3)), tri, preferred_element_type=F32)
    sums = sums[0:ALL_ROWS] + sums[ALL_ROWS:2 * ALL_ROWS] + sums[2 * ALL_ROWS:]
    return jnp.where(is_b, sums[:, SEG:], sums[:, 0:SEG])


def _on_lanes(cols, mirrored, is_b):
    chunk_of = _head_of(_iota((ALL_ROWS, SEG), 1))
    out = jnp.zeros((ALL_ROWS, SEG), F32)
    for c in range(N_CHUNK):
        col = jnp.where(is_b, cols[N_CHUNK - 1 - c], cols[c]) if mirrored else cols[c]
        out = jnp.where(chunk_of == c, col, out)
    return out


def _mlstm_compact(gf_ref, gb_ref, fb_col, m_col, expand2_ref, expand4_ref):
    is_b, pick = _gate_rows(gf_ref, gb_ref)
    ig = pick(0)
    lf = _log_sigmoid(pick(N_GATE_ROWS) + fb_col)
    b = _chunk_sums(lf, is_b)
    r = ig - b
    cm = jnp.where(is_b, _lane_scan(r, jnp.maximum, -jnp.inf, True), _lane_scan(r, jnp.maximum, -jnp.inf, False))
    on_lanes = functools.partial(_on_lanes, is_b=is_b)
    in_chunk = lambda x, c: x[:, c * CHUNK:(c + 1) * CHUNK]
    bl_cols = [jnp.sum(in_chunk(lf, c), axis=1, keepdims=True) for c in range(N_CHUNK)]
    bl = on_lanes(bl_cols, False)
    lw = bl - b + ig
    lw_cols = [jnp.max(in_chunk(lw, c), axis=1, keepdims=True) for c in range(N_CHUNK)]

    is_b_col = is_b[:, 0:1]
    at_step = lambda cols, i: jnp.where(is_b_col, cols[N_CHUNK - 1 - i], cols[i])
    m_in, m_out = [], []
    for i in range(N_CHUNK):
        m_in.append(m_col)
        m_col = jnp.maximum(at_step(bl_cols, i) + m_col, at_step(lw_cols, i))
        m_out.append(m_col)
    m_c, m_n = on_lanes(m_in, True), on_lanes(m_out, True)
    kw_fac, dec = _expand([jnp.exp(lw - m_n), jnp.exp(bl + m_c - m_n)], expand2_ref)
    mx = jnp.maximum(m_c, cm)
    r, mx, a_int, neg_mt = _expand([r, mx, jnp.exp(m_c - mx), -(b + mx)], expand4_ref)
    return (r, mx, a_int, neg_mt, kw_fac, dec), m_col


def _mlstm_kernel(zmf_ref, zmb_ref, gf_ref, gb_ref, fb_ref, expand2_ref, expand4_ref, c0_ref, n0_ref, m0_ref,
                  hf_ref, hb_ref, c_out, n_out, m_out, c_s, n_s, m_s, *, pairs):
    s = pl.program_id(0)
    members = range(N_PAIR)

    @pl.when(pairs.first(s))
    def _():
        ctx = pairs.is_ctx(s)
        for m in members:
            m_s[m * N_GATE_ROWS:(m + 1) * N_GATE_ROWS] = jnp.where(ctx, 0.0, m0_ref[m, 0])
            for d in range(N_DIR):
                c_s[m, d] = jnp.where(ctx, 0.0, _to_block_diag(c0_ref[m, 0, d]))
                n_s[m, d] = jnp.where(ctx, 0.0, n0_ref[m, 0, d])

    fb_col = jnp.concatenate([fb_ref[0]] * N_PAIR, axis=0)
    (r, mx, a_int, neg_mt, kw_fac, dec), m_fin = _mlstm_compact(gf_ref, gb_ref, fb_col, m_s[...],
                                                                expand2_ref, expand4_ref)
    col = lambda j: jnp.concatenate([_chunks(ref[m, :, j * SCAN_W:(j + 1) * SCAN_W])
                                     for m in members for ref in (zmf_ref, zmb_ref)], axis=0)
    q, k, v = col(0), col(1) * (1.0 / math.sqrt(HEAD_DIM)), col(2)

    diag, tri, _ = _problem_masks()
    mask_bd = _block_ones(SCAN_W)
    mask_bf = mask_bd.astype(BF16)

    r_row = jnp.sum(jnp.where(diag, r, 0.0), axis=1, keepdims=True)
    w = jnp.exp(jnp.where(tri, r_row - mx, -jnp.inf))
    sc = _bdot_nt(_bf(q), _bd3(_bf(k), mask_bf)) * w
    kw = kw_fac * k
    kw_sum = jnp.sum(kw, axis=1, keepdims=True)

    n_prob = N_PAIR * N_DIR * N_CHUNK
    c_at, n_at = [None] * n_prob, [None] * n_prob
    c_fin, n_fin = {}, {}
    for m in members:
        for d in range(N_DIR):
            c_bd, n_row = c_s[m, d], n_s[m, d]
            for c in _scan_order(d):
                p = (m * N_DIR + d) * N_CHUNK + c
                c_at[p], n_at[p] = c_bd, n_row
                dec_row = dec[p, 0:1, :]
                c_bd = c_bd * dec_row + jnp.where(mask_bd, _dot(kw[p].T, v[p]), 0.0)
                n_row = n_row * dec_row + kw_sum[p]
            c_fin[m, d], n_fin[m, d] = c_bd, n_row
    c_all = jnp.concatenate([_bf(x)[None] for x in c_at], axis=0)
    n_all = jnp.concatenate([x[None] for x in n_at], axis=0)

    num = _bdot(_bf(sc), _bd3(_bf(v), mask_bf)) + a_int * _bdot(_bf(q), c_all)
    n_rows = n_prob * CHUNK
    flat = lambda a: a.reshape(n_rows, SCAN_W)
    sums = _dot_exact_rhs(jnp.concatenate([flat(sc), flat(q * n_all)], axis=0), mask_bf)
    den = sums[0:n_rows].reshape(sc.shape) + a_int * sums[n_rows:].reshape(sc.shape)
    h = num / jnp.maximum(jnp.abs(den), jnp.exp(neg_mt))
    for m in members:
        for d, out_ref in enumerate((hf_ref, hb_ref)):
            p0 = (m * N_DIR + d) * N_CHUNK
            out_ref[m] = h[p0:p0 + N_CHUNK].reshape(SEG, SCAN_W)

    m_s[...] = m_fin
    for m in members:
        for d in range(N_DIR):
            c_s[m, d] = c_fin[m, d]
            n_s[m, d] = n_fin[m, d]

    @pl.when(pairs.last(s))
    def _():
        for m in members:
            m_out[0, m] = m_s[m * N_GATE_ROWS:(m + 1) * N_GATE_ROWS]
            for d in range(N_DIR):
                c_out[0, m, d] = _from_block_diag(c_fin[m, d])
                n_out[0, m, d] = n_fin[m, d]


def _mlstm(layer, pairs, zm, gates, fb_cols, state_c, n0, m0):
    row_tail, col_tail = (N_DIR, 1, SCAN_W), (N_GATE_ROWS, 1)
    c_spec, c_shape = pairs.state_out(_STATE_MAT)
    n_spec, n_shape = pairs.state_out(row_tail)
    m_spec, m_shape = pairs.state_out(col_tail)
    expand2, expand4 = _expander(2), _expander(4)
    h_shape = jax.ShapeDtypeStruct(zm.shape[:2] + (SCAN_W,), F32)
    return pl.pallas_call(
        functools.partial(_mlstm_kernel, pairs=pairs),
        grid=(pairs.n_step,),
        in_specs=[
            pairs.block(False, D_MODEL), pairs.block(True, D_MODEL),
            pairs.block(False, GATE_W), pairs.block(True, GATE_W),
            pl.BlockSpec((1, N_GATE_ROWS, 1), lambda s: (layer, 0, 0)),
            pl.BlockSpec(expand2.shape, lambda s: (0, 0)),
            pl.BlockSpec(expand4.shape, lambda s: (0, 0)),
            pairs.state_in(layer, _STATE_MAT), pairs.state_in(layer, row_tail), pairs.state_in(layer, col_tail),
        ],
        out_specs=[pairs.block(False, SCAN_W), pairs.block(True, SCAN_W), c_spec, n_spec, m_spec],
        out_shape=[h_shape, h_shape, c_shape, n_shape, m_shape],
        scratch_shapes=[pltpu.VMEM((N_PAIR, N_DIR, SCAN_W, SCAN_W), F32),
                        pltpu.VMEM((N_PAIR, N_DIR, 1, SCAN_W), F32),
                        pltpu.VMEM((ALL_ROWS, 1), F32)],
        compiler_params=pltpu.CompilerParams(
            dimension_semantics=("arbitrary",), vmem_limit_bytes=VMEM_LIMIT),
        name="mlstm",
    )(zm, zm, gates, gates, fb_cols, expand2, expand4, state_c, n0, m0)


def _short_conv(prev_ref, cur_ref, next_ref, w_ref, has_prev, has_next):
    w3 = 3 * SCAN_W
    prev = jnp.where(has_prev, prev_ref[:, 0:w3], 0.0)
    nxt = jnp.where(has_next, next_ref[:, 0:w3], 0.0)
    xp = jnp.concatenate([prev, cur_ref[:, 0:w3], nxt], axis=0)
    rows = xp.shape[0]
    acc = None
    for i in range(CONV_W):
        shift = (CONV_W // 2 - i) % rows
        y = xp if shift == 0 else pltpu.roll(xp, shift, 0)
        t = y[HALO:HALO + SEG] * w_ref[0, i:i + 1, :]
        acc = t if acc is None else acc + t
    return acc


def _neumann_inverse(n_all, mask_bf):
    eye = (_iota(n_all.shape[1:], 0) == _in_head(_iota(n_all.shape[1:], 1))).astype(F32)
    p = -n_all
    t = eye + p
    levels = 6
    for lvl in range(levels):
        first, last = lvl == 0, lvl == levels - 1
        p_hi, p_lo = _split(p, 2)
        w_hi, w_lo = _bd3(p_hi, mask_bf), _bd3(p_lo, mask_bf)
        lhs_hi, lhs_lo = [], []
        if not first:
            t_hi, t_lo = _split(t, 2)
            lhs_hi += [t_hi, t_lo]
            lhs_lo += [t_hi]
        if not last:
            lhs_hi += [p_hi, p_lo]
            lhs_lo += [p_hi]
        a = _bdot(jnp.concatenate(lhs_hi, axis=1), w_hi)
        b = _bdot(jnp.concatenate(lhs_lo, axis=1) if len(lhs_lo) > 1 else lhs_lo[0], w_lo)
        ra, rb = 0, 0
        if not first:
            t = t + (a[:, 0:CHUNK] + a[:, CHUNK:2 * CHUNK] + b[:, 0:CHUNK])
            ra, rb = 2 * CHUNK, CHUNK
        if not last:
            p = a[:, ra:ra + CHUNK] + a[:, ra + CHUNK:ra + 2 * CHUNK] + b[:, rb:rb + CHUNK]
    return t


def _delta_qkv(prev_ref, cur_ref, next_ref, cw_ref, has_prev, has_next):
    qkv = _silu(_short_conv(prev_ref, cur_ref, next_ref, cw_ref, has_prev, has_next))

    def l2(a):
        return a * lax.rsqrt(_head_sum(a * a) + EPS)

    q = l2(qkv[:, 0:SCAN_W]) * (1.0 / math.sqrt(HEAD_DIM))
    k = l2(qkv[:, SCAN_W:2 * SCAN_W])
    return jnp.concatenate([q, k, qkv[:, 2 * SCAN_W:3 * SCAN_W]], axis=1)


def _delta_stage_qkv(s, pairs, blocks, cw_ref, qkv_s, seen_s):
    members = range(N_PAIR)
    lat = jnp.logical_not(pairs.is_ctx(s))
    j = pairs.lat(s)[1]
    mirror = pairs.per_seq - 1 - j
    half = pairs.per_seq // 2

    @pl.when(pairs.is_ctx(s))
    def _():
        for m in members:
            pf, cf, nf = (r.at[m] for r in blocks[0])
            x = _delta_qkv(pf, cf, nf, cw_ref, False, False)
            qkv_s[m, 0] = x
            qkv_s[m, 1] = x

    @pl.when(jnp.logical_and(lat, j < half))
    def _():
        for m in members:
            (pf, cf, nf), (pb, cb, nb) = ((r.at[m] for r in refs) for refs in blocks)
            xf = _delta_qkv(pf, cf, nf, cw_ref, j > 0, True)
            xb = _delta_qkv(pb, cb, nb, cw_ref, True, j > 0)
            qkv_s[m, 0] = xf
            qkv_s[m, 1] = xb
            seen_s[m, pl.ds(j, 1)] = xf[None]
            seen_s[m, pl.ds(mirror, 1)] = xb[None]

    @pl.when(jnp.logical_and(lat, j >= half))
    def _():
        for m in members:
            qkv_s[m, 0] = seen_s[m, pl.ds(j, 1)][0]
            qkv_s[m, 1] = seen_s[m, pl.ds(mirror, 1)][0]


def _delta_gates(gf_ref, gb_ref, alog_col, dtb_col, expand_ref):
    is_b, pick = _gate_rows(gf_ref, gb_ref)
    g = -jnp.exp(alog_col) * _softplus(pick(2 * N_GATE_ROWS) + dtb_col)
    beta = _sigmoid(pick(3 * N_GATE_ROWS))
    gc = _chunk_sums(g, is_b)
    totals = [jnp.sum(g[:, c * CHUNK:(c + 1) * CHUNK], axis=1, keepdims=True) for c in range(N_CHUNK)]
    gl = _on_lanes(totals, False, is_b)
    return _expand([gc, jnp.exp(gc), jnp.exp(gl - gc), jnp.exp(gl), beta], expand_ref)


def _delta_kernel(pf_ref, cf_ref, nf_ref, pb_ref, cb_ref, nb_ref, gf_ref, gb_ref, cw_ref, al_ref, dt_ref,
                  expand_ref, s0_ref, of_ref, ob_ref, s_out, s_s, qkv_s, seen_s, *, pairs):
    s = pl.program_id(0)
    members = range(N_PAIR)
    lat = jnp.logical_not(pairs.is_ctx(s))

    @pl.when(pairs.first(s))
    def _():
        for m in members:
            for d in range(N_DIR):
                s_s[m, d] = jnp.where(lat, _to_block_diag(s0_ref[m, 0, d]), 0.0)

    _delta_stage_qkv(s, pairs, ((pf_ref, cf_ref, nf_ref), (pb_ref, cb_ref, nb_ref)), cw_ref, qkv_s, seen_s)
    both = lambda col: jnp.concatenate([col] * N_PAIR, axis=0)
    gc, eg, k_fac, g_last, beta = _delta_gates(gf_ref, gb_ref, both(al_ref[0]), both(dt_ref[0]), expand_ref)
    q, k, v = (jnp.concatenate([_chunks(qkv_s[m, d, :, i * SCAN_W:(i + 1) * SCAN_W])
                                for m in members for d in range(N_DIR)], axis=0) for i in range(3))
    kb = k * beta
    k_dec = k * k_fac

    mask_bd = _block_ones(SCAN_W)
    mask_bf = mask_bd.astype(BF16)
    diag, tri, strict = _problem_masks()
    neg_diag = jnp.sum(jnp.where(diag, -gc, 0.0), axis=1, keepdims=True)
    decay = jnp.exp(jnp.where(tri, gc + neg_diag, -jnp.inf))
    k_bd = _bd3(_bf(k), mask_bf)
    kq = _bdot_nt(_bf(jnp.concatenate([kb, q], axis=1)), k_bd)
    qk = _bf(kq[:, CHUNK:] * decay)
    t_all = _bf(_neumann_inverse(jnp.where(strict, kq[:, 0:CHUNK] * decay, 0.0), mask_bf))
    u = _bdot(t_all, _bd3(_bf(v * beta), mask_bf))
    w = _bdot(t_all, _bd3(_bf(kb * eg), mask_bf))
    wq = _bf(jnp.concatenate([w, q * eg], axis=1))

    outs = (of_ref, ob_ref)
    chains = [(m, d) for m in members for d in range(N_DIR)]
    k_dec_t = [k_dec[p].T for p in range(k_dec.shape[0])]
    state = jnp.concatenate([s_s[m, d][None] for m, d in chains], axis=0)
    for i in range(N_CHUNK):
        probs = [(m * N_DIR + d) * N_CHUNK + (N_CHUNK - 1 - i if d else i) for m, d in chains]
        pick = lambda x: jnp.concatenate([x[p][None] for p in probs], axis=0)
        ws = _bdot(pick(wq), _bf(state))
        v_new = pick(u) - ws[:, 0:CHUNK]
        out = ws[:, CHUNK:] + _bdot(pick(qk), _bd3(_bf(v_new), mask_bf))
        for n, (m, d) in enumerate(chains):
            c = N_CHUNK - 1 - i if d else i
            outs[d][m, c * CHUNK:(c + 1) * CHUNK, :] = out[n]
        kd_t = _bf(jnp.concatenate([k_dec_t[p][None] for p in probs], axis=0))
        state = state * pick(g_last)[:, 0:1, :] + jnp.where(mask_bd, _bdot(kd_t, _bf(v_new)), 0.0)
    for n, (m, d) in enumerate(chains):
        s_s[m, d] = state[n]

    @pl.when(pairs.last(s))
    def _():
        for n, (m, d) in enumerate(chains):
            s_out[0, m, d] = _from_block_diag(state[n])


def _delta(layer, pairs, zg, gates, conv_w, alog_cols, dtb_cols, state_s):
    per = SEG // HALO
    last_halo = pairs.rows // HALO - 1

    def prev(rev):
        return pl.BlockSpec((N_PAIR, HALO, D_MODEL),
                            lambda s: (pairs.macro(s), jnp.maximum(pairs.seg(s, rev) * per - 1, 0), 0))

    def nxt(rev):
        return pl.BlockSpec((N_PAIR, HALO, D_MODEL),
                            lambda s: (pairs.macro(s), jnp.minimum((pairs.seg(s, rev) + 1) * per, last_halo), 0))

    gate_cols = pl.BlockSpec((1, N_GATE_ROWS, 1), lambda s: (layer, 0, 0))
    s_spec, s_shape = pairs.state_out(_STATE_MAT)
    expand5 = _expander(5)
    o_shape = jax.ShapeDtypeStruct(zg.shape[:2] + (SCAN_W,), F32)
    return pl.pallas_call(
        functools.partial(_delta_kernel, pairs=pairs),
        grid=(pairs.n_step,),
        in_specs=[
            prev(False), pairs.block(False, D_MODEL), nxt(False),
            prev(True), pairs.block(True, D_MODEL), nxt(True),
            pairs.block(False, GATE_W), pairs.block(True, GATE_W),
            pl.BlockSpec((1, CONV_W, 3 * SCAN_W), lambda s: (layer, 0, 0)),
            gate_cols, gate_cols, pl.BlockSpec(expand5.shape, lambda s: (0, 0)),
            pairs.state_in(layer, _STATE_MAT),
        ],
        out_specs=[pairs.block(False, SCAN_W), pairs.block(True, SCAN_W), s_spec],
        out_shape=[o_shape, o_shape, s_shape],
        scratch_shapes=[pltpu.VMEM((N_PAIR, N_DIR, SCAN_W, SCAN_W), F32),
                        pltpu.VMEM((N_PAIR, N_DIR, SEG, 3 * SCAN_W), F32),
                        pltpu.VMEM((N_PAIR, pairs.per_seq, SEG, 3 * SCAN_W), F32)],
        compiler_params=pltpu.CompilerParams(
            dimension_semantics=("arbitrary",), vmem_limit_bytes=VMEM_LIMIT),
        name="delta",
    )(zg, zg, zg, zg, zg, zg, gates, gates, conv_w, alog_cols, dtb_cols, expand5, state_s)


def _attend(q_ref, kv_blocks, o_ref):
    qt = q_ref[...].T
    zeros = jnp.zeros((HEAD_DIM, A_GROUP * SEG), BF16)
    ones = lambda n: jnp.ones((16, n), F32)
    kv = [(_bf(k), _bf(jnp.concatenate([v.T, ones(v.shape[0])], axis=0))) for k, v in kv_blocks]
    w_groups = []
    for g in range(A_KV_HEADS):
        heads = range(g * A_GROUP, (g + 1) * A_GROUP)
        slab = _bf(jnp.concatenate([qt[h * HEAD_DIM:(h + 1) * HEAD_DIM, :] for h in heads], axis=1))
        w_groups.append(jnp.concatenate([slab if i == g else zeros for i in range(A_KV_HEADS)], axis=0))
    w_all = jnp.concatenate(w_groups, axis=1)
    group_out = [None] * A_KV_HEADS
    for q0 in range(0, A_HEADS * SEG, ATT_LANES):
        w_c = w_all[:, q0:q0 + ATT_LANES]
        m = acc = None
        nxt = jnp.dot(kv[0][0], w_c, preferred_element_type=F32)
        for i, (_, v_t) in enumerate(kv):
            sc = nxt
            if i + 1 < len(kv):
                nxt = jnp.dot(kv[i + 1][0], w_c, preferred_element_type=F32)
            m_blk = jnp.max(sc, axis=0, keepdims=True)
            m_new = m_blk if m is None else jnp.maximum(m, m_blk)
            pv = jnp.dot(v_t, _bf(jnp.exp(sc - m_new)), preferred_element_type=F32)
            acc = pv if m is None else acc * jnp.exp(m - m_new) + pv
            m = m_new
        out = acc[0:KV_WIDTH] / acc[KV_WIDTH:KV_WIDTH + 1]
        for g in range(A_KV_HEADS):
            g0, g1 = g * A_GROUP * SEG, (g + 1) * A_GROUP * SEG
            lo, hi = max(g0, q0), min(g1, q0 + ATT_LANES)
            if lo < hi:
                piece = out[g * HEAD_DIM:(g + 1) * HEAD_DIM, lo - q0:hi - q0]
                group_out[g] = piece if group_out[g] is None else jnp.concatenate([group_out[g], piece], axis=1)
    for col in range(A_WIDTH // KV_WIDTH):
        g, h0 = (2 * col) // A_GROUP, (2 * col) % A_GROUP
        pair = jnp.concatenate([group_out[g][:, h0 * SEG:(h0 + 1) * SEG],
                                group_out[g][:, (h0 + 1) * SEG:(h0 + 2) * SEG]], axis=0)
        o_ref[:, col * KV_WIDTH:(col + 1) * KV_WIDTH] = pair.T


def _attn_kernel(q_ref, kc_ref, vc_ref, kl_ref, vl_ref, ck_ref, cv_ref, o_ref, *, segs):
    s = pl.program_id(0)

    @pl.when(segs.is_ctx(s))
    def _():
        _attend(q_ref, [(kc_ref[...], vc_ref[...])], o_ref)

    @pl.when(jnp.logical_not(segs.is_ctx(s)))
    def _():
        blocks = []
        past, t_lat = ck_ref.shape[2], kl_ref.shape[0]
        for lo in range(0, past, KV_BLOCK):
            hi = min(lo + KV_BLOCK, past)
            blocks.append((ck_ref[0, 0, lo:hi, :], cv_ref[0, 0, lo:hi, :]))
        for lo in range(0, t_lat, KV_BLOCK):
            hi = min(lo + KV_BLOCK, t_lat)
            blocks.append((kl_ref[lo:hi, :], vl_ref[lo:hi, :]))
        _attend(q_ref, blocks, o_ref)


def _attention(layer, segs, aq, ak, av, cache_k, cache_v):
    n_tok = aq.shape[0]
    t_lat = segs.per_seq * SEG
    past = cache_k.shape[2]
    assert t_lat % KV_BLOCK == 0
    lat_blocks_before = segs.n_ctx_seg * SEG // t_lat
    own = lambda w: pl.BlockSpec((SEG, w), lambda s: (s, 0))
    lat_kv = pl.BlockSpec((t_lat, KV_WIDTH), lambda s: (lat_blocks_before + segs.lat_seq(s), 0))
    cache = pl.BlockSpec((1, 1, past, KV_WIDTH), lambda s: (segs.lat_seq(s), layer, 0, 0))
    return pl.pallas_call(
        functools.partial(_attn_kernel, segs=segs),
        grid=(segs.n_seg,),
        in_specs=[own(A_WIDTH), own(KV_WIDTH), own(KV_WIDTH), lat_kv, lat_kv, cache, cache],
        out_specs=own(A_WIDTH),
        out_shape=jax.ShapeDtypeStruct((n_tok, A_WIDTH), F32),
        compiler_params=pltpu.CompilerParams(
            dimension_semantics=("arbitrary",), vmem_limit_bytes=VMEM_LIMIT),
        name="attention",
    )(aq, ak, av, ak, av, cache_k, cache_v)


def _permute_w_in(w_in):
    m, g, a, kv = SCAN_W, SCAN_W, A_WIDTH, KV_WIDTH
    sizes = (m, m, m, m, 8, 8, 3 * g, g, 8, 8, a, kv, kv)
    offs = [0]
    for sz in sizes:
        offs.append(offs[-1] + sz)
    piece = lambda i: w_in[..., offs[i]:offs[i + 1]]
    pad = jnp.zeros(w_in.shape[:-1] + (GATE_W - 32,), w_in.dtype)
    order = [0, 1, 2, 3, 6, 7, 10, 11, 12, 4, 5, 8, 9]
    return jnp.concatenate([piece(i) for i in order] + [pad], axis=-1)


def _rope_tables(t_lat):
    rows = t_lat // GRID_W
    row = np.repeat(np.arange(rows, dtype=np.float64), GRID_W)
    col = np.tile(np.arange(GRID_W, dtype=np.float64), rows)
    n_freq = HEAD_DIM // 4
    inv = ROPE_BASE ** (-np.arange(n_freq, dtype=np.float64) / n_freq)
    ang = np.stack([row[:, None] * inv, col[:, None] * inv], axis=1)
    cos, sin = np.cos(ang), np.sin(ang)
    cos_h = np.concatenate([cos, cos], axis=-1).reshape(t_lat, HEAD_DIM)
    sin_h = np.concatenate([-sin, sin], axis=-1).reshape(t_lat, HEAD_DIM)
    wide = lambda a: np.tile(a, (1, KV_WIDTH // HEAD_DIM))
    cos_t = np.concatenate([wide(cos_h), np.ones((TM, KV_WIDTH))], axis=0)
    sin_t = np.concatenate([wide(sin_h), np.zeros((TM, KV_WIDTH))], axis=0)
    return jnp.asarray(cos_t, F32), jnp.asarray(sin_t, F32)


def kernel(x_prompt, x_sample, c, cache_k, cache_v, state_mlstm_C, state_mlstm_n, state_mlstm_m,
           state_delta_S, c_ctx, ada_w, ada_b, norm_g, ffn_w_in, ffn_w_out, w_in, w_out, mlstm_f_bias,
           mlstm_norm, delta_conv, delta_a_log, delta_dt_bias, delta_norm, attn_q_norm, attn_k_norm,
           final_norm):
    batch, t_ctx, d_model = x_prompt.shape
    n_lat_seq, t_lat, _ = x_sample.shape
    depth = ada_w.shape[0]
    assert d_model == D_MODEL and norm_g.shape[1] == 3 and ffn_w_in.shape[-1] == 2 * D_FF
    assert 1 + n_lat_seq <= MOD_ROWS
    n_ctx = batch * t_ctx
    n_tok = n_ctx + n_lat_seq * t_lat
    assert n_ctx % t_lat == 0
    segs = _Segs(n_ctx, t_ctx, n_lat_seq, t_lat)
    pairs = _Pairs(n_ctx, t_ctx, n_lat_seq, t_lat)
    tiles = _Tiles(n_ctx, t_lat)

    cond = jnp.concatenate([c_ctx[None, :], c, jnp.zeros((MOD_ROWS - 1 - n_lat_seq, D_MODEL), F32)], axis=0)
    mod = _modulation(cond, ada_w, ada_b)

    wfi = _bf(ffn_w_in).reshape(depth * 2, D_MODEL, 2 * D_FF)
    wfo = _bf(ffn_w_out).reshape(depth * 2, D_FF, D_MODEL)
    wz = _bf(_permute_w_in(w_in))
    wo = _bf(w_out)
    qn = jnp.tile(attn_q_norm, (1, A_HEADS)).reshape(depth, 1, A_WIDTH)
    kn = jnp.tile(attn_k_norm, (1, A_KV_HEADS)).reshape(depth, 1, KV_WIDTH)
    gn = jnp.tile(delta_norm, (1, N_HEADS)).reshape(depth, 1, SCAN_W)
    mn = mlstm_norm.reshape(depth, 1, SCAN_W)
    fin = final_norm.reshape(1, D_MODEL)
    fb_cols = mlstm_f_bias.reshape(depth, N_GATE_ROWS, 1)
    alog_cols = delta_a_log.reshape(depth, N_GATE_ROWS, 1)
    dtb_cols = delta_dt_bias.reshape(depth, N_GATE_ROWS, 1)
    cos_t, sin_t = _rope_tables(t_lat)
    ck = cache_k.reshape(cache_k.shape[:3] + (KV_WIDTH,))
    cv = cache_v.reshape(cache_v.shape[:3] + (KV_WIDTH,))
    n0 = state_mlstm_n.reshape(n_lat_seq, depth, N_DIR, 1, SCAN_W)
    m0 = state_mlstm_m.reshape(n_lat_seq, depth, N_GATE_ROWS, 1)

    x_ctx, x_lat, lat_tile0 = x_prompt.reshape(n_ctx, D_MODEL), x_sample.reshape(n_tok - n_ctx, D_MODEL), 0
    ks, vs, cs, ns, ms, ss = [], [], [], [], [], []
    for l in range(depth):
        x1, zm, zg, aq, ak, av, gates = _dense1(l, tiles, n_tok, x_ctx, x_lat, lat_tile0, mod, norm_g,
                                                wfi, wfo, wz, qn, kn, cos_t, sin_t)
        zm_v, zg_v, gates_v = pairs.view(zm), pairs.view(zg), pairs.view(gates)
        hf, hb, c_new, n_new, m_new = _mlstm(l, pairs, zm_v, gates_v, fb_cols, state_mlstm_C, n0, m0)
        o_f, o_b, s_new = _delta(l, pairs, zg_v, gates_v, delta_conv, alog_cols, dtb_cols, state_delta_S)
        hf, hb, o_f, o_b = (a.reshape(n_tok, SCAN_W) for a in (hf, hb, o_f, o_b))
        a_out = _attention(l, segs, aq, ak, av, ck, cv)
        x = _dense2(l, tiles, x1, mod, norm_g, hf, hb, zm, mn, o_f, o_b, zg, gn, a_out, wo, wfi, wfo, fin,
                    final=(l == depth - 1))
        if l < depth - 1:
            x_ctx, x_lat, lat_tile0 = x, x, tiles.n_ctx

        ks.append(ak[:n_ctx].reshape(batch, t_ctx, A_KV_HEADS, HEAD_DIM))
        vs.append(av[:n_ctx].reshape(batch, t_ctx, A_KV_HEADS, HEAD_DIM))
        cs.append(pairs.ctx_states(c_new))
        ns.append(pairs.ctx_states(n_new).reshape(batch, N_DIR, N_HEADS, HEAD_DIM))
        ms.append(pairs.ctx_states(m_new).reshape(batch, N_DIR, N_HEADS))
        ss.append(pairs.ctx_states(s_new))

    y_prompt = x[0].reshape(batch, t_ctx, D_MODEL)
    y_sample = x[1].reshape(n_lat_seq, t_lat, D_MODEL)
    stack = lambda xs: jnp.stack(xs, axis=1)
    return (y_prompt, y_sample, stack(ks), stack(vs), stack(cs), stack(ns), stack(ms), stack(ss))
```

```python
import functools
import math

import numpy as np

import jax
import jax.numpy as jnp
from jax import lax
from jax.experimental import pallas as pl
from jax.experimental.pallas import tpu as pltpu

F32 = jnp.float32
BF16 = jnp.bfloat16

D_MODEL = 1024
HEAD_DIM = 64
N_HEADS = 4
SCAN_W = N_HEADS * HEAD_DIM
A_HEADS = 8
A_KV_HEADS = 2
A_GROUP = A_HEADS // A_KV_HEADS
A_WIDTH = A_HEADS * HEAD_DIM
KV_WIDTH = A_KV_HEADS * HEAD_DIM
N_DIR = 2
CHUNK = 64
CONV_W = 5
D_FF = 2816
FF_CHUNK = 256
GRID_W = 64
ROPE_BASE = 10000.0
EPS = 1e-6
N_MOD = 9
MOD_ROWS = 8
GATE_W = 128
Z_WIDTH = 2 * D_MODEL + A_WIDTH + 2 * KV_WIDTH + GATE_W

SEG = 256
N_CHUNK = SEG // CHUNK
TM = 512
ROW_GROUP = 512
N_PAIR = 2
HALO = 8
KV_BLOCK = 512
ATT_LANES = 2048
VMEM_LIMIT = 56 * 1024 * 1024


def _bf(x):
    return x.astype(BF16)


def _dot(a, b):
    return jnp.dot(_bf(a), _bf(b), preferred_element_type=F32)


def _bdot(a, b):
    return lax.dot_general(a, b, (((2,), (1,)), ((0,), (0,))), preferred_element_type=F32)


def _bdot_nt(a, b):
    return lax.dot_general(a, b, (((2,), (2,)), ((0,), (0,))), preferred_element_type=F32)


def _split(a, terms):
    parts = []
    rest = a
    for _ in range(terms):
        p = _bf(rest)
        parts.append(p)
        rest = rest - p.astype(F32)
    return parts


def _dot_exact_rhs(a, b01, terms=2):
    parts = _split(a, terms)
    if a.shape[-1] % 128 == 0:
        return jnp.dot(jnp.concatenate(parts, axis=-1), jnp.concatenate([b01] * terms, axis=0),
                       preferred_element_type=F32)
    out = None
    for p in parts:
        d = jnp.dot(p, b01, preferred_element_type=F32)
        out = d if out is None else out + d
    return out


def _iota(shape, dim):
    return lax.broadcasted_iota(jnp.int32, shape, dim)


def _head_of(idx):
    return jnp.right_shift(idx, 6)


def _in_head(idx):
    return jnp.bitwise_and(idx, HEAD_DIM - 1)


def _block_ones(n):
    return (_head_of(_iota((n, n), 0)) == _head_of(_iota((n, n), 1)))


def _head_sum(x):
    n = x.shape[-1]
    return _dot_exact_rhs(x, _block_ones(n).astype(BF16), terms=1)


def _head_rms(x, g_row):
    ms = _head_sum(x * x) * (1.0 / HEAD_DIM)
    return x * lax.rsqrt(ms + EPS) * g_row


def _rms_mod(x, g_row, scale_row, shift_row):
    y = x * lax.rsqrt(jnp.mean(x * x, axis=-1, keepdims=True) + EPS) * g_row
    return y * (1.0 + scale_row) + shift_row


def _softplus(x):
    return jnp.maximum(x, 0.0) + jnp.log1p(jnp.exp(-jnp.abs(x)))


def _log_sigmoid(x):
    return -_softplus(-x)


def _sigmoid(x):
    return jax.nn.sigmoid(x)


def _silu(x):
    return x * jax.nn.sigmoid(x)


def _bd3(x, mask_bf):
    return jnp.concatenate([x] * N_HEADS, axis=1) * mask_bf


def _to_block_diag(x4):
    spread = (_iota((HEAD_DIM, SCAN_W), 0) == _in_head(_iota((HEAD_DIM, SCAN_W), 1))).astype(BF16)
    rows = x4.reshape(SCAN_W, HEAD_DIM)
    return jnp.where(_block_ones(SCAN_W), _dot_exact_rhs(rows, spread, terms=3), 0.0)


def _from_block_diag(x):
    fold = (_in_head(_iota((SCAN_W, HEAD_DIM), 0)) == _iota((SCAN_W, HEAD_DIM), 1)).astype(BF16)
    return _dot_exact_rhs(x, fold, terms=3).reshape(N_HEADS, HEAD_DIM, HEAD_DIM)


def _chunks(a):
    return a.reshape(N_CHUNK, CHUNK, SCAN_W)


def _problem_masks():
    shape = (N_PAIR * N_DIR * N_CHUNK, CHUNK, SCAN_W)
    t_idx, j_idx = _iota(shape, 1), _in_head(_iota(shape, 2))
    is_rev = jnp.bitwise_and(_iota(shape, 0), N_DIR * N_CHUNK - 1) >= N_CHUNK
    ahead = jnp.where(is_rev, j_idx - t_idx, t_idx - j_idx)
    return t_idx == j_idx, ahead >= 0, ahead > 0


def _scan_order(d):
    return range(N_CHUNK - 1, -1, -1) if d else range(N_CHUNK)


def _mod_kernel(cond_ref, w_ref, b_ref, o_ref):
    a = _silu(cond_ref[...])
    o_ref[0, 0] = _dot(a, w_ref[0]) + b_ref[0]


def _modulation(cond, ada_w, ada_b):
    depth = ada_w.shape[0]
    return pl.pallas_call(
        _mod_kernel,
        grid=(depth, N_MOD),
        in_specs=[
            pl.BlockSpec((MOD_ROWS, D_MODEL), lambda l, j: (0, 0)),
            pl.BlockSpec((1, D_MODEL, D_MODEL), lambda l, j: (l, 0, j)),
            pl.BlockSpec((1, 1, D_MODEL), lambda l, j: (l, 0, j)),
        ],
        out_specs=pl.BlockSpec((1, 1, MOD_ROWS, D_MODEL), lambda l, j: (l, j, 0, 0)),
        out_shape=jax.ShapeDtypeStruct((depth, N_MOD, MOD_ROWS, D_MODEL), F32),
        compiler_params=pltpu.CompilerParams(
            dimension_semantics=("arbitrary", "arbitrary"), vmem_limit_bytes=VMEM_LIMIT),
        name="modulation",
    )(cond, ada_w, ada_b.reshape(depth, 1, N_MOD * D_MODEL))


def _ffn(h, w_in_ref, w_out_ref):
    n_chunks = D_FF // FF_CHUNK

    def gate_up(c):
        lo = c * FF_CHUNK
        g = jnp.dot(h, w_in_ref[0, :, lo:lo + FF_CHUNK], preferred_element_type=F32)
        u = jnp.dot(h, w_in_ref[0, :, D_FF + lo:D_FF + lo + FF_CHUNK], preferred_element_type=F32)
        return g, u

    acc = None
    nxt = gate_up(0)
    for c in range(n_chunks):
        g, u = nxt
        if c + 1 < n_chunks:
            nxt = gate_up(c + 1)
        a = _bf(_silu(g) * u)
        d = jnp.dot(a, w_out_ref[0, c * FF_CHUNK:(c + 1) * FF_CHUNK, :], preferred_element_type=F32)
        acc = d if acc is None else acc + d
    return acc


def _rope(x, cos, sin_signed):
    w = x.shape[-1]
    reps = w // cos.shape[-1]
    if reps > 1:
        cos = jnp.concatenate([cos] * reps, axis=-1)
        sin_signed = jnp.concatenate([sin_signed] * reps, axis=-1)
    first = jnp.bitwise_and(_iota(x.shape, 1), 31) < 16
    partner = jnp.where(first, pltpu.roll(x, w - 16, 1), pltpu.roll(x, 16, 1))
    return x * cos + partner * sin_signed


class _Tiles:
    def __init__(self, n_ctx, t_lat):
        assert n_ctx % TM == 0 and t_lat % TM == 0
        self.n_ctx = n_ctx // TM
        self.per_seq = t_lat // TM

    def is_ctx(self, i):
        return i < self.n_ctx

    def lat(self, i):
        return jnp.maximum(i - self.n_ctx, 0)

    def mod_row(self, i):
        return jnp.where(self.is_ctx(i), 0, 1 + self.lat(i) // self.per_seq)


def _mod_rows(mod_ref, row):
    return lambda m: mod_ref[0, m, pl.ds(row, 1), :]


def _dense1_kernel(xa_ref, xb_ref, mod_ref, ng_ref, wfi_ref, wfo_ref, wz_ref, qn_ref, kn_ref, cos_ref, sin_ref,
                   x1_ref, zm_ref, zg_ref, aq_ref, ak_ref, av_ref, gate_ref, *, tiles):
    i = pl.program_id(0)
    mod = _mod_rows(mod_ref, tiles.mod_row(i))
    is_ctx = tiles.is_ctx(i)
    for r in range(TM // ROW_GROUP):
        rows = slice(r * ROW_GROUP, (r + 1) * ROW_GROUP)
        x = jnp.where(is_ctx, xa_ref[rows, :], xb_ref[rows, :])
        h = _bf(_rms_mod(x, ng_ref[0, 0:1], mod(0), mod(1)))
        x1 = x + 0.5 * mod(2) * _ffn(h, wfi_ref, wfo_ref)
        x1_ref[rows, :] = x1
        h2 = _bf(_rms_mod(x1, ng_ref[0, 1:2], mod(3), mod(4)))

        z = jnp.dot(h2, wz_ref[0], preferred_element_type=F32)

        def proj(lo, width):
            return z[:, lo:lo + width]

        zm_ref[rows, :] = proj(0, D_MODEL)
        zg_ref[rows, :] = proj(D_MODEL, D_MODEL)
        cos = cos_ref[rows, :]
        sin = sin_ref[rows, :]
        off = 2 * D_MODEL
        q = _head_rms(proj(off, A_WIDTH), qn_ref[0])
        aq_ref[rows, :] = _rope(q, cos, sin) * (1.0 / math.sqrt(HEAD_DIM))
        k = _head_rms(proj(off + A_WIDTH, KV_WIDTH), kn_ref[0])
        ak_ref[rows, :] = _rope(k, cos, sin)
        av_ref[rows, :] = proj(off + A_WIDTH + KV_WIDTH, KV_WIDTH)
        gate_ref[rows, :] = proj(off + A_WIDTH + 2 * KV_WIDTH, GATE_W)


def _resident(shape, index_map):
    return pl.BlockSpec(shape, index_map, pipeline_mode=pl.Buffered(1))


def _dense1(layer, tiles, n_tok, x_ctx, x_lat, lat_tile0, mod, norm_g, wfi, wfo, wz, qn, kn, cos_t, sin_t):
    row = lambda w: pl.BlockSpec((TM, w), lambda i: (i, 0))
    rope_tile = lambda i: jnp.where(tiles.is_ctx(i), tiles.per_seq, tiles.lat(i) % tiles.per_seq)
    rope = pl.BlockSpec((TM, KV_WIDTH), lambda i: (rope_tile(i), 0))
    out_w = (D_MODEL, D_MODEL, D_MODEL, A_WIDTH, KV_WIDTH, KV_WIDTH, GATE_W)
    return pl.pallas_call(
        functools.partial(_dense1_kernel, tiles=tiles),
        grid=(n_tok // TM,),
        in_specs=[
            pl.BlockSpec((TM, D_MODEL), lambda i: (jnp.minimum(i, tiles.n_ctx - 1), 0)),
            pl.BlockSpec((TM, D_MODEL), lambda i: (tiles.lat(i) + lat_tile0, 0)),
            pl.BlockSpec((1, N_MOD, MOD_ROWS, D_MODEL), lambda i: (layer, 0, 0, 0)),
            pl.BlockSpec((1, 3, D_MODEL), lambda i: (layer, 0, 0)),
            _resident((1, D_MODEL, 2 * D_FF), lambda i: (2 * layer, 0, 0)),
            _resident((1, D_FF, D_MODEL), lambda i: (2 * layer, 0, 0)),
            _resident((1, D_MODEL, Z_WIDTH), lambda i: (layer, 0, 0)),
            pl.BlockSpec((1, 1, A_WIDTH), lambda i: (layer, 0, 0)),
            pl.BlockSpec((1, 1, KV_WIDTH), lambda i: (layer, 0, 0)),
            rope, rope,
        ],
        out_specs=[row(w) for w in out_w],
        out_shape=[jax.ShapeDtypeStruct((n_tok, w), F32) for w in out_w],
        compiler_params=pltpu.CompilerParams(
            dimension_semantics=("arbitrary",), vmem_limit_bytes=VMEM_LIMIT),
        name="dense1",
    )(x_ctx, x_lat, mod, norm_g, wfi, wfo, wz, qn, kn, cos_t, sin_t)


def _dense2_kernel(x_ref, mod_ref, ng_ref, hf_ref, hb_ref, mo_ref, mn_ref, of_ref, ob_ref, gz_ref, gn_ref,
                   ao_ref, wo_ref, wfi_ref, wfo_ref, fin_ref, *out_refs, tiles, final):
    i = pl.program_id(0)
    mod = _mod_rows(mod_ref, tiles.mod_row(i))
    m_out = _head_rms(hf_ref[...] + hb_ref[...], mn_ref[0]) * _sigmoid(mo_ref[...])
    g_out = _head_rms(of_ref[...] + ob_ref[...], gn_ref[0]) * _silu(gz_ref[...])
    mix = jnp.dot(_bf(m_out), wo_ref[0, 0:SCAN_W, :], preferred_element_type=F32)
    mix += jnp.dot(_bf(g_out), wo_ref[0, SCAN_W:2 * SCAN_W, :], preferred_element_type=F32)
    mix += jnp.dot(_bf(ao_ref[...]), wo_ref[0, 2 * SCAN_W:, :], preferred_element_type=F32)
    x2 = x_ref[...] + mod(5) * mix
    h = _bf(_rms_mod(x2, ng_ref[0, 2:3], mod(6), mod(7)))
    x3 = x2 + 0.5 * mod(8) * _ffn(h, wfi_ref, wfo_ref)
    if not final:
        out_refs[0][...] = x3
        return
    y = x3 * lax.rsqrt(jnp.mean(x3 * x3, axis=-1, keepdims=True) + EPS) * fin_ref[...]
    y_ctx_ref, y_lat_ref = out_refs

    @pl.when(tiles.is_ctx(i))
    def _():
        y_ctx_ref[...] = y

    @pl.when(jnp.logical_not(tiles.is_ctx(i)))
    def _():
        y_lat_ref[...] = y


def _dense2(layer, tiles, x1, mod, norm_g, hf, hb, zm, m_norm, of, ob, zg, g_norm, a_out, wo, wfi, wfo, fin,
            final):
    n_tok = x1.shape[0]
    row = lambda w: pl.BlockSpec((TM, w), lambda i: (i, 0))
    last_quarter = pl.BlockSpec((TM, SCAN_W), lambda i: (i, 3))
    lane_row = lambda w: pl.BlockSpec((1, 1, w), lambda i: (layer, 0, 0))
    if final:
        n_ctx = tiles.n_ctx * TM
        out_specs = [pl.BlockSpec((TM, D_MODEL), lambda i: (jnp.minimum(i, tiles.n_ctx - 1), 0)),
                     pl.BlockSpec((TM, D_MODEL), lambda i: (tiles.lat(i), 0))]
        out_shape = [jax.ShapeDtypeStruct((n_ctx, D_MODEL), F32),
                     jax.ShapeDtypeStruct((n_tok - n_ctx, D_MODEL), F32)]
    else:
        out_specs, out_shape = row(D_MODEL), jax.ShapeDtypeStruct((n_tok, D_MODEL), F32)
    return pl.pallas_call(
        functools.partial(_dense2_kernel, tiles=tiles, final=final),
        grid=(n_tok // TM,),
        in_specs=[
            row(D_MODEL),
            pl.BlockSpec((1, N_MOD, MOD_ROWS, D_MODEL), lambda i: (layer, 0, 0, 0)),
            pl.BlockSpec((1, 3, D_MODEL), lambda i: (layer, 0, 0)),
            row(SCAN_W), row(SCAN_W), last_quarter, lane_row(SCAN_W),
            row(SCAN_W), row(SCAN_W), last_quarter, lane_row(SCAN_W),
            row(A_WIDTH),
            _resident((1, D_MODEL, D_MODEL), lambda i: (layer, 0, 0)),
            _resident((1, D_MODEL, 2 * D_FF), lambda i: (2 * layer + 1, 0, 0)),
            _resident((1, D_FF, D_MODEL), lambda i: (2 * layer + 1, 0, 0)),
            pl.BlockSpec((1, D_MODEL), lambda i: (0, 0)),
        ],
        out_specs=out_specs,
        out_shape=out_shape,
        compiler_params=pltpu.CompilerParams(
            dimension_semantics=("arbitrary",), vmem_limit_bytes=VMEM_LIMIT),
        name="dense2",
    )(x1, mod, norm_g, hf, hb, zm, m_norm, of, ob, zg, g_norm, a_out, wo, wfi, wfo, fin)


class _Segs:
    def __init__(self, n_ctx, t_ctx, n_lat_seq, t_lat):
        assert t_ctx == SEG and t_lat % SEG == 0
        self.n_ctx_seg = n_ctx // SEG
        self.per_seq = t_lat // SEG
        self.n_lat_seq = n_lat_seq
        self.n_seg = self.n_ctx_seg + n_lat_seq * self.per_seq
        self.n_seq = self.n_ctx_seg + n_lat_seq

    def is_ctx(self, s):
        return s < self.n_ctx_seg

    def lat(self, s):
        r = jnp.maximum(s - self.n_ctx_seg, 0)
        return r // self.per_seq, r % self.per_seq

    def blk(self, s, rev):
        b, j = self.lat(s)
        j = self.per_seq - 1 - j if rev else j
        return jnp.where(self.is_ctx(s), s, self.n_ctx_seg + b * self.per_seq + j)

    def seq(self, s):
        return jnp.where(self.is_ctx(s), s, self.n_ctx_seg + self.lat(s)[0])

    def lat_seq(self, s):
        return self.lat(s)[0]

    def first(self, s):
        return jnp.logical_or(self.is_ctx(s), self.lat(s)[1] == 0)

    def last(self, s):
        return jnp.logical_or(self.is_ctx(s), self.lat(s)[1] == self.per_seq - 1)


_STATE_MAT = (N_DIR, N_HEADS, HEAD_DIM, HEAD_DIM)


class _Pairs:
    def __init__(self, n_ctx, t_ctx, n_lat_seq, t_lat):
        assert t_ctx == SEG and t_lat % SEG == 0 and n_ctx == N_PAIR * t_lat and n_lat_seq % N_PAIR == 0
        self.rows = t_lat
        self.per_seq = t_lat // SEG
        self.n_ctx_step = t_lat // t_ctx
        self.n_step = self.n_ctx_step + (n_lat_seq // N_PAIR) * self.per_seq
        self.n_slot = self.n_ctx_step + n_lat_seq // N_PAIR

    def view(self, x):
        return x.reshape(x.shape[0] // self.rows, self.rows, x.shape[1])

    def is_ctx(self, s):
        return s < self.n_ctx_step

    def lat(self, s):
        r = jnp.maximum(s - self.n_ctx_step, 0)
        return r // self.per_seq, r % self.per_seq

    def macro(self, s):
        return jnp.where(self.is_ctx(s), 0, 1 + self.lat(s)[0])

    def seg(self, s, rev):
        j = self.lat(s)[1]
        return jnp.where(self.is_ctx(s), s, self.per_seq - 1 - j if rev else j)

    def slot(self, s):
        return jnp.where(self.is_ctx(s), s, self.n_ctx_step + self.lat(s)[0])

    def first(self, s):
        return jnp.logical_or(self.is_ctx(s), self.lat(s)[1] == 0)

    def last(self, s):
        return jnp.logical_or(self.is_ctx(s), self.lat(s)[1] == self.per_seq - 1)

    def block(self, rev, w):
        return pl.BlockSpec((N_PAIR, SEG, w), lambda s: (self.macro(s), self.seg(s, rev), 0))

    def state_in(self, layer, tail):
        zeros = (0,) * len(tail)
        return pl.BlockSpec((N_PAIR, 1) + tail, lambda s: (self.lat(s)[0], layer) + zeros)

    def state_out(self, tail):
        zeros = (0,) * len(tail)
        spec = pl.BlockSpec((1, N_PAIR) + tail, lambda s: (self.slot(s), 0) + zeros)
        return spec, jax.ShapeDtypeStruct((self.n_slot, N_PAIR) + tail, F32)

    def ctx_states(self, x):
        ctx = jnp.swapaxes(x[:self.n_ctx_step], 0, 1)
        return ctx.reshape((N_PAIR * self.n_ctx_step,) + x.shape[2:])


N_GATE_ROWS = N_DIR * N_HEADS
ALL_ROWS = N_PAIR * N_GATE_ROWS


def _lane_scan(x, op, ident, rev):
    n = x.shape[-1]
    pos = _in_head(_iota(x.shape, x.ndim - 1))
    s = 1
    while s < CHUNK:
        if rev:
            y, ok = pltpu.roll(x, n - s, x.ndim - 1), pos < CHUNK - s
        else:
            y, ok = pltpu.roll(x, s, x.ndim - 1), pos >= s
        x = op(x, jnp.where(ok, y, ident))
        s *= 2
    return x


def _pieces(x, terms):
    out = []
    for _ in range(terms):
        p = _bf(x).astype(F32)
        out.append(p)
        x = x - p
    return jnp.concatenate(out, axis=0)


def _expander(n_qty):
    k = np.arange(SCAN_W)[:, None]
    n = np.arange(N_PAIR * N_DIR * n_qty * SCAN_W)[None, :]
    rows_per_piece = n_qty * ALL_ROWS
    assert 3 * rows_per_piece <= SCAN_W
    qty, row = (k % rows_per_piece) // ALL_ROWS, k % ALL_ROWS
    grp, qty_n, head = n // (n_qty * SCAN_W), (n // SCAN_W) % n_qty, (n % SCAN_W) // HEAD_DIM
    used = k < 3 * rows_per_piece
    return jnp.asarray(used & (qty == qty_n) & (row == grp * N_HEADS + head), BF16)


def _expand(quantities, expand_ref):
    n_qty = len(quantities)
    packed = jnp.concatenate(quantities, axis=0)
    pad = jnp.zeros((SCAN_W - 3 * packed.shape[0], SEG), F32)
    lhs = _bf(jnp.concatenate([_pieces(packed, 3), pad], axis=0).T)
    wide = jnp.dot(lhs, expand_ref[...], preferred_element_type=F32)

    def stack(j):
        cols = [wide[:, (g * n_qty + j) * SCAN_W:(g * n_qty + j + 1) * SCAN_W] for g in range(N_PAIR * N_DIR)]
        return jnp.concatenate([_chunks(x) for x in cols], axis=0)

    return [stack(j) for j in range(n_qty)]


def _gate_rows(gf_ref, gb_ref):
    is_b = jnp.bitwise_and(_iota((ALL_ROWS, SEG), 0), N_GATE_ROWS - 1) >= N_HEADS
    is_b8 = _iota((N_GATE_ROWS, SEG), 0) >= N_HEADS
    gt = [(gf_ref[m].T, gb_ref[m].T) for m in range(N_PAIR)]

    def pick(lo):
        return jnp.concatenate([jnp.where(is_b8, b[lo:lo + N_GATE_ROWS], f[lo:lo + N_GATE_ROWS])
                                for f, b in gt], axis=0)

    return is_b, pick


def _chunk_sums(x, is_b):
    s_idx, t_idx = _iota((SEG, 2 * SEG), 0), _iota((SEG, 2 * SEG), 1)
    t_loc = jnp.bitwise_and(t_idx, SEG - 1)
    tri = jnp.logical_and(_head_of(s_idx) == _head_of(t_loc),
                          jnp.where(t_idx >= SEG, s_idx - t_loc, t_loc - s_idx) >= 0).astype(BF16)
    sums = jnp.dot(_bf(_pieces(x, 3)), tri, preferred_element_type=F32)
    sums = sums[0:ALL_ROWS] + sums[ALL_ROWS:2 * ALL_ROWS] + sums[2 * ALL_ROWS:]
    return jnp.where(is_b, sums[:, SEG:], sums[:, 0:SEG])


def _on_lanes(cols, mirrored, is_b):
    chunk_of = _head_of(_iota((ALL_ROWS, SEG), 1))
    out = jnp.zeros((ALL_ROWS, SEG), F32)
    for c in range(N_CHUNK):
        col = jnp.where(is_b, cols[N_CHUNK - 1 - c], cols[c]) if mirrored else cols[c]
        out = jnp.where(chunk_of == c, col, out)
    return out


def _mlstm_compact(gf_ref, gb_ref, fb_col, m_col, expand2_ref, expand4_ref):
    is_b, pick = _gate_rows(gf_ref, gb_ref)
    ig = pick(0)
    lf = _log_sigmoid(pick(N_GATE_ROWS) + fb_col)
    b = _chunk_sums(lf, is_b)
    r = ig - b
    cm = jnp.where(is_b, _lane_scan(r, jnp.maximum, -jnp.inf, True), _lane_scan(r, jnp.maximum, -jnp.inf, False))
    on_lanes = functools.partial(_on_lanes, is_b=is_b)
    in_chunk = lambda x, c: x[:, c * CHUNK:(c + 1) * CHUNK]
    bl_cols = [jnp.sum(in_chunk(lf, c), axis=1, keepdims=True) for c in range(N_CHUNK)]
    bl = on_lanes(bl_cols, False)
    lw = bl - b + ig
    lw_cols = [jnp.max(in_chunk(lw, c), axis=1, keepdims=True) for c in range(N_CHUNK)]

    is_b_col = is_b[:, 0:1]
    at_step = lambda cols, i: jnp.where(is_b_col, cols[N_CHUNK - 1 - i], cols[i])
    m_in, m_out = [], []
    for i in range(N_CHUNK):
        m_in.append(m_col)
        m_col = jnp.maximum(at_step(bl_cols, i) + m_col, at_step(lw_cols, i))
        m_out.append(m_col)
    m_c, m_n = on_lanes(m_in, True), on_lanes(m_out, True)
    kw_fac, dec = _expand([jnp.exp(lw - m_n), jnp.exp(bl + m_c - m_n)], expand2_ref)
    mx = jnp.maximum(m_c, cm)
    r, mx, a_int, neg_mt = _expand([r, mx, jnp.exp(m_c - mx), -(b + mx)], expand4_ref)
    return (r, mx, a_int, neg_mt, kw_fac, dec), m_col


def _mlstm_kernel(zmf_ref, zmb_ref, gf_ref, gb_ref, fb_ref, expand2_ref, expand4_ref, c0_ref, n0_ref, m0_ref,
                  hf_ref, hb_ref, c_out, n_out, m_out, c_s, n_s, m_s, *, pairs):
    s = pl.program_id(0)
    members = range(N_PAIR)

    @pl.when(pairs.first(s))
    def _():
        ctx = pairs.is_ctx(s)
        for m in members:
            m_s[m * N_GATE_ROWS:(m + 1) * N_GATE_ROWS] = jnp.where(ctx, 0.0, m0_ref[m, 0])
            for d in range(N_DIR):
                c_s[m, d] = jnp.where(ctx, 0.0, _to_block_diag(c0_ref[m, 0, d]))
                n_s[m, d] = jnp.where(ctx, 0.0, n0_ref[m, 0, d])

    fb_col = jnp.concatenate([fb_ref[0]] * N_PAIR, axis=0)
    (r, mx, a_int, neg_mt, kw_fac, dec), m_fin = _mlstm_compact(gf_ref, gb_ref, fb_col, m_s[...],
                                                                expand2_ref, expand4_ref)
    col = lambda j: jnp.concatenate([_chunks(ref[m, :, j * SCAN_W:(j + 1) * SCAN_W])
                                     for m in members for ref in (zmf_ref, zmb_ref)], axis=0)
    q, k, v = col(0), col(1) * (1.0 / math.sqrt(HEAD_DIM)), col(2)

    diag, tri, _ = _problem_masks()
    mask_bd = _block_ones(SCAN_W)
    mask_bf = mask_bd.astype(BF16)

    r_row = jnp.sum(jnp.where(diag, r, 0.0), axis=1, keepdims=True)
    w = jnp.exp(jnp.where(tri, r_row - mx, -jnp.inf))
    sc = _bdot_nt(_bf(q), _bd3(_bf(k), mask_bf)) * w
    kw = kw_fac * k
    kw_sum = jnp.sum(kw, axis=1, keepdims=True)

    n_prob = N_PAIR * N_DIR * N_CHUNK
    c_at, n_at = [None] * n_prob, [None] * n_prob
    c_fin, n_fin = {}, {}
    for m in members:
        for d in range(N_DIR):
            c_bd, n_row = c_s[m, d], n_s[m, d]
            for c in _scan_order(d):
                p = (m * N_DIR + d) * N_CHUNK + c
                c_at[p], n_at[p] = c_bd, n_row
                dec_row = dec[p, 0:1, :]
                c_bd = c_bd * dec_row + jnp.where(mask_bd, _dot(kw[p].T, v[p]), 0.0)
                n_row = n_row * dec_row + kw_sum[p]
            c_fin[m, d], n_fin[m, d] = c_bd, n_row
    c_all = jnp.concatenate([_bf(x)[None] for x in c_at], axis=0)
    n_all = jnp.concatenate([x[None] for x in n_at], axis=0)

    num = _bdot(_bf(sc), _bd3(_bf(v), mask_bf)) + a_int * _bdot(_bf(q), c_all)
    n_rows = n_prob * CHUNK
    flat = lambda a: a.reshape(n_rows, SCAN_W)
    sums = _dot_exact_rhs(jnp.concatenate([flat(sc), flat(q * n_all)], axis=0), mask_bf)
    den = sums[0:n_rows].reshape(sc.shape) + a_int * sums[n_rows:].reshape(sc.shape)
    h = num / jnp.maximum(jnp.abs(den), jnp.exp(neg_mt))
    for m in members:
        for d, out_ref in enumerate((hf_ref, hb_ref)):
            p0 = (m * N_DIR + d) * N_CHUNK
            out_ref[m] = h[p0:p0 + N_CHUNK].reshape(SEG, SCAN_W)

    m_s[...] = m_fin
    for m in members:
        for d in range(N_DIR):
            c_s[m, d] = c_fin[m, d]
            n_s[m, d] = n_fin[m, d]

    @pl.when(pairs.last(s))
    def _():
        for m in members:
            m_out[0, m] = m_s[m * N_GATE_ROWS:(m + 1) * N_GATE_ROWS]
            for d in range(N_DIR):
                c_out[0, m, d] = _from_block_diag(c_fin[m, d])
                n_out[0, m, d] = n_fin[m, d]


def _mlstm(layer, pairs, zm, gates, fb_cols, state_c, n0, m0):
    row_tail, col_tail = (N_DIR, 1, SCAN_W), (N_GATE_ROWS, 1)
    c_spec, c_shape = pairs.state_out(_STATE_MAT)
    n_spec, n_shape = pairs.state_out(row_tail)
    m_spec, m_shape = pairs.state_out(col_tail)
    expand2, expand4 = _expander(2), _expander(4)
    h_shape = jax.ShapeDtypeStruct(zm.shape[:2] + (SCAN_W,), F32)
    return pl.pallas_call(
        functools.partial(_mlstm_kernel, pairs=pairs),
        grid=(pairs.n_step,),
        in_specs=[
            pairs.block(False, D_MODEL), pairs.block(True, D_MODEL),
            pairs.block(False, GATE_W), pairs.block(True, GATE_W),
            pl.BlockSpec((1, N_GATE_ROWS, 1), lambda s: (layer, 0, 0)),
            pl.BlockSpec(expand2.shape, lambda s: (0, 0)),
            pl.BlockSpec(expand4.shape, lambda s: (0, 0)),
            pairs.state_in(layer, _STATE_MAT), pairs.state_in(layer, row_tail), pairs.state_in(layer, col_tail),
        ],
        out_specs=[pairs.block(False, SCAN_W), pairs.block(True, SCAN_W), c_spec, n_spec, m_spec],
        out_shape=[h_shape, h_shape, c_shape, n_shape, m_shape],
        scratch_shapes=[pltpu.VMEM((N_PAIR, N_DIR, SCAN_W, SCAN_W), F32),
                        pltpu.VMEM((N_PAIR, N_DIR, 1, SCAN_W), F32),
                        pltpu.VMEM((ALL_ROWS, 1), F32)],
        compiler_params=pltpu.CompilerParams(
            dimension_semantics=("arbitrary",), vmem_limit_bytes=VMEM_LIMIT),
        name="mlstm",
    )(zm, zm, gates, gates, fb_cols, expand2, expand4, state_c, n0, m0)


def _short_conv(prev_ref, cur_ref, next_ref, w_ref, has_prev, has_next):
    w3 = 3 * SCAN_W
    prev = jnp.where(has_prev, prev_ref[:, 0:w3], 0.0)
    nxt = jnp.where(has_next, next_ref[:, 0:w3], 0.0)
    xp = jnp.concatenate([prev, cur_ref[:, 0:w3], nxt], axis=0)
    rows = xp.shape[0]
    acc = None
    for i in range(CONV_W):
        shift = (CONV_W // 2 - i) % rows
        y = xp if shift == 0 else pltpu.roll(xp, shift, 0)
        t = y[HALO:HALO + SEG] * w_ref[0, i:i + 1, :]
        acc = t if acc is None else acc + t
    return acc


def _neumann_inverse(n_all, mask_bf):
    eye = (_iota(n_all.shape[1:], 0) == _in_head(_iota(n_all.shape[1:], 1))).astype(F32)
    p = -n_all
    t = eye + p
    levels = 6
    for lvl in range(levels):
        first, last = lvl == 0, lvl == levels - 1
        p_hi, p_lo = _split(p, 2)
        w_hi, w_lo = _bd3(p_hi, mask_bf), _bd3(p_lo, mask_bf)
        lhs_hi, lhs_lo = [], []
        if not first:
            t_hi, t_lo = _split(t, 2)
            lhs_hi += [t_hi, t_lo]
            lhs_lo += [t_hi]
        if not last:
            lhs_hi += [p_hi, p_lo]
            lhs_lo += [p_hi]
        a = _bdot(jnp.concatenate(lhs_hi, axis=1), w_hi)
        b = _bdot(jnp.concatenate(lhs_lo, axis=1) if len(lhs_lo) > 1 else lhs_lo[0], w_lo)
        ra, rb = 0, 0
        if not first:
            t = t + (a[:, 0:CHUNK] + a[:, CHUNK:2 * CHUNK] + b[:, 0:CHUNK])
            ra, rb = 2 * CHUNK, CHUNK
        if not last:
            p = a[:, ra:ra + CHUNK] + a[:, ra + CHUNK:ra + 2 * CHUNK] + b[:, rb:rb + CHUNK]
    return t


def _delta_qkv(prev_ref, cur_ref, next_ref, cw_ref, has_prev, has_next):
    qkv = _silu(_short_conv(prev_ref, cur_ref, next_ref, cw_ref, has_prev, has_next))

    def l2(a):
        return a * lax.rsqrt(_head_sum(a * a) + EPS)

    q = l2(qkv[:, 0:SCAN_W]) * (1.0 / math.sqrt(HEAD_DIM))
    k = l2(qkv[:, SCAN_W:2 * SCAN_W])
    return jnp.concatenate([q, k, qkv[:, 2 * SCAN_W:3 * SCAN_W]], axis=1)


def _delta_stage_qkv(s, pairs, blocks, cw_ref, qkv_s, seen_s):
    members = range(N_PAIR)
    lat = jnp.logical_not(pairs.is_ctx(s))
    j = pairs.lat(s)[1]
    mirror = pairs.per_seq - 1 - j
    half = pairs.per_seq // 2

    @pl.when(pairs.is_ctx(s))
    def _():
        for m in members:
            pf, cf, nf = (r.at[m] for r in blocks[0])
            x = _delta_qkv(pf, cf, nf, cw_ref, False, False)
            qkv_s[m, 0] = x
            qkv_s[m, 1] = x

    @pl.when(jnp.logical_and(lat, j < half))
    def _():
        for m in members:
            (pf, cf, nf), (pb, cb, nb) = ((r.at[m] for r in refs) for refs in blocks)
            xf = _delta_qkv(pf, cf, nf, cw_ref, j > 0, True)
            xb = _delta_qkv(pb, cb, nb, cw_ref, True, j > 0)
            qkv_s[m, 0] = xf
            qkv_s[m, 1] = xb
            seen_s[m, pl.ds(j, 1)] = xf[None]
            seen_s[m, pl.ds(mirror, 1)] = xb[None]

    @pl.when(jnp.logical_and(lat, j >= half))
    def _():
        for m in members:
            qkv_s[m, 0] = seen_s[m, pl.ds(j, 1)][0]
            qkv_s[m, 1] = seen_s[m, pl.ds(mirror, 1)][0]


def _delta_gates(gf_ref, gb_ref, alog_col, dtb_col, expand_ref):
    is_b, pick = _gate_rows(gf_ref, gb_ref)
    g = -jnp.exp(alog_col) * _softplus(pick(2 * N_GATE_ROWS) + dtb_col)
    beta = _sigmoid(pick(3 * N_GATE_ROWS))
    gc = _chunk_sums(g, is_b)
    totals = [jnp.sum(g[:, c * CHUNK:(c + 1) * CHUNK], axis=1, keepdims=True) for c in range(N_CHUNK)]
    gl = _on_lanes(totals, False, is_b)
    return _expand([gc, jnp.exp(gc), jnp.exp(gl - gc), jnp.exp(gl), beta], expand_ref)


def _delta_kernel(pf_ref, cf_ref, nf_ref, pb_ref, cb_ref, nb_ref, gf_ref, gb_ref, cw_ref, al_ref, dt_ref,
                  expand_ref, s0_ref, of_ref, ob_ref, s_out, s_s, qkv_s, seen_s, *, pairs):
    s = pl.program_id(0)
    members = range(N_PAIR)
    lat = jnp.logical_not(pairs.is_ctx(s))

    @pl.when(pairs.first(s))
    def _():
        for m in members:
            for d in range(N_DIR):
                s_s[m, d] = jnp.where(lat, _to_block_diag(s0_ref[m, 0, d]), 0.0)

    _delta_stage_qkv(s, pairs, ((pf_ref, cf_ref, nf_ref), (pb_ref, cb_ref, nb_ref)), cw_ref, qkv_s, seen_s)
    both = lambda col: jnp.concatenate([col] * N_PAIR, axis=0)
    gc, eg, k_fac, g_last, beta = _delta_gates(gf_ref, gb_ref, both(al_ref[0]), both(dt_ref[0]), expand_ref)
    q, k, v = (jnp.concatenate([_chunks(qkv_s[m, d, :, i * SCAN_W:(i + 1) * SCAN_W])
                                for m in members for d in range(N_DIR)], axis=0) for i in range(3))
    kb = k * beta
    k_dec = k * k_fac

    mask_bd = _block_ones(SCAN_W)
    mask_bf = mask_bd.astype(BF16)
    diag, tri, strict = _problem_masks()
    neg_diag = jnp.sum(jnp.where(diag, -gc, 0.0), axis=1, keepdims=True)
    decay = jnp.exp(jnp.where(tri, gc + neg_diag, -jnp.inf))
    k_bd = _bd3(_bf(k), mask_bf)
    kq = _bdot_nt(_bf(jnp.concatenate([kb, q], axis=1)), k_bd)
    qk = _bf(kq[:, CHUNK:] * decay)
    t_all = _bf(_neumann_inverse(jnp.where(strict, kq[:, 0:CHUNK] * decay, 0.0), mask_bf))
    u = _bdot(t_all, _bd3(_bf(v * beta), mask_bf))
    w = _bdot(t_all, _bd3(_bf(kb * eg), mask_bf))
    wq = _bf(jnp.concatenate([w, q * eg], axis=1))

    outs = (of_ref, ob_ref)
    chains = [(m, d) for m in members for d in range(N_DIR)]
    k_dec_t = [k_dec[p].T for p in range(k_dec.shape[0])]
    state = jnp.concatenate([s_s[m, d][None] for m, d in chains], axis=0)
    for i in range(N_CHUNK):
        probs = [(m * N_DIR + d) * N_CHUNK + (N_CHUNK - 1 - i if d else i) for m, d in chains]
        pick = lambda x: jnp.concatenate([x[p][None] for p in probs], axis=0)
        ws = _bdot(pick(wq), _bf(state))
        v_new = pick(u) - ws[:, 0:CHUNK]
        out = ws[:, CHUNK:] + _bdot(pick(qk), _bd3(_bf(v_new), mask_bf))
        for n, (m, d) in enumerate(chains):
            c = N_CHUNK - 1 - i if d else i
            outs[d][m, c * CHUNK:(c + 1) * CHUNK, :] = out[n]
        kd_t = _bf(jnp.concatenate([k_dec_t[p][None] for p in probs], axis=0))
        state = state * pick(g_last)[:, 0:1, :] + jnp.where(mask_bd, _bdot(kd_t, _bf(v_new)), 0.0)
    for n, (m, d) in enumerate(chains):
        s_s[m, d] = state[n]

    @pl.when(pairs.last(s))
    def _():
        for n, (m, d) in enumerate(chains):
            s_out[0, m, d] = _from_block_diag(state[n])


def _delta(layer, pairs, zg, gates, conv_w, alog_cols, dtb_cols, state_s):
    per = SEG // HALO
    last_halo = pairs.rows // HALO - 1

    def prev(rev):
        return pl.BlockSpec((N_PAIR, HALO, D_MODEL),
                            lambda s: (pairs.macro(s), jnp.maximum(pairs.seg(s, rev) * per - 1, 0), 0))

    def nxt(rev):
        return pl.BlockSpec((N_PAIR, HALO, D_MODEL),
                            lambda s: (pairs.macro(s), jnp.minimum((pairs.seg(s, rev) + 1) * per, last_halo), 0))

    gate_cols = pl.BlockSpec((1, N_GATE_ROWS, 1), lambda s: (layer, 0, 0))
    s_spec, s_shape = pairs.state_out(_STATE_MAT)
    expand5 = _expander(5)
    o_shape = jax.ShapeDtypeStruct(zg.shape[:2] + (SCAN_W,), F32)
    return pl.pallas_call(
        functools.partial(_delta_kernel, pairs=pairs),
        grid=(pairs.n_step,),
        in_specs=[
            prev(False), pairs.block(False, D_MODEL), nxt(False),
            prev(True), pairs.block(True, D_MODEL), nxt(True),
            pairs.block(False, GATE_W), pairs.block(True, GATE_W),
            pl.BlockSpec((1, CONV_W, 3 * SCAN_W), lambda s: (layer, 0, 0)),
            gate_cols, gate_cols, pl.BlockSpec(expand5.shape, lambda s: (0, 0)),
            pairs.state_in(layer, _STATE_MAT),
        ],
        out_specs=[pairs.block(False, SCAN_W), pairs.block(True, SCAN_W), s_spec],
        out_shape=[o_shape, o_shape, s_shape],
        scratch_shapes=[pltpu.VMEM((N_PAIR, N_DIR, SCAN_W, SCAN_W), F32),
                        pltpu.VMEM((N_PAIR, N_DIR, SEG, 3 * SCAN_W), F32),
                        pltpu.VMEM((N_PAIR, pairs.per_seq, SEG, 3 * SCAN_W), F32)],
        compiler_params=pltpu.CompilerParams(
            dimension_semantics=("arbitrary",), vmem_limit_bytes=VMEM_LIMIT),
        name="delta",
    )(zg, zg, zg, zg, zg, zg, gates, gates, conv_w, alog_cols, dtb_cols, expand5, state_s)


def _attend(q_ref, kv_blocks, o_ref):
    qt = q_ref[...].T
    zeros = jnp.zeros((HEAD_DIM, A_GROUP * SEG), BF16)
    ones = lambda n: jnp.ones((16, n), F32)
    kv = [(_bf(k), _bf(jnp.concatenate([v.T, ones(v.shape[0])], axis=0))) for k, v in kv_blocks]
    w_groups = []
    for g in range(A_KV_HEADS):
        heads = range(g * A_GROUP, (g + 1) * A_GROUP)
        slab = _bf(jnp.concatenate([qt[h * HEAD_DIM:(h + 1) * HEAD_DIM, :] for h in heads], axis=1))
        w_groups.append(jnp.concatenate([slab if i == g else zeros for i in range(A_KV_HEADS)], axis=0))
    w_all = jnp.concatenate(w_groups, axis=1)
    group_out = [None] * A_KV_HEADS
    for q0 in range(0, A_HEADS * SEG, ATT_LANES):
        w_c = w_all[:, q0:q0 + ATT_LANES]
        m = acc = None
        nxt = jnp.dot(kv[0][0], w_c, preferred_element_type=F32)
        for i, (_, v_t) in enumerate(kv):
            sc = nxt
            if i + 1 < len(kv):
                nxt = jnp.dot(kv[i + 1][0], w_c, preferred_element_type=F32)
            m_blk = jnp.max(sc, axis=0, keepdims=True)
            m_new = m_blk if m is None else jnp.maximum(m, m_blk)
            pv = jnp.dot(v_t, _bf(jnp.exp(sc - m_new)), preferred_element_type=F32)
            acc = pv if m is None else acc * jnp.exp(m - m_new) + pv
            m = m_new
        out = acc[0:KV_WIDTH] / acc[KV_WIDTH:KV_WIDTH + 1]
        for g in range(A_KV_HEADS):
            g0, g1 = g * A_GROUP * SEG, (g + 1) * A_GROUP * SEG
            lo, hi = max(g0, q0), min(g1, q0 + ATT_LANES)
            if lo < hi:
                piece = out[g * HEAD_DIM:(g + 1) * HEAD_DIM, lo - q0:hi - q0]
                group_out[g] = piece if group_out[g] is None else jnp.concatenate([group_out[g], piece], axis=1)
    for col in range(A_WIDTH // KV_WIDTH):
        g, h0 = (2 * col) // A_GROUP, (2 * col) % A_GROUP
        pair = jnp.concatenate([group_out[g][:, h0 * SEG:(h0 + 1) * SEG],
                                group_out[g][:, (h0 + 1) * SEG:(h0 + 2) * SEG]], axis=0)
        o_ref[:, col * KV_WIDTH:(col + 1) * KV_WIDTH] = pair.T


def _attn_kernel(q_ref, kc_ref, vc_ref, kl_ref, vl_ref, ck_ref, cv_ref, o_ref, *, segs):
    s = pl.program_id(0)

    @pl.when(segs.is_ctx(s))
    def _():
        _attend(q_ref, [(kc_ref[...], vc_ref[...])], o_ref)

    @pl.when(jnp.logical_not(segs.is_ctx(s)))
    def _():
        blocks = []
        past, t_lat = ck_ref.shape[2], kl_ref.shape[0]
        for lo in range(0, past, KV_BLOCK):
            hi = min(lo + KV_BLOCK, past)
            blocks.append((ck_ref[0, 0, lo:hi, :], cv_ref[0, 0, lo:hi, :]))
        for lo in range(0, t_lat, KV_BLOCK):
            hi = min(lo + KV_BLOCK, t_lat)
            blocks.append((kl_ref[lo:hi, :], vl_ref[lo:hi, :]))
        _attend(q_ref, blocks, o_ref)


def _attention(layer, segs, aq, ak, av, cache_k, cache_v):
    n_tok = aq.shape[0]
    t_lat = segs.per_seq * SEG
    past = cache_k.shape[2]
    assert t_lat % KV_BLOCK == 0
    lat_blocks_before = segs.n_ctx_seg * SEG // t_lat
    own = lambda w: pl.BlockSpec((SEG, w), lambda s: (s, 0))
    lat_kv = pl.BlockSpec((t_lat, KV_WIDTH), lambda s: (lat_blocks_before + segs.lat_seq(s), 0))
    cache = pl.BlockSpec((1, 1, past, KV_WIDTH), lambda s: (segs.lat_seq(s), layer, 0, 0))
    return pl.pallas_call(
        functools.partial(_attn_kernel, segs=segs),
        grid=(segs.n_seg,),
        in_specs=[own(A_WIDTH), own(KV_WIDTH), own(KV_WIDTH), lat_kv, lat_kv, cache, cache],
        out_specs=own(A_WIDTH),
        out_shape=jax.ShapeDtypeStruct((n_tok, A_WIDTH), F32),
        compiler_params=pltpu.CompilerParams(
            dimension_semantics=("arbitrary",), vmem_limit_bytes=VMEM_LIMIT),
        name="attention",
    )(aq, ak, av, ak, av, cache_k, cache_v)


def _permute_w_in(w_in):
    m, g, a, kv = SCAN_W, SCAN_W, A_WIDTH, KV_WIDTH
    sizes = (m, m, m, m, 8, 8, 3 * g, g, 8, 8, a, kv, kv)
    offs = [0]
    for sz in sizes:
        offs.append(offs[-1] + sz)
    piece = lambda i: _bf(w_in[..., offs[i]:offs[i + 1]])
    pad = jnp.zeros(w_in.shape[:-1] + (GATE_W - 32,), BF16)
    order = [0, 1, 2, 3, 6, 7, 10, 11, 12, 4, 5, 8, 9]
    return jnp.concatenate([piece(i) for i in order] + [pad], axis=-1)


def _rope_tables(t_lat):
    rows = t_lat // GRID_W
    row = np.repeat(np.arange(rows, dtype=np.float64), GRID_W)
    col = np.tile(np.arange(GRID_W, dtype=np.float64), rows)
    n_freq = HEAD_DIM // 4
    inv = ROPE_BASE ** (-np.arange(n_freq, dtype=np.float64) / n_freq)
    ang = np.stack([row[:, None] * inv, col[:, None] * inv], axis=1)
    cos, sin = np.cos(ang), np.sin(ang)
    cos_h = np.concatenate([cos, cos], axis=-1).reshape(t_lat, HEAD_DIM)
    sin_h = np.concatenate([-sin, sin], axis=-1).reshape(t_lat, HEAD_DIM)
    wide = lambda a: np.tile(a, (1, KV_WIDTH // HEAD_DIM))
    cos_t = np.concatenate([wide(cos_h), np.ones((TM, KV_WIDTH))], axis=0)
    sin_t = np.concatenate([wide(sin_h), np.zeros((TM, KV_WIDTH))], axis=0)
    return jnp.asarray(cos_t, F32), jnp.asarray(sin_t, F32)


def kernel(x_prompt, x_sample, c, cache_k, cache_v, state_mlstm_C, state_mlstm_n, state_mlstm_m,
           state_delta_S, c_ctx, ada_w, ada_b, norm_g, ffn_w_in, ffn_w_out, w_in, w_out, mlstm_f_bias,
           mlstm_norm, delta_conv, delta_a_log, delta_dt_bias, delta_norm, attn_q_norm, attn_k_norm,
           final_norm):
    batch, t_ctx, d_model = x_prompt.shape
    n_lat_seq, t_lat, _ = x_sample.shape
    depth = ada_w.shape[0]
    assert d_model == D_MODEL and norm_g.shape[1] == 3 and ffn_w_in.shape[-1] == 2 * D_FF
    assert 1 + n_lat_seq <= MOD_ROWS
    n_ctx = batch * t_ctx
    n_tok = n_ctx + n_lat_seq * t_lat
    assert n_ctx % t_lat == 0
    segs = _Segs(n_ctx, t_ctx, n_lat_seq, t_lat)
    pairs = _Pairs(n_ctx, t_ctx, n_lat_seq, t_lat)
    tiles = _Tiles(n_ctx, t_lat)

    cond = jnp.concatenate([c_ctx[None, :], c, jnp.zeros((MOD_ROWS - 1 - n_lat_seq, D_MODEL), F32)], axis=0)
    mod = _modulation(cond, ada_w, ada_b)

    wfi = _bf(ffn_w_in).reshape(depth * 2, D_MODEL, 2 * D_FF)
    wfo = _bf(ffn_w_out).reshape(depth * 2, D_FF, D_MODEL)
    wz = _permute_w_in(w_in)
    wo = _bf(w_out)
    qn = jnp.tile(attn_q_norm, (1, A_HEADS)).reshape(depth, 1, A_WIDTH)
    kn = jnp.tile(attn_k_norm, (1, A_KV_HEADS)).reshape(depth, 1, KV_WIDTH)
    gn = jnp.tile(delta_norm, (1, N_HEADS)).reshape(depth, 1, SCAN_W)
    mn = mlstm_norm.reshape(depth, 1, SCAN_W)
    fin = final_norm.reshape(1, D_MODEL)
    fb_cols = mlstm_f_bias.reshape(depth, N_GATE_ROWS, 1)
    alog_cols = delta_a_log.reshape(depth, N_GATE_ROWS, 1)
    dtb_cols = delta_dt_bias.reshape(depth, N_GATE_ROWS, 1)
    cos_t, sin_t = _rope_tables(t_lat)
    ck = cache_k.reshape(cache_k.shape[:3] + (KV_WIDTH,))
    cv = cache_v.reshape(cache_v.shape[:3] + (KV_WIDTH,))
    n0 = state_mlstm_n.reshape(n_lat_seq, depth, N_DIR, 1, SCAN_W)
    m0 = state_mlstm_m.reshape(n_lat_seq, depth, N_GATE_ROWS, 1)

    x_ctx, x_lat, lat_tile0 = x_prompt.reshape(n_ctx, D_MODEL), x_sample.reshape(n_tok - n_ctx, D_MODEL), 0
    ks, vs, cs, ns, ms, ss = [], [], [], [], [], []
    for l in range(depth):
        x1, zm, zg, aq, ak, av, gates = _dense1(l, tiles, n_tok, x_ctx, x_lat, lat_tile0, mod, norm_g,
                                                wfi, wfo, wz, qn, kn, cos_t, sin_t)
        zm_v, zg_v, gates_v = pairs.view(zm), pairs.view(zg), pairs.view(gates)
        hf, hb, c_new, n_new, m_new = _mlstm(l, pairs, zm_v, gates_v, fb_cols, state_mlstm_C, n0, m0)
        o_f, o_b, s_new = _delta(l, pairs, zg_v, gates_v, delta_conv, alog_cols, dtb_cols, state_delta_S)
        hf, hb, o_f, o_b = (a.reshape(n_tok, SCAN_W) for a in (hf, hb, o_f, o_b))
        a_out = _attention(l, segs, aq, ak, av, ck, cv)
        x = _dense2(l, tiles, x1, mod, norm_g, hf, hb, zm, mn, o_f, o_b, zg, gn, a_out, wo, wfi, wfo, fin,
                    final=(l == depth - 1))
        if l < depth - 1:
            x_ctx, x_lat, lat_tile0 = x, x, tiles.n_ctx

        ks.append(ak[:n_ctx].reshape(batch, t_ctx, A_KV_HEADS, HEAD_DIM))
        vs.append(av[:n_ctx].reshape(batch, t_ctx, A_KV_HEADS, HEAD_DIM))
        cs.append(pairs.ctx_states(c_new))
        ns.append(pairs.ctx_states(n_new).reshape(batch, N_DIR, N_HEADS, HEAD_DIM))
        ms.append(pairs.ctx_states(m_new).reshape(batch, N_DIR, N_HEADS))
        ss.append(pairs.ctx_states(s_new))

    y_prompt = x[0].reshape(batch, t_ctx, D_MODEL)
    y_sample = x[1].reshape(n_lat_seq, t_lat, D_MODEL)
    stack = lambda xs: jnp.stack(xs, axis=1)
    return (y_prompt, y_sample, stack(ks), stack(vs), stack(cs), stack(ns), stack(ms), stack(ss))
```

```python
import functools
import math

import numpy as np

import jax
import jax.numpy as jnp
from jax import lax
from jax.experimental import pallas as pl
from jax.experimental.pallas import tpu as pltpu

F32 = jnp.float32
BF16 = jnp.bfloat16

D_MODEL = 1024
HEAD_DIM = 64
N_HEADS = 4
SCAN_W = N_HEADS * HEAD_DIM
A_HEADS = 8
A_KV_HEADS = 2
A_GROUP = A_HEADS // A_KV_HEADS
A_WIDTH = A_HEADS * HEAD_DIM
KV_WIDTH = A_KV_HEADS * HEAD_DIM
N_DIR = 2
CHUNK = 64
CONV_W = 5
D_FF = 2816
FF_CHUNK = 256
GRID_W = 64
ROPE_BASE = 10000.0
EPS = 1e-6
N_MOD = 9
MOD_ROWS = 8
GATE_W = 128
Z_WIDTH = 2 * D_MODEL + A_WIDTH + 2 * KV_WIDTH + GATE_W

SEG = 256
N_CHUNK = SEG // CHUNK
TM = 512
N_PAIR = 2
HALO = 8
KV_BLOCK = 512
BF16_ROWS = 16
VMEM_LIMIT = 56 * 1024 * 1024


def _bf(x):
    return x.astype(BF16)


def _dot(a, b):
    return jnp.dot(_bf(a), _bf(b), preferred_element_type=F32)


def _bdot(a, b):
    return lax.dot_general(a, b, (((2,), (1,)), ((0,), (0,))), preferred_element_type=F32)


def _bdot_nt(a, b):
    return lax.dot_general(a, b, (((2,), (2,)), ((0,), (0,))), preferred_element_type=F32)


def _split(a, terms):
    parts = []
    rest = a
    for _ in range(terms):
        p = _bf(rest)
        parts.append(p)
        rest = rest - p.astype(F32)
    return parts


def _dot_exact_rhs(a, b01, terms=2):
    parts = _split(a, terms)
    if a.shape[-1] % 128 == 0:
        return jnp.dot(jnp.concatenate(parts, axis=-1), jnp.concatenate([b01] * terms, axis=0),
                       preferred_element_type=F32)
    out = None
    for p in parts:
        d = jnp.dot(p, b01, preferred_element_type=F32)
        out = d if out is None else out + d
    return out


def _iota(shape, dim):
    return lax.broadcasted_iota(jnp.int32, shape, dim)


def _head_of(idx):
    return jnp.right_shift(idx, HEAD_DIM.bit_length() - 1)


def _in_head(idx):
    return jnp.bitwise_and(idx, HEAD_DIM - 1)


def _block_ones(n):
    return (_head_of(_iota((n, n), 0)) == _head_of(_iota((n, n), 1)))


def _head_sum(x):
    n = x.shape[-1]
    return _dot_exact_rhs(x, _block_ones(n).astype(BF16), terms=1)


def _head_rms(x, g_row):
    ms = _head_sum(x * x) * (1.0 / HEAD_DIM)
    return x * lax.rsqrt(ms + EPS) * g_row


def _rms_mod(x, g_row, scale_row, shift_row):
    y = x * lax.rsqrt(jnp.mean(x * x, axis=-1, keepdims=True) + EPS) * g_row
    return y * (1.0 + scale_row) + shift_row


def _softplus(x):
    return jnp.maximum(x, 0.0) + jnp.log1p(jnp.exp(-jnp.abs(x)))


def _log_sigmoid(x):
    return -_softplus(-x)


def _sigmoid(x):
    return jax.nn.sigmoid(x)


def _silu(x):
    return x * jax.nn.sigmoid(x)


def _bd3(x, mask_bf):
    return jnp.concatenate([x] * N_HEADS, axis=1) * mask_bf


def _to_block_diag(x4):
    spread = (_iota((HEAD_DIM, SCAN_W), 0) == _in_head(_iota((HEAD_DIM, SCAN_W), 1))).astype(BF16)
    rows = x4.reshape(SCAN_W, HEAD_DIM)
    return jnp.where(_block_ones(SCAN_W), _dot_exact_rhs(rows, spread, terms=3), 0.0)


def _from_block_diag(x):
    fold = (_in_head(_iota((SCAN_W, HEAD_DIM), 0)) == _iota((SCAN_W, HEAD_DIM), 1)).astype(BF16)
    return _dot_exact_rhs(x, fold, terms=3).reshape(N_HEADS, HEAD_DIM, HEAD_DIM)


def _chunks(a):
    return a.reshape(N_CHUNK, CHUNK, SCAN_W)


def _problem_masks():
    shape = (N_PAIR * N_DIR * N_CHUNK, CHUNK, SCAN_W)
    t_idx, j_idx = _iota(shape, 1), _in_head(_iota(shape, 2))
    is_rev = jnp.bitwise_and(_iota(shape, 0), N_DIR * N_CHUNK - 1) >= N_CHUNK
    ahead = jnp.where(is_rev, j_idx - t_idx, t_idx - j_idx)
    return t_idx == j_idx, ahead >= 0, ahead > 0


def _scan_order(d):
    return range(N_CHUNK - 1, -1, -1) if d else range(N_CHUNK)


def _mod_kernel(cond_ref, w_ref, b_ref, o_ref):
    a = _silu(cond_ref[...])
    o_ref[0, 0] = _dot(a, w_ref[0]) + b_ref[0]


def _modulation(cond, ada_w, ada_b):
    depth = ada_w.shape[0]
    return pl.pallas_call(
        _mod_kernel,
        grid=(depth, N_MOD),
        in_specs=[
            pl.BlockSpec((MOD_ROWS, D_MODEL), lambda l, j: (0, 0)),
            pl.BlockSpec((1, D_MODEL, D_MODEL), lambda l, j: (l, 0, j)),
            pl.BlockSpec((1, 1, D_MODEL), lambda l, j: (l, 0, j)),
        ],
        out_specs=pl.BlockSpec((1, 1, MOD_ROWS, D_MODEL), lambda l, j: (l, j, 0, 0)),
        out_shape=jax.ShapeDtypeStruct((depth, N_MOD, MOD_ROWS, D_MODEL), F32),
        compiler_params=pltpu.CompilerParams(
            dimension_semantics=("arbitrary", "arbitrary"), vmem_limit_bytes=VMEM_LIMIT),
        name="modulation",
    )(cond, ada_w, ada_b.reshape(depth, 1, N_MOD * D_MODEL))


def _ffn(h, w_in_ref, w_out_ref):
    acc = None
    for c in range(D_FF // FF_CHUNK):
        lo = c * FF_CHUNK
        g = jnp.dot(h, w_in_ref[0, :, lo:lo + FF_CHUNK], preferred_element_type=F32)
        u = jnp.dot(h, w_in_ref[0, :, D_FF + lo:D_FF + lo + FF_CHUNK], preferred_element_type=F32)
        a = _bf(_silu(g) * u)
        d = jnp.dot(a, w_out_ref[0, lo:lo + FF_CHUNK, :], preferred_element_type=F32)
        acc = d if acc is None else acc + d
    return acc


def _rope(x, cos, sin_signed):
    w = x.shape[-1]
    reps = w // cos.shape[-1]
    if reps > 1:
        cos = jnp.concatenate([cos] * reps, axis=-1)
        sin_signed = jnp.concatenate([sin_signed] * reps, axis=-1)
    n_freq = HEAD_DIM // 4
    first = jnp.bitwise_and(_iota(x.shape, 1), 2 * n_freq - 1) < n_freq
    partner = jnp.where(first, pltpu.roll(x, w - n_freq, 1), pltpu.roll(x, n_freq, 1))
    return x * cos + partner * sin_signed


class _Tiles:
    def __init__(self, n_ctx, t_lat):
        assert n_ctx % TM == 0 and t_lat % TM == 0
        self.n_ctx = n_ctx // TM
        self.per_seq = t_lat // TM

    def is_ctx(self, i):
        return i < self.n_ctx

    def lat(self, i):
        return jnp.maximum(i - self.n_ctx, 0)

    def mod_row(self, i):
        return jnp.where(self.is_ctx(i), 0, 1 + self.lat(i) // self.per_seq)


def _mod_rows(mod_ref, row):
    return lambda m: mod_ref[0, m, pl.ds(row, 1), :]


def _dense1_kernel(xa_ref, xb_ref, mod_ref, ng_ref, wfi_ref, wfo_ref, wz_ref, qn_ref, kn_ref, cos_ref, sin_ref,
                   x1_ref, zm_ref, zg_ref, aq_ref, ak_ref, av_ref, gate_ref, *, tiles):
    i = pl.program_id(0)
    mod = _mod_rows(mod_ref, tiles.mod_row(i))
    x = jnp.where(tiles.is_ctx(i), xa_ref[...], xb_ref[...])
    h = _bf(_rms_mod(x, ng_ref[0, 0:1], mod(0), mod(1)))
    x1 = x + 0.5 * mod(2) * _ffn(h, wfi_ref, wfo_ref)
    x1_ref[...] = x1
    h2 = _bf(_rms_mod(x1, ng_ref[0, 1:2], mod(3), mod(4)))
    z = jnp.dot(h2, wz_ref[0], preferred_element_type=F32)

    def proj(lo, width):
        return z[:, lo:lo + width]

    zm_ref[...] = proj(0, D_MODEL)
    zg_ref[...] = proj(D_MODEL, D_MODEL)
    cos = cos_ref[...]
    sin = sin_ref[...]
    off = 2 * D_MODEL
    q = _head_rms(proj(off, A_WIDTH), qn_ref[0])
    aq_ref[...] = _rope(q, cos, sin) * (1.0 / math.sqrt(HEAD_DIM))
    k = _head_rms(proj(off + A_WIDTH, KV_WIDTH), kn_ref[0])
    ak_ref[...] = _rope(k, cos, sin)
    av_ref[...] = proj(off + A_WIDTH + KV_WIDTH, KV_WIDTH)
    gate_ref[...] = proj(off + A_WIDTH + 2 * KV_WIDTH, GATE_W)


def _resident(shape, index_map):
    return pl.BlockSpec(shape, index_map, pipeline_mode=pl.Buffered(1))


def _dense1(layer, tiles, n_tok, x_ctx, x_lat, lat_tile0, mod, norm_g, wfi, wfo, wz, qn, kn, cos_t, sin_t):
    row = lambda w: pl.BlockSpec((TM, w), lambda i: (i, 0))
    rope_tile = lambda i: jnp.where(tiles.is_ctx(i), tiles.per_seq, tiles.lat(i) % tiles.per_seq)
    rope = pl.BlockSpec((TM, KV_WIDTH), lambda i: (rope_tile(i), 0))
    out_w = (D_MODEL, D_MODEL, D_MODEL, A_WIDTH, KV_WIDTH, KV_WIDTH, GATE_W)
    return pl.pallas_call(
        functools.partial(_dense1_kernel, tiles=tiles),
        grid=(n_tok // TM,),
        in_specs=[
            pl.BlockSpec((TM, D_MODEL), lambda i: (jnp.minimum(i, tiles.n_ctx - 1), 0)),
            pl.BlockSpec((TM, D_MODEL), lambda i: (tiles.lat(i) + lat_tile0, 0)),
            pl.BlockSpec((1, N_MOD, MOD_ROWS, D_MODEL), lambda i: (layer, 0, 0, 0)),
            pl.BlockSpec((1, 3, D_MODEL), lambda i: (layer, 0, 0)),
            _resident((1, D_MODEL, 2 * D_FF), lambda i: (2 * layer, 0, 0)),
            _resident((1, D_FF, D_MODEL), lambda i: (2 * layer, 0, 0)),
            _resident((1, D_MODEL, Z_WIDTH), lambda i: (layer, 0, 0)),
            pl.BlockSpec((1, 1, A_WIDTH), lambda i: (layer, 0, 0)),
            pl.BlockSpec((1, 1, KV_WIDTH), lambda i: (layer, 0, 0)),
            rope, rope,
        ],
        out_specs=[row(w) for w in out_w],
        out_shape=[jax.ShapeDtypeStruct((n_tok, w), F32) for w in out_w],
        compiler_params=pltpu.CompilerParams(
            dimension_semantics=("arbitrary",), vmem_limit_bytes=VMEM_LIMIT),
        name="dense1",
    )(x_ctx, x_lat, mod, norm_g, wfi, wfo, wz, qn, kn, cos_t, sin_t)


def _dense2_kernel(x_ref, mod_ref, ng_ref, hf_ref, hb_ref, mo_ref, mn_ref, of_ref, ob_ref, gz_ref, gn_ref,
                   ao_ref, wo_ref, wfi_ref, wfo_ref, fin_ref, *out_refs, tiles, final):
    i = pl.program_id(0)
    mod = _mod_rows(mod_ref, tiles.mod_row(i))
    m_out = _head_rms(hf_ref[...] + hb_ref[...], mn_ref[0]) * _sigmoid(mo_ref[...])
    g_out = _head_rms(of_ref[...] + ob_ref[...], gn_ref[0]) * _silu(gz_ref[...])
    mix = jnp.dot(_bf(m_out), wo_ref[0, 0:SCAN_W, :], preferred_element_type=F32)
    mix += jnp.dot(_bf(g_out), wo_ref[0, SCAN_W:2 * SCAN_W, :], preferred_element_type=F32)
    mix += jnp.dot(_bf(ao_ref[...]), wo_ref[0, 2 * SCAN_W:, :], preferred_element_type=F32)
    x2 = x_ref[...] + mod(5) * mix
    h = _bf(_rms_mod(x2, ng_ref[0, 2:3], mod(6), mod(7)))
    x3 = x2 + 0.5 * mod(8) * _ffn(h, wfi_ref, wfo_ref)
    if not final:
        out_refs[0][...] = x3
        return
    y = x3 * lax.rsqrt(jnp.mean(x3 * x3, axis=-1, keepdims=True) + EPS) * fin_ref[...]
    y_ctx_ref, y_lat_ref = out_refs

    @pl.when(tiles.is_ctx(i))
    def _():
        y_ctx_ref[...] = y

    @pl.when(jnp.logical_not(tiles.is_ctx(i)))
    def _():
        y_lat_ref[...] = y


def _dense2(layer, tiles, x1, mod, norm_g, hf, hb, zm, m_norm, of, ob, zg, g_norm, a_out, wo, wfi, wfo, fin,
            final):
    n_tok = x1.shape[0]
    row = lambda w: pl.BlockSpec((TM, w), lambda i: (i, 0))
    last_quarter = pl.BlockSpec((TM, SCAN_W), lambda i: (i, 3))
    lane_row = lambda w: pl.BlockSpec((1, 1, w), lambda i: (layer, 0, 0))
    if final:
        n_ctx = tiles.n_ctx * TM
        out_specs = [pl.BlockSpec((TM, D_MODEL), lambda i: (jnp.minimum(i, tiles.n_ctx - 1), 0)),
                     pl.BlockSpec((TM, D_MODEL), lambda i: (tiles.lat(i), 0))]
        out_shape = [jax.ShapeDtypeStruct((n_ctx, D_MODEL), F32),
                     jax.ShapeDtypeStruct((n_tok - n_ctx, D_MODEL), F32)]
    else:
        out_specs, out_shape = row(D_MODEL), jax.ShapeDtypeStruct((n_tok, D_MODEL), F32)
    return pl.pallas_call(
        functools.partial(_dense2_kernel, tiles=tiles, final=final),
        grid=(n_tok // TM,),
        in_specs=[
            row(D_MODEL),
            pl.BlockSpec((1, N_MOD, MOD_ROWS, D_MODEL), lambda i: (layer, 0, 0, 0)),
            pl.BlockSpec((1, 3, D_MODEL), lambda i: (layer, 0, 0)),
            row(SCAN_W), row(SCAN_W), last_quarter, lane_row(SCAN_W),
            row(SCAN_W), row(SCAN_W), last_quarter, lane_row(SCAN_W),
            row(A_WIDTH),
            _resident((1, D_MODEL, D_MODEL), lambda i: (layer, 0, 0)),
            _resident((1, D_MODEL, 2 * D_FF), lambda i: (2 * layer + 1, 0, 0)),
            _resident((1, D_FF, D_MODEL), lambda i: (2 * layer + 1, 0, 0)),
            pl.BlockSpec((1, D_MODEL), lambda i: (0, 0)),
        ],
        out_specs=out_specs,
        out_shape=out_shape,
        compiler_params=pltpu.CompilerParams(
            dimension_semantics=("arbitrary",), vmem_limit_bytes=VMEM_LIMIT),
        name="dense2",
    )(x1, mod, norm_g, hf, hb, zm, m_norm, of, ob, zg, g_norm, a_out, wo, wfi, wfo, fin)


class _Segs:
    def __init__(self, n_ctx, t_ctx, n_lat_seq, t_lat):
        assert t_ctx == SEG and t_lat % SEG == 0
        self.n_ctx_seg = n_ctx // SEG
        self.per_seq = t_lat // SEG
        self.n_seg = self.n_ctx_seg + n_lat_seq * self.per_seq

    def is_ctx(self, s):
        return s < self.n_ctx_seg

    def lat_seq(self, s):
        return jnp.maximum(s - self.n_ctx_seg, 0) // self.per_seq


_STATE_MAT = (N_DIR, N_HEADS, HEAD_DIM, HEAD_DIM)


class _Pairs:
    def __init__(self, n_ctx, t_ctx, n_lat_seq, t_lat):
        assert t_ctx == SEG and t_lat % SEG == 0 and n_ctx == N_PAIR * t_lat and n_lat_seq % N_PAIR == 0
        self.rows = t_lat
        self.per_seq = t_lat // SEG
        self.n_ctx_step = t_lat // t_ctx
        self.n_step = self.n_ctx_step + (n_lat_seq // N_PAIR) * self.per_seq
        self.n_slot = self.n_ctx_step + n_lat_seq // N_PAIR

    def view(self, x):
        return x.reshape(x.shape[0] // self.rows, self.rows, x.shape[1])

    def is_ctx(self, s):
        return s < self.n_ctx_step

    def lat(self, s):
        r = jnp.maximum(s - self.n_ctx_step, 0)
        return r // self.per_seq, r % self.per_seq

    def macro(self, s):
        return jnp.where(self.is_ctx(s), 0, 1 + self.lat(s)[0])

    def seg(self, s, rev):
        j = self.lat(s)[1]
        return jnp.where(self.is_ctx(s), s, self.per_seq - 1 - j if rev else j)

    def slot(self, s):
        return jnp.where(self.is_ctx(s), s, self.n_ctx_step + self.lat(s)[0])

    def first(self, s):
        return jnp.logical_or(self.is_ctx(s), self.lat(s)[1] == 0)

    def last(self, s):
        return jnp.logical_or(self.is_ctx(s), self.lat(s)[1] == self.per_seq - 1)

    def block(self, rev, w):
        return pl.BlockSpec((N_PAIR, SEG, w), lambda s: (self.macro(s), self.seg(s, rev), 0))

    def state_in(self, layer, tail):
        zeros = (0,) * len(tail)
        return pl.BlockSpec((N_PAIR, 1) + tail, lambda s: (self.lat(s)[0], layer) + zeros)

    def state_out(self, tail):
        zeros = (0,) * len(tail)
        spec = pl.BlockSpec((1, N_PAIR) + tail, lambda s: (self.slot(s), 0) + zeros)
        return spec, jax.ShapeDtypeStruct((self.n_slot, N_PAIR) + tail, F32)

    def ctx_states(self, x):
        ctx = jnp.swapaxes(x[:self.n_ctx_step], 0, 1)
        return ctx.reshape((N_PAIR * self.n_ctx_step,) + x.shape[2:])


N_GATE_ROWS = N_DIR * N_HEADS
ALL_ROWS = N_PAIR * N_GATE_ROWS


def _lane_scan(x, op, ident, rev):
    n = x.shape[-1]
    pos = _in_head(_iota(x.shape, x.ndim - 1))
    s = 1
    while s < CHUNK:
        if rev:
            y, ok = pltpu.roll(x, n - s, x.ndim - 1), pos < CHUNK - s
        else:
            y, ok = pltpu.roll(x, s, x.ndim - 1), pos >= s
        x = op(x, jnp.where(ok, y, ident))
        s *= 2
    return x


def _pieces(x, terms):
    out = []
    for _ in range(terms):
        p = _bf(x).astype(F32)
        out.append(p)
        x = x - p
    return jnp.concatenate(out, axis=0)


def _expander(n_qty):
    k = np.arange(SCAN_W)[:, None]
    n = np.arange(N_PAIR * N_DIR * n_qty * SCAN_W)[None, :]
    rows_per_piece = n_qty * ALL_ROWS
    assert 3 * rows_per_piece <= SCAN_W
    qty, row = (k % rows_per_piece) // ALL_ROWS, k % ALL_ROWS
    grp, qty_n, head = n // (n_qty * SCAN_W), (n // SCAN_W) % n_qty, (n % SCAN_W) // HEAD_DIM
    used = k < 3 * rows_per_piece
    return jnp.asarray(used & (qty == qty_n) & (row == grp * N_HEADS + head), BF16)


def _expand(quantities, expand_ref):
    n_qty = len(quantities)
    packed = jnp.concatenate(quantities, axis=0)
    pad = jnp.zeros((SCAN_W - 3 * packed.shape[0], SEG), F32)
    lhs = _bf(jnp.concatenate([_pieces(packed, 3), pad], axis=0).T)
    wide = jnp.dot(lhs, expand_ref[...], preferred_element_type=F32)

    def stack(j):
        cols = [wide[:, (g * n_qty + j) * SCAN_W:(g * n_qty + j + 1) * SCAN_W] for g in range(N_PAIR * N_DIR)]
        return jnp.concatenate([_chunks(x) for x in cols], axis=0)

    return [stack(j) for j in range(n_qty)]


def _gate_rows(gf_ref, gb_ref):
    is_b = jnp.bitwise_and(_iota((ALL_ROWS, SEG), 0), N_GATE_ROWS - 1) >= N_HEADS
    is_b8 = _iota((N_GATE_ROWS, SEG), 0) >= N_HEADS
    gt = [(gf_ref[m].T, gb_ref[m].T) for m in range(N_PAIR)]

    def pick(lo):
        return jnp.concatenate([jnp.where(is_b8, b[lo:lo + N_GATE_ROWS], f[lo:lo + N_GATE_ROWS])
                                for f, b in gt], axis=0)

    return is_b, pick


def _chunk_sums(x, is_b):
    s_idx, t_idx = _iota((SEG, 2 * SEG), 0), _iota((SEG, 2 * SEG), 1)
    t_loc = jnp.bitwise_and(t_idx, SEG - 1)
    tri = jnp.logical_and(_head_of(s_idx) == _head_of(t_loc),
                          jnp.where(t_idx >= SEG, s_idx - t_loc, t_loc - s_idx) >= 0).astype(BF16)
    sums = jnp.dot(_bf(_pieces(x, 3)), tri, preferred_element_type=F32)
    sums = sums[0:ALL_ROWS] + sums[ALL_ROWS:2 * ALL_ROWS] + sums[2 * ALL_ROWS:]
    return jnp.where(is_b, sums[:, SEG:], sums[:, 0:SEG])


def _on_lanes(cols, mirrored, is_b):
    chunk_of = _head_of(_iota((ALL_ROWS, SEG), 1))
    out = jnp.zeros((ALL_ROWS, SEG), F32)
    for c in range(N_CHUNK):
        col = jnp.where(is_b, cols[N_CHUNK - 1 - c], cols[c]) if mirrored else cols[c]
        out = jnp.where(chunk_of == c, col, out)
    return out


def _mlstm_compact(gf_ref, gb_ref, fb_col, m_col, expand2_ref, expand4_ref):
    is_b, pick = _gate_rows(gf_ref, gb_ref)
    ig = pick(0)
    lf = _log_sigmoid(pick(N_GATE_ROWS) + fb_col)
    b = _chunk_sums(lf, is_b)
    r = ig - b
    cm = jnp.where(is_b, _lane_scan(r, jnp.maximum, -jnp.inf, True), _lane_scan(r, jnp.maximum, -jnp.inf, False))
    on_lanes = functools.partial(_on_lanes, is_b=is_b)
    in_chunk = lambda x, c: x[:, c * CHUNK:(c + 1) * CHUNK]
    bl_cols = [jnp.sum(in_chunk(lf, c), axis=1, keepdims=True) for c in range(N_CHUNK)]
    bl = on_lanes(bl_cols, False)
    lw = bl - b + ig
    lw_cols = [jnp.max(in_chunk(lw, c), axis=1, keepdims=True) for c in range(N_CHUNK)]

    is_b_col = is_b[:, 0:1]
    at_step = lambda cols, i: jnp.where(is_b_col, cols[N_CHUNK - 1 - i], cols[i])
    m_in, m_out = [], []
    for i in range(N_CHUNK):
        m_in.append(m_col)
        m_col = jnp.maximum(at_step(bl_cols, i) + m_col, at_step(lw_cols, i))
        m_out.append(m_col)
    m_c, m_n = on_lanes(m_in, True), on_lanes(m_out, True)
    kw_fac, dec = _expand([jnp.exp(lw - m_n), jnp.exp(bl + m_c - m_n)], expand2_ref)
    mx = jnp.maximum(m_c, cm)
    r, mx, a_int, neg_mt = _expand([r, mx, jnp.exp(m_c - mx), -(b + mx)], expand4_ref)
    return (r, mx, a_int, neg_mt, kw_fac, dec), m_col


def _mlstm_kernel(zmf_ref, zmb_ref, gf_ref, gb_ref, fb_ref, expand2_ref, expand4_ref, c0_ref, n0_ref, m0_ref,
                  hf_ref, hb_ref, c_out, n_out, m_out, c_s, n_s, m_s, *, pairs):
    s = pl.program_id(0)
    members = range(N_PAIR)

    @pl.when(pairs.first(s))
    def _():
        ctx = pairs.is_ctx(s)
        for m in members:
            m_s[m * N_GATE_ROWS:(m + 1) * N_GATE_ROWS] = jnp.where(ctx, 0.0, m0_ref[m, 0])
            for d in range(N_DIR):
                c_s[m, d] = jnp.where(ctx, 0.0, _to_block_diag(c0_ref[m, 0, d]))
                n_s[m, d] = jnp.where(ctx, 0.0, n0_ref[m, 0, d])

    fb_col = jnp.concatenate([fb_ref[0]] * N_PAIR, axis=0)
    (r, mx, a_int, neg_mt, kw_fac, dec), m_fin = _mlstm_compact(gf_ref, gb_ref, fb_col, m_s[...],
                                                                expand2_ref, expand4_ref)
    col = lambda j: jnp.concatenate([_chunks(ref[m, :, j * SCAN_W:(j + 1) * SCAN_W])
                                     for m in members for ref in (zmf_ref, zmb_ref)], axis=0)
    q, k, v = col(0), col(1) * (1.0 / math.sqrt(HEAD_DIM)), col(2)

    diag, tri, _ = _problem_masks()
    mask_bd = _block_ones(SCAN_W)
    mask_bf = mask_bd.astype(BF16)

    r_row = jnp.sum(jnp.where(diag, r, 0.0), axis=1, keepdims=True)
    w = jnp.exp(jnp.where(tri, r_row - mx, -jnp.inf))
    sc = _bdot_nt(_bf(q), _bd3(_bf(k), mask_bf)) * w
    kw = kw_fac * k
    kw_sum = jnp.sum(kw, axis=1, keepdims=True)

    n_prob = N_PAIR * N_DIR * N_CHUNK
    c_at, n_at = [None] * n_prob, [None] * n_prob
    c_fin, n_fin = {}, {}
    for m in members:
        for d in range(N_DIR):
            c_bd, n_row = c_s[m, d], n_s[m, d]
            for c in _scan_order(d):
                p = (m * N_DIR + d) * N_CHUNK + c
                c_at[p], n_at[p] = c_bd, n_row
                dec_row = dec[p, 0:1, :]
                c_bd = c_bd * dec_row + jnp.where(mask_bd, _dot(kw[p].T, v[p]), 0.0)
                n_row = n_row * dec_row + kw_sum[p]
            c_fin[m, d], n_fin[m, d] = c_bd, n_row
    c_all = jnp.concatenate([_bf(x)[None] for x in c_at], axis=0)
    n_all = jnp.concatenate([x[None] for x in n_at], axis=0)

    num = _bdot(_bf(sc), _bd3(_bf(v), mask_bf)) + a_int * _bdot(_bf(q), c_all)
    n_rows = n_prob * CHUNK
    flat = lambda a: a.reshape(n_rows, SCAN_W)
    sums = _dot_exact_rhs(jnp.concatenate([flat(sc), flat(q * n_all)], axis=0), mask_bf)
    den = sums[0:n_rows].reshape(sc.shape) + a_int * sums[n_rows:].reshape(sc.shape)
    h = num / jnp.maximum(jnp.abs(den), jnp.exp(neg_mt))
    for m in members:
        for d, out_ref in enumerate((hf_ref, hb_ref)):
            p0 = (m * N_DIR + d) * N_CHUNK
            out_ref[m] = h[p0:p0 + N_CHUNK].reshape(SEG, SCAN_W)

    m_s[...] = m_fin
    for m in members:
        for d in range(N_DIR):
            c_s[m, d] = c_fin[m, d]
            n_s[m, d] = n_fin[m, d]

    @pl.when(pairs.last(s))
    def _():
        for m in members:
            m_out[0, m] = m_s[m * N_GATE_ROWS:(m + 1) * N_GATE_ROWS]
            for d in range(N_DIR):
                c_out[0, m, d] = _from_block_diag(c_fin[m, d])
                n_out[0, m, d] = n_fin[m, d]


def _mlstm(layer, pairs, zm, gates, fb_cols, state_c, n0, m0):
    row_tail, col_tail = (N_DIR, 1, SCAN_W), (N_GATE_ROWS, 1)
    c_spec, c_shape = pairs.state_out(_STATE_MAT)
    n_spec, n_shape = pairs.state_out(row_tail)
    m_spec, m_shape = pairs.state_out(col_tail)
    expand2, expand4 = _expander(2), _expander(4)
    h_shape = jax.ShapeDtypeStruct(zm.shape[:2] + (SCAN_W,), F32)
    return pl.pallas_call(
        functools.partial(_mlstm_kernel, pairs=pairs),
        grid=(pairs.n_step,),
        in_specs=[
            pairs.block(False, D_MODEL), pairs.block(True, D_MODEL),
            pairs.block(False, GATE_W), pairs.block(True, GATE_W),
            pl.BlockSpec((1, N_GATE_ROWS, 1), lambda s: (layer, 0, 0)),
            pl.BlockSpec(expand2.shape, lambda s: (0, 0)),
            pl.BlockSpec(expand4.shape, lambda s: (0, 0)),
            pairs.state_in(layer, _STATE_MAT), pairs.state_in(layer, row_tail), pairs.state_in(layer, col_tail),
        ],
        out_specs=[pairs.block(False, SCAN_W), pairs.block(True, SCAN_W), c_spec, n_spec, m_spec],
        out_shape=[h_shape, h_shape, c_shape, n_shape, m_shape],
        scratch_shapes=[pltpu.VMEM((N_PAIR, N_DIR, SCAN_W, SCAN_W), F32),
                        pltpu.VMEM((N_PAIR, N_DIR, 1, SCAN_W), F32),
                        pltpu.VMEM((ALL_ROWS, 1), F32)],
        compiler_params=pltpu.CompilerParams(
            dimension_semantics=("arbitrary",), vmem_limit_bytes=VMEM_LIMIT),
        name="mlstm",
    )(zm, zm, gates, gates, fb_cols, expand2, expand4, state_c, n0, m0)


def _short_conv(prev_ref, cur_ref, next_ref, w_ref, has_prev, has_next):
    w3 = 3 * SCAN_W
    prev = jnp.where(has_prev, prev_ref[:, 0:w3], 0.0)
    nxt = jnp.where(has_next, next_ref[:, 0:w3], 0.0)
    xp = jnp.concatenate([prev, cur_ref[:, 0:w3], nxt], axis=0)
    rows = xp.shape[0]
    acc = None
    for i in range(CONV_W):
        shift = (CONV_W // 2 - i) % rows
        y = xp if shift == 0 else pltpu.roll(xp, shift, 0)
        t = y[HALO:HALO + SEG] * w_ref[0, i:i + 1, :]
        acc = t if acc is None else acc + t
    return acc


def _neumann_inverse(n_all, mask_bf):
    eye = (_iota(n_all.shape[1:], 0) == _in_head(_iota(n_all.shape[1:], 1))).astype(F32)
    p = -n_all
    t = eye + p
    levels = 6
    for lvl in range(levels):
        first, last = lvl == 0, lvl == levels - 1
        p_hi, p_lo = _split(p, 2)
        w_hi, w_lo = _bd3(p_hi, mask_bf), _bd3(p_lo, mask_bf)
        lhs_hi, lhs_lo = [], []
        if not first:
            t_hi, t_lo = _split(t, 2)
            lhs_hi += [t_hi, t_lo]
            lhs_lo += [t_hi]
        if not last:
            lhs_hi += [p_hi, p_lo]
            lhs_lo += [p_hi]
        a = _bdot(jnp.concatenate(lhs_hi, axis=1), w_hi)
        b = _bdot(jnp.concatenate(lhs_lo, axis=1) if len(lhs_lo) > 1 else lhs_lo[0], w_lo)
        ra, rb = 0, 0
        if not first:
            t = t + (a[:, 0:CHUNK] + a[:, CHUNK:2 * CHUNK] + b[:, 0:CHUNK])
            ra, rb = 2 * CHUNK, CHUNK
        if not last:
            p = a[:, ra:ra + CHUNK] + a[:, ra + CHUNK:ra + 2 * CHUNK] + b[:, rb:rb + CHUNK]
    return t


def _delta_qkv(prev_ref, cur_ref, next_ref, cw_ref, has_prev, has_next):
    qkv = _silu(_short_conv(prev_ref, cur_ref, next_ref, cw_ref, has_prev, has_next))

    def l2(a):
        return a * lax.rsqrt(_head_sum(a * a) + EPS)

    q = l2(qkv[:, 0:SCAN_W]) * (1.0 / math.sqrt(HEAD_DIM))
    k = l2(qkv[:, SCAN_W:2 * SCAN_W])
    return jnp.concatenate([q, k, qkv[:, 2 * SCAN_W:3 * SCAN_W]], axis=1)


def _delta_stage_qkv(s, pairs, blocks, cw_ref, qkv_s, seen_s):
    members = range(N_PAIR)
    lat = jnp.logical_not(pairs.is_ctx(s))
    j = pairs.lat(s)[1]
    mirror = pairs.per_seq - 1 - j
    half = pairs.per_seq // 2

    @pl.when(pairs.is_ctx(s))
    def _():
        for m in members:
            pf, cf, nf = (r.at[m] for r in blocks[0])
            x = _delta_qkv(pf, cf, nf, cw_ref, False, False)
            qkv_s[m, 0] = x
            qkv_s[m, 1] = x

    @pl.when(jnp.logical_and(lat, j < half))
    def _():
        for m in members:
            (pf, cf, nf), (pb, cb, nb) = ((r.at[m] for r in refs) for refs in blocks)
            xf = _delta_qkv(pf, cf, nf, cw_ref, j > 0, True)
            xb = _delta_qkv(pb, cb, nb, cw_ref, True, j > 0)
            qkv_s[m, 0] = xf
            qkv_s[m, 1] = xb
            seen_s[m, pl.ds(j, 1)] = xf[None]
            seen_s[m, pl.ds(mirror, 1)] = xb[None]

    @pl.when(jnp.logical_and(lat, j >= half))
    def _():
        for m in members:
            qkv_s[m, 0] = seen_s[m, pl.ds(j, 1)][0]
            qkv_s[m, 1] = seen_s[m, pl.ds(mirror, 1)][0]


def _delta_gates(gf_ref, gb_ref, alog_col, dtb_col, expand_ref):
    is_b, pick = _gate_rows(gf_ref, gb_ref)
    g = -jnp.exp(alog_col) * _softplus(pick(2 * N_GATE_ROWS) + dtb_col)
    beta = _sigmoid(pick(3 * N_GATE_ROWS))
    gc = _chunk_sums(g, is_b)
    totals = [jnp.sum(g[:, c * CHUNK:(c + 1) * CHUNK], axis=1, keepdims=True) for c in range(N_CHUNK)]
    gl = _on_lanes(totals, False, is_b)
    return _expand([gc, jnp.exp(gc), jnp.exp(gl - gc), jnp.exp(gl), beta], expand_ref)


def _delta_kernel(pf_ref, cf_ref, nf_ref, pb_ref, cb_ref, nb_ref, gf_ref, gb_ref, cw_ref, al_ref, dt_ref,
                  expand_ref, s0_ref, of_ref, ob_ref, s_out, s_s, qkv_s, seen_s, *, pairs):
    s = pl.program_id(0)
    members = range(N_PAIR)
    lat = jnp.logical_not(pairs.is_ctx(s))

    @pl.when(pairs.first(s))
    def _():
        for m in members:
            for d in range(N_DIR):
                s_s[m, d] = jnp.where(lat, _to_block_diag(s0_ref[m, 0, d]), 0.0)

    _delta_stage_qkv(s, pairs, ((pf_ref, cf_ref, nf_ref), (pb_ref, cb_ref, nb_ref)), cw_ref, qkv_s, seen_s)
    both = lambda col: jnp.concatenate([col] * N_PAIR, axis=0)
    gc, eg, k_fac, g_last, beta = _delta_gates(gf_ref, gb_ref, both(al_ref[0]), both(dt_ref[0]), expand_ref)
    q, k, v = (jnp.concatenate([_chunks(qkv_s[m, d, :, i * SCAN_W:(i + 1) * SCAN_W])
                                for m in members for d in range(N_DIR)], axis=0) for i in range(3))
    kb = k * beta
    k_dec = k * k_fac

    mask_bd = _block_ones(SCAN_W)
    mask_bf = mask_bd.astype(BF16)
    diag, tri, strict = _problem_masks()
    neg_diag = jnp.sum(jnp.where(diag, -gc, 0.0), axis=1, keepdims=True)
    decay = jnp.exp(jnp.where(tri, gc + neg_diag, -jnp.inf))
    k_bd = _bd3(_bf(k), mask_bf)
    kq = _bdot_nt(_bf(jnp.concatenate([kb, q], axis=1)), k_bd)
    qk = _bf(kq[:, CHUNK:] * decay)
    t_all = _bf(_neumann_inverse(jnp.where(strict, kq[:, 0:CHUNK] * decay, 0.0), mask_bf))
    u = _bdot(t_all, _bd3(_bf(v * beta), mask_bf))
    w = _bdot(t_all, _bd3(_bf(kb * eg), mask_bf))
    wq = _bf(jnp.concatenate([w, q * eg], axis=1))

    outs = (of_ref, ob_ref)
    chains = [(m, d) for m in members for d in range(N_DIR)]
    k_dec_t = [k_dec[p].T for p in range(k_dec.shape[0])]
    state = jnp.concatenate([s_s[m, d][None] for m, d in chains], axis=0)
    for i in range(N_CHUNK):
        probs = [(m * N_DIR + d) * N_CHUNK + (N_CHUNK - 1 - i if d else i) for m, d in chains]
        pick = lambda x: jnp.concatenate([x[p][None] for p in probs], axis=0)
        ws = _bdot(pick(wq), _bf(state))
        v_new = pick(u) - ws[:, 0:CHUNK]
        out = ws[:, CHUNK:] + _bdot(pick(qk), _bd3(_bf(v_new), mask_bf))
        for n, (m, d) in enumerate(chains):
            c = N_CHUNK - 1 - i if d else i
            outs[d][m, c * CHUNK:(c + 1) * CHUNK, :] = out[n]
        kd_t = _bf(jnp.concatenate([k_dec_t[p][None] for p in probs], axis=0))
        state = state * pick(g_last)[:, 0:1, :] + jnp.where(mask_bd, _bdot(kd_t, _bf(v_new)), 0.0)
    for n, (m, d) in enumerate(chains):
        s_s[m, d] = state[n]

    @pl.when(pairs.last(s))
    def _():
        for n, (m, d) in enumerate(chains):
            s_out[0, m, d] = _from_block_diag(state[n])


def _delta(layer, pairs, zg, gates, conv_w, alog_cols, dtb_cols, state_s):
    per = SEG // HALO
    last_halo = pairs.rows // HALO - 1

    def prev(rev):
        return pl.BlockSpec((N_PAIR, HALO, D_MODEL),
                            lambda s: (pairs.macro(s), jnp.maximum(pairs.seg(s, rev) * per - 1, 0), 0))

    def nxt(rev):
        return pl.BlockSpec((N_PAIR, HALO, D_MODEL),
                            lambda s: (pairs.macro(s), jnp.minimum((pairs.seg(s, rev) + 1) * per, last_halo), 0))

    gate_cols = pl.BlockSpec((1, N_GATE_ROWS, 1), lambda s: (layer, 0, 0))
    s_spec, s_shape = pairs.state_out(_STATE_MAT)
    expand5 = _expander(5)
    o_shape = jax.ShapeDtypeStruct(zg.shape[:2] + (SCAN_W,), F32)
    return pl.pallas_call(
        functools.partial(_delta_kernel, pairs=pairs),
        grid=(pairs.n_step,),
        in_specs=[
            prev(False), pairs.block(False, D_MODEL), nxt(False),
            prev(True), pairs.block(True, D_MODEL), nxt(True),
            pairs.block(False, GATE_W), pairs.block(True, GATE_W),
            pl.BlockSpec((1, CONV_W, 3 * SCAN_W), lambda s: (layer, 0, 0)),
            gate_cols, gate_cols, pl.BlockSpec(expand5.shape, lambda s: (0, 0)),
            pairs.state_in(layer, _STATE_MAT),
        ],
        out_specs=[pairs.block(False, SCAN_W), pairs.block(True, SCAN_W), s_spec],
        out_shape=[o_shape, o_shape, s_shape],
        scratch_shapes=[pltpu.VMEM((N_PAIR, N_DIR, SCAN_W, SCAN_W), F32),
                        pltpu.VMEM((N_PAIR, N_DIR, SEG, 3 * SCAN_W), F32),
                        pltpu.VMEM((N_PAIR, pairs.per_seq, SEG, 3 * SCAN_W), F32)],
        compiler_params=pltpu.CompilerParams(
            dimension_semantics=("arbitrary",), vmem_limit_bytes=VMEM_LIMIT),
        name="delta",
    )(zg, zg, zg, zg, zg, zg, gates, gates, conv_w, alog_cols, dtb_cols, expand5, state_s)


def _attend(q_ref, kv_blocks, o_ref):
    qt = q_ref[...].T
    zeros = jnp.zeros((HEAD_DIM, A_GROUP * SEG), BF16)
    ones = lambda n: jnp.ones((BF16_ROWS, n), F32)
    kv = [(_bf(k), _bf(jnp.concatenate([v.T, ones(v.shape[0])], axis=0))) for k, v in kv_blocks]
    w_groups = []
    for g in range(A_KV_HEADS):
        heads = range(g * A_GROUP, (g + 1) * A_GROUP)
        slab = _bf(jnp.concatenate([qt[h * HEAD_DIM:(h + 1) * HEAD_DIM, :] for h in heads], axis=1))
        w_groups.append(jnp.concatenate([slab if i == g else zeros for i in range(A_KV_HEADS)], axis=0))
    w_all = jnp.concatenate(w_groups, axis=1)
    m = acc = None
    nxt = jnp.dot(kv[0][0], w_all, preferred_element_type=F32)
    for i, (_, v_t) in enumerate(kv):
        sc = nxt
        if i + 1 < len(kv):
            nxt = jnp.dot(kv[i + 1][0], w_all, preferred_element_type=F32)
        m_blk = jnp.max(sc, axis=0, keepdims=True)
        m_new = m_blk if m is None else jnp.maximum(m, m_blk)
        pv = jnp.dot(v_t, _bf(jnp.exp(sc - m_new)), preferred_element_type=F32)
        acc = pv if m is None else acc * jnp.exp(m - m_new) + pv
        m = m_new
    out = acc[0:KV_WIDTH] / acc[KV_WIDTH:KV_WIDTH + 1]
    lanes = A_GROUP * SEG
    group_out = [out[g * HEAD_DIM:(g + 1) * HEAD_DIM, g * lanes:(g + 1) * lanes] for g in range(A_KV_HEADS)]
    for col in range(A_WIDTH // KV_WIDTH):
        g, h0 = (2 * col) // A_GROUP, (2 * col) % A_GROUP
        pair = jnp.concatenate([group_out[g][:, h0 * SEG:(h0 + 1) * SEG],
                                group_out[g][:, (h0 + 1) * SEG:(h0 + 2) * SEG]], axis=0)
        o_ref[:, col * KV_WIDTH:(col + 1) * KV_WIDTH] = pair.T


def _attn_kernel(q_ref, kc_ref, vc_ref, kl_ref, vl_ref, ck_ref, cv_ref, o_ref, *, segs):
    s = pl.program_id(0)

    @pl.when(segs.is_ctx(s))
    def _():
        _attend(q_ref, [(kc_ref[...], vc_ref[...])], o_ref)

    @pl.when(jnp.logical_not(segs.is_ctx(s)))
    def _():
        blocks = []
        past, t_lat = ck_ref.shape[2], kl_ref.shape[0]
        for lo in range(0, past, KV_BLOCK):
            hi = min(lo + KV_BLOCK, past)
            blocks.append((ck_ref[0, 0, lo:hi, :], cv_ref[0, 0, lo:hi, :]))
        for lo in range(0, t_lat, KV_BLOCK):
            hi = min(lo + KV_BLOCK, t_lat)
            blocks.append((kl_ref[lo:hi, :], vl_ref[lo:hi, :]))
        _attend(q_ref, blocks, o_ref)


def _attention(layer, segs, aq, ak, av, cache_k, cache_v):
    n_tok = aq.shape[0]
    t_lat = segs.per_seq * SEG
    past = cache_k.shape[2]
    assert t_lat % KV_BLOCK == 0
    lat_blocks_before = segs.n_ctx_seg * SEG // t_lat
    own = lambda w: pl.BlockSpec((SEG, w), lambda s: (s, 0))
    lat_kv = pl.BlockSpec((t_lat, KV_WIDTH), lambda s: (lat_blocks_before + segs.lat_seq(s), 0))
    cache = pl.BlockSpec((1, 1, past, KV_WIDTH), lambda s: (segs.lat_seq(s), layer, 0, 0))
    return pl.pallas_call(
        functools.partial(_attn_kernel, segs=segs),
        grid=(segs.n_seg,),
        in_specs=[own(A_WIDTH), own(KV_WIDTH), own(KV_WIDTH), lat_kv, lat_kv, cache, cache],
        out_specs=own(A_WIDTH),
        out_shape=jax.ShapeDtypeStruct((n_tok, A_WIDTH), F32),
        compiler_params=pltpu.CompilerParams(
            dimension_semantics=("arbitrary",), vmem_limit_bytes=VMEM_LIMIT),
        name="attention",
    )(aq, ak, av, ak, av, cache_k, cache_v)


def _permute_w_in(w_in):
    m, g, a, kv = SCAN_W, SCAN_W, A_WIDTH, KV_WIDTH
    sizes = (m, m, m, m, 8, 8, 3 * g, g, 8, 8, a, kv, kv)
    offs = [0]
    for sz in sizes:
        offs.append(offs[-1] + sz)
    piece = lambda i: _bf(w_in[..., offs[i]:offs[i + 1]])
    pad = jnp.zeros(w_in.shape[:-1] + (GATE_W - 32,), BF16)
    order = [0, 1, 2, 3, 6, 7, 10, 11, 12, 4, 5, 8, 9]
    return jnp.concatenate([piece(i) for i in order] + [pad], axis=-1)


def _rope_tables(t_lat):
    rows = t_lat // GRID_W
    row = np.repeat(np.arange(rows, dtype=np.float64), GRID_W)
    col = np.tile(np.arange(GRID_W, dtype=np.float64), rows)
    n_freq = HEAD_DIM // 4
    inv = ROPE_BASE ** (-np.arange(n_freq, dtype=np.float64) / n_freq)
    ang = np.stack([row[:, None] * inv, col[:, None] * inv], axis=1)
    cos, sin = np.cos(ang), np.sin(ang)
    cos_h = np.concatenate([cos, cos], axis=-1).reshape(t_lat, HEAD_DIM)
    sin_h = np.concatenate([-sin, sin], axis=-1).reshape(t_lat, HEAD_DIM)
    wide = lambda a: np.tile(a, (1, KV_WIDTH // HEAD_DIM))
    cos_t = np.concatenate([wide(cos_h), np.ones((TM, KV_WIDTH))], axis=0)
    sin_t = np.concatenate([wide(sin_h), np.zeros((TM, KV_WIDTH))], axis=0)
    return jnp.asarray(cos_t, F32), jnp.asarray(sin_t, F32)


def kernel(x_prompt, x_sample, c, cache_k, cache_v, state_mlstm_C, state_mlstm_n, state_mlstm_m,
           state_delta_S, c_ctx, ada_w, ada_b, norm_g, ffn_w_in, ffn_w_out, w_in, w_out, mlstm_f_bias,
           mlstm_norm, delta_conv, delta_a_log, delta_dt_bias, delta_norm, attn_q_norm, attn_k_norm,
           final_norm):
    batch, t_ctx, d_model = x_prompt.shape
    n_lat_seq, t_lat, _ = x_sample.shape
    depth = ada_w.shape[0]
    assert d_model == D_MODEL and norm_g.shape[1] == 3 and ffn_w_in.shape[-1] == 2 * D_FF
    assert 1 + n_lat_seq <= MOD_ROWS
    n_ctx = batch * t_ctx
    n_tok = n_ctx + n_lat_seq * t_lat
    assert n_ctx % t_lat == 0
    segs = _Segs(n_ctx, t_ctx, n_lat_seq, t_lat)
    pairs = _Pairs(n_ctx, t_ctx, n_lat_seq, t_lat)
    tiles = _Tiles(n_ctx, t_lat)

    cond = jnp.concatenate([c_ctx[None, :], c, jnp.zeros((MOD_ROWS - 1 - n_lat_seq, D_MODEL), F32)], axis=0)
    mod = _modulation(cond, ada_w, ada_b)

    wfi = _bf(ffn_w_in).reshape(depth * 2, D_MODEL, 2 * D_FF)
    wfo = _bf(ffn_w_out).reshape(depth * 2, D_FF, D_MODEL)
    wz = _permute_w_in(w_in)
    wo = _bf(w_out)
    qn = jnp.tile(attn_q_norm, (1, A_HEADS)).reshape(depth, 1, A_WIDTH)
    kn = jnp.tile(attn_k_norm, (1, A_KV_HEADS)).reshape(depth, 1, KV_WIDTH)
    gn = jnp.tile(delta_norm, (1, N_HEADS)).reshape(depth, 1, SCAN_W)
    mn = mlstm_norm.reshape(depth, 1, SCAN_W)
    fin = final_norm.reshape(1, D_MODEL)
    fb_cols = mlstm_f_bias.reshape(depth, N_GATE_ROWS, 1)
    alog_cols = delta_a_log.reshape(depth, N_GATE_ROWS, 1)
    dtb_cols = delta_dt_bias.reshape(depth, N_GATE_ROWS, 1)
    cos_t, sin_t = _rope_tables(t_lat)
    ck = cache_k.reshape(cache_k.shape[:3] + (KV_WIDTH,))
    cv = cache_v.reshape(cache_v.shape[:3] + (KV_WIDTH,))
    n0 = state_mlstm_n.reshape(n_lat_seq, depth, N_DIR, 1, SCAN_W)
    m0 = state_mlstm_m.reshape(n_lat_seq, depth, N_GATE_ROWS, 1)

    x_ctx, x_lat, lat_tile0 = x_prompt.reshape(n_ctx, D_MODEL), x_sample.reshape(n_tok - n_ctx, D_MODEL), 0
    ks, vs, cs, ns, ms, ss = [], [], [], [], [], []
    for l in range(depth):
        x1, zm, zg, aq, ak, av, gates = _dense1(l, tiles, n_tok, x_ctx, x_lat, lat_tile0, mod, norm_g,
                                                wfi, wfo, wz, qn, kn, cos_t, sin_t)
        zm_v, zg_v, gates_v = pairs.view(zm), pairs.view(zg), pairs.view(gates)
        hf, hb, c_new, n_new, m_new = _mlstm(l, pairs, zm_v, gates_v, fb_cols, state_mlstm_C, n0, m0)
        o_f, o_b, s_new = _delta(l, pairs, zg_v, gates_v, delta_conv, alog_cols, dtb_cols, state_delta_S)
        hf, hb, o_f, o_b = (a.reshape(n_tok, SCAN_W) for a in (hf, hb, o_f, o_b))
        a_out = _attention(l, segs, aq, ak, av, ck, cv)
        x = _dense2(l, tiles, x1, mod, norm_g, hf, hb, zm, mn, o_f, o_b, zg, gn, a_out, wo, wfi, wfo, fin,
                    final=(l == depth - 1))
        if l < depth - 1:
            x_ctx, x_lat, lat_tile0 = x, x, tiles.n_ctx

        ks.append(ak[:n_ctx].reshape(batch, t_ctx, A_KV_HEADS, HEAD_DIM))
        vs.append(av[:n_ctx].reshape(batch, t_ctx, A_KV_HEADS, HEAD_DIM))
        cs.append(pairs.ctx_states(c_new))
        ns.append(pairs.ctx_states(n_new).reshape(batch, N_DIR, N_HEADS, HEAD_DIM))
        ms.append(pairs.ctx_states(m_new).reshape(batch, N_DIR, N_HEADS))
        ss.append(pairs.ctx_states(s_new))

    y_prompt = x[0].reshape(batch, t_ctx, D_MODEL)
    y_sample = x[1].reshape(n_lat_seq, t_lat, D_MODEL)
    stack = lambda xs: jnp.stack(xs, axis=1)
    return (y_prompt, y_sample, stack(ks), stack(vs), stack(cs), stack(ns), stack(ms), stack(ss))
```

```python
import functools
import math

import numpy as np

import jax
import jax.numpy as jnp
from jax import lax
from jax.experimental import pallas as pl
from jax.experimental.pallas import tpu as pltpu

F32 = jnp.float32
BF16 = jnp.bfloat16

D_MODEL = 1024
HEAD_DIM = 64
N_HEADS = 4
SCAN_W = N_HEADS * HEAD_DIM
A_HEADS = 8
A_KV_HEADS = 2
A_GROUP = A_HEADS // A_KV_HEADS
A_WIDTH = A_HEADS * HEAD_DIM
KV_WIDTH = A_KV_HEADS * HEAD_DIM
N_DIR = 2
CHUNK = 64
CONV_W = 5
D_FF = 2816
FF_CHUNK = 256
GRID_W = 64
ROPE_BASE = 10000.0
EPS = 1e-6
N_MOD = 9
MOD_ROWS = 8
GATE_W = 128
Z_WIDTH = 2 * D_MODEL + A_WIDTH + 2 * KV_WIDTH + GATE_W

SEG = 256
N_CHUNK = SEG // CHUNK
TM = 512
N_PAIR = 2
HALO = 8
KV_BLOCK = 512
BF16_ROWS = 16
VMEM_LIMIT = 56 * 1024 * 1024


def _bf(x):
    return x.astype(BF16)


def _dot(a, b):
    return jnp.dot(_bf(a), _bf(b), preferred_element_type=F32)


def _bdot(a, b):
    return lax.dot_general(a, b, (((2,), (1,)), ((0,), (0,))), preferred_element_type=F32)


def _bdot_nt(a, b):
    return lax.dot_general(a, b, (((2,), (2,)), ((0,), (0,))), preferred_element_type=F32)


def _split(a, terms):
    parts = []
    rest = a
    for _ in range(terms):
        p = _bf(rest)
        parts.append(p)
        rest = rest - p.astype(F32)
    return parts


def _dot_exact_rhs(a, b01, terms=2):
    parts = _split(a, terms)
    if a.shape[-1] % 128 == 0:
        return jnp.dot(jnp.concatenate(parts, axis=-1), jnp.concatenate([b01] * terms, axis=0),
                       preferred_element_type=F32)
    out = None
    for p in parts:
        d = jnp.dot(p, b01, preferred_element_type=F32)
        out = d if out is None else out + d
    return out


def _iota(shape, dim):
    return lax.broadcasted_iota(jnp.int32, shape, dim)


def _head_of(idx):
    return jnp.right_shift(idx, HEAD_DIM.bit_length() - 1)


def _in_head(idx):
    return jnp.bitwise_and(idx, HEAD_DIM - 1)


def _block_ones(n):
    return (_head_of(_iota((n, n), 0)) == _head_of(_iota((n, n), 1)))


def _head_sum(x):
    n = x.shape[-1]
    return _dot_exact_rhs(x, _block_ones(n).astype(BF16), terms=1)


def _head_rms(x, g_row):
    ms = _head_sum(x * x) * (1.0 / HEAD_DIM)
    return x * lax.rsqrt(ms + EPS) * g_row


def _rms_mod(x, g_row, scale_row, shift_row):
    y = x * lax.rsqrt(jnp.mean(x * x, axis=-1, keepdims=True) + EPS) * g_row
    return y * (1.0 + scale_row) + shift_row


def _softplus(x):
    return jnp.maximum(x, 0.0) + jnp.log1p(jnp.exp(-jnp.abs(x)))


def _log_sigmoid(x):
    return -_softplus(-x)


def _sigmoid(x):
    return jax.nn.sigmoid(x)


def _silu(x):
    return x * jax.nn.sigmoid(x)


def _bd3(x, mask_bf):
    return jnp.concatenate([x] * N_HEADS, axis=1) * mask_bf


def _to_block_diag(x4):
    spread = (_iota((HEAD_DIM, SCAN_W), 0) == _in_head(_iota((HEAD_DIM, SCAN_W), 1))).astype(BF16)
    rows = x4.reshape(SCAN_W, HEAD_DIM)
    return jnp.where(_block_ones(SCAN_W), _dot_exact_rhs(rows, spread, terms=3), 0.0)


def _from_block_diag(x):
    fold = (_in_head(_iota((SCAN_W, HEAD_DIM), 0)) == _iota((SCAN_W, HEAD_DIM), 1)).astype(BF16)
    return _dot_exact_rhs(x, fold, terms=3).reshape(N_HEADS, HEAD_DIM, HEAD_DIM)


def _chunks(a):
    return a.reshape(N_CHUNK, CHUNK, SCAN_W)


def _problem_masks():
    shape = (N_PAIR * N_DIR * N_CHUNK, CHUNK, SCAN_W)
    t_idx, j_idx = _iota(shape, 1), _in_head(_iota(shape, 2))
    is_rev = jnp.bitwise_and(_iota(shape, 0), N_DIR * N_CHUNK - 1) >= N_CHUNK
    ahead = jnp.where(is_rev, j_idx - t_idx, t_idx - j_idx)
    return t_idx == j_idx, ahead >= 0, ahead > 0


def _scan_order(d):
    return range(N_CHUNK - 1, -1, -1) if d else range(N_CHUNK)


def _mod_kernel(cond_ref, w_ref, b_ref, o_ref):
    a = _silu(cond_ref[...])
    o_ref[0, 0] = _dot(a, w_ref[0]) + b_ref[0]


def _modulation(cond, ada_w, ada_b):
    depth = ada_w.shape[0]
    return pl.pallas_call(
        _mod_kernel,
        grid=(depth, N_MOD),
        in_specs=[
            pl.BlockSpec((MOD_ROWS, D_MODEL), lambda l, j: (0, 0)),
            pl.BlockSpec((1, D_MODEL, D_MODEL), lambda l, j: (l, 0, j)),
            pl.BlockSpec((1, 1, D_MODEL), lambda l, j: (l, 0, j)),
        ],
        out_specs=pl.BlockSpec((1, 1, MOD_ROWS, D_MODEL), lambda l, j: (l, j, 0, 0)),
        out_shape=jax.ShapeDtypeStruct((depth, N_MOD, MOD_ROWS, D_MODEL), F32),
        compiler_params=pltpu.CompilerParams(
            dimension_semantics=("arbitrary", "arbitrary"), vmem_limit_bytes=VMEM_LIMIT),
        name="modulation",
    )(cond, ada_w, ada_b.reshape(depth, 1, N_MOD * D_MODEL))


def _ffn(h, w_in_ref, w_out_ref):
    acc = None
    for c in range(D_FF // FF_CHUNK):
        lo = c * FF_CHUNK
        g = jnp.dot(h, w_in_ref[0, :, lo:lo + FF_CHUNK], preferred_element_type=F32)
        u = jnp.dot(h, w_in_ref[0, :, D_FF + lo:D_FF + lo + FF_CHUNK], preferred_element_type=F32)
        a = _bf(_silu(g) * u)
        d = jnp.dot(a, w_out_ref[0, lo:lo + FF_CHUNK, :], preferred_element_type=F32)
        acc = d if acc is None else acc + d
    return acc


def _rope(x, cos, sin_signed):
    w = x.shape[-1]
    reps = w // cos.shape[-1]
    if reps > 1:
        cos = jnp.concatenate([cos] * reps, axis=-1)
        sin_signed = jnp.concatenate([sin_signed] * reps, axis=-1)
    n_freq = HEAD_DIM // 4
    first = jnp.bitwise_and(_iota(x.shape, 1), 2 * n_freq - 1) < n_freq
    partner = jnp.where(first, pltpu.roll(x, w - n_freq, 1), pltpu.roll(x, n_freq, 1))
    return x * cos + partner * sin_signed


class _Tiles:
    def __init__(self, n_ctx, t_lat):
        assert n_ctx % TM == 0 and t_lat % TM == 0
        self.n_ctx = n_ctx // TM
        self.per_seq = t_lat // TM

    def is_ctx(self, i):
        return i < self.n_ctx

    def lat(self, i):
        return jnp.maximum(i - self.n_ctx, 0)

    def mod_row(self, i):
        return jnp.where(self.is_ctx(i), 0, 1 + self.lat(i) // self.per_seq)


def _mod_rows(mod_ref, row):
    return lambda m: mod_ref[0, m, pl.ds(row, 1), :]


def _dense1_kernel(xa_ref, xb_ref, mod_ref, ng_ref, wfi_ref, wfo_ref, wz_ref, qn_ref, kn_ref, cos_ref, sin_ref,
                   x1_ref, zm_ref, zg_ref, aq_ref, ak_ref, av_ref, gate_ref, *, tiles):
    i = pl.program_id(0)
    mod = _mod_rows(mod_ref, tiles.mod_row(i))
    x = jnp.where(tiles.is_ctx(i), xa_ref[...], xb_ref[...])
    h = _bf(_rms_mod(x, ng_ref[0, 0:1], mod(0), mod(1)))
    x1 = x + 0.5 * mod(2) * _ffn(h, wfi_ref, wfo_ref)
    x1_ref[...] = x1
    h2 = _bf(_rms_mod(x1, ng_ref[0, 1:2], mod(3), mod(4)))
    z = jnp.dot(h2, wz_ref[0], preferred_element_type=F32)

    def proj(lo, width):
        return z[:, lo:lo + width]

    zm_ref[...] = proj(0, D_MODEL)
    zg_ref[...] = proj(D_MODEL, D_MODEL)
    cos = cos_ref[...]
    sin = sin_ref[...]
    off = 2 * D_MODEL
    q = _head_rms(proj(off, A_WIDTH), qn_ref[0])
    aq_ref[...] = _rope(q, cos, sin) * (1.0 / math.sqrt(HEAD_DIM))
    k = _head_rms(proj(off + A_WIDTH, KV_WIDTH), kn_ref[0])
    ak_ref[...] = _rope(k, cos, sin)
    av_ref[...] = proj(off + A_WIDTH + KV_WIDTH, KV_WIDTH)
    gate_ref[...] = proj(off + A_WIDTH + 2 * KV_WIDTH, GATE_W)


def _resident(shape, index_map):
    return pl.BlockSpec(shape, index_map, pipeline_mode=pl.Buffered(1))


def _dense1(layer, tiles, n_tok, x_ctx, x_lat, lat_tile0, mod, norm_g, wfi, wfo, wz, qn, kn, cos_t, sin_t):
    row = lambda w: pl.BlockSpec((TM, w), lambda i: (i, 0))
    rope_tile = lambda i: jnp.where(tiles.is_ctx(i), tiles.per_seq, tiles.lat(i) % tiles.per_seq)
    rope = pl.BlockSpec((TM, KV_WIDTH), lambda i: (rope_tile(i), 0))
    out_w = (D_MODEL, D_MODEL, D_MODEL, A_WIDTH, KV_WIDTH, KV_WIDTH, GATE_W)
    return pl.pallas_call(
        functools.partial(_dense1_kernel, tiles=tiles),
        grid=(n_tok // TM,),
        in_specs=[
            pl.BlockSpec((TM, D_MODEL), lambda i: (jnp.minimum(i, tiles.n_ctx - 1), 0)),
            pl.BlockSpec((TM, D_MODEL), lambda i: (tiles.lat(i) + lat_tile0, 0)),
            pl.BlockSpec((1, N_MOD, MOD_ROWS, D_MODEL), lambda i: (layer, 0, 0, 0)),
            pl.BlockSpec((1, 3, D_MODEL), lambda i: (layer, 0, 0)),
            _resident((1, D_MODEL, 2 * D_FF), lambda i: (2 * layer, 0, 0)),
            _resident((1, D_FF, D_MODEL), lambda i: (2 * layer, 0, 0)),
            _resident((1, D_MODEL, Z_WIDTH), lambda i: (layer, 0, 0)),
            pl.BlockSpec((1, 1, A_WIDTH), lambda i: (layer, 0, 0)),
            pl.BlockSpec((1, 1, KV_WIDTH), lambda i: (layer, 0, 0)),
            rope, rope,
        ],
        out_specs=[row(w) for w in out_w],
        out_shape=[jax.ShapeDtypeStruct((n_tok, w), F32) for w in out_w],
        compiler_params=pltpu.CompilerParams(
            dimension_semantics=("arbitrary",), vmem_limit_bytes=VMEM_LIMIT),
        name="dense1",
    )(x_ctx, x_lat, mod, norm_g, wfi, wfo, wz, qn, kn, cos_t, sin_t)


def _dense2_kernel(x_ref, mod_ref, ng_ref, hf_ref, hb_ref, mo_ref, mn_ref, of_ref, ob_ref, gz_ref, gn_ref,
                   ao_ref, wo_ref, wfi_ref, wfo_ref, fin_ref, *out_refs, tiles, final):
    i = pl.program_id(0)
    mod = _mod_rows(mod_ref, tiles.mod_row(i))
    m_out = _head_rms(hf_ref[...] + hb_ref[...], mn_ref[0]) * _sigmoid(mo_ref[...])
    g_out = _head_rms(of_ref[...] + ob_ref[...], gn_ref[0]) * _silu(gz_ref[...])
    mix = jnp.dot(_bf(m_out), wo_ref[0, 0:SCAN_W, :], preferred_element_type=F32)
    mix += jnp.dot(_bf(g_out), wo_ref[0, SCAN_W:2 * SCAN_W, :], preferred_element_type=F32)
    mix += jnp.dot(_bf(ao_ref[...]), wo_ref[0, 2 * SCAN_W:, :], preferred_element_type=F32)
    x2 = x_ref[...] + mod(5) * mix
    h = _bf(_rms_mod(x2, ng_ref[0, 2:3], mod(6), mod(7)))
    x3 = x2 + 0.5 * mod(8) * _ffn(h, wfi_ref, wfo_ref)
    if not final:
        out_refs[0][...] = x3
        return
    y = x3 * lax.rsqrt(jnp.mean(x3 * x3, axis=-1, keepdims=True) + EPS) * fin_ref[...]
    y_ctx_ref, y_lat_ref = out_refs

    @pl.when(tiles.is_ctx(i))
    def _():
        y_ctx_ref[...] = y

    @pl.when(jnp.logical_not(tiles.is_ctx(i)))
    def _():
        y_lat_ref[...] = y


def _dense2(layer, tiles, x1, mod, norm_g, hf, hb, zm, m_norm, of, ob, zg, g_norm, a_out, wo, wfi, wfo, fin,
            final):
    n_tok = x1.shape[0]
    row = lambda w: pl.BlockSpec((TM, w), lambda i: (i, 0))
    last_quarter = pl.BlockSpec((TM, SCAN_W), lambda i: (i, 3))
    lane_row = lambda w: pl.BlockSpec((1, 1, w), lambda i: (layer, 0, 0))
    if final:
        n_ctx = tiles.n_ctx * TM
        out_specs = [pl.BlockSpec((TM, D_MODEL), lambda i: (jnp.minimum(i, tiles.n_ctx - 1), 0)),
                     pl.BlockSpec((TM, D_MODEL), lambda i: (tiles.lat(i), 0))]
        out_shape = [jax.ShapeDtypeStruct((n_ctx, D_MODEL), F32),
                     jax.ShapeDtypeStruct((n_tok - n_ctx, D_MODEL), F32)]
    else:
        out_specs, out_shape = row(D_MODEL), jax.ShapeDtypeStruct((n_tok, D_MODEL), F32)
    return pl.pallas_call(
        functools.partial(_dense2_kernel, tiles=tiles, final=final),
        grid=(n_tok // TM,),
        in_specs=[
            row(D_MODEL),
            pl.BlockSpec((1, N_MOD, MOD_ROWS, D_MODEL), lambda i: (layer, 0, 0, 0)),
            pl.BlockSpec((1, 3, D_MODEL), lambda i: (layer, 0, 0)),
            row(SCAN_W), row(SCAN_W), last_quarter, lane_row(SCAN_W),
            row(SCAN_W), row(SCAN_W), last_quarter, lane_row(SCAN_W),
            row(A_WIDTH),
            _resident((1, D_MODEL, D_MODEL), lambda i: (layer, 0, 0)),
            _resident((1, D_MODEL, 2 * D_FF), lambda i: (2 * layer + 1, 0, 0)),
            _resident((1, D_FF, D_MODEL), lambda i: (2 * layer + 1, 0, 0)),
            pl.BlockSpec((1, D_MODEL), lambda i: (0, 0)),
        ],
        out_specs=out_specs,
        out_shape=out_shape,
        compiler_params=pltpu.CompilerParams(
            dimension_semantics=("arbitrary",), vmem_limit_bytes=VMEM_LIMIT),
        name="dense2",
    )(x1, mod, norm_g, hf, hb, zm, m_norm, of, ob, zg, g_norm, a_out, wo, wfi, wfo, fin)


class _Segs:
    def __init__(self, n_ctx, t_ctx, n_lat_seq, t_lat):
        assert t_ctx == SEG and t_lat % SEG == 0
        self.n_ctx_seg = n_ctx // SEG
        self.per_seq = t_lat // SEG
        self.n_seg = self.n_ctx_seg + n_lat_seq * self.per_seq

    def is_ctx(self, s):
        return s < self.n_ctx_seg

    def lat_seq(self, s):
        return jnp.maximum(s - self.n_ctx_seg, 0) // self.per_seq


_STATE_MAT = (N_DIR, N_HEADS, HEAD_DIM, HEAD_DIM)


class _Pairs:
    def __init__(self, n_ctx, t_ctx, n_lat_seq, t_lat):
        assert t_ctx == SEG and t_lat % SEG == 0 and n_ctx == N_PAIR * t_lat and n_lat_seq % N_PAIR == 0
        self.rows = t_lat
        self.per_seq = t_lat // SEG
        self.n_ctx_step = t_lat // t_ctx
        self.n_step = self.n_ctx_step + (n_lat_seq // N_PAIR) * self.per_seq
        self.n_slot = self.n_ctx_step + n_lat_seq // N_PAIR

    def view(self, x):
        return x.reshape(x.shape[0] // self.rows, self.rows, x.shape[1])

    def is_ctx(self, s):
        return s < self.n_ctx_step

    def lat(self, s):
        r = jnp.maximum(s - self.n_ctx_step, 0)
        return r // self.per_seq, r % self.per_seq

    def macro(self, s):
        return jnp.where(self.is_ctx(s), 0, 1 + self.lat(s)[0])

    def seg(self, s, rev):
        j = self.lat(s)[1]
        return jnp.where(self.is_ctx(s), s, self.per_seq - 1 - j if rev else j)

    def slot(self, s):
        return jnp.where(self.is_ctx(s), s, self.n_ctx_step + self.lat(s)[0])

    def first(self, s):
        return jnp.logical_or(self.is_ctx(s), self.lat(s)[1] == 0)

    def last(self, s):
        return jnp.logical_or(self.is_ctx(s), self.lat(s)[1] == self.per_seq - 1)

    def block(self, rev, w):
        return pl.BlockSpec((N_PAIR, SEG, w), lambda s: (self.macro(s), self.seg(s, rev), 0))

    def state_in(self, layer, tail):
        zeros = (0,) * len(tail)
        return pl.BlockSpec((N_PAIR, 1) + tail, lambda s: (self.lat(s)[0], layer) + zeros)

    def state_out(self, tail):
        zeros = (0,) * len(tail)
        spec = pl.BlockSpec((1, N_PAIR) + tail, lambda s: (self.slot(s), 0) + zeros)
        return spec, jax.ShapeDtypeStruct((self.n_slot, N_PAIR) + tail, F32)

    def ctx_states(self, x):
        ctx = jnp.swapaxes(x[:self.n_ctx_step], 0, 1)
        return ctx.reshape((N_PAIR * self.n_ctx_step,) + x.shape[2:])


N_GATE_ROWS = N_DIR * N_HEADS
ALL_ROWS = N_PAIR * N_GATE_ROWS


def _lane_scan(x, op, ident, rev):
    n = x.shape[-1]
    pos = _in_head(_iota(x.shape, x.ndim - 1))
    s = 1
    while s < CHUNK:
        if rev:
            y, ok = pltpu.roll(x, n - s, x.ndim - 1), pos < CHUNK - s
        else:
            y, ok = pltpu.roll(x, s, x.ndim - 1), pos >= s
        x = op(x, jnp.where(ok, y, ident))
        s *= 2
    return x


def _pieces(x, terms):
    out = []
    for _ in range(terms):
        p = _bf(x).astype(F32)
        out.append(p)
        x = x - p
    return jnp.concatenate(out, axis=0)


def _expander(n_qty):
    k = np.arange(SCAN_W)[:, None]
    n = np.arange(N_PAIR * N_DIR * n_qty * SCAN_W)[None, :]
    rows_per_piece = n_qty * ALL_ROWS
    assert 3 * rows_per_piece <= SCAN_W
    qty, row = (k % rows_per_piece) // ALL_ROWS, k % ALL_ROWS
    grp, qty_n, head = n // (n_qty * SCAN_W), (n // SCAN_W) % n_qty, (n % SCAN_W) // HEAD_DIM
    used = k < 3 * rows_per_piece
    return jnp.asarray(used & (qty == qty_n) & (row == grp * N_HEADS + head), BF16)


def _expand(quantities, expand_ref):
    n_qty = len(quantities)
    packed = jnp.concatenate(quantities, axis=0)
    pad = jnp.zeros((SCAN_W - 3 * packed.shape[0], SEG), F32)
    lhs = _bf(jnp.concatenate([_pieces(packed, 3), pad], axis=0).T)
    wide = jnp.dot(lhs, expand_ref[...], preferred_element_type=F32)

    def stack(j):
        cols = [wide[:, (g * n_qty + j) * SCAN_W:(g * n_qty + j + 1) * SCAN_W] for g in range(N_PAIR * N_DIR)]
        return jnp.concatenate([_chunks(x) for x in cols], axis=0)

    return [stack(j) for j in range(n_qty)]


def _gate_rows(gf_ref, gb_ref):
    is_b = jnp.bitwise_and(_iota((ALL_ROWS, SEG), 0), N_GATE_ROWS - 1) >= N_HEADS
    is_b8 = _iota((N_GATE_ROWS, SEG), 0) >= N_HEADS
    gt = [(gf_ref[m].T, gb_ref[m].T) for m in range(N_PAIR)]

    def pick(lo):
        return jnp.concatenate([jnp.where(is_b8, b[lo:lo + N_GATE_ROWS], f[lo:lo + N_GATE_ROWS])
                                for f, b in gt], axis=0)

    return is_b, pick


def _chunk_sums(x, is_b):
    s_idx, t_idx = _iota((SEG, 2 * SEG), 0), _iota((SEG, 2 * SEG), 1)
    t_loc = jnp.bitwise_and(t_idx, SEG - 1)
    tri = jnp.logical_and(_head_of(s_idx) == _head_of(t_loc),
                          jnp.where(t_idx >= SEG, s_idx - t_loc, t_loc - s_idx) >= 0).astype(BF16)
    sums = jnp.dot(_bf(_pieces(x, 3)), tri, preferred_element_type=F32)
    sums = sums[0:ALL_ROWS] + sums[ALL_ROWS:2 * ALL_ROWS] + sums[2 * ALL_ROWS:]
    return jnp.where(is_b, sums[:, SEG:], sums[:, 0:SEG])


def _on_lanes(cols, mirrored, is_b):
    chunk_of = _head_of(_iota((ALL_ROWS, SEG), 1))
    out = jnp.zeros((ALL_ROWS, SEG), F32)
    for c in range(N_CHUNK):
        col = jnp.where(is_b, cols[N_CHUNK - 1 - c], cols[c]) if mirrored else cols[c]
        out = jnp.where(chunk_of == c, col, out)
    return out


def _mlstm_compact(gf_ref, gb_ref, fb_col, m_col, expand2_ref, expand4_ref):
    is_b, pick = _gate_rows(gf_ref, gb_ref)
    ig = pick(0)
    lf = _log_sigmoid(pick(N_GATE_ROWS) + fb_col)
    b = _chunk_sums(lf, is_b)
    r = ig - b
    cm = jnp.where(is_b, _lane_scan(r, jnp.maximum, -jnp.inf, True), _lane_scan(r, jnp.maximum, -jnp.inf, False))
    on_lanes = functools.partial(_on_lanes, is_b=is_b)
    in_chunk = lambda x, c: x[:, c * CHUNK:(c + 1) * CHUNK]
    bl_cols = [jnp.sum(in_chunk(lf, c), axis=1, keepdims=True) for c in range(N_CHUNK)]
    bl = on_lanes(bl_cols, False)
    lw = bl - b + ig
    lw_cols = [jnp.max(in_chunk(lw, c), axis=1, keepdims=True) for c in range(N_CHUNK)]

    is_b_col = is_b[:, 0:1]
    at_step = lambda cols, i: jnp.where(is_b_col, cols[N_CHUNK - 1 - i], cols[i])
    m_in, m_out = [], []
    for i in range(N_CHUNK):
        m_in.append(m_col)
        m_col = jnp.maximum(at_step(bl_cols, i) + m_col, at_step(lw_cols, i))
        m_out.append(m_col)
    m_c, m_n = on_lanes(m_in, True), on_lanes(m_out, True)
    kw_fac, dec = _expand([jnp.exp(lw - m_n), jnp.exp(bl + m_c - m_n)], expand2_ref)
    mx = jnp.maximum(m_c, cm)
    r, mx, a_int, neg_mt = _expand([r, mx, jnp.exp(m_c - mx), -(b + mx)], expand4_ref)
    return (r, mx, a_int, neg_mt, kw_fac, dec), m_col


def _mlstm_kernel(zmf_ref, zmb_ref, gf_ref, gb_ref, fb_ref, expand2_ref, expand4_ref, c0_ref, n0_ref, m0_ref,
                  hf_ref, hb_ref, c_out, n_out, m_out, c_s, n_s, m_s, *, pairs):
    s = pl.program_id(0)
    members = range(N_PAIR)

    @pl.when(pairs.first(s))
    def _():
        ctx = pairs.is_ctx(s)
        for m in members:
            m_s[m * N_GATE_ROWS:(m + 1) * N_GATE_ROWS] = jnp.where(ctx, 0.0, m0_ref[m, 0])
            for d in range(N_DIR):
                c_s[m, d] = jnp.where(ctx, 0.0, _to_block_diag(c0_ref[m, 0, d]))
                n_s[m, d] = jnp.where(ctx, 0.0, n0_ref[m, 0, d])

    fb_col = jnp.concatenate([fb_ref[0]] * N_PAIR, axis=0)
    (r, mx, a_int, neg_mt, kw_fac, dec), m_fin = _mlstm_compact(gf_ref, gb_ref, fb_col, m_s[...],
                                                                expand2_ref, expand4_ref)
    col = lambda j: jnp.concatenate([_chunks(ref[m, :, j * SCAN_W:(j + 1) * SCAN_W])
                                     for m in members for ref in (zmf_ref, zmb_ref)], axis=0)
    q, k, v = col(0), col(1) * (1.0 / math.sqrt(HEAD_DIM)), col(2)

    diag, tri, _ = _problem_masks()
    mask_bd = _block_ones(SCAN_W)
    mask_bf = mask_bd.astype(BF16)

    r_row = jnp.sum(jnp.where(diag, r, 0.0), axis=1, keepdims=True)
    w = jnp.exp(jnp.where(tri, r_row - mx, -jnp.inf))
    sc = _bdot_nt(_bf(q), _bd3(_bf(k), mask_bf)) * w
    kw = kw_fac * k
    kw_sum = jnp.sum(kw, axis=1, keepdims=True)

    n_prob = N_PAIR * N_DIR * N_CHUNK
    c_at, n_at = [None] * n_prob, [None] * n_prob
    c_fin, n_fin = {}, {}
    for m in members:
        for d in range(N_DIR):
            c_bd, n_row = c_s[m, d], n_s[m, d]
            for c in _scan_order(d):
                p = (m * N_DIR + d) * N_CHUNK + c
                c_at[p], n_at[p] = c_bd, n_row
                dec_row = dec[p, 0:1, :]
                outer = lax.dot_general(_bf(kw[p]), _bf(v[p]), (((0,), (0,)), ((), ())),
                                        preferred_element_type=F32)
                c_bd = c_bd * dec_row + jnp.where(mask_bd, outer, 0.0)
                n_row = n_row * dec_row + kw_sum[p]
            c_fin[m, d], n_fin[m, d] = c_bd, n_row
    c_all = jnp.concatenate([_bf(x)[None] for x in c_at], axis=0)
    n_all = jnp.concatenate([x[None] for x in n_at], axis=0)

    num = _bdot(_bf(sc), _bd3(_bf(v), mask_bf)) + a_int * _bdot(_bf(q), c_all)
    n_rows = n_prob * CHUNK
    flat = lambda a: a.reshape(n_rows, SCAN_W)
    sums = _dot_exact_rhs(jnp.concatenate([flat(sc), flat(q * n_all)], axis=0), mask_bf)
    den = sums[0:n_rows].reshape(sc.shape) + a_int * sums[n_rows:].reshape(sc.shape)
    h = num / jnp.maximum(jnp.abs(den), jnp.exp(neg_mt))
    for m in members:
        for d, out_ref in enumerate((hf_ref, hb_ref)):
            p0 = (m * N_DIR + d) * N_CHUNK
            out_ref[m] = h[p0:p0 + N_CHUNK].reshape(SEG, SCAN_W)

    m_s[...] = m_fin
    for m in members:
        for d in range(N_DIR):
            c_s[m, d] = c_fin[m, d]
            n_s[m, d] = n_fin[m, d]

    @pl.when(pairs.last(s))
    def _():
        for m in members:
            m_out[0, m] = m_s[m * N_GATE_ROWS:(m + 1) * N_GATE_ROWS]
            for d in range(N_DIR):
                c_out[0, m, d] = _from_block_diag(c_fin[m, d])
                n_out[0, m, d] = n_fin[m, d]


def _mlstm(layer, pairs, zm, gates, fb_cols, state_c, n0, m0):
    row_tail, col_tail = (N_DIR, 1, SCAN_W), (N_GATE_ROWS, 1)
    c_spec, c_shape = pairs.state_out(_STATE_MAT)
    n_spec, n_shape = pairs.state_out(row_tail)
    m_spec, m_shape = pairs.state_out(col_tail)
    expand2, expand4 = _expander(2), _expander(4)
    h_shape = jax.ShapeDtypeStruct(zm.shape[:2] + (SCAN_W,), F32)
    return pl.pallas_call(
        functools.partial(_mlstm_kernel, pairs=pairs),
        grid=(pairs.n_step,),
        in_specs=[
            pairs.block(False, D_MODEL), pairs.block(True, D_MODEL),
            pairs.block(False, GATE_W), pairs.block(True, GATE_W),
            pl.BlockSpec((1, N_GATE_ROWS, 1), lambda s: (layer, 0, 0)),
            pl.BlockSpec(expand2.shape, lambda s: (0, 0)),
            pl.BlockSpec(expand4.shape, lambda s: (0, 0)),
            pairs.state_in(layer, _STATE_MAT), pairs.state_in(layer, row_tail), pairs.state_in(layer, col_tail),
        ],
        out_specs=[pairs.block(False, SCAN_W), pairs.block(True, SCAN_W), c_spec, n_spec, m_spec],
        out_shape=[h_shape, h_shape, c_shape, n_shape, m_shape],
        scratch_shapes=[pltpu.VMEM((N_PAIR, N_DIR, SCAN_W, SCAN_W), F32),
                        pltpu.VMEM((N_PAIR, N_DIR, 1, SCAN_W), F32),
                        pltpu.VMEM((ALL_ROWS, 1), F32)],
        compiler_params=pltpu.CompilerParams(
            dimension_semantics=("arbitrary",), vmem_limit_bytes=VMEM_LIMIT),
        name="mlstm",
    )(zm, zm, gates, gates, fb_cols, expand2, expand4, state_c, n0, m0)


def _short_conv(prev_ref, cur_ref, next_ref, w_ref, has_prev, has_next):
    w3 = 3 * SCAN_W
    prev = jnp.where(has_prev, prev_ref[:, 0:w3], 0.0)
    nxt = jnp.where(has_next, next_ref[:, 0:w3], 0.0)
    xp = jnp.concatenate([prev, cur_ref[:, 0:w3], nxt], axis=0)
    rows = xp.shape[0]
    acc = None
    for i in range(CONV_W):
        shift = (CONV_W // 2 - i) % rows
        y = xp if shift == 0 else pltpu.roll(xp, shift, 0)
        t = y[HALO:HALO + SEG] * w_ref[0, i:i + 1, :]
        acc = t if acc is None else acc + t
    return acc


def _neumann_inverse(n_all, mask_bf):
    eye = (_iota(n_all.shape[1:], 0) == _in_head(_iota(n_all.shape[1:], 1))).astype(F32)
    p = -n_all
    t = eye + p
    levels = 6
    for lvl in range(levels):
        first, last = lvl == 0, lvl == levels - 1
        p_hi, p_lo = _split(p, 2)
        w_hi, w_lo = _bd3(p_hi, mask_bf), _bd3(p_lo, mask_bf)
        lhs_hi, lhs_lo = [], []
        if not first:
            t_hi, t_lo = _split(t, 2)
            lhs_hi += [t_hi, t_lo]
            lhs_lo += [t_hi]
        if not last:
            lhs_hi += [p_hi, p_lo]
            lhs_lo += [p_hi]
        a = _bdot(jnp.concatenate(lhs_hi, axis=1), w_hi)
        b = _bdot(jnp.concatenate(lhs_lo, axis=1) if len(lhs_lo) > 1 else lhs_lo[0], w_lo)
        ra, rb = 0, 0
        if not first:
            t = t + (a[:, 0:CHUNK] + a[:, CHUNK:2 * CHUNK] + b[:, 0:CHUNK])
            ra, rb = 2 * CHUNK, CHUNK
        if not last:
            p = a[:, ra:ra + CHUNK] + a[:, ra + CHUNK:ra + 2 * CHUNK] + b[:, rb:rb + CHUNK]
    return t


def _delta_qkv(prev_ref, cur_ref, next_ref, cw_ref, has_prev, has_next):
    qkv = _silu(_short_conv(prev_ref, cur_ref, next_ref, cw_ref, has_prev, has_next))

    def l2(a):
        return a * lax.rsqrt(_head_sum(a * a) + EPS)

    q = l2(qkv[:, 0:SCAN_W]) * (1.0 / math.sqrt(HEAD_DIM))
    k = l2(qkv[:, SCAN_W:2 * SCAN_W])
    return jnp.concatenate([q, k, qkv[:, 2 * SCAN_W:3 * SCAN_W]], axis=1)


def _delta_stage_qkv(s, pairs, blocks, cw_ref, qkv_s, seen_s):
    members = range(N_PAIR)
    lat = jnp.logical_not(pairs.is_ctx(s))
    j = pairs.lat(s)[1]
    mirror = pairs.per_seq - 1 - j
    half = pairs.per_seq // 2

    @pl.when(pairs.is_ctx(s))
    def _():
        for m in members:
            pf, cf, nf = (r.at[m] for r in blocks[0])
            x = _delta_qkv(pf, cf, nf, cw_ref, False, False)
            qkv_s[m, 0] = x
            qkv_s[m, 1] = x

    @pl.when(jnp.logical_and(lat, j < half))
    def _():
        for m in members:
            (pf, cf, nf), (pb, cb, nb) = ((r.at[m] for r in refs) for refs in blocks)
            xf = _delta_qkv(pf, cf, nf, cw_ref, j > 0, True)
            xb = _delta_qkv(pb, cb, nb, cw_ref, True, j > 0)
            qkv_s[m, 0] = xf
            qkv_s[m, 1] = xb
            seen_s[m, pl.ds(j, 1)] = xf[None]
            seen_s[m, pl.ds(mirror, 1)] = xb[None]

    @pl.when(jnp.logical_and(lat, j >= half))
    def _():
        for m in members:
            qkv_s[m, 0] = seen_s[m, pl.ds(j, 1)][0]
            qkv_s[m, 1] = seen_s[m, pl.ds(mirror, 1)][0]


def _delta_gates(gf_ref, gb_ref, alog_col, dtb_col, expand_ref):
    is_b, pick = _gate_rows(gf_ref, gb_ref)
    g = -jnp.exp(alog_col) * _softplus(pick(2 * N_GATE_ROWS) + dtb_col)
    beta = _sigmoid(pick(3 * N_GATE_ROWS))
    gc = _chunk_sums(g, is_b)
    totals = [jnp.sum(g[:, c * CHUNK:(c + 1) * CHUNK], axis=1, keepdims=True) for c in range(N_CHUNK)]
    gl = _on_lanes(totals, False, is_b)
    return _expand([gc, jnp.exp(gc), jnp.exp(gl - gc), jnp.exp(gl), beta], expand_ref)


def _delta_kernel(pf_ref, cf_ref, nf_ref, pb_ref, cb_ref, nb_ref, gf_ref, gb_ref, cw_ref, al_ref, dt_ref,
                  expand_ref, s0_ref, of_ref, ob_ref, s_out, s_s, qkv_s, seen_s, *, pairs):
    s = pl.program_id(0)
    members = range(N_PAIR)
    lat = jnp.logical_not(pairs.is_ctx(s))

    @pl.when(pairs.first(s))
    def _():
        for m in members:
            for d in range(N_DIR):
                s_s[m, d] = jnp.where(lat, _to_block_diag(s0_ref[m, 0, d]), 0.0)

    _delta_stage_qkv(s, pairs, ((pf_ref, cf_ref, nf_ref), (pb_ref, cb_ref, nb_ref)), cw_ref, qkv_s, seen_s)
    both = lambda col: jnp.concatenate([col] * N_PAIR, axis=0)
    gc, eg, k_fac, g_last, beta = _delta_gates(gf_ref, gb_ref, both(al_ref[0]), both(dt_ref[0]), expand_ref)
    q, k, v = (jnp.concatenate([_chunks(qkv_s[m, d, :, i * SCAN_W:(i + 1) * SCAN_W])
                                for m in members for d in range(N_DIR)], axis=0) for i in range(3))
    kb = k * beta
    k_dec = k * k_fac

    mask_bd = _block_ones(SCAN_W)
    mask_bf = mask_bd.astype(BF16)
    diag, tri, strict = _problem_masks()
    neg_diag = jnp.sum(jnp.where(diag, -gc, 0.0), axis=1, keepdims=True)
    decay = jnp.exp(jnp.where(tri, gc + neg_diag, -jnp.inf))
    k_bd = _bd3(_bf(k), mask_bf)
    kq = _bdot_nt(_bf(jnp.concatenate([kb, q], axis=1)), k_bd)
    qk = _bf(kq[:, CHUNK:] * decay)
    t_all = _bf(_neumann_inverse(jnp.where(strict, kq[:, 0:CHUNK] * decay, 0.0), mask_bf))
    u = _bdot(t_all, _bd3(_bf(v * beta), mask_bf))
    w = _bdot(t_all, _bd3(_bf(kb * eg), mask_bf))
    wq = _bf(jnp.concatenate([w, q * eg], axis=1))

    outs = (of_ref, ob_ref)
    chains = [(m, d) for m in members for d in range(N_DIR)]
    state = jnp.concatenate([s_s[m, d][None] for m, d in chains], axis=0)
    for i in range(N_CHUNK):
        probs = [(m * N_DIR + d) * N_CHUNK + (N_CHUNK - 1 - i if d else i) for m, d in chains]
        pick = lambda x: jnp.concatenate([x[p][None] for p in probs], axis=0)
        ws = _bdot(pick(wq), _bf(state))
        v_new = pick(u) - ws[:, 0:CHUNK]
        out = ws[:, CHUNK:] + _bdot(pick(qk), _bd3(_bf(v_new), mask_bf))
        for n, (m, d) in enumerate(chains):
            c = N_CHUNK - 1 - i if d else i
            outs[d][m, c * CHUNK:(c + 1) * CHUNK, :] = out[n]
        kv_outer = lax.dot_general(_bf(pick(k_dec)), _bf(v_new), (((1,), (1,)), ((0,), (0,))),
                                   preferred_element_type=F32)
        state = state * pick(g_last)[:, 0:1, :] + jnp.where(mask_bd, kv_outer, 0.0)
    for n, (m, d) in enumerate(chains):
        s_s[m, d] = state[n]

    @pl.when(pairs.last(s))
    def _():
        for n, (m, d) in enumerate(chains):
            s_out[0, m, d] = _from_block_diag(state[n])


def _delta(layer, pairs, zg, gates, conv_w, alog_cols, dtb_cols, state_s):
    per = SEG // HALO
    last_halo = pairs.rows // HALO - 1

    def prev(rev):
        return pl.BlockSpec((N_PAIR, HALO, D_MODEL),
                            lambda s: (pairs.macro(s), jnp.maximum(pairs.seg(s, rev) * per - 1, 0), 0))

    def nxt(rev):
        return pl.BlockSpec((N_PAIR, HALO, D_MODEL),
                            lambda s: (pairs.macro(s), jnp.minimum((pairs.seg(s, rev) + 1) * per, last_halo), 0))

    gate_cols = pl.BlockSpec((1, N_GATE_ROWS, 1), lambda s: (layer, 0, 0))
    s_spec, s_shape = pairs.state_out(_STATE_MAT)
    expand5 = _expander(5)
    o_shape = jax.ShapeDtypeStruct(zg.shape[:2] + (SCAN_W,), F32)
    return pl.pallas_call(
        functools.partial(_delta_kernel, pairs=pairs),
        grid=(pairs.n_step,),
        in_specs=[
            prev(False), pairs.block(False, D_MODEL), nxt(False),
            prev(True), pairs.block(True, D_MODEL), nxt(True),
            pairs.block(False, GATE_W), pairs.block(True, GATE_W),
            pl.BlockSpec((1, CONV_W, 3 * SCAN_W), lambda s: (layer, 0, 0)),
            gate_cols, gate_cols, pl.BlockSpec(expand5.shape, lambda s: (0, 0)),
            pairs.state_in(layer, _STATE_MAT),
        ],
        out_specs=[pairs.block(False, SCAN_W), pairs.block(True, SCAN_W), s_spec],
        out_shape=[o_shape, o_shape, s_shape],
        scratch_shapes=[pltpu.VMEM((N_PAIR, N_DIR, SCAN_W, SCAN_W), F32),
                        pltpu.VMEM((N_PAIR, N_DIR, SEG, 3 * SCAN_W), F32),
                        pltpu.VMEM((N_PAIR, pairs.per_seq, SEG, 3 * SCAN_W), F32)],
        compiler_params=pltpu.CompilerParams(
            dimension_semantics=("arbitrary",), vmem_limit_bytes=VMEM_LIMIT),
        name="delta",
    )(zg, zg, zg, zg, zg, zg, gates, gates, conv_w, alog_cols, dtb_cols, expand5, state_s)


def _attend(q_ref, kv_blocks, o_ref):
    qt = q_ref[...].T
    zeros = jnp.zeros((HEAD_DIM, A_GROUP * SEG), BF16)
    ones = lambda n: jnp.ones((BF16_ROWS, n), F32)
    kv = [(_bf(k), _bf(jnp.concatenate([v.T, ones(v.shape[0])], axis=0))) for k, v in kv_blocks]
    w_groups = []
    for g in range(A_KV_HEADS):
        heads = range(g * A_GROUP, (g + 1) * A_GROUP)
        slab = _bf(jnp.concatenate([qt[h * HEAD_DIM:(h + 1) * HEAD_DIM, :] for h in heads], axis=1))
        w_groups.append(jnp.concatenate([slab if i == g else zeros for i in range(A_KV_HEADS)], axis=0))
    w_all = jnp.concatenate(w_groups, axis=1)
    m = acc = None
    nxt = jnp.dot(kv[0][0], w_all, preferred_element_type=F32)
    for i, (_, v_t) in enumerate(kv):
        sc = nxt
        if i + 1 < len(kv):
            nxt = jnp.dot(kv[i + 1][0], w_all, preferred_element_type=F32)
        m_blk = jnp.max(sc, axis=0, keepdims=True)
        m_new = m_blk if m is None else jnp.maximum(m, m_blk)
        pv = jnp.dot(v_t, _bf(jnp.exp(sc - m_new)), preferred_element_type=F32)
        acc = pv if m is None else acc * jnp.exp(m - m_new) + pv
        m = m_new
    out = acc[0:KV_WIDTH] / acc[KV_WIDTH:KV_WIDTH + 1]
    lanes = A_GROUP * SEG
    group_out = [out[g * HEAD_DIM:(g + 1) * HEAD_DIM, g * lanes:(g + 1) * lanes] for g in range(A_KV_HEADS)]
    for col in range(A_WIDTH // KV_WIDTH):
        g, h0 = (2 * col) // A_GROUP, (2 * col) % A_GROUP
        pair = jnp.concatenate([group_out[g][:, h0 * SEG:(h0 + 1) * SEG],
                                group_out[g][:, (h0 + 1) * SEG:(h0 + 2) * SEG]], axis=0)
        o_ref[:, col * KV_WIDTH:(col + 1) * KV_WIDTH] = pair.T


def _attn_kernel(q_ref, kc_ref, vc_ref, kl_ref, vl_ref, ck_ref, cv_ref, o_ref, *, segs):
    s = pl.program_id(0)

    @pl.when(segs.is_ctx(s))
    def _():
        _attend(q_ref, [(kc_ref[...], vc_ref[...])], o_ref)

    @pl.when(jnp.logical_not(segs.is_ctx(s)))
    def _():
        blocks = []
        past, t_lat = ck_ref.shape[2], kl_ref.shape[0]
        for lo in range(0, past, KV_BLOCK):
            hi = min(lo + KV_BLOCK, past)
            blocks.append((ck_ref[0, 0, lo:hi, :], cv_ref[0, 0, lo:hi, :]))
        for lo in range(0, t_lat, KV_BLOCK):
            hi = min(lo + KV_BLOCK, t_lat)
            blocks.append((kl_ref[lo:hi, :], vl_ref[lo:hi, :]))
        _attend(q_ref, blocks, o_ref)


def _attention(layer, segs, aq, ak, av, cache_k, cache_v):
    n_tok = aq.shape[0]
    t_lat = segs.per_seq * SEG
    past = cache_k.shape[2]
    assert t_lat % KV_BLOCK == 0
    lat_blocks_before = segs.n_ctx_seg * SEG // t_lat
    own = lambda w: pl.BlockSpec((SEG, w), lambda s: (s, 0))
    lat_kv = pl.BlockSpec((t_lat, KV_WIDTH), lambda s: (lat_blocks_before + segs.lat_seq(s), 0))
    cache = pl.BlockSpec((1, 1, past, KV_WIDTH), lambda s: (segs.lat_seq(s), layer, 0, 0))
    return pl.pallas_call(
        functools.partial(_attn_kernel, segs=segs),
        grid=(segs.n_seg,),
        in_specs=[own(A_WIDTH), own(KV_WIDTH), own(KV_WIDTH), lat_kv, lat_kv, cache, cache],
        out_specs=own(A_WIDTH),
        out_shape=jax.ShapeDtypeStruct((n_tok, A_WIDTH), F32),
        compiler_params=pltpu.CompilerParams(
            dimension_semantics=("arbitrary",), vmem_limit_bytes=VMEM_LIMIT),
        name="attention",
    )(aq, ak, av, ak, av, cache_k, cache_v)


def _permute_w_in(w_in):
    m, g, a, kv = SCAN_W, SCAN_W, A_WIDTH, KV_WIDTH
    sizes = (m, m, m, m, 8, 8, 3 * g, g, 8, 8, a, kv, kv)
    offs = [0]
    for sz in sizes:
        offs.append(offs[-1] + sz)
    piece = lambda i: _bf(w_in[..., offs[i]:offs[i + 1]])
    pad = jnp.zeros(w_in.shape[:-1] + (GATE_W - 32,), BF16)
    order = [0, 1, 2, 3, 6, 7, 10, 11, 12, 4, 5, 8, 9]
    return jnp.concatenate([piece(i) for i in order] + [pad], axis=-1)


def _rope_tables(t_lat):
    rows = t_lat // GRID_W
    row = np.repeat(np.arange(rows, dtype=np.float64), GRID_W)
    col = np.tile(np.arange(GRID_W, dtype=np.float64), rows)
    n_freq = HEAD_DIM // 4
    inv = ROPE_BASE ** (-np.arange(n_freq, dtype=np.float64) / n_freq)
    ang = np.stack([row[:, None] * inv, col[:, None] * inv], axis=1)
    cos, sin = np.cos(ang), np.sin(ang)
    cos_h = np.concatenate([cos, cos], axis=-1).reshape(t_lat, HEAD_DIM)
    sin_h = np.concatenate([-sin, sin], axis=-1).reshape(t_lat, HEAD_DIM)
    wide = lambda a: np.tile(a, (1, KV_WIDTH // HEAD_DIM))
    cos_t = np.concatenate([wide(cos_h), np.ones((TM, KV_WIDTH))], axis=0)
    sin_t = np.concatenate([wide(sin_h), np.zeros((TM, KV_WIDTH))], axis=0)
    return jnp.asarray(cos_t, F32), jnp.asarray(sin_t, F32)


def kernel(x_prompt, x_sample, c, cache_k, cache_v, state_mlstm_C, state_mlstm_n, state_mlstm_m,
           state_delta_S, c_ctx, ada_w, ada_b, norm_g, ffn_w_in, ffn_w_out, w_in, w_out, mlstm_f_bias,
           mlstm_norm, delta_conv, delta_a_log, delta_dt_bias, delta_norm, attn_q_norm, attn_k_norm,
           final_norm):
    batch, t_ctx, d_model = x_prompt.shape
    n_lat_seq, t_lat, _ = x_sample.shape
    depth = ada_w.shape[0]
    assert d_model == D_MODEL and norm_g.shape[1] == 3 and ffn_w_in.shape[-1] == 2 * D_FF
    assert 1 + n_lat_seq <= MOD_ROWS
    n_ctx = batch * t_ctx
    n_tok = n_ctx + n_lat_seq * t_lat
    assert n_ctx % t_lat == 0
    segs = _Segs(n_ctx, t_ctx, n_lat_seq, t_lat)
    pairs = _Pairs(n_ctx, t_ctx, n_lat_seq, t_lat)
    tiles = _Tiles(n_ctx, t_lat)

    cond = jnp.concatenate([c_ctx[None, :], c, jnp.zeros((MOD_ROWS - 1 - n_lat_seq, D_MODEL), F32)], axis=0)
    mod = _modulation(cond, ada_w, ada_b)

    wfi = _bf(ffn_w_in).reshape(depth * 2, D_MODEL, 2 * D_FF)
    wfo = _bf(ffn_w_out).reshape(depth * 2, D_FF, D_MODEL)
    wz = _permute_w_in(w_in)
    wo = _bf(w_out)
    qn = jnp.tile(attn_q_norm, (1, A_HEADS)).reshape(depth, 1, A_WIDTH)
    kn = jnp.tile(attn_k_norm, (1, A_KV_HEADS)).reshape(depth, 1, KV_WIDTH)
    gn = jnp.tile(delta_norm, (1, N_HEADS)).reshape(depth, 1, SCAN_W)
    mn = mlstm_norm.reshape(depth, 1, SCAN_W)
    fin = final_norm.reshape(1, D_MODEL)
    fb_cols = mlstm_f_bias.reshape(depth, N_GATE_ROWS, 1)
    alog_cols = delta_a_log.reshape(depth, N_GATE_ROWS, 1)
    dtb_cols = delta_dt_bias.reshape(depth, N_GATE_ROWS, 1)
    cos_t, sin_t = _rope_tables(t_lat)
    ck = cache_k.reshape(cache_k.shape[:3] + (KV_WIDTH,))
    cv = cache_v.reshape(cache_v.shape[:3] + (KV_WIDTH,))
    n0 = state_mlstm_n.reshape(n_lat_seq, depth, N_DIR, 1, SCAN_W)
    m0 = state_mlstm_m.reshape(n_lat_seq, depth, N_GATE_ROWS, 1)

    x_ctx, x_lat, lat_tile0 = x_prompt.reshape(n_ctx, D_MODEL), x_sample.reshape(n_tok - n_ctx, D_MODEL), 0
    ks, vs, cs, ns, ms, ss = [], [], [], [], [], []
    for l in range(depth):
        x1, zm, zg, aq, ak, av, gates = _dense1(l, tiles, n_tok, x_ctx, x_lat, lat_tile0, mod, norm_g,
                                                wfi, wfo, wz, qn, kn, cos_t, sin_t)
        zm_v, zg_v, gates_v = pairs.view(zm), pairs.view(zg), pairs.view(gates)
        hf, hb, c_new, n_new, m_new = _mlstm(l, pairs, zm_v, gates_v, fb_cols, state_mlstm_C, n0, m0)
        o_f, o_b, s_new = _delta(l, pairs, zg_v, gates_v, delta_conv, alog_cols, dtb_cols, state_delta_S)
        hf, hb, o_f, o_b = (a.reshape(n_tok, SCAN_W) for a in (hf, hb, o_f, o_b))
        a_out = _attention(l, segs, aq, ak, av, ck, cv)
        x = _dense2(l, tiles, x1, mod, norm_g, hf, hb, zm, mn, o_f, o_b, zg, gn, a_out, wo, wfi, wfo, fin,
                    final=(l == depth - 1))
        if l < depth - 1:
            x_ctx, x_lat, lat_tile0 = x, x, tiles.n_ctx

        ks.append(ak[:n_ctx].reshape(batch, t_ctx, A_KV_HEADS, HEAD_DIM))
        vs.append(av[:n_ctx].reshape(batch, t_ctx, A_KV_HEADS, HEAD_DIM))
        cs.append(pairs.ctx_states(c_new))
        ns.append(pairs.ctx_states(n_new).reshape(batch, N_DIR, N_HEADS, HEAD_DIM))
        ms.append(pairs.ctx_states(m_new).reshape(batch, N_DIR, N_HEADS))
        ss.append(pairs.ctx_states(s_new))

    y_prompt = x[0].reshape(batch, t_ctx, D_MODEL)
    y_sample = x[1].reshape(n_lat_seq, t_lat, D_MODEL)
    stack = lambda xs: jnp.stack(xs, axis=1)
    return (y_prompt, y_sample, stack(ks), stack(vs), stack(cs), stack(ns), stack(ms), stack(ss))
```

```python
import functools
import math

import numpy as np

import jax
import jax.numpy as jnp
from jax import lax
from jax.experimental import pallas as pl
from jax.experimental.pallas import tpu as pltpu

F32 = jnp.float32
BF16 = jnp.bfloat16

D_MODEL = 1024
HEAD_DIM = 64
N_HEADS = 4
SCAN_W = N_HEADS * HEAD_DIM
A_HEADS = 8
A_KV_HEADS = 2
A_GROUP = A_HEADS // A_KV_HEADS
A_WIDTH = A_HEADS * HEAD_DIM
KV_WIDTH = A_KV_HEADS * HEAD_DIM
N_DIR = 2
CHUNK = 64
CONV_W = 5
D_FF = 2816
FF_CHUNK = 256
GRID_W = 64
ROPE_BASE = 10000.0
EPS = 1e-6
N_MOD = 9
MOD_ROWS = 8
GATE_W = 128
Z_WIDTH = 2 * D_MODEL + A_WIDTH + 2 * KV_WIDTH + GATE_W

SEG = 256
N_CHUNK = SEG // CHUNK
TM = 512
N_PAIR = 2
HALO = 8
KV_BLOCK = 512
BF16_ROWS = 16
VMEM_LIMIT = 56 * 1024 * 1024
SCAN_VMEM_LIMIT = 62 * 1024 * 1024


def _bf(x):
    return x.astype(BF16)


def _dot(a, b):
    return jnp.dot(_bf(a), _bf(b), preferred_element_type=F32)


def _bdot(a, b):
    return lax.dot_general(a, b, (((2,), (1,)), ((0,), (0,))), preferred_element_type=F32)


def _bdot_nt(a, b):
    return lax.dot_general(a, b, (((2,), (2,)), ((0,), (0,))), preferred_element_type=F32)


def _split(a, terms):
    parts = []
    rest = a
    for _ in range(terms):
        p = _bf(rest)
        parts.append(p)
        rest = rest - p.astype(F32)
    return parts


def _dot_exact_rhs(a, b01, terms=2):
    parts = _split(a, terms)
    if a.shape[-1] % 128 == 0:
        return jnp.dot(jnp.concatenate(parts, axis=-1), jnp.concatenate([b01] * terms, axis=0),
                       preferred_element_type=F32)
    out = None
    for p in parts:
        d = jnp.dot(p, b01, preferred_element_type=F32)
        out = d if out is None else out + d
    return out


def _iota(shape, dim):
    return lax.broadcasted_iota(jnp.int32, shape, dim)


def _head_of(idx):
    return jnp.right_shift(idx, HEAD_DIM.bit_length() - 1)


def _in_head(idx):
    return jnp.bitwise_and(idx, HEAD_DIM - 1)


def _block_ones(n):
    return (_head_of(_iota((n, n), 0)) == _head_of(_iota((n, n), 1)))


def _head_sum(x):
    n = x.shape[-1]
    return _dot_exact_rhs(x, _block_ones(n).astype(BF16), terms=1)


def _head_rms(x, g_row):
    ms = _head_sum(x * x) * (1.0 / HEAD_DIM)
    return x * lax.rsqrt(ms + EPS) * g_row


def _rms_mod(x, g_row, scale_row, shift_row):
    y = x * lax.rsqrt(jnp.mean(x * x, axis=-1, keepdims=True) + EPS) * g_row
    return y * (1.0 + scale_row) + shift_row


def _softplus(x):
    return jnp.maximum(x, 0.0) + jnp.log1p(jnp.exp(-jnp.abs(x)))


def _log_sigmoid(x):
    return -_softplus(-x)


def _sigmoid(x):
    return jax.nn.sigmoid(x)


def _silu(x):
    return x * jax.nn.sigmoid(x)


def _bd3(x, mask_bf):
    return jnp.concatenate([x] * N_HEADS, axis=1) * mask_bf


def _to_block_diag(x4):
    spread = (_iota((HEAD_DIM, SCAN_W), 0) == _in_head(_iota((HEAD_DIM, SCAN_W), 1))).astype(BF16)
    rows = x4.reshape(SCAN_W, HEAD_DIM)
    return jnp.where(_block_ones(SCAN_W), _dot_exact_rhs(rows, spread, terms=3), 0.0)


def _from_block_diag(x):
    fold = (_in_head(_iota((SCAN_W, HEAD_DIM), 0)) == _iota((SCAN_W, HEAD_DIM), 1)).astype(BF16)
    return _dot_exact_rhs(x, fold, terms=3).reshape(N_HEADS, HEAD_DIM, HEAD_DIM)


def _chunks(a):
    return a.reshape(N_CHUNK, CHUNK, SCAN_W)


def _problem_masks():
    shape = (N_PAIR * N_DIR * N_CHUNK, CHUNK, SCAN_W)
    t_idx, j_idx = _iota(shape, 1), _in_head(_iota(shape, 2))
    is_rev = jnp.bitwise_and(_iota(shape, 0), N_DIR * N_CHUNK - 1) >= N_CHUNK
    ahead = jnp.where(is_rev, j_idx - t_idx, t_idx - j_idx)
    return t_idx == j_idx, ahead >= 0, ahead > 0


def _scan_order(d):
    return range(N_CHUNK - 1, -1, -1) if d else range(N_CHUNK)


def _mod_kernel(cond_ref, w_ref, b_ref, o_ref):
    a = _silu(cond_ref[...])
    o_ref[0, 0] = _dot(a, w_ref[0]) + b_ref[0]


def _modulation(cond, ada_w, ada_b):
    depth = ada_w.shape[0]
    return pl.pallas_call(
        _mod_kernel,
        grid=(depth, N_MOD),
        in_specs=[
            pl.BlockSpec((MOD_ROWS, D_MODEL), lambda l, j: (0, 0)),
            pl.BlockSpec((1, D_MODEL, D_MODEL), lambda l, j: (l, 0, j)),
            pl.BlockSpec((1, 1, D_MODEL), lambda l, j: (l, 0, j)),
        ],
        out_specs=pl.BlockSpec((1, 1, MOD_ROWS, D_MODEL), lambda l, j: (l, j, 0, 0)),
        out_shape=jax.ShapeDtypeStruct((depth, N_MOD, MOD_ROWS, D_MODEL), F32),
        compiler_params=pltpu.CompilerParams(
            dimension_semantics=("arbitrary", "arbitrary"), vmem_limit_bytes=VMEM_LIMIT),
        name="modulation",
    )(cond, ada_w, ada_b.reshape(depth, 1, N_MOD * D_MODEL))


def _ffn(h, w_in_ref, w_out_ref):
    acc = None
    for c in range(D_FF // FF_CHUNK):
        lo = c * FF_CHUNK
        g = jnp.dot(h, w_in_ref[0, :, lo:lo + FF_CHUNK], preferred_element_type=F32)
        u = jnp.dot(h, w_in_ref[0, :, D_FF + lo:D_FF + lo + FF_CHUNK], preferred_element_type=F32)
        a = _bf(_silu(g) * u)
        d = jnp.dot(a, w_out_ref[0, lo:lo + FF_CHUNK, :], preferred_element_type=F32)
        acc = d if acc is None else acc + d
    return acc


def _rope(x, cos, sin_signed):
    w = x.shape[-1]
    reps = w // cos.shape[-1]
    if reps > 1:
        cos = jnp.concatenate([cos] * reps, axis=-1)
        sin_signed = jnp.concatenate([sin_signed] * reps, axis=-1)
    n_freq = HEAD_DIM // 4
    first = jnp.bitwise_and(_iota(x.shape, 1), 2 * n_freq - 1) < n_freq
    partner = jnp.where(first, pltpu.roll(x, w - n_freq, 1), pltpu.roll(x, n_freq, 1))
    return x * cos + partner * sin_signed


class _Tiles:
    def __init__(self, n_ctx, t_lat):
        assert n_ctx % TM == 0 and t_lat % TM == 0
        self.n_ctx = n_ctx // TM
        self.per_seq = t_lat // TM

    def is_ctx(self, i):
        return i < self.n_ctx

    def lat(self, i):
        return jnp.maximum(i - self.n_ctx, 0)

    def mod_row(self, i):
        return jnp.where(self.is_ctx(i), 0, 1 + self.lat(i) // self.per_seq)


def _mod_rows(mod_ref, row):
    return lambda m: mod_ref[0, m, pl.ds(row, 1), :]


def _dense1_kernel(xa_ref, xb_ref, mod_ref, ng_ref, wfi_ref, wfo_ref, wz_ref, qn_ref, kn_ref, cos_ref, sin_ref,
                   x1_ref, zm_ref, zg_ref, aq_ref, ak_ref, av_ref, gate_ref, *, tiles):
    i = pl.program_id(0)
    mod = _mod_rows(mod_ref, tiles.mod_row(i))
    x = jnp.where(tiles.is_ctx(i), xa_ref[...], xb_ref[...])
    h = _bf(_rms_mod(x, ng_ref[0, 0:1], mod(0), mod(1)))
    x1 = x + 0.5 * mod(2) * _ffn(h, wfi_ref, wfo_ref)
    x1_ref[...] = x1
    h2 = _bf(_rms_mod(x1, ng_ref[0, 1:2], mod(3), mod(4)))
    z = jnp.dot(h2, wz_ref[0], preferred_element_type=F32)

    def proj(lo, width):
        return z[:, lo:lo + width]

    zm_ref[...] = proj(0, D_MODEL)
    zg_ref[...] = proj(D_MODEL, D_MODEL)
    cos = cos_ref[...]
    sin = sin_ref[...]
    off = 2 * D_MODEL
    q = _head_rms(proj(off, A_WIDTH), qn_ref[0])
    aq_ref[...] = _rope(q, cos, sin) * (1.0 / math.sqrt(HEAD_DIM))
    k = _head_rms(proj(off + A_WIDTH, KV_WIDTH), kn_ref[0])
    ak_ref[...] = _rope(k, cos, sin)
    av_ref[...] = proj(off + A_WIDTH + KV_WIDTH, KV_WIDTH)
    gate_ref[...] = proj(off + A_WIDTH + 2 * KV_WIDTH, GATE_W)


def _resident(shape, index_map):
    return pl.BlockSpec(shape, index_map, pipeline_mode=pl.Buffered(1))


def _dense1(layer, tiles, n_tok, x_ctx, x_lat, lat_tile0, mod, norm_g, wfi, wfo, wz, qn, kn, cos_t, sin_t):
    row = lambda w: pl.BlockSpec((TM, w), lambda i: (i, 0))
    rope_tile = lambda i: jnp.where(tiles.is_ctx(i), tiles.per_seq, tiles.lat(i) % tiles.per_seq)
    rope = pl.BlockSpec((TM, KV_WIDTH), lambda i: (rope_tile(i), 0))
    out_w = (D_MODEL, D_MODEL, D_MODEL, A_WIDTH, KV_WIDTH, KV_WIDTH, GATE_W)
    return pl.pallas_call(
        functools.partial(_dense1_kernel, tiles=tiles),
        grid=(n_tok // TM,),
        in_specs=[
            pl.BlockSpec((TM, D_MODEL), lambda i: (jnp.minimum(i, tiles.n_ctx - 1), 0)),
            pl.BlockSpec((TM, D_MODEL), lambda i: (tiles.lat(i) + lat_tile0, 0)),
            pl.BlockSpec((1, N_MOD, MOD_ROWS, D_MODEL), lambda i: (layer, 0, 0, 0)),
            pl.BlockSpec((1, 3, D_MODEL), lambda i: (layer, 0, 0)),
            _resident((1, D_MODEL, 2 * D_FF), lambda i: (2 * layer, 0, 0)),
            _resident((1, D_FF, D_MODEL), lambda i: (2 * layer, 0, 0)),
            _resident((1, D_MODEL, Z_WIDTH), lambda i: (layer, 0, 0)),
            pl.BlockSpec((1, 1, A_WIDTH), lambda i: (layer, 0, 0)),
            pl.BlockSpec((1, 1, KV_WIDTH), lambda i: (layer, 0, 0)),
            rope, rope,
        ],
        out_specs=[row(w) for w in out_w],
        out_shape=[jax.ShapeDtypeStruct((n_tok, w), F32) for w in out_w],
        compiler_params=pltpu.CompilerParams(
            dimension_semantics=("arbitrary",), vmem_limit_bytes=VMEM_LIMIT),
        name="dense1",
    )(x_ctx, x_lat, mod, norm_g, wfi, wfo, wz, qn, kn, cos_t, sin_t)


def _dense2_kernel(x_ref, mod_ref, ng_ref, hf_ref, hb_ref, mo_ref, mn_ref, of_ref, ob_ref, gz_ref, gn_ref,
                   ao_ref, wo_ref, wfi_ref, wfo_ref, fin_ref, *out_refs, tiles, final):
    i = pl.program_id(0)
    mod = _mod_rows(mod_ref, tiles.mod_row(i))
    m_out = _head_rms(hf_ref[...] + hb_ref[...], mn_ref[0]) * _sigmoid(mo_ref[...])
    g_out = _head_rms(of_ref[...] + ob_ref[...], gn_ref[0]) * _silu(gz_ref[...])
    mix = jnp.dot(_bf(m_out), wo_ref[0, 0:SCAN_W, :], preferred_element_type=F32)
    mix += jnp.dot(_bf(g_out), wo_ref[0, SCAN_W:2 * SCAN_W, :], preferred_element_type=F32)
    mix += jnp.dot(_bf(ao_ref[...]), wo_ref[0, 2 * SCAN_W:, :], preferred_element_type=F32)
    x2 = x_ref[...] + mod(5) * mix
    h = _bf(_rms_mod(x2, ng_ref[0, 2:3], mod(6), mod(7)))
    x3 = x2 + 0.5 * mod(8) * _ffn(h, wfi_ref, wfo_ref)
    if not final:
        out_refs[0][...] = x3
        return
    y = x3 * lax.rsqrt(jnp.mean(x3 * x3, axis=-1, keepdims=True) + EPS) * fin_ref[...]
    y_ctx_ref, y_lat_ref = out_refs

    @pl.when(tiles.is_ctx(i))
    def _():
        y_ctx_ref[...] = y

    @pl.when(jnp.logical_not(tiles.is_ctx(i)))
    def _():
        y_lat_ref[...] = y


def _dense2(layer, tiles, x1, mod, norm_g, hf, hb, zm, m_norm, of, ob, zg, g_norm, a_out, wo, wfi, wfo, fin,
            final):
    n_tok = x1.shape[0]
    row = lambda w: pl.BlockSpec((TM, w), lambda i: (i, 0))
    last_quarter = pl.BlockSpec((TM, SCAN_W), lambda i: (i, 3))
    lane_row = lambda w: pl.BlockSpec((1, 1, w), lambda i: (layer, 0, 0))
    if final:
        n_ctx = tiles.n_ctx * TM
        out_specs = [pl.BlockSpec((TM, D_MODEL), lambda i: (jnp.minimum(i, tiles.n_ctx - 1), 0)),
                     pl.BlockSpec((TM, D_MODEL), lambda i: (tiles.lat(i), 0))]
        out_shape = [jax.ShapeDtypeStruct((n_ctx, D_MODEL), F32),
                     jax.ShapeDtypeStruct((n_tok - n_ctx, D_MODEL), F32)]
    else:
        out_specs, out_shape = row(D_MODEL), jax.ShapeDtypeStruct((n_tok, D_MODEL), F32)
    return pl.pallas_call(
        functools.partial(_dense2_kernel, tiles=tiles, final=final),
        grid=(n_tok // TM,),
        in_specs=[
            row(D_MODEL),
            pl.BlockSpec((1, N_MOD, MOD_ROWS, D_MODEL), lambda i: (layer, 0, 0, 0)),
            pl.BlockSpec((1, 3, D_MODEL), lambda i: (layer, 0, 0)),
            row(SCAN_W), row(SCAN_W), last_quarter, lane_row(SCAN_W),
            row(SCAN_W), row(SCAN_W), last_quarter, lane_row(SCAN_W),
            row(A_WIDTH),
            _resident((1, D_MODEL, D_MODEL), lambda i: (layer, 0, 0)),
            _resident((1, D_MODEL, 2 * D_FF), lambda i: (2 * layer + 1, 0, 0)),
            _resident((1, D_FF, D_MODEL), lambda i: (2 * layer + 1, 0, 0)),
            pl.BlockSpec((1, D_MODEL), lambda i: (0, 0)),
        ],
        out_specs=out_specs,
        out_shape=out_shape,
        compiler_params=pltpu.CompilerParams(
            dimension_semantics=("arbitrary",), vmem_limit_bytes=VMEM_LIMIT),
        name="dense2",
    )(x1, mod, norm_g, hf, hb, zm, m_norm, of, ob, zg, g_norm, a_out, wo, wfi, wfo, fin)


class _Segs:
    def __init__(self, n_ctx, t_ctx, n_lat_seq, t_lat):
        assert t_ctx == SEG and t_lat % SEG == 0
        self.n_ctx_seg = n_ctx // SEG
        self.per_seq = t_lat // SEG
        self.n_seg = self.n_ctx_seg + n_lat_seq * self.per_seq

    def is_ctx(self, s):
        return s < self.n_ctx_seg

    def lat_seq(self, s):
        return jnp.maximum(s - self.n_ctx_seg, 0) // self.per_seq


_STATE_MAT = (N_DIR, N_HEADS, HEAD_DIM, HEAD_DIM)


class _Pairs:
    def __init__(self, n_ctx, t_ctx, n_lat_seq, t_lat):
        assert t_ctx == SEG and t_lat % SEG == 0 and n_ctx == N_PAIR * t_lat and n_lat_seq % N_PAIR == 0
        self.rows = t_lat
        self.per_seq = t_lat // SEG
        self.n_ctx_step = t_lat // t_ctx
        self.n_step = self.n_ctx_step + (n_lat_seq // N_PAIR) * self.per_seq
        self.n_slot = self.n_ctx_step + n_lat_seq // N_PAIR

    def view(self, x):
        return x.reshape(x.shape[0] // self.rows, self.rows, x.shape[1])

    def is_ctx(self, s):
        return s < self.n_ctx_step

    def lat(self, s):
        r = jnp.maximum(s - self.n_ctx_step, 0)
        return r // self.per_seq, r % self.per_seq

    def macro(self, s):
        return jnp.where(self.is_ctx(s), 0, 1 + self.lat(s)[0])

    def seg(self, s, rev):
        j = self.lat(s)[1]
        return jnp.where(self.is_ctx(s), s, self.per_seq - 1 - j if rev else j)

    def slot(self, s):
        return jnp.where(self.is_ctx(s), s, self.n_ctx_step + self.lat(s)[0])

    def first(self, s):
        return jnp.logical_or(self.is_ctx(s), self.lat(s)[1] == 0)

    def last(self, s):
        return jnp.logical_or(self.is_ctx(s), self.lat(s)[1] == self.per_seq - 1)

    def block(self, rev, w):
        return pl.BlockSpec((N_PAIR, SEG, w), lambda s: (self.macro(s), self.seg(s, rev), 0))

    def state_in(self, layer, tail):
        zeros = (0,) * len(tail)
        return pl.BlockSpec((N_PAIR, 1) + tail, lambda s: (self.lat(s)[0], layer) + zeros)

    def state_out(self, tail):
        zeros = (0,) * len(tail)
        spec = pl.BlockSpec((1, N_PAIR) + tail, lambda s: (self.slot(s), 0) + zeros)
        return spec, jax.ShapeDtypeStruct((self.n_slot, N_PAIR) + tail, F32)

    def ctx_states(self, x):
        ctx = jnp.swapaxes(x[:self.n_ctx_step], 0, 1)
        return ctx.reshape((N_PAIR * self.n_ctx_step,) + x.shape[2:])


N_GATE_ROWS = N_DIR * N_HEADS
ALL_ROWS = N_PAIR * N_GATE_ROWS


def _lane_scan(x, op, ident, rev):
    n = x.shape[-1]
    pos = _in_head(_iota(x.shape, x.ndim - 1))
    s = 1
    while s < CHUNK:
        if rev:
            y, ok = pltpu.roll(x, n - s, x.ndim - 1), pos < CHUNK - s
        else:
            y, ok = pltpu.roll(x, s, x.ndim - 1), pos >= s
        x = op(x, jnp.where(ok, y, ident))
        s *= 2
    return x


def _pieces(x, terms):
    out = []
    for _ in range(terms):
        p = _bf(x).astype(F32)
        out.append(p)
        x = x - p
    return jnp.concatenate(out, axis=0)


def _expander(n_qty):
    k = np.arange(SCAN_W)[:, None]
    n = np.arange(N_PAIR * N_DIR * n_qty * SCAN_W)[None, :]
    rows_per_piece = n_qty * ALL_ROWS
    assert 3 * rows_per_piece <= SCAN_W
    qty, row = (k % rows_per_piece) // ALL_ROWS, k % ALL_ROWS
    grp, qty_n, head = n // (n_qty * SCAN_W), (n // SCAN_W) % n_qty, (n % SCAN_W) // HEAD_DIM
    used = k < 3 * rows_per_piece
    return jnp.asarray(used & (qty == qty_n) & (row == grp * N_HEADS + head), BF16)


def _expand(quantities, expand_ref):
    n_qty = len(quantities)
    packed = jnp.concatenate(quantities, axis=0)
    pad = jnp.zeros((SCAN_W - 3 * packed.shape[0], SEG), F32)
    lhs = _bf(jnp.concatenate([_pieces(packed, 3), pad], axis=0).T)
    wide = jnp.dot(lhs, expand_ref[...], preferred_element_type=F32)

    def stack(j):
        cols = [wide[:, (g * n_qty + j) * SCAN_W:(g * n_qty + j + 1) * SCAN_W] for g in range(N_PAIR * N_DIR)]
        return jnp.concatenate([_chunks(x) for x in cols], axis=0)

    return [stack(j) for j in range(n_qty)]


def _gate_rows(gf_ref, gb_ref):
    is_b = jnp.bitwise_and(_iota((ALL_ROWS, SEG), 0), N_GATE_ROWS - 1) >= N_HEADS
    is_b8 = _iota((N_GATE_ROWS, SEG), 0) >= N_HEADS
    gt = [(gf_ref[m].T, gb_ref[m].T) for m in range(N_PAIR)]

    def pick(lo):
        return jnp.concatenate([jnp.where(is_b8, b[lo:lo + N_GATE_ROWS], f[lo:lo + N_GATE_ROWS])
                                for f, b in gt], axis=0)

    return is_b, pick


def _chunk_sums(x, is_b):
    s_idx, t_idx = _iota((SEG, 2 * SEG), 0), _iota((SEG, 2 * SEG), 1)
    t_loc = jnp.bitwise_and(t_idx, SEG - 1)
    tri = jnp.logical_and(_head_of(s_idx) == _head_of(t_loc),
                          jnp.where(t_idx >= SEG, s_idx - t_loc, t_loc - s_idx) >= 0).astype(BF16)
    sums = jnp.dot(_bf(_pieces(x, 3)), tri, preferred_element_type=F32)
    sums = sums[0:ALL_ROWS] + sums[ALL_ROWS:2 * ALL_ROWS] + sums[2 * ALL_ROWS:]
    return jnp.where(is_b, sums[:, SEG:], sums[:, 0:SEG])


def _on_lanes(cols, mirrored, is_b):
    chunk_of = _head_of(_iota((ALL_ROWS, SEG), 1))
    out = jnp.zeros((ALL_ROWS, SEG), F32)
    for c in range(N_CHUNK):
        col = jnp.where(is_b, cols[N_CHUNK - 1 - c], cols[c]) if mirrored else cols[c]
        out = jnp.where(chunk_of == c, col, out)
    return out


def _mlstm_compact(gf_ref, gb_ref, fb_col, m_col, expand2_ref, expand4_ref):
    is_b, pick = _gate_rows(gf_ref, gb_ref)
    ig = pick(0)
    lf = _log_sigmoid(pick(N_GATE_ROWS) + fb_col)
    b = _chunk_sums(lf, is_b)
    r = ig - b
    cm = jnp.where(is_b, _lane_scan(r, jnp.maximum, -jnp.inf, True), _lane_scan(r, jnp.maximum, -jnp.inf, False))
    on_lanes = functools.partial(_on_lanes, is_b=is_b)
    in_chunk = lambda x, c: x[:, c * CHUNK:(c + 1) * CHUNK]
    bl_cols = [jnp.sum(in_chunk(lf, c), axis=1, keepdims=True) for c in range(N_CHUNK)]
    bl = on_lanes(bl_cols, False)
    lw = bl - b + ig
    lw_cols = [jnp.max(in_chunk(lw, c), axis=1, keepdims=True) for c in range(N_CHUNK)]

    is_b_col = is_b[:, 0:1]
    at_step = lambda cols, i: jnp.where(is_b_col, cols[N_CHUNK - 1 - i], cols[i])
    m_in, m_out = [], []
    for i in range(N_CHUNK):
        m_in.append(m_col)
        m_col = jnp.maximum(at_step(bl_cols, i) + m_col, at_step(lw_cols, i))
        m_out.append(m_col)
    m_c, m_n = on_lanes(m_in, True), on_lanes(m_out, True)
    kw_fac, dec = _expand([jnp.exp(lw - m_n), jnp.exp(bl + m_c - m_n)], expand2_ref)
    mx = jnp.maximum(m_c, cm)
    r, mx, a_int, neg_mt = _expand([r, mx, jnp.exp(m_c - mx), -(b + mx)], expand4_ref)
    return (r, mx, a_int, neg_mt, kw_fac, dec), m_col


def _mlstm_kernel(zmf_ref, zmb_ref, gf_ref, gb_ref, fb_ref, expand2_ref, expand4_ref, c0_ref, n0_ref, m0_ref,
                  hf_ref, hb_ref, c_out, n_out, m_out, c_s, n_s, m_s, *, pairs):
    s = pl.program_id(0)
    members = range(N_PAIR)

    @pl.when(pairs.first(s))
    def _():
        ctx = pairs.is_ctx(s)
        for m in members:
            m_s[m * N_GATE_ROWS:(m + 1) * N_GATE_ROWS] = jnp.where(ctx, 0.0, m0_ref[m, 0])
            for d in range(N_DIR):
                c_s[m, d] = jnp.where(ctx, 0.0, _to_block_diag(c0_ref[m, 0, d]))
                n_s[m, d] = jnp.where(ctx, 0.0, n0_ref[m, 0, d])

    fb_col = jnp.concatenate([fb_ref[0]] * N_PAIR, axis=0)
    (r, mx, a_int, neg_mt, kw_fac, dec), m_fin = _mlstm_compact(gf_ref, gb_ref, fb_col, m_s[...],
                                                                expand2_ref, expand4_ref)
    col = lambda j: jnp.concatenate([_chunks(ref[m, :, j * SCAN_W:(j + 1) * SCAN_W])
                                     for m in members for ref in (zmf_ref, zmb_ref)], axis=0)
    q, k, v = col(0), col(1) * (1.0 / math.sqrt(HEAD_DIM)), col(2)

    diag, tri, _ = _problem_masks()
    mask_bd = _block_ones(SCAN_W)
    mask_bf = mask_bd.astype(BF16)

    r_row = jnp.sum(jnp.where(diag, r, 0.0), axis=1, keepdims=True)
    w = jnp.exp(jnp.where(tri, r_row - mx, -jnp.inf))
    sc = _bdot_nt(_bf(q), _bd3(_bf(k), mask_bf)) * w
    kw = kw_fac * k
    kw_sum = jnp.sum(kw, axis=1, keepdims=True)

    n_prob = N_PAIR * N_DIR * N_CHUNK
    c_at, n_at = [None] * n_prob, [None] * n_prob
    c_fin, n_fin = {}, {}
    for m in members:
        for d in range(N_DIR):
            c_bd, n_row = c_s[m, d], n_s[m, d]
            for c in _scan_order(d):
                p = (m * N_DIR + d) * N_CHUNK + c
                c_at[p], n_at[p] = c_bd, n_row
                dec_row = dec[p, 0:1, :]
                outer = lax.dot_general(_bf(kw[p]), _bf(v[p]), (((0,), (0,)), ((), ())),
                                        preferred_element_type=F32)
                c_bd = c_bd * dec_row + jnp.where(mask_bd, outer, 0.0)
                n_row = n_row * dec_row + kw_sum[p]
            c_fin[m, d], n_fin[m, d] = c_bd, n_row
    c_all = jnp.concatenate([_bf(x)[None] for x in c_at], axis=0)
    n_all = jnp.concatenate([x[None] for x in n_at], axis=0)

    num = _bdot(_bf(sc), _bd3(_bf(v), mask_bf)) + a_int * _bdot(_bf(q), c_all)
    n_rows = n_prob * CHUNK
    flat = lambda a: a.reshape(n_rows, SCAN_W)
    sums = _dot_exact_rhs(jnp.concatenate([flat(sc), flat(q * n_all)], axis=0), mask_bf)
    den = sums[0:n_rows].reshape(sc.shape) + a_int * sums[n_rows:].reshape(sc.shape)
    h = num / jnp.maximum(jnp.abs(den), jnp.exp(neg_mt))
    for m in members:
        for d, out_ref in enumerate((hf_ref, hb_ref)):
            p0 = (m * N_DIR + d) * N_CHUNK
            out_ref[m] = h[p0:p0 + N_CHUNK].reshape(SEG, SCAN_W)

    m_s[...] = m_fin
    for m in members:
        for d in range(N_DIR):
            c_s[m, d] = c_fin[m, d]
            n_s[m, d] = n_fin[m, d]

    @pl.when(pairs.last(s))
    def _():
        for m in members:
            m_out[0, m] = m_s[m * N_GATE_ROWS:(m + 1) * N_GATE_ROWS]
            for d in range(N_DIR):
                c_out[0, m, d] = _from_block_diag(c_fin[m, d])
                n_out[0, m, d] = n_fin[m, d]


def _mlstm(layer, pairs, zm, gates, fb_cols, state_c, n0, m0):
    row_tail, col_tail = (N_DIR, 1, SCAN_W), (N_GATE_ROWS, 1)
    c_spec, c_shape = pairs.state_out(_STATE_MAT)
    n_spec, n_shape = pairs.state_out(row_tail)
    m_spec, m_shape = pairs.state_out(col_tail)
    expand2, expand4 = _expander(2), _expander(4)
    h_shape = jax.ShapeDtypeStruct(zm.shape[:2] + (SCAN_W,), F32)
    return dict(
        kernel=functools.partial(_mlstm_kernel, pairs=pairs),
        in_specs=[
            pairs.block(False, D_MODEL), pairs.block(True, D_MODEL),
            pairs.block(False, GATE_W), pairs.block(True, GATE_W),
            pl.BlockSpec((1, N_GATE_ROWS, 1), lambda s: (layer, 0, 0)),
            _resident(expand2.shape, lambda s: (0, 0)),
            _resident(expand4.shape, lambda s: (0, 0)),
            pairs.state_in(layer, _STATE_MAT), pairs.state_in(layer, row_tail), pairs.state_in(layer, col_tail),
        ],
        out_specs=[pairs.block(False, SCAN_W), pairs.block(True, SCAN_W), c_spec, n_spec, m_spec],
        out_shape=[h_shape, h_shape, c_shape, n_shape, m_shape],
        scratch_shapes=[pltpu.VMEM((N_PAIR, N_DIR, SCAN_W, SCAN_W), F32),
                        pltpu.VMEM((N_PAIR, N_DIR, 1, SCAN_W), F32),
                        pltpu.VMEM((ALL_ROWS, 1), F32)],
        operands=(zm, zm, gates, gates, fb_cols, expand2, expand4, state_c, n0, m0))


def _short_conv(prev_ref, cur_ref, next_ref, w_ref, has_prev, has_next):
    w3 = 3 * SCAN_W
    prev = jnp.where(has_prev, prev_ref[:, 0:w3], 0.0)
    nxt = jnp.where(has_next, next_ref[:, 0:w3], 0.0)
    xp = jnp.concatenate([prev, cur_ref[:, 0:w3], nxt], axis=0)
    rows = xp.shape[0]
    acc = None
    for i in range(CONV_W):
        shift = (CONV_W // 2 - i) % rows
        y = xp if shift == 0 else pltpu.roll(xp, shift, 0)
        t = y[HALO:HALO + SEG] * w_ref[0, i:i + 1, :]
        acc = t if acc is None else acc + t
    return acc


def _neumann_inverse(n_all, mask_bf):
    eye = (_iota(n_all.shape[1:], 0) == _in_head(_iota(n_all.shape[1:], 1))).astype(F32)
    p = -n_all
    t = eye + p
    levels = 6
    for lvl in range(levels):
        first, last = lvl == 0, lvl == levels - 1
        p_hi, p_lo = _split(p, 2)
        w_hi, w_lo = _bd3(p_hi, mask_bf), _bd3(p_lo, mask_bf)
        lhs_hi, lhs_lo = [], []
        if not first:
            t_hi, t_lo = _split(t, 2)
            lhs_hi += [t_hi, t_lo]
            lhs_lo += [t_hi]
        if not last:
            lhs_hi += [p_hi, p_lo]
            lhs_lo += [p_hi]
        a = _bdot(jnp.concatenate(lhs_hi, axis=1), w_hi)
        b = _bdot(jnp.concatenate(lhs_lo, axis=1) if len(lhs_lo) > 1 else lhs_lo[0], w_lo)
        ra, rb = 0, 0
        if not first:
            t = t + (a[:, 0:CHUNK] + a[:, CHUNK:2 * CHUNK] + b[:, 0:CHUNK])
            ra, rb = 2 * CHUNK, CHUNK
        if not last:
            p = a[:, ra:ra + CHUNK] + a[:, ra + CHUNK:ra + 2 * CHUNK] + b[:, rb:rb + CHUNK]
    return t


def _delta_qkv(prev_ref, cur_ref, next_ref, cw_ref, has_prev, has_next):
    qkv = _silu(_short_conv(prev_ref, cur_ref, next_ref, cw_ref, has_prev, has_next))

    def l2(a):
        return a * lax.rsqrt(_head_sum(a * a) + EPS)

    q = l2(qkv[:, 0:SCAN_W]) * (1.0 / math.sqrt(HEAD_DIM))
    k = l2(qkv[:, SCAN_W:2 * SCAN_W])
    return jnp.concatenate([q, k, qkv[:, 2 * SCAN_W:3 * SCAN_W]], axis=1)


def _delta_stage_qkv(s, pairs, blocks, cw_ref, qkv_s, seen_s):
    members = range(N_PAIR)
    lat = jnp.logical_not(pairs.is_ctx(s))
    j = pairs.lat(s)[1]
    mirror = pairs.per_seq - 1 - j
    half = pairs.per_seq // 2

    @pl.when(pairs.is_ctx(s))
    def _():
        for m in members:
            pf, cf, nf = (r.at[m] for r in blocks[0])
            x = _delta_qkv(pf, cf, nf, cw_ref, False, False)
            qkv_s[m, 0] = x
            qkv_s[m, 1] = x

    @pl.when(jnp.logical_and(lat, j < half))
    def _():
        for m in members:
            (pf, cf, nf), (pb, cb, nb) = ((r.at[m] for r in refs) for refs in blocks)
            xf = _delta_qkv(pf, cf, nf, cw_ref, j > 0, True)
            xb = _delta_qkv(pb, cb, nb, cw_ref, True, j > 0)
            qkv_s[m, 0] = xf
            qkv_s[m, 1] = xb
            seen_s[m, pl.ds(j, 1)] = xf[None]
            seen_s[m, pl.ds(mirror, 1)] = xb[None]

    @pl.when(jnp.logical_and(lat, j >= half))
    def _():
        for m in members:
            qkv_s[m, 0] = seen_s[m, pl.ds(j, 1)][0]
            qkv_s[m, 1] = seen_s[m, pl.ds(mirror, 1)][0]


def _delta_gates(gf_ref, gb_ref, alog_col, dtb_col, expand_ref):
    is_b, pick = _gate_rows(gf_ref, gb_ref)
    g = -jnp.exp(alog_col) * _softplus(pick(2 * N_GATE_ROWS) + dtb_col)
    beta = _sigmoid(pick(3 * N_GATE_ROWS))
    gc = _chunk_sums(g, is_b)
    totals = [jnp.sum(g[:, c * CHUNK:(c + 1) * CHUNK], axis=1, keepdims=True) for c in range(N_CHUNK)]
    gl = _on_lanes(totals, False, is_b)
    return _expand([gc, jnp.exp(gc), jnp.exp(gl - gc), jnp.exp(gl), beta], expand_ref)


def _delta_kernel(pf_ref, cf_ref, nf_ref, pb_ref, cb_ref, nb_ref, gf_ref, gb_ref, cw_ref, al_ref, dt_ref,
                  expand_ref, s0_ref, of_ref, ob_ref, s_out, s_s, qkv_s, seen_s, *, pairs):
    s = pl.program_id(0)
    members = range(N_PAIR)
    lat = jnp.logical_not(pairs.is_ctx(s))

    @pl.when(pairs.first(s))
    def _():
        for m in members:
            for d in range(N_DIR):
                s_s[m, d] = jnp.where(lat, _to_block_diag(s0_ref[m, 0, d]), 0.0)

    _delta_stage_qkv(s, pairs, ((pf_ref, cf_ref, nf_ref), (pb_ref, cb_ref, nb_ref)), cw_ref, qkv_s, seen_s)
    both = lambda col: jnp.concatenate([col] * N_PAIR, axis=0)
    gc, eg, k_fac, g_last, beta = _delta_gates(gf_ref, gb_ref, both(al_ref[0]), both(dt_ref[0]), expand_ref)
    q, k, v = (jnp.concatenate([_chunks(qkv_s[m, d, :, i * SCAN_W:(i + 1) * SCAN_W])
                                for m in members for d in range(N_DIR)], axis=0) for i in range(3))
    kb = k * beta
    k_dec = k * k_fac

    mask_bd = _block_ones(SCAN_W)
    mask_bf = mask_bd.astype(BF16)
    diag, tri, strict = _problem_masks()
    neg_diag = jnp.sum(jnp.where(diag, -gc, 0.0), axis=1, keepdims=True)
    decay = jnp.exp(jnp.where(tri, gc + neg_diag, -jnp.inf))
    k_bd = _bd3(_bf(k), mask_bf)
    kq = _bdot_nt(_bf(jnp.concatenate([kb, q], axis=1)), k_bd)
    qk = _bf(kq[:, CHUNK:] * decay)
    t_all = _bf(_neumann_inverse(jnp.where(strict, kq[:, 0:CHUNK] * decay, 0.0), mask_bf))
    u = _bdot(t_all, _bd3(_bf(v * beta), mask_bf))
    w = _bdot(t_all, _bd3(_bf(kb * eg), mask_bf))
    wq = _bf(jnp.concatenate([w, q * eg], axis=1))

    outs = (of_ref, ob_ref)
    chains = [(m, d) for m in members for d in range(N_DIR)]
    state = jnp.concatenate([s_s[m, d][None] for m, d in chains], axis=0)
    for i in range(N_CHUNK):
        probs = [(m * N_DIR + d) * N_CHUNK + (N_CHUNK - 1 - i if d else i) for m, d in chains]
        pick = lambda x: jnp.concatenate([x[p][None] for p in probs], axis=0)
        ws = _bdot(pick(wq), _bf(state))
        v_new = pick(u) - ws[:, 0:CHUNK]
        out = ws[:, CHUNK:] + _bdot(pick(qk), _bd3(_bf(v_new), mask_bf))
        for n, (m, d) in enumerate(chains):
            c = N_CHUNK - 1 - i if d else i
            outs[d][m, c * CHUNK:(c + 1) * CHUNK, :] = out[n]
        kv_outer = lax.dot_general(_bf(pick(k_dec)), _bf(v_new), (((1,), (1,)), ((0,), (0,))),
                                   preferred_element_type=F32)
        state = state * pick(g_last)[:, 0:1, :] + jnp.where(mask_bd, kv_outer, 0.0)
    for n, (m, d) in enumerate(chains):
        s_s[m, d] = state[n]

    @pl.when(pairs.last(s))
    def _():
        for n, (m, d) in enumerate(chains):
            s_out[0, m, d] = _from_block_diag(state[n])


def _delta(layer, pairs, zg, gates, conv_w, alog_cols, dtb_cols, state_s):
    per = SEG // HALO
    last_halo = pairs.rows // HALO - 1

    def prev(rev):
        return pl.BlockSpec((N_PAIR, HALO, D_MODEL),
                            lambda s: (pairs.macro(s), jnp.maximum(pairs.seg(s, rev) * per - 1, 0), 0))

    def nxt(rev):
        return pl.BlockSpec((N_PAIR, HALO, D_MODEL),
                            lambda s: (pairs.macro(s), jnp.minimum((pairs.seg(s, rev) + 1) * per, last_halo), 0))

    gate_cols = pl.BlockSpec((1, N_GATE_ROWS, 1), lambda s: (layer, 0, 0))
    s_spec, s_shape = pairs.state_out(_STATE_MAT)
    expand5 = _expander(5)
    o_shape = jax.ShapeDtypeStruct(zg.shape[:2] + (SCAN_W,), F32)
    return dict(
        kernel=functools.partial(_delta_kernel, pairs=pairs),
        in_specs=[
            prev(False), pairs.block(False, D_MODEL), nxt(False),
            prev(True), pairs.block(True, D_MODEL), nxt(True),
            pairs.block(False, GATE_W), pairs.block(True, GATE_W),
            pl.BlockSpec((1, CONV_W, 3 * SCAN_W), lambda s: (layer, 0, 0)),
            gate_cols, gate_cols, _resident(expand5.shape, lambda s: (0, 0)),
            pairs.state_in(layer, _STATE_MAT),
        ],
        out_specs=[pairs.block(False, SCAN_W), pairs.block(True, SCAN_W), s_spec],
        out_shape=[o_shape, o_shape, s_shape],
        scratch_shapes=[pltpu.VMEM((N_PAIR, N_DIR, SCAN_W, SCAN_W), F32),
                        pltpu.VMEM((N_PAIR, N_DIR, SEG, 3 * SCAN_W), F32),
                        pltpu.VMEM((N_PAIR, pairs.per_seq, SEG, 3 * SCAN_W), F32)],
        operands=(zg, zg, zg, zg, zg, zg, gates, gates, conv_w, alog_cols, dtb_cols, expand5, state_s))


def _scan_kernel(*refs, parts):
    n_in = sum(len(p["in_specs"]) for p in parts)
    n_out = sum(len(p["out_specs"]) for p in parts)
    ins, outs, scratch = refs[:n_in], refs[n_in:n_in + n_out], refs[n_in + n_out:]
    for p in parts:
        k_in, k_out, k_scr = len(p["in_specs"]), len(p["out_specs"]), len(p["scratch_shapes"])
        p["kernel"](*ins[:k_in], *outs[:k_out], *scratch[:k_scr])
        ins, outs, scratch = ins[k_in:], outs[k_out:], scratch[k_scr:]


def _scan_mixers(pairs, parts):
    join = lambda key: [x for p in parts for x in p[key]]
    outs = pl.pallas_call(
        functools.partial(_scan_kernel, parts=parts),
        grid=(pairs.n_step,),
        in_specs=join("in_specs"),
        out_specs=join("out_specs"),
        out_shape=join("out_shape"),
        scratch_shapes=join("scratch_shapes"),
        compiler_params=pltpu.CompilerParams(
            dimension_semantics=("arbitrary",), vmem_limit_bytes=SCAN_VMEM_LIMIT),
        name="scans",
    )(*join("operands"))
    split, lo = [], 0
    for p in parts:
        split.append(outs[lo:lo + len(p["out_specs"])])
        lo += len(p["out_specs"])
    return split


def _attend(q_ref, kv_blocks, o_ref):
    qt = q_ref[...].T
    zeros = jnp.zeros((HEAD_DIM, A_GROUP * SEG), BF16)
    ones = lambda n: jnp.ones((BF16_ROWS, n), F32)
    kv = [(_bf(k), _bf(jnp.concatenate([v.T, ones(v.shape[0])], axis=0))) for k, v in kv_blocks]
    w_groups = []
    for g in range(A_KV_HEADS):
        heads = range(g * A_GROUP, (g + 1) * A_GROUP)
        slab = _bf(jnp.concatenate([qt[h * HEAD_DIM:(h + 1) * HEAD_DIM, :] for h in heads], axis=1))
        w_groups.append(jnp.concatenate([slab if i == g else zeros for i in range(A_KV_HEADS)], axis=0))
    w_all = jnp.concatenate(w_groups, axis=1)
    m = acc = None
    nxt = jnp.dot(kv[0][0], w_all, preferred_element_type=F32)
    for i, (_, v_t) in enumerate(kv):
        sc = nxt
        if i + 1 < len(kv):
            nxt = jnp.dot(kv[i + 1][0], w_all, preferred_element_type=F32)
        m_blk = jnp.max(sc, axis=0, keepdims=True)
        m_new = m_blk if m is None else jnp.maximum(m, m_blk)
        pv = jnp.dot(v_t, _bf(jnp.exp(sc - m_new)), preferred_element_type=F32)
        acc = pv if m is None else acc * jnp.exp(m - m_new) + pv
        m = m_new
    out = acc[0:KV_WIDTH] / acc[KV_WIDTH:KV_WIDTH + 1]
    lanes = A_GROUP * SEG
    group_out = [out[g * HEAD_DIM:(g + 1) * HEAD_DIM, g * lanes:(g + 1) * lanes] for g in range(A_KV_HEADS)]
    for col in range(A_WIDTH // KV_WIDTH):
        g, h0 = (2 * col) // A_GROUP, (2 * col) % A_GROUP
        pair = jnp.concatenate([group_out[g][:, h0 * SEG:(h0 + 1) * SEG],
                                group_out[g][:, (h0 + 1) * SEG:(h0 + 2) * SEG]], axis=0)
        o_ref[:, col * KV_WIDTH:(col + 1) * KV_WIDTH] = pair.T


def _attn_kernel(q_ref, kc_ref, vc_ref, kl_ref, vl_ref, ck_ref, cv_ref, o_ref, *, segs):
    s = pl.program_id(0)

    @pl.when(segs.is_ctx(s))
    def _():
        _attend(q_ref, [(kc_ref[...], vc_ref[...])], o_ref)

    @pl.when(jnp.logical_not(segs.is_ctx(s)))
    def _():
        blocks = []
        past, t_lat = ck_ref.shape[2], kl_ref.shape[0]
        for lo in range(0, past, KV_BLOCK):
            hi = min(lo + KV_BLOCK, past)
            blocks.append((ck_ref[0, 0, lo:hi, :], cv_ref[0, 0, lo:hi, :]))
        for lo in range(0, t_lat, KV_BLOCK):
            hi = min(lo + KV_BLOCK, t_lat)
            blocks.append((kl_ref[lo:hi, :], vl_ref[lo:hi, :]))
        _attend(q_ref, blocks, o_ref)


def _attention(layer, segs, aq, ak, av, cache_k, cache_v):
    n_tok = aq.shape[0]
    t_lat = segs.per_seq * SEG
    past = cache_k.shape[2]
    assert t_lat % KV_BLOCK == 0
    lat_blocks_before = segs.n_ctx_seg * SEG // t_lat
    own = lambda w: pl.BlockSpec((SEG, w), lambda s: (s, 0))
    lat_kv = pl.BlockSpec((t_lat, KV_WIDTH), lambda s: (lat_blocks_before + segs.lat_seq(s), 0))
    cache = pl.BlockSpec((1, 1, past, KV_WIDTH), lambda s: (segs.lat_seq(s), layer, 0, 0))
    return pl.pallas_call(
        functools.partial(_attn_kernel, segs=segs),
        grid=(segs.n_seg,),
        in_specs=[own(A_WIDTH), own(KV_WIDTH), own(KV_WIDTH), lat_kv, lat_kv, cache, cache],
        out_specs=own(A_WIDTH),
        out_shape=jax.ShapeDtypeStruct((n_tok, A_WIDTH), F32),
        compiler_params=pltpu.CompilerParams(
            dimension_semantics=("arbitrary",), vmem_limit_bytes=VMEM_LIMIT),
        name="attention",
    )(aq, ak, av, ak, av, cache_k, cache_v)


def _permute_w_in(w_in):
    m, g, a, kv = SCAN_W, SCAN_W, A_WIDTH, KV_WIDTH
    sizes = (m, m, m, m, 8, 8, 3 * g, g, 8, 8, a, kv, kv)
    offs = [0]
    for sz in sizes:
        offs.append(offs[-1] + sz)
    piece = lambda i: _bf(w_in[..., offs[i]:offs[i + 1]])
    pad = jnp.zeros(w_in.shape[:-1] + (GATE_W - 32,), BF16)
    order = [0, 1, 2, 3, 6, 7, 10, 11, 12, 4, 5, 8, 9]
    return jnp.concatenate([piece(i) for i in order] + [pad], axis=-1)


def _rope_tables(t_lat):
    rows = t_lat // GRID_W
    row = np.repeat(np.arange(rows, dtype=np.float64), GRID_W)
    col = np.tile(np.arange(GRID_W, dtype=np.float64), rows)
    n_freq = HEAD_DIM // 4
    inv = ROPE_BASE ** (-np.arange(n_freq, dtype=np.float64) / n_freq)
    ang = np.stack([row[:, None] * inv, col[:, None] * inv], axis=1)
    cos, sin = np.cos(ang), np.sin(ang)
    cos_h = np.concatenate([cos, cos], axis=-1).reshape(t_lat, HEAD_DIM)
    sin_h = np.concatenate([-sin, sin], axis=-1).reshape(t_lat, HEAD_DIM)
    wide = lambda a: np.tile(a, (1, KV_WIDTH // HEAD_DIM))
    cos_t = np.concatenate([wide(cos_h), np.ones((TM, KV_WIDTH))], axis=0)
    sin_t = np.concatenate([wide(sin_h), np.zeros((TM, KV_WIDTH))], axis=0)
    return jnp.asarray(cos_t, F32), jnp.asarray(sin_t, F32)


def kernel(x_prompt, x_sample, c, cache_k, cache_v, state_mlstm_C, state_mlstm_n, state_mlstm_m,
           state_delta_S, c_ctx, ada_w, ada_b, norm_g, ffn_w_in, ffn_w_out, w_in, w_out, mlstm_f_bias,
           mlstm_norm, delta_conv, delta_a_log, delta_dt_bias, delta_norm, attn_q_norm, attn_k_norm,
           final_norm):
    batch, t_ctx, d_model = x_prompt.shape
    n_lat_seq, t_lat, _ = x_sample.shape
    depth = ada_w.shape[0]
    assert d_model == D_MODEL and norm_g.shape[1] == 3 and ffn_w_in.shape[-1] == 2 * D_FF
    assert 1 + n_lat_seq <= MOD_ROWS
    n_ctx = batch * t_ctx
    n_tok = n_ctx + n_lat_seq * t_lat
    assert n_ctx % t_lat == 0
    segs = _Segs(n_ctx, t_ctx, n_lat_seq, t_lat)
    pairs = _Pairs(n_ctx, t_ctx, n_lat_seq, t_lat)
    tiles = _Tiles(n_ctx, t_lat)

    cond = jnp.concatenate([c_ctx[None, :], c, jnp.zeros((MOD_ROWS - 1 - n_lat_seq, D_MODEL), F32)], axis=0)
    mod = _modulation(cond, ada_w, ada_b)

    wfi = _bf(ffn_w_in).reshape(depth * 2, D_MODEL, 2 * D_FF)
    wfo = _bf(ffn_w_out).reshape(depth * 2, D_FF, D_MODEL)
    wz = _permute_w_in(w_in)
    wo = _bf(w_out)
    qn = jnp.tile(attn_q_norm, (1, A_HEADS)).reshape(depth, 1, A_WIDTH)
    kn = jnp.tile(attn_k_norm, (1, A_KV_HEADS)).reshape(depth, 1, KV_WIDTH)
    gn = jnp.tile(delta_norm, (1, N_HEADS)).reshape(depth, 1, SCAN_W)
    mn = mlstm_norm.reshape(depth, 1, SCAN_W)
    fin = final_norm.reshape(1, D_MODEL)
    fb_cols = mlstm_f_bias.reshape(depth, N_GATE_ROWS, 1)
    alog_cols = delta_a_log.reshape(depth, N_GATE_ROWS, 1)
    dtb_cols = delta_dt_bias.reshape(depth, N_GATE_ROWS, 1)
    cos_t, sin_t = _rope_tables(t_lat)
    ck = cache_k.reshape(cache_k.shape[:3] + (KV_WIDTH,))
    cv = cache_v.reshape(cache_v.shape[:3] + (KV_WIDTH,))
    n0 = state_mlstm_n.reshape(n_lat_seq, depth, N_DIR, 1, SCAN_W)
    m0 = state_mlstm_m.reshape(n_lat_seq, depth, N_GATE_ROWS, 1)

    x_ctx, x_lat, lat_tile0 = x_prompt.reshape(n_ctx, D_MODEL), x_sample.reshape(n_tok - n_ctx, D_MODEL), 0
    ks, vs, cs, ns, ms, ss = [], [], [], [], [], []
    for l in range(depth):
        x1, zm, zg, aq, ak, av, gates = _dense1(l, tiles, n_tok, x_ctx, x_lat, lat_tile0, mod, norm_g,
                                                wfi, wfo, wz, qn, kn, cos_t, sin_t)
        zm_v, zg_v, gates_v = pairs.view(zm), pairs.view(zg), pairs.view(gates)
        (hf, hb, c_new, n_new, m_new), (o_f, o_b, s_new) = _scan_mixers(pairs, [
            _mlstm(l, pairs, zm_v, gates_v, fb_cols, state_mlstm_C, n0, m0),
            _delta(l, pairs, zg_v, gates_v, delta_conv, alog_cols, dtb_cols, state_delta_S)])
        hf, hb, o_f, o_b = (a.reshape(n_tok, SCAN_W) for a in (hf, hb, o_f, o_b))
        a_out = _attention(l, segs, aq, ak, av, ck, cv)
        x = _dense2(l, tiles, x1, mod, norm_g, hf, hb, zm, mn, o_f, o_b, zg, gn, a_out, wo, wfi, wfo, fin,
                    final=(l == depth - 1))
        if l < depth - 1:
            x_ctx, x_lat, lat_tile0 = x, x, tiles.n_ctx

        ks.append(ak[:n_ctx].reshape(batch, t_ctx, A_KV_HEADS, HEAD_DIM))
        vs.append(av[:n_ctx].reshape(batch, t_ctx, A_KV_HEADS, HEAD_DIM))
        cs.append(pairs.ctx_states(c_new))
        ns.append(pairs.ctx_states(n_new).reshape(batch, N_DIR, N_HEADS, HEAD_DIM))
        ms.append(pairs.ctx_states(m_new).reshape(batch, N_DIR, N_HEADS))
        ss.append(pairs.ctx_states(s_new))

    y_prompt = x[0].reshape(batch, t_ctx, D_MODEL)
    y_sample = x[1].reshape(n_lat_seq, t_lat, D_MODEL)
    stack = lambda xs: jnp.stack(xs, axis=1)
    return (y_prompt, y_sample, stack(ks), stack(vs), stack(cs), stack(ns), stack(ms), stack(ss))
```
